```python
import math
import jax, jax.numpy as jnp
from jax import lax
import numpy as np

D_MODEL = 1024
BATCH = 4
SEQ = 4096
DEPTH = 2

N_META = 16
D_FF = 2816
NORM_EPS = 1e-6
NEG_INF = -1e30
SWA_Q_HEADS = 8
SWA_KV_HEADS = 2
SWA_HEAD_DIM = 64
SWA_WINDOW = 128
SWA_BLOCK = 128
REL_BUCKETS = 32
REL_MAX_DIST = 128
SWA_Q_DIM = SWA_Q_HEADS * SWA_HEAD_DIM
SWA_KV_DIM = SWA_KV_HEADS * SWA_HEAD_DIM
DN_HEADS = 4
DN_HEAD_DIM = 128
DN_DIM = DN_HEADS * DN_HEAD_DIM
DN_CONV = 4
DN_CHUNK = 64
GLA_HEADS = 4
GLA_KEY_DIM = D_MODEL // 2
GLA_VAL_DIM = D_MODEL
GLA_GATE_RANK = 16
GLA_GATE_NORM = 16.0
GLA_CHUNK = 64
N_EVEN = (DEPTH + 1) // 2
N_ODD = DEPTH // 2
EVEN_IN_SIZES = (SWA_Q_DIM, SWA_KV_DIM, SWA_KV_DIM, DN_DIM, DN_DIM, DN_DIM, DN_DIM, DN_HEADS, DN_HEADS)
EVEN_IN_DIM = sum(EVEN_IN_SIZES)
EVEN_MIX_DIM = SWA_Q_DIM + DN_DIM
ODD_IN_SIZES = (GLA_KEY_DIM, GLA_KEY_DIM, GLA_VAL_DIM, GLA_VAL_DIM, GLA_GATE_RANK)
ODD_IN_DIM = sum(ODD_IN_SIZES)

kernel_name = "hybrid_swa_deltanet_gla_macaron"


def _split(t, sizes):
    idx = np.cumsum(sizes)[:-1].tolist()
    return jnp.split(t, idx, axis=-1)


def rms_norm(x, w):
    x32 = x.astype(jnp.float32)
    y = x32 * lax.rsqrt(jnp.mean(x32 * x32, axis=-1, keepdims=True) + NORM_EPS)
    return (y * w.astype(jnp.float32)).astype(x.dtype)


def l2_norm(x):
    return x * lax.rsqrt(jnp.sum(x * x, axis=-1, keepdims=True) + 1e-6)


def swiglu(x, w_gate, w_up, w_down):
    return (jax.nn.silu(x @ w_gate) * (x @ w_up)) @ w_down


def causal_depthwise_conv(x, w):
    k = w.shape[0]
    return lax.conv_general_dilated(x, w[:, None, :].astype(x.dtype), window_strides=(1,),
                                    padding=[(k - 1, 0)], dimension_numbers=('NWC', 'WIO', 'NWC'),
                                    feature_group_count=x.shape[-1])


def t5_bucket(rel):
    n = jnp.maximum(rel, 0)
    max_exact = REL_BUCKETS // 2
    n_f = jnp.maximum(n, 1).astype(jnp.float32)
    large = max_exact + (jnp.log(n_f / max_exact) / math.log(REL_MAX_DIST / max_exact)
                         * (REL_BUCKETS - max_exact)).astype(jnp.int32)
    large = jnp.minimum(large, REL_BUCKETS - 1)
    return jnp.where(n < max_exact, n, large)


def to_chunks(t, chunk):
    pad = chunk - N_META
    t = jnp.pad(t, [(0, 0), (pad, 0)] + [(0, 0)] * (t.ndim - 2))
    b, lp = t.shape[:2]
    t = t.reshape(b, lp // chunk, chunk, *t.shape[2:])
    return jnp.moveaxis(t, 3, 1)


def from_chunks(t, chunk):
    t = jnp.moveaxis(t, 1, 3)
    b, n, c = t.shape[:3]
    return t.reshape(b, n * c, *t.shape[3:])[:, chunk - N_META:]


def _chunk_first(t):
    return jnp.moveaxis(t, 2, 0)


def sliding_window_attention(q, k, v, sinks, rel_table):
    f32 = jnp.float32
    b, l = q.shape[:2]
    g = SWA_Q_HEADS // SWA_KV_HEADS
    dh = SWA_HEAD_DIM
    pad = SWA_BLOCK - N_META
    lp = l + pad
    nb = lp // SWA_BLOCK
    padw = ((0, 0), (pad, 0), (0, 0), (0, 0))
    qb = jnp.pad(q.astype(f32), padw).reshape(b, nb, SWA_BLOCK, SWA_KV_HEADS, g, dh)
    kb = jnp.pad(k.astype(f32), padw).reshape(b, nb, SWA_BLOCK, SWA_KV_HEADS, dh)
    vb = jnp.pad(v.astype(f32), padw).reshape(b, nb, SWA_BLOCK, SWA_KV_HEADS, dh)
    prev = ((0, 0), (1, 0), (0, 0), (0, 0), (0, 0))
    k_band = jnp.concatenate([jnp.pad(kb[:, :-1], prev), kb], axis=2)
    v_band = jnp.concatenate([jnp.pad(vb[:, :-1], prev), vb], axis=2)
    k_meta = k[:, :N_META].astype(f32)
    v_meta = v[:, :N_META].astype(f32)
    scale = dh ** -0.5
    s_band = jnp.einsum('bnqhgd,bnkhd->bhgnqk', qb, k_band) * scale
    s_meta = jnp.einsum('bnqhgd,bmhd->bhgnqm', qb, k_meta) * scale
    blk = jnp.arange(nb)[:, None]
    pos_q = blk * SWA_BLOCK + jnp.arange(SWA_BLOCK)[None, :] - pad
    pos_kb = blk * SWA_BLOCK + jnp.arange(2 * SWA_BLOCK)[None, :] - SWA_BLOCK - pad
    rel_b = pos_q[:, :, None] - pos_kb[:, None, :]
    mask_b = (pos_kb[:, None, :] >= N_META) & (rel_b >= 0) & (rel_b < SWA_WINDOW)
    rel_m = pos_q[:, :, None] - jnp.arange(N_META)[None, None, :]
    mask_m = rel_m >= 0
    table = rel_table.astype(f32)

    def rel_bias(rel):
        bias = table[t5_bucket(rel)]
        return jnp.moveaxis(bias, -1, 0).reshape(SWA_KV_HEADS, g, *rel.shape)

    s_band = jnp.where(mask_b, s_band + rel_bias(rel_b), NEG_INF)
    s_meta = jnp.where(mask_m, s_meta + rel_bias(rel_m), NEG_INF)
    sink = jnp.broadcast_to(sinks.astype(f32).reshape(SWA_KV_HEADS, g, 1, 1, 1),
                            (b, SWA_KV_HEADS, g, nb, SWA_BLOCK, 1))
    p = jax.nn.softmax(jnp.concatenate([s_band, s_meta, sink], axis=-1), axis=-1)
    nk = 2 * SWA_BLOCK
    o = (jnp.einsum('bhgnqk,bnkhd->bnqhgd', p[..., :nk], v_band)
         + jnp.einsum('bhgnqm,bmhd->bnqhgd', p[..., nk:nk + N_META], v_meta))
    return o.reshape(b, lp, SWA_Q_DIM)[:, pad:]


def chunk_gated_delta_rule(q, k, v, g, beta):
    c = q.shape[-2]
    tri_incl = jnp.tril(jnp.ones((c, c), bool))
    tri_strict = jnp.tril(jnp.ones((c, c), bool), -1)
    gc = jnp.cumsum(g, axis=-1)
    diff = gc[..., :, None] - gc[..., None, :]
    gamma = jnp.where(tri_incl, jnp.exp(jnp.where(tri_incl, diff, 0.0)), 0.0)
    k_beta = k * beta[..., None]
    v_beta = v * beta[..., None]
    a_strict = jnp.where(tri_strict, jnp.einsum('bhnid,bhnjd->bhnij', k_beta, k) * gamma, 0.0)
    m = a_strict + jnp.eye(c, dtype=jnp.float32)
    u = lax.linalg.triangular_solve(m, v_beta, left_side=True, lower=True, unit_diagonal=True)
    w = lax.linalg.triangular_solve(m, k_beta * jnp.exp(gc)[..., None], left_side=True,
                                    lower=True, unit_diagonal=True)
    attn = jnp.einsum('bhnid,bhnjd->bhnij', q, k) * gamma
    q_dec = q * jnp.exp(gc)[..., None]
    k_dec = k * jnp.exp(gc[..., -1:] - gc)[..., None]
    g_last = jnp.exp(gc[..., -1])

    def step(s, xs):
        attn_n, u_n, w_n, qd_n, kd_n, gl_n = xs
        v_new = u_n - jnp.einsum('bhik,bhkv->bhiv', w_n, s)
        o = jnp.einsum('bhik,bhkv->bhiv', qd_n, s) + jnp.einsum('bhij,bhjv->bhiv', attn_n, v_new)
        s = s * gl_n[..., None, None] + jnp.einsum('bhik,bhiv->bhkv', kd_n, v_new)
        return s, o

    bsz, h, _, _, dk = q.shape
    s0 = jnp.zeros((bsz, h, dk, v.shape[-1]), jnp.float32)
    xs = tuple(_chunk_first(t) for t in (attn, u, w, q_dec, k_dec, g_last))
    _, o = lax.scan(step, s0, xs)
    return jnp.moveaxis(o, 0, 2)


def chunk_gla(q, k, v, glog):
    c = q.shape[-2]
    tri = jnp.tril(jnp.ones((c, c), bool))
    bcum = jnp.cumsum(glog, axis=-2)
    q_dec = q * jnp.exp(bcum)
    attn = jnp.where(tri, jnp.einsum('bhnik,bhnjk->bhnij', q_dec, k * jnp.exp(-bcum)), 0.0)
    o_intra = jnp.einsum('bhnij,bhnjv->bhniv', attn, v)
    b_last = bcum[..., -1:, :]
    k_dec = k * jnp.exp(b_last - bcum)
    decay = jnp.exp(b_last[..., 0, :])

    def step(s, xs):
        qd, kd, vv, dl = xs
        o = jnp.einsum('bhik,bhkv->bhiv', qd, s)
        s = s * dl[..., :, None] + jnp.einsum('bhik,bhiv->bhkv', kd, vv)
        return s, o

    bsz, h, _, _, dk = q.shape
    s0 = jnp.zeros((bsz, h, dk, v.shape[-1]), jnp.float32)
    _, o_inter = lax.scan(step, s0, tuple(_chunk_first(t) for t in (q_dec, k_dec, v, decay)))
    return o_intra + jnp.moveaxis(o_inter, 0, 2)


def gated_deltanet(q, k, v, a, bgate, z, conv_w, a_log, dt_bias, norm_w):
    f32 = jnp.float32
    bsz, l, _ = q.shape
    hd = (bsz, l, DN_HEADS, DN_HEAD_DIM)
    qkv = jax.nn.silu(causal_depthwise_conv(jnp.concatenate([q, k, v], axis=-1), conv_w))
    q, k, v = _split(qkv, (DN_DIM, DN_DIM, DN_DIM))
    q = l2_norm(q.reshape(hd).astype(f32)) * DN_HEAD_DIM ** -0.5
    k = l2_norm(k.reshape(hd).astype(f32))
    v = v.reshape(hd).astype(f32)
    beta = jax.nn.sigmoid(bgate.astype(f32))
    g = -jnp.exp(a_log.astype(f32)) * jax.nn.softplus(a.astype(f32) + dt_bias.astype(f32))
    o = chunk_gated_delta_rule(to_chunks(q, DN_CHUNK), to_chunks(k, DN_CHUNK), to_chunks(v, DN_CHUNK),
                               to_chunks(g, DN_CHUNK), to_chunks(beta, DN_CHUNK))
    o = from_chunks(o, DN_CHUNK)
    o = rms_norm(o, norm_w) * jax.nn.silu(z.reshape(hd).astype(f32))
    return o.reshape(bsz, l, DN_DIM).astype(z.dtype)


def even_mixer(h, w_in, conv_w, sinks, a_log, dt_bias, dn_norm_w, w_out, rel_table):
    bsz, l, _ = h.shape
    qa, ka, va, qb, kb, vb, zb, bb, ab = _split(h @ w_in, EVEN_IN_SIZES)
    o_a = sliding_window_attention(qa.reshape(bsz, l, SWA_Q_HEADS, SWA_HEAD_DIM),
                                   ka.reshape(bsz, l, SWA_KV_HEADS, SWA_HEAD_DIM),
                                   va.reshape(bsz, l, SWA_KV_HEADS, SWA_HEAD_DIM), sinks, rel_table)
    o_b = gated_deltanet(qb, kb, vb, ab, bb, zb, conv_w, a_log, dt_bias, dn_norm_w)
    return jnp.concatenate([o_a.astype(h.dtype), o_b.astype(h.dtype)], axis=-1) @ w_out


def odd_mixer(h, w_in, w_gate_up, b_gate, norm_w, w_out):
    f32 = jnp.float32
    bsz, l, _ = h.shape
    dk = GLA_KEY_DIM // GLA_HEADS
    dv = GLA_VAL_DIM // GLA_HEADS
    q, k, v, g, gk_low = _split(h @ w_in, ODD_IN_SIZES)
    glog = jax.nn.log_sigmoid((gk_low @ w_gate_up + b_gate).astype(f32)) / GLA_GATE_NORM
    q = q.reshape(bsz, l, GLA_HEADS, dk).astype(f32) * dk ** -0.5
    k = k.reshape(bsz, l, GLA_HEADS, dk).astype(f32)
    v = v.reshape(bsz, l, GLA_HEADS, dv).astype(f32)
    glog = glog.reshape(bsz, l, GLA_HEADS, dk)
    o = chunk_gla(to_chunks(q, GLA_CHUNK), to_chunks(k, GLA_CHUNK), to_chunks(v, GLA_CHUNK),
                  to_chunks(glog, GLA_CHUNK))
    o = from_chunks(o, GLA_CHUNK)
    o = rms_norm(o, norm_w) * jax.nn.silu(g.reshape(bsz, l, GLA_HEADS, dv).astype(f32))
    return o.reshape(bsz, l, GLA_VAL_DIM).astype(h.dtype) @ w_out


def setup_inputs(seed: int = 0) -> dict:
    key = jax.random.key(seed)
    ks = jax.random.split(key, 24)
    f32 = jnp.float32

    def nrm(k, shape, fan_in):
        return jax.random.normal(k, shape, f32) * fan_in ** -0.5

    dt = jnp.exp(jax.random.uniform(ks[10], (N_EVEN, DN_HEADS), f32) * (math.log(0.1) - math.log(0.001))
                 + math.log(0.001))
    return {
        "x": jax.random.normal(ks[0], (BATCH, SEQ, D_MODEL), f32),
        "meta_tokens": jax.random.normal(ks[1], (N_META, D_MODEL), f32),
        "norm_w": 1.0 + 0.02 * jax.random.normal(ks[2], (DEPTH, 6, D_MODEL), f32),
        "ffn_w_gate": nrm(ks[3], (DEPTH, 2, D_MODEL, D_FF), D_MODEL),
        "ffn_w_up": nrm(ks[4], (DEPTH, 2, D_MODEL, D_FF), D_MODEL),
        "ffn_w_down": nrm(ks[5], (DEPTH, 2, D_FF, D_MODEL), D_FF),
        "rel_bias_table": 0.5 * jax.random.normal(ks[6], (REL_BUCKETS, SWA_Q_HEADS), f32),
        "even_w_in": nrm(ks[7], (N_EVEN, D_MODEL, EVEN_IN_DIM), D_MODEL),
        "even_conv_w": nrm(ks[8], (N_EVEN, DN_CONV, 3 * DN_DIM), DN_CONV),
        "swa_sinks": jax.random.normal(ks[9], (N_EVEN, SWA_Q_HEADS), f32),
        "dn_a_log": jnp.log(jax.random.uniform(ks[11], (N_EVEN, DN_HEADS), f32, 1.0, 16.0)),
        "dn_dt_bias": dt + jnp.log(-jnp.expm1(-dt)),
        "dn_norm_w": 1.0 + 0.02 * jax.random.normal(ks[12], (N_EVEN, DN_HEAD_DIM), f32),
        "even_w_out": nrm(ks[13], (N_EVEN, EVEN_MIX_DIM, D_MODEL), EVEN_MIX_DIM),
        "odd_w_in": nrm(ks[14], (N_ODD, D_MODEL, ODD_IN_DIM), D_MODEL),
        "gla_w_gate_up": nrm(ks[15], (N_ODD, GLA_GATE_RANK, GLA_KEY_DIM), GLA_GATE_RANK),
        "gla_b_gate": 0.1 * jax.random.normal(ks[16], (N_ODD, GLA_KEY_DIM), f32),
        "gla_norm_w": 1.0 + 0.02 * jax.random.normal(ks[17], (N_ODD, GLA_VAL_DIM // GLA_HEADS), f32),
        "odd_w_out": nrm(ks[18], (N_ODD, GLA_VAL_DIM, D_MODEL), GLA_VAL_DIM),
    }


def reference(x, meta_tokens, norm_w, ffn_w_gate, ffn_w_up, ffn_w_down, rel_bias_table,
              even_w_in, even_conv_w, swa_sinks, dn_a_log, dn_dt_bias, dn_norm_w, even_w_out,
              odd_w_in, gla_w_gate_up, gla_b_gate, gla_norm_w, odd_w_out):
    bsz = x.shape[0]
    meta = jnp.broadcast_to(meta_tokens[None].astype(x.dtype), (bsz, N_META, x.shape[-1]))
    h = jnp.concatenate([meta, x], axis=1)
    for layer in range(DEPTH):
        nw = norm_w[layer]
        f = swiglu(rms_norm(h, nw[0]), ffn_w_gate[layer, 0], ffn_w_up[layer, 0], ffn_w_down[layer, 0])
        h = h + 0.5 * rms_norm(f, nw[1])
        hn = rms_norm(h, nw[2])
        if layer % 2 == 0:
            i = layer // 2
            mix = even_mixer(hn, even_w_in[i], even_conv_w[i], swa_sinks[i], dn_a_log[i], dn_dt_bias[i],
                             dn_norm_w[i], even_w_out[i], rel_bias_table)
        else:
            i = layer // 2
            mix = odd_mixer(hn, odd_w_in[i], gla_w_gate_up[i], gla_b_gate[i], gla_norm_w[i], odd_w_out[i])
        h = h + rms_norm(mix, nw[3])
        f = swiglu(rms_norm(h, nw[4]), ffn_w_gate[layer, 1], ffn_w_up[layer, 1], ffn_w_down[layer, 1])
        h = h + 0.5 * rms_norm(f, nw[5])
    return h[:, N_META:]
```

```python
import functools
import math

import numpy as np
import jax
import jax.numpy as jnp
from jax import lax
from jax.experimental import pallas as pl
from jax.experimental.pallas import tpu as pltpu

F32 = jnp.float32
BF16 = jnp.bfloat16
HIGHEST = lax.Precision.HIGHEST

N_META = 16
NORM_EPS = 1e-6
NEG_INF = -1e30
SWA_Q_HEADS = 8
SWA_KV_HEADS = 2
SWA_HEAD_DIM = 64
SWA_BLOCK = 128
SWA_KEYS = 3 * SWA_BLOCK
REL_BUCKETS = 32
REL_MAX_DIST = 128
DN_HEADS = 4
DN_HEAD_DIM = 128
DN_DIM = DN_HEADS * DN_HEAD_DIM
DN_CONV = 4
GLA_HEADS = 4
GLA_GATE_RANK = 16
GLA_GATE_NORM = 16.0
CHUNK = 64

PAD = SWA_BLOCK - N_META
LANE = 128
ROW_TILE = 512
FF_TILE = 256
VMEM_LIMIT = 56 * 1024 * 1024


def _rms(x, w):
    return x * lax.rsqrt(jnp.mean(x * x, axis=-1, keepdims=True) + NORM_EPS) * w


def _silu(x):
    return x * jax.nn.sigmoid(x)


def _softplus(x):
    return jnp.maximum(x, 0.0) + jnp.log(1.0 + jnp.exp(-jnp.abs(x)))


def _dot(a, b, precision=None):
    return jnp.dot(a, b, preferred_element_type=F32, precision=precision)


def _dot_nt(a, b, precision=None):
    return lax.dot_general(a, b, (((1,), (1,)), ((), ())), preferred_element_type=F32,
                           precision=precision)


def _dot_tn(a, b, precision=None):
    return lax.dot_general(a, b, (((0,), (0,)), ((), ())), preferred_element_type=F32,
                           precision=precision)


def _resident(shape):
    nd = len(shape)
    return pl.BlockSpec(shape, lambda *_: (0,) * nd, pipeline_mode=pl.Buffered(1))


def _params(*semantics):
    return pltpu.CompilerParams(dimension_semantics=semantics, vmem_limit_bytes=VMEM_LIMIT)


def _ffn_body(h_ref, nwa_ref, nwb_ref, wg_ref, wu_ref, wd_ref, o_ref, a_ref):
    x = h_ref[...]
    hn = _rms(x, nwa_ref[...]).astype(BF16)
    d_ff = wg_ref.shape[1]
    for c in range(d_ff // FF_TILE):
        cols = slice(c * FF_TILE, (c + 1) * FF_TILE)
        g = _dot(hn, wg_ref[:, cols])
        u = _dot(hn, wu_ref[:, cols])
        a_ref[:, cols] = (_silu(g) * u).astype(BF16)
    f = _dot(a_ref[...], wd_ref[...])
    o_ref[...] = x + 0.5 * _rms(f, nwb_ref[...])


def _ffn(h, nwa, nwb, wg, wu, wd):
    rows, d = h.shape
    d_ff = wg.shape[1]
    row_spec = pl.BlockSpec((ROW_TILE, d), lambda i: (i, 0))
    return pl.pallas_call(
        _ffn_body,
        grid=(rows // ROW_TILE,),
        in_specs=[row_spec, _resident((1, d)), _resident((1, d)),
                  _resident((d, d_ff)), _resident((d, d_ff)), _resident((d_ff, d))],
        out_specs=row_spec,
        out_shape=jax.ShapeDtypeStruct((rows, d), F32),
        scratch_shapes=[pltpu.VMEM((ROW_TILE, d_ff), BF16)],
        compiler_params=_params("parallel"),
        name="ffn",
    )(h, nwa, nwb, wg, wu, wd)


def _proj_body(h_ref, nw_ref, w_ref, *out_refs):
    hn = _rms(h_ref[...], nw_ref[...]).astype(BF16)
    off = 0
    for o_ref in out_refs:
        n = o_ref.shape[1]
        o_ref[...] = _dot(hn, w_ref[:, off:off + n]).astype(o_ref.dtype)
        off += n


def _norm_proj(h, nw, w, widths, dtypes):
    rows, d = h.shape
    assert sum(widths) == w.shape[1]
    return pl.pallas_call(
        _proj_body,
        grid=(rows // ROW_TILE,),
        in_specs=[pl.BlockSpec((ROW_TILE, d), lambda i: (i, 0)), _resident((1, d)),
                  _resident(w.shape)],
        out_specs=[pl.BlockSpec((ROW_TILE, n), lambda i: (i, 0)) for n in widths],
        out_shape=[jax.ShapeDtypeStruct((rows, n), dt) for n, dt in zip(widths, dtypes)],
        compiler_params=_params("parallel"),
        name="norm_proj",
    )(h, nw, w)


def _outproj_body(*refs):
    *x_refs, w_ref, h_ref, nw_ref, o_ref = refs
    acc = None
    off = 0
    for x_ref in x_refs:
        k = x_ref.shape[1]
        p = _dot(x_ref[...], w_ref[off:off + k, :])
        acc = p if acc is None else acc + p
        off += k
    o_ref[...] = h_ref[...] + _rms(acc, nw_ref[...])


def _out_proj(xs, w, h, nw):
    rows, d = h.shape
    return pl.pallas_call(
        _outproj_body,
        grid=(rows // ROW_TILE,),
        in_specs=[pl.BlockSpec((ROW_TILE, x.shape[1]), lambda i: (i, 0)) for x in xs]
        + [_resident(w.shape), pl.BlockSpec((ROW_TILE, d), lambda i: (i, 0)), _resident((1, d))],
        out_specs=pl.BlockSpec((ROW_TILE, d), lambda i: (i, 0)),
        out_shape=jax.ShapeDtypeStruct((rows, d), F32),
        compiler_params=_params("parallel"),
        name="out_proj",
    )(*xs, w, h, nw)


def _t5_bucket_np(rel):
    n = np.maximum(rel, 0)
    max_exact = REL_BUCKETS // 2
    n_f = np.maximum(n, 1).astype(np.float32)
    large = max_exact + (np.log(n_f / np.float32(max_exact))
                         / np.float32(math.log(REL_MAX_DIST / max_exact))
                         * np.float32(REL_BUCKETS - max_exact)).astype(np.int32)
    large = np.minimum(large, REL_BUCKETS - 1)
    return np.where(n < max_exact, n, large).astype(np.int32)


def _swa_bucket_index():
    i = np.arange(SWA_BLOCK)[:, None]
    j = np.arange(SWA_KEYS)[None, :]
    out = np.full((3, SWA_BLOCK, SWA_KEYS), -1, np.int32)
    for variant in range(3):
        pos_q = variant * SWA_BLOCK + i - PAD
        pos_k = variant * SWA_BLOCK + j - SWA_BLOCK - PAD
        rel_b = pos_q - pos_k
        band = (j < 2 * SWA_BLOCK) & (pos_k >= N_META) & (rel_b >= 0) & (rel_b < SWA_BLOCK)
        m = j - 2 * SWA_BLOCK
        rel_m = pos_q - m
        meta = (m >= 0) & (m < N_META) & (rel_m >= 0)
        bucket = np.where(band, _t5_bucket_np(rel_b), _t5_bucket_np(rel_m))
        out[variant] = np.where(band | meta, bucket, -1)
    return out


_SWA_BUCKET_INDEX = _swa_bucket_index()


def _bias_body(tbl_ref, idx_ref, o_ref):
    head = pl.program_id(1)
    idx = idx_ref[0]
    acc = jnp.full(idx.shape, NEG_INF, F32)
    for b in range(REL_BUCKETS):
        acc = jnp.where(idx == b, tbl_ref[b, head], acc)
    o_ref[0, 0] = acc


def _swa_bias(rel_table):
    idx = jnp.asarray(_SWA_BUCKET_INDEX)
    return pl.pallas_call(
        _bias_body,
        grid=(3, SWA_Q_HEADS),
        in_specs=[pl.BlockSpec(memory_space=pltpu.SMEM),
                  pl.BlockSpec((1, SWA_BLOCK, SWA_KEYS), lambda v, h: (v, 0, 0))],
        out_specs=pl.BlockSpec((1, 1, SWA_BLOCK, SWA_KEYS), lambda v, h: (v, h, 0, 0)),
        out_shape=jax.ShapeDtypeStruct((3, SWA_Q_HEADS, SWA_BLOCK, SWA_KEYS), F32),
        compiler_params=_params("arbitrary", "arbitrary"),
        name="swa_bias",
    )(rel_table, idx)


def _swa_body(sink_ref, q_ref, kc_ref, kp_ref, km_ref, vc_ref, vp_ref, vm_ref, bias_ref, o_ref):
    dh = SWA_HEAD_DIM
    group = SWA_Q_HEADS // SWA_KV_HEADS
    scale = dh ** -0.5
    zpad = jnp.zeros((SWA_BLOCK - N_META, dh), BF16)
    for hk in range(SWA_KV_HEADS):
        kv = slice(hk * dh, (hk + 1) * dh)
        k_all = jnp.concatenate([kp_ref[:, kv], kc_ref[:, kv], km_ref[:, kv], zpad], axis=0)
        v_all = jnp.concatenate([vp_ref[:, kv], vc_ref[:, kv], vm_ref[:, kv], zpad], axis=0)
        for g in range(group):
            hq = hk * group + g
            cols = slice(hq * dh, (hq + 1) * dh)
            s = _dot_nt(q_ref[:, cols], k_all) * scale + bias_ref[0, hq]
            sink = sink_ref[hq]
            m = jnp.maximum(jnp.max(s, axis=-1, keepdims=True), sink)
            e = jnp.exp(s - m)
            den = jnp.sum(e, axis=-1, keepdims=True) + jnp.exp(sink - m)
            o = _dot(e.astype(BF16), v_all) / den
            o_ref[:, cols] = o.astype(o_ref.dtype)


def _swa(q, k, v, bias, sinks, batch, lp):
    rows = q.shape[0]
    nb = lp // SWA_BLOCK
    meta_blocks = lp // N_META
    cur = lambda b, n: (b * nb + n, 0)
    prev = lambda b, n: (b * nb + jnp.maximum(n - 1, 0), 0)
    meta = lambda b, n: (b * meta_blocks + PAD // N_META, 0)
    kv_w = k.shape[1]
    return pl.pallas_call(
        _swa_body,
        grid=(batch, nb),
        in_specs=[pl.BlockSpec(memory_space=pltpu.SMEM),
                  pl.BlockSpec((SWA_BLOCK, q.shape[1]), cur),
                  pl.BlockSpec((SWA_BLOCK, kv_w), cur), pl.BlockSpec((SWA_BLOCK, kv_w), prev),
                  pl.BlockSpec((N_META, kv_w), meta),
                  pl.BlockSpec((SWA_BLOCK, kv_w), cur), pl.BlockSpec((SWA_BLOCK, kv_w), prev),
                  pl.BlockSpec((N_META, kv_w), meta),
                  pl.BlockSpec((1, SWA_Q_HEADS, SWA_BLOCK, SWA_KEYS),
                               lambda b, n: (jnp.minimum(n, 2), 0, 0, 0))],
        out_specs=pl.BlockSpec((SWA_BLOCK, q.shape[1]), cur),
        out_shape=jax.ShapeDtypeStruct((rows, q.shape[1]), BF16),
        compiler_params=_params("parallel", "arbitrary"),
        name="swa",
    )(sinks, q, k, k, k, v, v, v, bias)


def _tri_masks():
    row = lax.broadcasted_iota(jnp.int32, (CHUNK, CHUNK), 0)
    col = lax.broadcasted_iota(jnp.int32, (CHUNK, CHUNK), 1)
    return row >= col, row > col, row == col


def _deltanet_body(qkv_ref, z_ref, gates_ref, convw_ref, alog_ref, dtb_ref, nw_ref, o_ref,
                   s_ref, xpad_ref, act_ref, *, chunks):
    nh, dh = DN_HEADS, DN_HEAD_DIM
    rows = chunks * CHUNK
    halo = 8

    @pl.when(pl.program_id(1) == 0)
    def _():
        s_ref[...] = jnp.zeros_like(s_ref)
        xpad_ref[0:halo, :] = jnp.zeros((halo, xpad_ref.shape[1]), F32)

    xpad_ref[halo:halo + rows, :] = qkv_ref[...].astype(F32)
    cw = convw_ref[...]
    y = cw[DN_CONV - 1:DN_CONV, :] * xpad_ref[halo:halo + rows, :]
    for j in range(DN_CONV - 1):
        shift = DN_CONV - 1 - j
        y = y + cw[j:j + 1, :] * xpad_ref[halo - shift:halo - shift + rows, :]
    tail = xpad_ref[rows:rows + halo, :]
    act_ref[...] = _silu(y)
    xpad_ref[0:halo, :] = tail

    gt = gates_ref[...]
    beta_full = jax.nn.sigmoid(gt)
    g_full = -jnp.exp(alog_ref[...]) * _softplus(gt + dtb_ref[...])
    lane_in = lax.broadcasted_iota(jnp.int32, (LANE, nh * dh), 0)
    head_of_col = lax.broadcasted_iota(jnp.int32, (LANE, nh * dh), 1) // dh
    sel_beta = (lane_in == head_of_col).astype(F32)
    sel_g = (lane_in == head_of_col + nh).astype(F32)
    beta_all = _dot(beta_full, sel_beta, HIGHEST)
    g_all = _dot(g_full, sel_g, HIGHEST)

    incl, strict, diag = _tri_masks()
    tri = incl.astype(F32)
    eye = diag.astype(F32)
    first_lane = (lax.broadcasted_iota(jnp.int32, (CHUNK, dh), 1) == 0).astype(F32)
    nw = nw_ref[...]

    for c in range(chunks):
        r = slice(c * CHUNK, (c + 1) * CHUNK)
        gc_all = _dot(tri, g_all[r, :], HIGHEST)
        for h in range(nh):
            hc = slice(h * dh, (h + 1) * dh)
            q = act_ref[r, h * dh:(h + 1) * dh]
            k = act_ref[r, DN_DIM + h * dh:DN_DIM + (h + 1) * dh]
            v = act_ref[r, 2 * DN_DIM + h * dh:2 * DN_DIM + (h + 1) * dh]
            q = q * lax.rsqrt(jnp.sum(q * q, axis=-1, keepdims=True) + 1e-6) * dh ** -0.5
            k = k * lax.rsqrt(jnp.sum(k * k, axis=-1, keepdims=True) + 1e-6)
            beta = beta_all[r, hc]
            gc = gc_all[:, hc]
            gc_col = gc[:, :CHUNK]
            gc_row = _dot_nt(first_lane, gc, HIGHEST)
            gamma = jnp.where(incl, jnp.exp(jnp.where(incl, gc_col - gc_row, 0.0)), 0.0)
            k_beta = k * beta
            v_beta = v * beta
            kb16 = k_beta.astype(BF16)
            k16 = k.astype(BF16)
            a = jnp.where(strict, _dot_nt(kb16, k16) * gamma, 0.0)
            p = -a
            t = eye + p
            for _ in range(int(math.log2(CHUNK)) - 1):
                p16 = p.astype(BF16)
                p = _dot(p16, p16)
                t = t + _dot(t.astype(BF16), p.astype(BF16))
            t16 = t.astype(BF16)
            u = _dot(t16, v_beta.astype(BF16))
            w = _dot(t16, (k_beta * jnp.exp(gc)).astype(BF16))
            attn = jnp.where(incl, _dot_nt(q.astype(BF16), k16) * gamma, 0.0)
            gc_last = gc[CHUNK - 1:CHUNK, :]
            q_dec = q * jnp.exp(gc)
            k_dec = k * jnp.exp(gc_last - gc)
            s = s_ref[h]
            s16 = s.astype(BF16)
            v_new = u - _dot(w.astype(BF16), s16)
            vn16 = v_new.astype(BF16)
            o = _dot(q_dec.astype(BF16), s16) + _dot(attn.astype(BF16), vn16)
            s_ref[h] = s * jnp.exp(gc_last) + _dot_tn(k_dec.astype(BF16), vn16)
            o = _rms(o, nw) * _silu(z_ref[r, hc].astype(F32))
            o_ref[r, hc] = o.astype(o_ref.dtype)


def _deltanet(qkv, z, gates, conv_w, alog_vec, dtb_vec, nw, batch, lp, chunks):
    rows = qkv.shape[0]
    rs = chunks * CHUNK
    steps = lp // rs
    idx = lambda b, s: (b * steps + s, 0)
    return pl.pallas_call(
        functools.partial(_deltanet_body, chunks=chunks),
        grid=(batch, steps),
        in_specs=[pl.BlockSpec((rs, qkv.shape[1]), idx), pl.BlockSpec((rs, z.shape[1]), idx),
                  pl.BlockSpec((rs, gates.shape[1]), idx), _resident(conv_w.shape),
                  _resident(alog_vec.shape), _resident(dtb_vec.shape), _resident(nw.shape)],
        out_specs=pl.BlockSpec((rs, DN_DIM), idx),
        out_shape=jax.ShapeDtypeStruct((rows, DN_DIM), BF16),
        scratch_shapes=[pltpu.VMEM((DN_HEADS, DN_HEAD_DIM, DN_HEAD_DIM), F32),
                        pltpu.VMEM((rs + 8, qkv.shape[1]), F32),
                        pltpu.VMEM((rs, qkv.shape[1]), F32)],
        compiler_params=_params("parallel", "arbitrary"),
        name="deltanet",
    )(qkv, z, gates, conv_w, alog_vec, dtb_vec, nw)


def _gla_body(q_ref, k_ref, v_ref, g_ref, gk_ref, wgu_ref, bg_ref, nw_ref, o_ref, s_ref, *,
              chunks):
    nh = GLA_HEADS
    dk = q_ref.shape[1] // nh
    dv = v_ref.shape[1] // nh

    @pl.when(pl.program_id(1) == 0)
    def _():
        s_ref[...] = jnp.zeros_like(s_ref)

    logits = _dot(gk_ref[...].astype(BF16), wgu_ref[...]) + bg_ref[...]
    glog = (jnp.minimum(logits, 0.0) - jnp.log(1.0 + jnp.exp(-jnp.abs(logits)))) / GLA_GATE_NORM
    incl, _, _ = _tri_masks()
    tri = incl.astype(F32)
    nw = nw_ref[...]
    for c in range(chunks):
        r = slice(c * CHUNK, (c + 1) * CHUNK)
        bcum_all = _dot(tri, glog[r, :], HIGHEST)
        for h in range(nh):
            kc = slice(h * dk, (h + 1) * dk)
            vc = slice(h * dv, (h + 1) * dv)
            bc = bcum_all[:, kc]
            q = q_ref[r, kc].astype(F32) * dk ** -0.5
            k = k_ref[r, kc].astype(F32)
            v16 = v_ref[r, vc]
            q_dec = (q * jnp.exp(bc)).astype(BF16)
            attn = jnp.where(incl, _dot_nt(q_dec, (k * jnp.exp(-bc)).astype(BF16)), 0.0)
            b_last = bc[CHUNK - 1:CHUNK, :]
            k_dec = (k * jnp.exp(b_last - bc)).astype(BF16)
            st = s_ref[h]
            o = _dot(attn.astype(BF16), v16) + _dot_nt(q_dec, st.astype(BF16))
            s_ref[h] = st * jnp.exp(b_last) + _dot_tn(v16, k_dec)
            o = _rms(o, nw) * _silu(g_ref[r, vc].astype(F32))
            o_ref[r, vc] = o.astype(o_ref.dtype)


def _gla(q, k, v, g, gk, wgu, bg, nw, batch, lp, chunks):
    rows = q.shape[0]
    rs = chunks * CHUNK
    steps = lp // rs
    idx = lambda b, s: (b * steps + s, 0)
    dk = q.shape[1] // GLA_HEADS
    dv = v.shape[1] // GLA_HEADS
    return pl.pallas_call(
        functools.partial(_gla_body, chunks=chunks),
        grid=(batch, steps),
        in_specs=[pl.BlockSpec((rs, q.shape[1]), idx), pl.BlockSpec((rs, k.shape[1]), idx),
                  pl.BlockSpec((rs, v.shape[1]), idx), pl.BlockSpec((rs, g.shape[1]), idx),
                  pl.BlockSpec((rs, gk.shape[1]), idx), _resident(wgu.shape),
                  _resident(bg.shape), _resident(nw.shape)],
        out_specs=pl.BlockSpec((rs, v.shape[1]), idx),
        out_shape=jax.ShapeDtypeStruct((rows, v.shape[1]), BF16),
        scratch_shapes=[pltpu.VMEM((GLA_HEADS, dv, dk), F32)],
        compiler_params=_params("parallel", "arbitrary"),
        name="gla",
    )(q, k, v, g, gk, wgu, bg, nw)


def _pad_cols(w, width):
    return jnp.pad(w, ((0, 0), (0, width - w.shape[1])))


def _chunks_per_step(lp):
    n = lp // CHUNK
    for c in (3, 2):
        if n % c == 0:
            return c
    return 1


def kernel(x, meta_tokens, norm_w, ffn_w_gate, ffn_w_up, ffn_w_down, rel_bias_table,
           even_w_in, even_conv_w, swa_sinks, dn_a_log, dn_dt_bias, dn_norm_w, even_w_out,
           odd_w_in, gla_w_gate_up, gla_b_gate, gla_norm_w, odd_w_out):
    batch, seq, d = x.shape
    depth = norm_w.shape[0]
    lp = SWA_BLOCK + seq
    assert seq % SWA_BLOCK == 0 and (batch * lp) % ROW_TILE == 0
    chunks = _chunks_per_step(lp)

    meta = jnp.broadcast_to(meta_tokens[None].astype(x.dtype), (batch, N_META, d))
    h = jnp.concatenate([jnp.zeros((batch, PAD, d), x.dtype), meta, x], axis=1)
    h = h.reshape(batch * lp, d)

    swa_q = SWA_Q_HEADS * SWA_HEAD_DIM
    swa_kv = SWA_KV_HEADS * SWA_HEAD_DIM
    bias = None
    for layer in range(depth):
        nw = norm_w[layer][:, None, :]
        wg = ffn_w_gate[layer].astype(BF16)
        wu = ffn_w_up[layer].astype(BF16)
        wd = ffn_w_down[layer].astype(BF16)
        h = _ffn(h, nw[0], nw[1], wg[0], wu[0], wd[0])
        i = layer // 2
        if layer % 2 == 0:
            gate_cols = even_w_in.shape[2] - (swa_q + 2 * swa_kv + 4 * DN_DIM)
            widths = (swa_q, swa_kv, swa_kv, 3 * DN_DIM, DN_DIM, LANE)
            w_in = _pad_cols(even_w_in[i], sum(widths)).astype(BF16)
            qa, ka, va, qkv_b, z_b, gates = _norm_proj(
                h, nw[2], w_in, widths, (BF16, BF16, BF16, BF16, BF16, F32))
            if bias is None:
                bias = _swa_bias(rel_bias_table)
            o_a = _swa(qa, ka, va, bias, swa_sinks[i], batch, lp)
            assert gate_cols == 2 * DN_HEADS
            lane_pad = (DN_HEADS, LANE - 2 * DN_HEADS)
            alog_vec = jnp.pad(dn_a_log[i], lane_pad)[None, :]
            dtb_vec = jnp.pad(dn_dt_bias[i], lane_pad)[None, :]
            o_b = _deltanet(qkv_b, z_b, gates, even_conv_w[i], alog_vec, dtb_vec,
                            dn_norm_w[i][None, :], batch, lp, chunks)
            h = _out_proj([o_a, o_b], even_w_out[i].astype(BF16), h, nw[3])
        else:
            key_dim = gla_w_gate_up.shape[2]
            val_dim = odd_w_out.shape[1]
            widths = (key_dim, key_dim, val_dim, val_dim, LANE)
            w_in = _pad_cols(odd_w_in[i], sum(widths)).astype(BF16)
            q, k, v, g, gk = _norm_proj(h, nw[2], w_in, widths, (BF16, BF16, BF16, BF16, F32))
            wgu = jnp.pad(gla_w_gate_up[i], ((0, LANE - GLA_GATE_RANK), (0, 0))).astype(BF16)
            o = _gla(q, k, v, g, gk, wgu, gla_b_gate[i][None, :], gla_norm_w[i][None, :],
                     batch, lp, chunks)
            h = _out_proj([o], odd_w_out[i].astype(BF16), h, nw[3])
        h = _ffn(h, nw[4], nw[5], wg[1], wu[1], wd[1])
    return h.reshape(batch, lp, d)[:, PAD + N_META:]
```

```python
import functools
import math

import numpy as np
import jax
import jax.numpy as jnp
from jax import lax
from jax.experimental import pallas as pl
from jax.experimental.pallas import tpu as pltpu

F32 = jnp.float32
BF16 = jnp.bfloat16
HIGHEST = lax.Precision.HIGHEST

N_META = 16
NORM_EPS = 1e-6
NEG_INF = -1e30
SWA_Q_HEADS = 8
SWA_KV_HEADS = 2
SWA_HEAD_DIM = 64
SWA_BLOCK = 128
REL_BUCKETS = 32
REL_MAX_DIST = 128
DN_HEADS = 4
DN_HEAD_DIM = 128
DN_DIM = DN_HEADS * DN_HEAD_DIM
DN_CONV = 4
GLA_HEADS = 4
GLA_GATE_RANK = 16
GLA_GATE_NORM = 16.0
CHUNK = 64

PAD = SWA_BLOCK - N_META
LANE = 128
ROW_TILE = 512
FF_TILE = 256
VMEM_LIMIT = 56 * 1024 * 1024


def _rms(x, w):
    return x * lax.rsqrt(jnp.mean(x * x, axis=-1, keepdims=True) + NORM_EPS) * w


def _silu(x):
    return x * jax.nn.sigmoid(x)


def _softplus(x):
    return jnp.maximum(x, 0.0) + jnp.log(1.0 + jnp.exp(-jnp.abs(x)))


def _dot(a, b, precision=None):
    return jnp.dot(a, b, preferred_element_type=F32, precision=precision)


def _dot_nt(a, b, precision=None):
    return lax.dot_general(a, b, (((1,), (1,)), ((), ())), preferred_element_type=F32,
                           precision=precision)


def _dot_tn(a, b, precision=None):
    return lax.dot_general(a, b, (((0,), (0,)), ((), ())), preferred_element_type=F32,
                           precision=precision)


def _resident(shape):
    nd = len(shape)
    return pl.BlockSpec(shape, lambda *_: (0,) * nd, pipeline_mode=pl.Buffered(1))


def _params(*semantics):
    return pltpu.CompilerParams(dimension_semantics=semantics, vmem_limit_bytes=VMEM_LIMIT)


def _ffn_body(h_ref, nwa_ref, nwb_ref, wg_ref, wu_ref, wd_ref, o_ref, a_ref):
    x = h_ref[...]
    hn = _rms(x, nwa_ref[...]).astype(BF16)
    d_ff = wg_ref.shape[1]
    for c in range(d_ff // FF_TILE):
        cols = slice(c * FF_TILE, (c + 1) * FF_TILE)
        g = _dot(hn, wg_ref[:, cols])
        u = _dot(hn, wu_ref[:, cols])
        a_ref[:, cols] = (_silu(g) * u).astype(BF16)
    f = _dot(a_ref[...], wd_ref[...])
    o_ref[...] = x + 0.5 * _rms(f, nwb_ref[...])


def _ffn(h, nwa, nwb, wg, wu, wd):
    rows, d = h.shape
    d_ff = wg.shape[1]
    row_spec = pl.BlockSpec((ROW_TILE, d), lambda i: (i, 0))
    return pl.pallas_call(
        _ffn_body,
        grid=(rows // ROW_TILE,),
        in_specs=[row_spec, _resident((1, d)), _resident((1, d)),
                  _resident((d, d_ff)), _resident((d, d_ff)), _resident((d_ff, d))],
        out_specs=row_spec,
        out_shape=jax.ShapeDtypeStruct((rows, d), F32),
        scratch_shapes=[pltpu.VMEM((ROW_TILE, d_ff), BF16)],
        compiler_params=_params("parallel"),
        name="ffn",
    )(h, nwa, nwb, wg, wu, wd)


def _proj_body(h_ref, nw_ref, w_ref, *out_refs):
    hn = _rms(h_ref[...], nw_ref[...]).astype(BF16)
    off = 0
    for o_ref in out_refs:
        n = o_ref.shape[1]
        o_ref[...] = _dot(hn, w_ref[:, off:off + n]).astype(o_ref.dtype)
        off += n


def _norm_proj(h, nw, w, widths, dtypes):
    rows, d = h.shape
    assert sum(widths) == w.shape[1]
    return pl.pallas_call(
        _proj_body,
        grid=(rows // ROW_TILE,),
        in_specs=[pl.BlockSpec((ROW_TILE, d), lambda i: (i, 0)), _resident((1, d)),
                  _resident(w.shape)],
        out_specs=[pl.BlockSpec((ROW_TILE, n), lambda i: (i, 0)) for n in widths],
        out_shape=[jax.ShapeDtypeStruct((rows, n), dt) for n, dt in zip(widths, dtypes)],
        compiler_params=_params("parallel"),
        name="norm_proj",
    )(h, nw, w)


def _outproj_body(*refs):
    *x_refs, w_ref, h_ref, nw_ref, o_ref = refs
    acc = None
    off = 0
    for x_ref in x_refs:
        k = x_ref.shape[1]
        p = _dot(x_ref[...], w_ref[off:off + k, :])
        acc = p if acc is None else acc + p
        off += k
    o_ref[...] = h_ref[...] + _rms(acc, nw_ref[...])


def _out_proj(xs, w, h, nw):
    rows, d = h.shape
    return pl.pallas_call(
        _outproj_body,
        grid=(rows // ROW_TILE,),
        in_specs=[pl.BlockSpec((ROW_TILE, x.shape[1]), lambda i: (i, 0)) for x in xs]
        + [_resident(w.shape), pl.BlockSpec((ROW_TILE, d), lambda i: (i, 0)), _resident((1, d))],
        out_specs=pl.BlockSpec((ROW_TILE, d), lambda i: (i, 0)),
        out_shape=jax.ShapeDtypeStruct((rows, d), F32),
        compiler_params=_params("parallel"),
        name="out_proj",
    )(*xs, w, h, nw)


def _t5_bucket_np(rel):
    n = np.maximum(rel, 0)
    max_exact = REL_BUCKETS // 2
    n_f = np.maximum(n, 1).astype(np.float32)
    large = max_exact + (np.log(n_f / np.float32(max_exact))
                         / np.float32(math.log(REL_MAX_DIST / max_exact))
                         * np.float32(REL_BUCKETS - max_exact)).astype(np.int32)
    large = np.minimum(large, REL_BUCKETS - 1)
    return np.where(n < max_exact, n, large).astype(np.int32)


def _swa_bucket_index():
    i = np.arange(SWA_BLOCK)[:, None]
    c = np.arange(SWA_BLOCK)[None, :]
    out = np.full((3, SWA_BLOCK, 2 * SWA_BLOCK), -1, np.int32)
    for variant in range(3):
        pos_q = variant * SWA_BLOCK + i - PAD
        pos_k = variant * SWA_BLOCK + c - PAD - np.where(c > i, SWA_BLOCK, 0)
        rel_b = pos_q - pos_k
        assert ((rel_b >= 0) & (rel_b < SWA_BLOCK)).all()
        out[variant, :, :SWA_BLOCK] = np.where(pos_k >= N_META, _t5_bucket_np(rel_b), -1)
        rel_m = pos_q - c
        meta = (c < N_META) & (rel_m >= 0)
        out[variant, :, SWA_BLOCK:] = np.where(meta, _t5_bucket_np(rel_m), -1)
    return out


_SWA_BUCKET_INDEX = _swa_bucket_index()


def _bias_body(tbl_ref, idx_ref, o_ref):
    head = pl.program_id(1)
    idx = idx_ref[0]
    acc = jnp.full(idx.shape, NEG_INF, F32)
    for b in range(REL_BUCKETS):
        acc = jnp.where(idx == b, tbl_ref[b, head], acc)
    o_ref[0, 0] = acc


def _swa_bias(rel_table):
    idx = jnp.asarray(_SWA_BUCKET_INDEX)
    group = SWA_Q_HEADS // SWA_KV_HEADS
    width = idx.shape[2]
    return pl.pallas_call(
        _bias_body,
        grid=(3, SWA_Q_HEADS),
        in_specs=[pl.BlockSpec(memory_space=pltpu.SMEM),
                  pl.BlockSpec((1, SWA_BLOCK, width), lambda v, h: (v, 0, 0))],
        out_specs=pl.BlockSpec((1, 1, SWA_BLOCK, width),
                               lambda v, h: (v, h // group, h % group, 0)),
        out_shape=jax.ShapeDtypeStruct((3, SWA_KV_HEADS, group * SWA_BLOCK, width), F32),
        compiler_params=_params("arbitrary", "arbitrary"),
        name="swa_bias",
    )(rel_table, idx)


def _swa_body(sink_ref, q_ref, kc_ref, kp_ref, km_ref, vc_ref, vp_ref, vm_ref, bias_ref, o_ref,
              *, blocks):
    dh = SWA_HEAD_DIM
    blk = SWA_BLOCK
    group = SWA_Q_HEADS // SWA_KV_HEADS
    scale = dh ** -0.5
    step = pl.program_id(1)
    zpad = jnp.zeros((blk - N_META, dh), BF16)
    row = lax.broadcasted_iota(jnp.int32, (group * blk, blk), 0) % blk
    col = lax.broadcasted_iota(jnp.int32, (group * blk, blk), 1)
    from_prev = col > row

    units = [(j, hk) for j in range(blocks) for hk in range(SWA_KV_HEADS)]
    scores, values = [], []
    for j, hk in units:
        r = slice(j * blk, (j + 1) * blk)
        kv = slice(hk * dh, (hk + 1) * dh)
        k_prev = kp_ref[:, kv] if j == 0 else kc_ref[(j - 1) * blk:j * blk, kv]
        v_prev = vp_ref[:, kv] if j == 0 else vc_ref[(j - 1) * blk:j * blk, kv]
        k_all = jnp.concatenate([k_prev, kc_ref[r, kv], km_ref[:, kv], zpad], axis=0)
        values.append(jnp.concatenate([v_prev, vc_ref[r, kv], vm_ref[:, kv], zpad], axis=0))
        q = jnp.concatenate([q_ref[r, (hk * group + g) * dh:(hk * group + g + 1) * dh]
                             for g in range(group)], axis=0) * scale
        scores.append(_dot_nt(q, k_all))

    probs, dens = [], []
    for (j, hk), s in zip(units, scores):
        bias = bias_ref[jnp.minimum(step * blocks + j, 2), hk]
        s_band = jnp.where(from_prev, s[:, :blk], s[:, blk:2 * blk]) + bias[:, :blk]
        s_meta = s[:, 2 * blk:] + bias[:, blk:]
        sink = jnp.concatenate([jnp.full((blk, 1), sink_ref[hk * group + g], F32)
                                for g in range(group)], axis=0)
        m = jnp.maximum(jnp.maximum(jnp.max(s_band, axis=-1, keepdims=True),
                                    jnp.max(s_meta, axis=-1, keepdims=True)), sink)
        e_band = jnp.exp(s_band - m)
        e_meta = jnp.exp(s_meta - m)
        dens.append(jnp.sum(e_band, axis=-1, keepdims=True)
                    + jnp.sum(e_meta, axis=-1, keepdims=True) + jnp.exp(sink - m))
        probs.append(jnp.concatenate([jnp.where(from_prev, e_band, 0.0),
                                      jnp.where(from_prev, 0.0, e_band), e_meta],
                                     axis=1).astype(BF16))

    for (j, hk), p, v_all, den in zip(units, probs, values, dens):
        o = _dot(p, v_all) / den
        for g in range(group):
            hq = hk * group + g
            o_ref[j * blk:(j + 1) * blk, hq * dh:(hq + 1) * dh] = (
                o[g * blk:(g + 1) * blk].astype(o_ref.dtype))


def _swa(q, k, v, bias, sinks, batch, lp, blocks):
    rows = q.shape[0]
    nb = lp // SWA_BLOCK
    steps = nb // blocks
    meta_blocks = lp // N_META
    cur = lambda b, s: (b * steps + s, 0)
    prev = lambda b, s: (b * nb + jnp.maximum(s * blocks - 1, 0), 0)
    meta = lambda b, s: (b * meta_blocks + PAD // N_META, 0)
    kv_w = k.shape[1]
    rs = blocks * SWA_BLOCK
    return pl.pallas_call(
        functools.partial(_swa_body, blocks=blocks),
        grid=(batch, steps),
        in_specs=[pl.BlockSpec(memory_space=pltpu.SMEM),
                  pl.BlockSpec((rs, q.shape[1]), cur),
                  pl.BlockSpec((rs, kv_w), cur), pl.BlockSpec((SWA_BLOCK, kv_w), prev),
                  pl.BlockSpec((N_META, kv_w), meta),
                  pl.BlockSpec((rs, kv_w), cur), pl.BlockSpec((SWA_BLOCK, kv_w), prev),
                  pl.BlockSpec((N_META, kv_w), meta),
                  _resident(bias.shape)],
        out_specs=pl.BlockSpec((rs, q.shape[1]), cur),
        out_shape=jax.ShapeDtypeStruct((rows, q.shape[1]), BF16),
        compiler_params=_params("parallel", "arbitrary"),
        name="swa",
    )(sinks, q, k, k, k, v, v, v, bias)


def _tri_masks():
    row = lax.broadcasted_iota(jnp.int32, (CHUNK, CHUNK), 0)
    col = lax.broadcasted_iota(jnp.int32, (CHUNK, CHUNK), 1)
    return row >= col, row > col, row == col


def _deltanet_body(qkv_ref, z_ref, gates_ref, convw_ref, alog_ref, dtb_ref, nw_ref, o_ref,
                   s_ref, xpad_ref, act_ref, *, chunks):
    nh, dh = DN_HEADS, DN_HEAD_DIM
    rows = chunks * CHUNK
    halo = 8

    @pl.when(pl.program_id(1) == 0)
    def _():
        s_ref[...] = jnp.zeros_like(s_ref)
        xpad_ref[0:halo, :] = jnp.zeros((halo, xpad_ref.shape[1]), F32)

    xpad_ref[halo:halo + rows, :] = qkv_ref[...].astype(F32)
    cw = convw_ref[...]
    y = cw[DN_CONV - 1:DN_CONV, :] * xpad_ref[halo:halo + rows, :]
    for j in range(DN_CONV - 1):
        shift = DN_CONV - 1 - j
        y = y + cw[j:j + 1, :] * xpad_ref[halo - shift:halo - shift + rows, :]
    tail = xpad_ref[rows:rows + halo, :]
    act_ref[...] = _silu(y)
    xpad_ref[0:halo, :] = tail

    gt = gates_ref[...]
    beta_full = jax.nn.sigmoid(gt)
    g_full = -jnp.exp(alog_ref[...]) * _softplus(gt + dtb_ref[...])
    ri = lax.broadcasted_iota(jnp.int32, (rows, rows), 0)
    ci = lax.broadcasted_iota(jnp.int32, (rows, rows), 1)
    tri_blocks = ((ri >= ci) & (ri // CHUNK == ci // CHUNK)).astype(F32)
    gc_full = _dot(tri_blocks, g_full, HIGHEST)

    incl, strict, diag = _tri_masks()
    eye = diag.astype(F32)
    items = [(c, h) for c in range(chunks) for h in range(nh)]

    neg_a, attn16, qd16, kdt16, rhs16, decay = [], [], [], [], [], []
    for c, h in items:
        r = slice(c * CHUNK, (c + 1) * CHUNK)
        q = act_ref[r, h * dh:(h + 1) * dh]
        k = act_ref[r, DN_DIM + h * dh:DN_DIM + (h + 1) * dh]
        v = act_ref[r, 2 * DN_DIM + h * dh:2 * DN_DIM + (h + 1) * dh]
        q = q * lax.rsqrt(jnp.sum(q * q, axis=-1, keepdims=True) + 1e-6) * dh ** -0.5
        k = k * lax.rsqrt(jnp.sum(k * k, axis=-1, keepdims=True) + 1e-6)
        beta = jnp.broadcast_to(beta_full[r, h:h + 1], (CHUNK, dh))
        gc = jnp.broadcast_to(gc_full[r, nh + h:nh + h + 1], (CHUNK, dh))
        gc_row = jnp.transpose(gc)[:CHUNK, :]
        gamma = jnp.where(incl, jnp.exp(jnp.where(incl, gc[:, :CHUNK] - gc_row, 0.0)), 0.0)
        k_beta = k * beta
        k16 = k.astype(BF16)
        neg_a.append(jnp.where(strict, -(_dot_nt(k_beta.astype(BF16), k16) * gamma), 0.0))
        attn16.append(jnp.where(incl, _dot_nt(q.astype(BF16), k16) * gamma, 0.0).astype(BF16))
        gc_last = gc[CHUNK - 1:CHUNK, :]
        qd16.append((q * jnp.exp(gc)).astype(BF16))
        kdt16.append(jnp.transpose(k * jnp.exp(gc_last - gc)).astype(BF16))
        rhs16.append(jnp.concatenate([v * beta, k_beta * jnp.exp(gc)], axis=1).astype(BF16))
        decay.append(jnp.exp(gc_last))

    p = neg_a
    t = [eye + x for x in p]
    for _ in range(int(math.log2(CHUNK)) - 1):
        p16 = [x.astype(BF16) for x in p]
        p = [_dot(x, x) for x in p16]
        t = [y + _dot(y.astype(BF16), x.astype(BF16)) for y, x in zip(t, p)]
    uw = [_dot(y.astype(BF16), b) for y, b in zip(t, rhs16)]

    nw = nw_ref[...]
    state = [s_ref[h] for h in range(nh)]
    for c in range(chunks):
        r = slice(c * CHUNK, (c + 1) * CHUNK)
        pair = [c * nh + h for h in range(nh)]
        s16 = [s.astype(BF16) for s in state]
        ws = [_dot(uw[i][:, dh:].astype(BF16), s16[h]) for h, i in enumerate(pair)]
        qs = [_dot(qd16[i], s16[h]) for h, i in enumerate(pair)]
        vn16 = [(uw[i][:, :dh] - ws[h]).astype(BF16) for h, i in enumerate(pair)]
        out = [qs[h] + _dot(attn16[i], vn16[h]) for h, i in enumerate(pair)]
        state = [state[h] * decay[i] + _dot(kdt16[i], vn16[h]) for h, i in enumerate(pair)]
        for h in range(nh):
            hc = slice(h * dh, (h + 1) * dh)
            o = _rms(out[h], nw) * _silu(z_ref[r, hc].astype(F32))
            o_ref[r, hc] = o.astype(o_ref.dtype)
    for h in range(nh):
        s_ref[h] = state[h]


def _deltanet(qkv, z, gates, conv_w, alog_vec, dtb_vec, nw, batch, lp, chunks):
    rows = qkv.shape[0]
    rs = chunks * CHUNK
    steps = lp // rs
    idx = lambda b, s: (b * steps + s, 0)
    return pl.pallas_call(
        functools.partial(_deltanet_body, chunks=chunks),
        grid=(batch, steps),
        in_specs=[pl.BlockSpec((rs, qkv.shape[1]), idx), pl.BlockSpec((rs, z.shape[1]), idx),
                  pl.BlockSpec((rs, gates.shape[1]), idx), _resident(conv_w.shape),
                  _resident(alog_vec.shape), _resident(dtb_vec.shape), _resident(nw.shape)],
        out_specs=pl.BlockSpec((rs, DN_DIM), idx),
        out_shape=jax.ShapeDtypeStruct((rows, DN_DIM), BF16),
        scratch_shapes=[pltpu.VMEM((DN_HEADS, DN_HEAD_DIM, DN_HEAD_DIM), F32),
                        pltpu.VMEM((rs + 8, qkv.shape[1]), F32),
                        pltpu.VMEM((rs, qkv.shape[1]), F32)],
        compiler_params=_params("parallel", "arbitrary"),
        name="deltanet",
    )(qkv, z, gates, conv_w, alog_vec, dtb_vec, nw)


def _gla_body(q_ref, k_ref, v_ref, g_ref, gk_ref, wgu_ref, bg_ref, nw_ref, o_ref, s_ref, *,
              chunks):
    nh = GLA_HEADS
    dk = q_ref.shape[1] // nh
    dv = v_ref.shape[1] // nh

    @pl.when(pl.program_id(1) == 0)
    def _():
        s_ref[...] = jnp.zeros_like(s_ref)

    logits = _dot(gk_ref[...].astype(BF16), wgu_ref[...]) + bg_ref[...]
    glog = (jnp.minimum(logits, 0.0) - jnp.log(1.0 + jnp.exp(-jnp.abs(logits)))) / GLA_GATE_NORM
    incl, _, _ = _tri_masks()
    tri = incl.astype(F32)
    nw = nw_ref[...]
    for c in range(chunks):
        r = slice(c * CHUNK, (c + 1) * CHUNK)
        bcum_all = _dot(tri, glog[r, :], HIGHEST)
        for h in range(nh):
            kc = slice(h * dk, (h + 1) * dk)
            vc = slice(h * dv, (h + 1) * dv)
            bc = bcum_all[:, kc]
            q = q_ref[r, kc].astype(F32) * dk ** -0.5
            k = k_ref[r, kc].astype(F32)
            v16 = v_ref[r, vc]
            q_dec = (q * jnp.exp(bc)).astype(BF16)
            attn = jnp.where(incl, _dot_nt(q_dec, (k * jnp.exp(-bc)).astype(BF16)), 0.0)
            b_last = bc[CHUNK - 1:CHUNK, :]
            k_dec = (k * jnp.exp(b_last - bc)).astype(BF16)
            st = s_ref[h]
            o = _dot(attn.astype(BF16), v16) + _dot_nt(q_dec, st.astype(BF16))
            s_ref[h] = st * jnp.exp(b_last) + _dot_tn(v16, k_dec)
            o = _rms(o, nw) * _silu(g_ref[r, vc].astype(F32))
            o_ref[r, vc] = o.astype(o_ref.dtype)


def _gla(q, k, v, g, gk, wgu, bg, nw, batch, lp, chunks):
    rows = q.shape[0]
    rs = chunks * CHUNK
    steps = lp // rs
    idx = lambda b, s: (b * steps + s, 0)
    dk = q.shape[1] // GLA_HEADS
    dv = v.shape[1] // GLA_HEADS
    return pl.pallas_call(
        functools.partial(_gla_body, chunks=chunks),
        grid=(batch, steps),
        in_specs=[pl.BlockSpec((rs, q.shape[1]), idx), pl.BlockSpec((rs, k.shape[1]), idx),
                  pl.BlockSpec((rs, v.shape[1]), idx), pl.BlockSpec((rs, g.shape[1]), idx),
                  pl.BlockSpec((rs, gk.shape[1]), idx), _resident(wgu.shape),
                  _resident(bg.shape), _resident(nw.shape)],
        out_specs=pl.BlockSpec((rs, v.shape[1]), idx),
        out_shape=jax.ShapeDtypeStruct((rows, v.shape[1]), BF16),
        scratch_shapes=[pltpu.VMEM((GLA_HEADS, dv, dk), F32)],
        compiler_params=_params("parallel", "arbitrary"),
        name="gla",
    )(q, k, v, g, gk, wgu, bg, nw)


def _pad_cols(w, width):
    return jnp.pad(w, ((0, 0), (0, width - w.shape[1])))


def _per_step(n, preferred):
    for c in preferred:
        if n % c == 0:
            return c
    return 1


def kernel(x, meta_tokens, norm_w, ffn_w_gate, ffn_w_up, ffn_w_down, rel_bias_table,
           even_w_in, even_conv_w, swa_sinks, dn_a_log, dn_dt_bias, dn_norm_w, even_w_out,
           odd_w_in, gla_w_gate_up, gla_b_gate, gla_norm_w, odd_w_out):
    batch, seq, d = x.shape
    depth = norm_w.shape[0]
    lp = SWA_BLOCK + seq
    assert seq % SWA_BLOCK == 0 and (batch * lp) % ROW_TILE == 0
    chunks = _per_step(lp // CHUNK, (3, 2))
    swa_blocks = _per_step(lp // SWA_BLOCK, (3, 2))

    meta = jnp.broadcast_to(meta_tokens[None].astype(x.dtype), (batch, N_META, d))
    h = jnp.concatenate([jnp.zeros((batch, PAD, d), x.dtype), meta, x], axis=1)
    h = h.reshape(batch * lp, d)

    swa_q = SWA_Q_HEADS * SWA_HEAD_DIM
    swa_kv = SWA_KV_HEADS * SWA_HEAD_DIM
    bias = None
    for layer in range(depth):
        nw = norm_w[layer][:, None, :]
        wg = ffn_w_gate[layer].astype(BF16)
        wu = ffn_w_up[layer].astype(BF16)
        wd = ffn_w_down[layer].astype(BF16)
        h = _ffn(h, nw[0], nw[1], wg[0], wu[0], wd[0])
        i = layer // 2
        if layer % 2 == 0:
            gate_cols = even_w_in.shape[2] - (swa_q + 2 * swa_kv + 4 * DN_DIM)
            widths = (swa_q, swa_kv, swa_kv, 3 * DN_DIM, DN_DIM, LANE)
            w_in = _pad_cols(even_w_in[i], sum(widths)).astype(BF16)
            qa, ka, va, qkv_b, z_b, gates = _norm_proj(
                h, nw[2], w_in, widths, (BF16, BF16, BF16, BF16, BF16, F32))
            if bias is None:
                bias = _swa_bias(rel_bias_table)
            o_a = _swa(qa, ka, va, bias, swa_sinks[i], batch, lp, swa_blocks)
            assert gate_cols == 2 * DN_HEADS
            lane_pad = (DN_HEADS, LANE - 2 * DN_HEADS)
            alog_vec = jnp.pad(dn_a_log[i], lane_pad)[None, :]
            dtb_vec = jnp.pad(dn_dt_bias[i], lane_pad)[None, :]
            o_b = _deltanet(qkv_b, z_b, gates, even_conv_w[i], alog_vec, dtb_vec,
                            dn_norm_w[i][None, :], batch, lp, chunks)
            h = _out_proj([o_a, o_b], even_w_out[i].astype(BF16), h, nw[3])
        else:
            key_dim = gla_w_gate_up.shape[2]
            val_dim = odd_w_out.shape[1]
            widths = (key_dim, key_dim, val_dim, val_dim, LANE)
            w_in = _pad_cols(odd_w_in[i], sum(widths)).astype(BF16)
            q, k, v, g, gk = _norm_proj(h, nw[2], w_in, widths, (BF16, BF16, BF16, BF16, F32))
            wgu = jnp.pad(gla_w_gate_up[i], ((0, LANE - GLA_GATE_RANK), (0, 0))).astype(BF16)
            o = _gla(q, k, v, g, gk, wgu, gla_b_gate[i][None, :], gla_norm_w[i][None, :],
                     batch, lp, chunks)
            h = _out_proj([o], odd_w_out[i].astype(BF16), h, nw[3])
        h = _ffn(h, nw[4], nw[5], wg[1], wu[1], wd[1])
    return h.reshape(batch, lp, d)[:, PAD + N_META:]
```

```python
import functools
import math

import numpy as np
import jax
import jax.numpy as jnp
from jax import lax
from jax.experimental import pallas as pl
from jax.experimental.pallas import tpu as pltpu

F32 = jnp.float32
BF16 = jnp.bfloat16
HIGHEST = lax.Precision.HIGHEST

N_META = 16
NORM_EPS = 1e-6
NEG_INF = -1e30
SWA_Q_HEADS = 8
SWA_KV_HEADS = 2
SWA_HEAD_DIM = 64
SWA_BLOCK = 128
REL_BUCKETS = 32
REL_MAX_DIST = 128
DN_HEADS = 4
DN_HEAD_DIM = 128
DN_DIM = DN_HEADS * DN_HEAD_DIM
DN_CONV = 4
GLA_HEADS = 4
GLA_GATE_RANK = 16
GLA_GATE_NORM = 16.0
CHUNK = 64

PAD = SWA_BLOCK - N_META
LANE = 128
ROW_TILE = 512
FF_TILE = 256
VMEM_LIMIT = 56 * 1024 * 1024


def _rms(x, w):
    return x * lax.rsqrt(jnp.mean(x * x, axis=-1, keepdims=True) + NORM_EPS) * w


def _silu(x):
    return x * jax.nn.sigmoid(x)


def _softplus(x):
    return jnp.maximum(x, 0.0) + jnp.log(1.0 + jnp.exp(-jnp.abs(x)))


def _dot(a, b, precision=None):
    return jnp.dot(a, b, preferred_element_type=F32, precision=precision)


def _dot_nt(a, b, precision=None):
    return lax.dot_general(a, b, (((1,), (1,)), ((), ())), preferred_element_type=F32,
                           precision=precision)


def _dot_tn(a, b, precision=None):
    return lax.dot_general(a, b, (((0,), (0,)), ((), ())), preferred_element_type=F32,
                           precision=precision)


def _resident(shape):
    nd = len(shape)
    return pl.BlockSpec(shape, lambda *_: (0,) * nd, pipeline_mode=pl.Buffered(1))


def _params(*semantics):
    return pltpu.CompilerParams(dimension_semantics=semantics, vmem_limit_bytes=VMEM_LIMIT)


SUB = ROW_TILE // SWA_BLOCK


def _pick(arr, *lead):
    rest = arr.shape[len(lead):]
    index = tuple(lead) + (0,) * len(rest)
    return pl.BlockSpec((None,) * len(lead) + rest, lambda i: index,
                        pipeline_mode=pl.Buffered(1))


def _tile_specs(width):
    return [pl.BlockSpec((ROW_TILE, width), lambda i: (i, 0))]


def _real_specs(width, lp, seq):
    tiles = seq // ROW_TILE
    blocks = lp // SWA_BLOCK

    def spec(j):
        return pl.BlockSpec((SWA_BLOCK, width),
                            lambda i: ((i // tiles) * blocks + 1 + (i % tiles) * SUB + j, 0))
    return [spec(j) for j in range(SUB)]


def _embed_specs(width, lp, seq):
    blocks = lp // SWA_BLOCK
    x_blocks = seq // SWA_BLOCK

    def spec(j):
        def index(i):
            blk = i * SUB + j
            return ((blk // blocks) * x_blocks + jnp.maximum(blk % blocks - 1, 0), 0)
        return pl.BlockSpec((SWA_BLOCK, width), index)
    return [spec(j) for j in range(SUB)]


def _load_rows(refs):
    if len(refs) == 1:
        return refs[0][...]
    return jnp.concatenate([r[...] for r in refs], axis=0)


def _ffn_core(x, nwa, nwb, wg_ref, wu_ref, wd_ref, a_ref):
    hn = _rms(x, nwa).astype(BF16)
    d_ff = wg_ref.shape[1]
    for c in range(d_ff // FF_TILE):
        cols = slice(c * FF_TILE, (c + 1) * FF_TILE)
        g = _dot(hn, wg_ref[:, cols])
        u = _dot(hn, wu_ref[:, cols])
        a_ref[:, cols] = (_silu(g) * u).astype(BF16)
    f = _dot(a_ref[...], wd_ref[...])
    return x + 0.5 * _rms(f, nwb)


def _ffn_proj_body(*refs, n_rows, embed_blocks, n_out):
    row_refs = refs[:n_rows]
    refs = refs[n_rows:]
    if embed_blocks:
        meta_ref, refs = refs[0], refs[1:]
        first = pl.program_id(0) * SUB
        x = jnp.concatenate(
            [jnp.where((first + j) % embed_blocks == 0, meta_ref[...], r[...])
             for j, r in enumerate(row_refs)], axis=0)
    else:
        x = _load_rows(row_refs)
    nwa_ref, nwb_ref, wg_ref, wu_ref, wd_ref, nwm_ref, win_ref = refs[:7]
    h_ref = refs[7]
    out_refs = refs[8:8 + n_out]
    a_ref = refs[8 + n_out]
    h = _ffn_core(x, nwa_ref[...], nwb_ref[...], wg_ref, wu_ref, wd_ref, a_ref)
    h_ref[...] = h
    hn = _rms(h, nwm_ref[...]).astype(BF16)
    off = 0
    for o_ref in out_refs:
        n = o_ref.shape[1]
        o_ref[...] = _dot(hn, win_ref[:, off:off + n]).astype(o_ref.dtype)
        off += n


def _ffn_proj(src, meta_block, norm4, layer, wg, wu, wd, w_in, widths, dtypes, rows, lp, seq):
    d = norm4.shape[-1]
    d_ff = wg.shape[-1]
    assert sum(widths) == w_in.shape[1]
    if meta_block is None:
        row_specs, extra, extra_specs, embed_blocks = _tile_specs(d), [], [], 0
    else:
        row_specs = _embed_specs(d, lp, seq)
        extra, extra_specs, embed_blocks = [meta_block], [_resident(meta_block.shape)], (
            lp // SWA_BLOCK)
    tile = lambda n: pl.BlockSpec((ROW_TILE, n), lambda i: (i, 0))
    return pl.pallas_call(
        functools.partial(_ffn_proj_body, n_rows=len(row_specs), embed_blocks=embed_blocks,
                          n_out=len(widths)),
        grid=(rows // ROW_TILE,),
        in_specs=row_specs + extra_specs
        + [_pick(norm4, layer, 0), _pick(norm4, layer, 1), _pick(wg, layer, 0),
           _pick(wu, layer, 0), _pick(wd, layer, 0), _pick(norm4, layer, 2),
           _resident(w_in.shape)],
        out_specs=[tile(d)] + [tile(n) for n in widths],
        out_shape=[jax.ShapeDtypeStruct((rows, d), F32)]
        + [jax.ShapeDtypeStruct((rows, n), dt) for n, dt in zip(widths, dtypes)],
        scratch_shapes=[pltpu.VMEM((ROW_TILE, d_ff), BF16)],
        compiler_params=_params("parallel"),
        name="ffn_proj",
    )(*([src] * len(row_specs)), *extra, norm4, norm4, wg, wu, wd, norm4, w_in)


def _proj_ffn_body(*refs, n_rows, n_x):
    x_groups = [refs[g * n_rows:(g + 1) * n_rows] for g in range(n_x)]
    refs = refs[n_x * n_rows:]
    h_refs = refs[:n_rows]
    wo_ref, nwo_ref, nwa_ref, nwb_ref, wg_ref, wu_ref, wd_ref, o_ref, a_ref = refs[n_rows:]
    acc = None
    off = 0
    for group in x_groups:
        xs = _load_rows(group)
        k = xs.shape[1]
        p = _dot(xs, wo_ref[off:off + k, :])
        acc = p if acc is None else acc + p
        off += k
    h = _load_rows(h_refs) + _rms(acc, nwo_ref[...])
    o_ref[...] = _ffn_core(h, nwa_ref[...], nwb_ref[...], wg_ref, wu_ref, wd_ref, a_ref)


def _proj_ffn(xs, h, w_out, norm4, layer, wg, wu, wd, final, batch, lp, seq):
    d = norm4.shape[-1]
    d_ff = wg.shape[-1]
    if final:
        specs = lambda width: _real_specs(width, lp, seq)
        out_rows = batch * seq
    else:
        specs = _tile_specs
        out_rows = batch * lp
    h_specs = specs(d)
    n_rows = len(h_specs)
    in_specs, args = [], []
    for x in xs:
        in_specs += specs(x.shape[1])
        args += [x] * n_rows
    in_specs += h_specs + [_resident(w_out.shape), _pick(norm4, layer, 3),
                           _pick(norm4, layer, 4), _pick(norm4, layer, 5),
                           _pick(wg, layer, 1), _pick(wu, layer, 1), _pick(wd, layer, 1)]
    args += [h] * n_rows + [w_out, norm4, norm4, norm4, wg, wu, wd]
    return pl.pallas_call(
        functools.partial(_proj_ffn_body, n_rows=n_rows, n_x=len(xs)),
        grid=(out_rows // ROW_TILE,),
        in_specs=in_specs,
        out_specs=pl.BlockSpec((ROW_TILE, d), lambda i: (i, 0)),
        out_shape=jax.ShapeDtypeStruct((out_rows, d), F32),
        scratch_shapes=[pltpu.VMEM((ROW_TILE, d_ff), BF16)],
        compiler_params=_params("parallel"),
        name="proj_ffn",
    )(*args)


def _t5_bucket_np(rel):
    n = np.maximum(rel, 0)
    max_exact = REL_BUCKETS // 2
    n_f = np.maximum(n, 1).astype(np.float32)
    large = max_exact + (np.log(n_f / np.float32(max_exact))
                         / np.float32(math.log(REL_MAX_DIST / max_exact))
                         * np.float32(REL_BUCKETS - max_exact)).astype(np.int32)
    large = np.minimum(large, REL_BUCKETS - 1)
    return np.where(n < max_exact, n, large).astype(np.int32)


def _swa_bucket_index():
    i = np.arange(SWA_BLOCK)[:, None]
    c = np.arange(SWA_BLOCK)[None, :]
    out = np.full((3, SWA_BLOCK, 2 * SWA_BLOCK), -1, np.int32)
    for variant in range(3):
        pos_q = variant * SWA_BLOCK + i - PAD
        pos_k = variant * SWA_BLOCK + c - PAD - np.where(c > i, SWA_BLOCK, 0)
        rel_b = pos_q - pos_k
        assert ((rel_b >= 0) & (rel_b < SWA_BLOCK)).all()
        out[variant, :, :SWA_BLOCK] = np.where(pos_k >= N_META, _t5_bucket_np(rel_b), -1)
        rel_m = pos_q - c
        meta = (c < N_META) & (rel_m >= 0)
        out[variant, :, SWA_BLOCK:] = np.where(meta, _t5_bucket_np(rel_m), -1)
    return out


_SWA_BUCKET_INDEX = _swa_bucket_index()


def _bias_body(tbl_ref, idx_ref, o_ref):
    head = pl.program_id(1)
    idx = idx_ref[0]
    acc = jnp.full(idx.shape, NEG_INF, F32)
    for b in range(REL_BUCKETS):
        acc = jnp.where(idx == b, tbl_ref[b, head], acc)
    o_ref[0, 0] = acc


def _swa_bias(rel_table):
    idx = jnp.asarray(_SWA_BUCKET_INDEX)
    group = SWA_Q_HEADS // SWA_KV_HEADS
    width = idx.shape[2]
    return pl.pallas_call(
        _bias_body,
        grid=(3, SWA_Q_HEADS),
        in_specs=[pl.BlockSpec(memory_space=pltpu.SMEM),
                  pl.BlockSpec((1, SWA_BLOCK, width), lambda v, h: (v, 0, 0))],
        out_specs=pl.BlockSpec((1, 1, SWA_BLOCK, width),
                               lambda v, h: (v, h // group, h % group, 0)),
        out_shape=jax.ShapeDtypeStruct((3, SWA_KV_HEADS, group * SWA_BLOCK, width), F32),
        compiler_params=_params("arbitrary", "arbitrary"),
        name="swa_bias",
    )(rel_table, idx)


def _swa_body(sink_ref, q_ref, kc_ref, kp_ref, km_ref, vc_ref, vp_ref, vm_ref, bias_ref, o_ref,
              *, blocks):
    dh = SWA_HEAD_DIM
    blk = SWA_BLOCK
    group = SWA_Q_HEADS // SWA_KV_HEADS
    scale = dh ** -0.5
    step = pl.program_id(1)
    zpad = jnp.zeros((blk - N_META, dh), BF16)
    row = lax.broadcasted_iota(jnp.int32, (group * blk, blk), 0) % blk
    col = lax.broadcasted_iota(jnp.int32, (group * blk, blk), 1)
    from_prev = col > row

    units = [(j, hk) for j in range(blocks) for hk in range(SWA_KV_HEADS)]
    scores, values = [], []
    for j, hk in units:
        r = slice(j * blk, (j + 1) * blk)
        kv = slice(hk * dh, (hk + 1) * dh)
        k_prev = kp_ref[:, kv] if j == 0 else kc_ref[(j - 1) * blk:j * blk, kv]
        v_prev = vp_ref[:, kv] if j == 0 else vc_ref[(j - 1) * blk:j * blk, kv]
        k_all = jnp.concatenate([k_prev, kc_ref[r, kv], km_ref[:, kv], zpad], axis=0)
        values.append(jnp.concatenate([v_prev, vc_ref[r, kv], vm_ref[:, kv], zpad], axis=0))
        q = jnp.concatenate([q_ref[r, (hk * group + g) * dh:(hk * group + g + 1) * dh]
                             for g in range(group)], axis=0) * scale
        scores.append(_dot_nt(q, k_all))

    probs, dens = [], []
    for (j, hk), s in zip(units, scores):
        bias = bias_ref[jnp.minimum(step * blocks + j, 2), hk]
        s_band = jnp.where(from_prev, s[:, :blk], s[:, blk:2 * blk]) + bias[:, :blk]
        s_meta = s[:, 2 * blk:] + bias[:, blk:]
        sink = jnp.concatenate([jnp.full((blk, 1), sink_ref[hk * group + g], F32)
                                for g in range(group)], axis=0)
        m = jnp.maximum(jnp.maximum(jnp.max(s_band, axis=-1, keepdims=True),
                                    jnp.max(s_meta, axis=-1, keepdims=True)), sink)
        e_band = jnp.exp(s_band - m)
        e_meta = jnp.exp(s_meta - m)
        dens.append(jnp.sum(e_band, axis=-1, keepdims=True)
                    + jnp.sum(e_meta, axis=-1, keepdims=True) + jnp.exp(sink - m))
        probs.append(jnp.concatenate([jnp.where(from_prev, e_band, 0.0),
                                      jnp.where(from_prev, 0.0, e_band), e_meta],
                                     axis=1).astype(BF16))

    for (j, hk), p, v_all, den in zip(units, probs, values, dens):
        o = _dot(p, v_all) / den
        for g in range(group):
            hq = hk * group + g
            o_ref[j * blk:(j + 1) * blk, hq * dh:(hq + 1) * dh] = (
                o[g * blk:(g + 1) * blk].astype(o_ref.dtype))


def _swa(q, k, v, bias, sinks, batch, lp, blocks):
    rows = q.shape[0]
    nb = lp // SWA_BLOCK
    steps = nb // blocks
    meta_blocks = lp // N_META
    cur = lambda b, s: (b * steps + s, 0)
    prev = lambda b, s: (b * nb + jnp.maximum(s * blocks - 1, 0), 0)
    meta = lambda b, s: (b * meta_blocks + PAD // N_META, 0)
    kv_w = k.shape[1]
    rs = blocks * SWA_BLOCK
    return pl.pallas_call(
        functools.partial(_swa_body, blocks=blocks),
        grid=(batch, steps),
        in_specs=[pl.BlockSpec(memory_space=pltpu.SMEM),
                  pl.BlockSpec((rs, q.shape[1]), cur),
                  pl.BlockSpec((rs, kv_w), cur), pl.BlockSpec((SWA_BLOCK, kv_w), prev),
                  pl.BlockSpec((N_META, kv_w), meta),
                  pl.BlockSpec((rs, kv_w), cur), pl.BlockSpec((SWA_BLOCK, kv_w), prev),
                  pl.BlockSpec((N_META, kv_w), meta),
                  _resident(bias.shape)],
        out_specs=pl.BlockSpec((rs, q.shape[1]), cur),
        out_shape=jax.ShapeDtypeStruct((rows, q.shape[1]), BF16),
        compiler_params=_params("parallel", "arbitrary"),
        name="swa",
    )(sinks, q, k, k, k, v, v, v, bias)


def _tri_masks():
    row = lax.broadcasted_iota(jnp.int32, (CHUNK, CHUNK), 0)
    col = lax.broadcasted_iota(jnp.int32, (CHUNK, CHUNK), 1)
    return row >= col, row > col, row == col


def _deltanet_body(qkv_ref, z_ref, gates_ref, convw_ref, alog_ref, dtb_ref, nw_ref, o_ref,
                   s_ref, xpad_ref, act_ref, *, chunks):
    nh, dh = DN_HEADS, DN_HEAD_DIM
    rows = chunks * CHUNK
    halo = 8

    @pl.when(pl.program_id(1) == 0)
    def _():
        s_ref[...] = jnp.zeros_like(s_ref)
        xpad_ref[0:halo, :] = jnp.zeros((halo, xpad_ref.shape[1]), F32)

    xpad_ref[halo:halo + rows, :] = qkv_ref[...].astype(F32)
    cw = convw_ref[...]
    y = cw[DN_CONV - 1:DN_CONV, :] * xpad_ref[halo:halo + rows, :]
    for j in range(DN_CONV - 1):
        shift = DN_CONV - 1 - j
        y = y + cw[j:j + 1, :] * xpad_ref[halo - shift:halo - shift + rows, :]
    tail = xpad_ref[rows:rows + halo, :]
    act_ref[...] = _silu(y)
    xpad_ref[0:halo, :] = tail

    gt = gates_ref[...]
    beta_full = jax.nn.sigmoid(gt)
    g_full = -jnp.exp(alog_ref[...]) * _softplus(gt + dtb_ref[...])
    ri = lax.broadcasted_iota(jnp.int32, (rows, rows), 0)
    ci = lax.broadcasted_iota(jnp.int32, (rows, rows), 1)
    tri_blocks = ((ri >= ci) & (ri // CHUNK == ci // CHUNK)).astype(F32)
    gc_full = _dot(tri_blocks, g_full, HIGHEST)

    incl, strict, diag = _tri_masks()
    eye = diag.astype(F32)
    items = [(c, h) for c in range(chunks) for h in range(nh)]

    neg_a, attn16, qd16, kdt16, rhs16, decay = [], [], [], [], [], []
    for c, h in items:
        r = slice(c * CHUNK, (c + 1) * CHUNK)
        q = act_ref[r, h * dh:(h + 1) * dh]
        k = act_ref[r, DN_DIM + h * dh:DN_DIM + (h + 1) * dh]
        v = act_ref[r, 2 * DN_DIM + h * dh:2 * DN_DIM + (h + 1) * dh]
        q = q * lax.rsqrt(jnp.sum(q * q, axis=-1, keepdims=True) + 1e-6) * dh ** -0.5
        k = k * lax.rsqrt(jnp.sum(k * k, axis=-1, keepdims=True) + 1e-6)
        beta = jnp.broadcast_to(beta_full[r, h:h + 1], (CHUNK, dh))
        gc = jnp.broadcast_to(gc_full[r, nh + h:nh + h + 1], (CHUNK, dh))
        gc_row = jnp.transpose(gc)[:CHUNK, :]
        gamma = jnp.where(incl, jnp.exp(jnp.where(incl, gc[:, :CHUNK] - gc_row, 0.0)), 0.0)
        k_beta = k * beta
        k16 = k.astype(BF16)
        neg_a.append(jnp.where(strict, -(_dot_nt(k_beta.astype(BF16), k16) * gamma), 0.0))
        attn16.append(jnp.where(incl, _dot_nt(q.astype(BF16), k16) * gamma, 0.0).astype(BF16))
        gc_last = gc[CHUNK - 1:CHUNK, :]
        qd16.append((q * jnp.exp(gc)).astype(BF16))
        kdt16.append(jnp.transpose(k * jnp.exp(gc_last - gc)).astype(BF16))
        rhs16.append(jnp.concatenate([v * beta, k_beta * jnp.exp(gc)], axis=1).astype(BF16))
        decay.append(jnp.exp(gc_last))

    p = neg_a
    t = [eye + x for x in p]
    for _ in range(int(math.log2(CHUNK)) - 1):
        p16 = [x.astype(BF16) for x in p]
        p = [_dot(x, x) for x in p16]
        t = [y + _dot(y.astype(BF16), x.astype(BF16)) for y, x in zip(t, p)]
    uw = [_dot(y.astype(BF16), b) for y, b in zip(t, rhs16)]

    nw = nw_ref[...]
    state = [s_ref[h] for h in range(nh)]
    for c in range(chunks):
        r = slice(c * CHUNK, (c + 1) * CHUNK)
        pair = [c * nh + h for h in range(nh)]
        s16 = [s.astype(BF16) for s in state]
        ws = [_dot(uw[i][:, dh:].astype(BF16), s16[h]) for h, i in enumerate(pair)]
        qs = [_dot(qd16[i], s16[h]) for h, i in enumerate(pair)]
        vn16 = [(uw[i][:, :dh] - ws[h]).astype(BF16) for h, i in enumerate(pair)]
        out = [qs[h] + _dot(attn16[i], vn16[h]) for h, i in enumerate(pair)]
        state = [state[h] * decay[i] + _dot(kdt16[i], vn16[h]) for h, i in enumerate(pair)]
        for h in range(nh):
            hc = slice(h * dh, (h + 1) * dh)
            o = _rms(out[h], nw) * _silu(z_ref[r, hc].astype(F32))
            o_ref[r, hc] = o.astype(o_ref.dtype)
    for h in range(nh):
        s_ref[h] = state[h]


def _deltanet(qkv, z, gates, conv_w, alog_vec, dtb_vec, nw, batch, lp, chunks):
    rows = qkv.shape[0]
    rs = chunks * CHUNK
    steps = lp // rs
    idx = lambda b, s: (b * steps + s, 0)
    return pl.pallas_call(
        functools.partial(_deltanet_body, chunks=chunks),
        grid=(batch, steps),
        in_specs=[pl.BlockSpec((rs, qkv.shape[1]), idx), pl.BlockSpec((rs, z.shape[1]), idx),
                  pl.BlockSpec((rs, gates.shape[1]), idx), _resident(conv_w.shape),
                  _resident(alog_vec.shape), _resident(dtb_vec.shape), _resident(nw.shape)],
        out_specs=pl.BlockSpec((rs, DN_DIM), idx),
        out_shape=jax.ShapeDtypeStruct((rows, DN_DIM), BF16),
        scratch_shapes=[pltpu.VMEM((DN_HEADS, DN_HEAD_DIM, DN_HEAD_DIM), F32),
                        pltpu.VMEM((rs + 8, qkv.shape[1]), F32),
                        pltpu.VMEM((rs, qkv.shape[1]), F32)],
        compiler_params=_params("parallel", "arbitrary"),
        name="deltanet",
    )(qkv, z, gates, conv_w, alog_vec, dtb_vec, nw)


def _gla_body(q_ref, k_ref, v_ref, g_ref, gk_ref, wgu_ref, bg_ref, nw_ref, o_ref, s_ref, *,
              chunks):
    nh = GLA_HEADS
    dk = q_ref.shape[1] // nh
    dv = v_ref.shape[1] // nh

    @pl.when(pl.program_id(1) == 0)
    def _():
        s_ref[...] = jnp.zeros_like(s_ref)

    logits = _dot(gk_ref[...].astype(BF16), wgu_ref[...]) + bg_ref[...]
    glog = (jnp.minimum(logits, 0.0) - jnp.log(1.0 + jnp.exp(-jnp.abs(logits)))) / GLA_GATE_NORM
    incl, _, _ = _tri_masks()
    tri = incl.astype(F32)
    nw = nw_ref[...]
    for c in range(chunks):
        r = slice(c * CHUNK, (c + 1) * CHUNK)
        bcum_all = _dot(tri, glog[r, :], HIGHEST)
        for h in range(nh):
            kc = slice(h * dk, (h + 1) * dk)
            vc = slice(h * dv, (h + 1) * dv)
            bc = bcum_all[:, kc]
            q = q_ref[r, kc].astype(F32) * dk ** -0.5
            k = k_ref[r, kc].astype(F32)
            v16 = v_ref[r, vc]
            q_dec = (q * jnp.exp(bc)).astype(BF16)
            attn = jnp.where(incl, _dot_nt(q_dec, (k * jnp.exp(-bc)).astype(BF16)), 0.0)
            b_last = bc[CHUNK - 1:CHUNK, :]
            k_dec = (k * jnp.exp(b_last - bc)).astype(BF16)
            st = s_ref[h]
            o = _dot(attn.astype(BF16), v16) + _dot_nt(q_dec, st.astype(BF16))
            s_ref[h] = st * jnp.exp(b_last) + _dot_tn(v16, k_dec)
            o = _rms(o, nw) * _silu(g_ref[r, vc].astype(F32))
            o_ref[r, vc] = o.astype(o_ref.dtype)


def _gla(q, k, v, g, gk, wgu, bg, nw, batch, lp, chunks):
    rows = q.shape[0]
    rs = chunks * CHUNK
    steps = lp // rs
    idx = lambda b, s: (b * steps + s, 0)
    dk = q.shape[1] // GLA_HEADS
    dv = v.shape[1] // GLA_HEADS
    return pl.pallas_call(
        functools.partial(_gla_body, chunks=chunks),
        grid=(batch, steps),
        in_specs=[pl.BlockSpec((rs, q.shape[1]), idx), pl.BlockSpec((rs, k.shape[1]), idx),
                  pl.BlockSpec((rs, v.shape[1]), idx), pl.BlockSpec((rs, g.shape[1]), idx),
                  pl.BlockSpec((rs, gk.shape[1]), idx), _resident(wgu.shape),
                  _resident(bg.shape), _resident(nw.shape)],
        out_specs=pl.BlockSpec((rs, v.shape[1]), idx),
        out_shape=jax.ShapeDtypeStruct((rows, v.shape[1]), BF16),
        scratch_shapes=[pltpu.VMEM((GLA_HEADS, dv, dk), F32)],
        compiler_params=_params("parallel", "arbitrary"),
        name="gla",
    )(q, k, v, g, gk, wgu, bg, nw)


def _pad_cols(w, width):
    return jnp.pad(w, ((0, 0), (0, width - w.shape[1])))


def _per_step(n, preferred):
    for c in preferred:
        if n % c == 0:
            return c
    return 1


def kernel(x, meta_tokens, norm_w, ffn_w_gate, ffn_w_up, ffn_w_down, rel_bias_table,
           even_w_in, even_conv_w, swa_sinks, dn_a_log, dn_dt_bias, dn_norm_w, even_w_out,
           odd_w_in, gla_w_gate_up, gla_b_gate, gla_norm_w, odd_w_out):
    batch, seq, d = x.shape
    depth = norm_w.shape[0]
    lp = SWA_BLOCK + seq
    assert seq % ROW_TILE == 0 and (batch * lp) % ROW_TILE == 0
    rows = batch * lp
    chunks = _per_step(lp // CHUNK, (3, 2))
    swa_blocks = _per_step(lp // SWA_BLOCK, (3, 2))

    meta_block = jnp.pad(meta_tokens.astype(x.dtype), ((PAD, 0), (0, 0)))
    norm4 = norm_w[:, :, None, :]
    wg = ffn_w_gate.astype(BF16)
    wu = ffn_w_up.astype(BF16)
    wd = ffn_w_down.astype(BF16)

    swa_q = SWA_Q_HEADS * SWA_HEAD_DIM
    swa_kv = SWA_KV_HEADS * SWA_HEAD_DIM
    bias = None
    h = x.reshape(batch * seq, d)
    for layer in range(depth):
        i = layer // 2
        embed = meta_block if layer == 0 else None
        final = layer == depth - 1
        if layer % 2 == 0:
            assert even_w_in.shape[2] == swa_q + 2 * swa_kv + 4 * DN_DIM + 2 * DN_HEADS
            widths = (swa_q, swa_kv, swa_kv, 3 * DN_DIM, DN_DIM, LANE)
            w_in = _pad_cols(even_w_in[i], sum(widths)).astype(BF16)
            h, qa, ka, va, qkv_b, z_b, gates = _ffn_proj(
                h, embed, norm4, layer, wg, wu, wd, w_in, widths,
                (BF16, BF16, BF16, BF16, BF16, F32), rows, lp, seq)
            if bias is None:
                bias = _swa_bias(rel_bias_table)
            o_a = _swa(qa, ka, va, bias, swa_sinks[i], batch, lp, swa_blocks)
            lane_pad = (DN_HEADS, LANE - 2 * DN_HEADS)
            alog_vec = jnp.pad(dn_a_log[i], lane_pad)[None, :]
            dtb_vec = jnp.pad(dn_dt_bias[i], lane_pad)[None, :]
            o_b = _deltanet(qkv_b, z_b, gates, even_conv_w[i], alog_vec, dtb_vec,
                            dn_norm_w[i][None, :], batch, lp, chunks)
            mixed, w_out = [o_a, o_b], even_w_out[i]
        else:
            key_dim = gla_w_gate_up.shape[2]
            val_dim = odd_w_out.shape[1]
            widths = (key_dim, key_dim, val_dim, val_dim, LANE)
            w_in = _pad_cols(odd_w_in[i], sum(widths)).astype(BF16)
            h, q, k, v, g, gk = _ffn_proj(
                h, embed, norm4, layer, wg, wu, wd, w_in, widths,
                (BF16, BF16, BF16, BF16, F32), rows, lp, seq)
            wgu = jnp.pad(gla_w_gate_up[i], ((0, LANE - GLA_GATE_RANK), (0, 0))).astype(BF16)
            o = _gla(q, k, v, g, gk, wgu, gla_b_gate[i][None, :], gla_norm_w[i][None, :],
                     batch, lp, chunks)
            mixed, w_out = [o], odd_w_out[i]
        h = _proj_ffn(mixed, h, w_out.astype(BF16), norm4, layer, wg, wu, wd, final,
                      batch, lp, seq)
    return h.reshape(batch, seq, d)
```

```python
import functools
import math

import numpy as np
import jax
import jax.numpy as jnp
from jax import lax
from jax.experimental import pallas as pl
from jax.experimental.pallas import tpu as pltpu

F32 = jnp.float32
BF16 = jnp.bfloat16
HIGHEST = lax.Precision.HIGHEST

N_META = 16
NORM_EPS = 1e-6
NEG_INF = -1e30
SWA_Q_HEADS = 8
SWA_KV_HEADS = 2
SWA_HEAD_DIM = 64
SWA_BLOCK = 128
REL_BUCKETS = 32
REL_MAX_DIST = 128
DN_HEADS = 4
DN_HEAD_DIM = 128
DN_DIM = DN_HEADS * DN_HEAD_DIM
DN_CONV = 4
GLA_HEADS = 4
GLA_GATE_RANK = 16
GLA_GATE_NORM = 16.0
CHUNK = 64

PAD = SWA_BLOCK - N_META
LANE = 128
ROW_TILE = 512
FF_TILE = 256
VMEM_LIMIT = 56 * 1024 * 1024


def _rms(x, w):
    return x * lax.rsqrt(jnp.mean(x * x, axis=-1, keepdims=True) + NORM_EPS) * w


def _silu(x):
    return x * jax.nn.sigmoid(x)


def _softplus(x):
    return jnp.maximum(x, 0.0) + jnp.log(1.0 + jnp.exp(-jnp.abs(x)))


def _dot(a, b, precision=None):
    return jnp.dot(a, b, preferred_element_type=F32, precision=precision)


def _dot_nt(a, b, precision=None):
    return lax.dot_general(a, b, (((1,), (1,)), ((), ())), preferred_element_type=F32,
                           precision=precision)


def _dot_tn(a, b, precision=None):
    return lax.dot_general(a, b, (((0,), (0,)), ((), ())), preferred_element_type=F32,
                           precision=precision)


def _resident(shape):
    nd = len(shape)
    return pl.BlockSpec(shape, lambda *_: (0,) * nd, pipeline_mode=pl.Buffered(1))


def _params(*semantics):
    return pltpu.CompilerParams(dimension_semantics=semantics, vmem_limit_bytes=VMEM_LIMIT)


SUB = ROW_TILE // SWA_BLOCK


def _pick(arr, *lead):
    rest = arr.shape[len(lead):]
    index = tuple(lead) + (0,) * len(rest)
    return pl.BlockSpec((None,) * len(lead) + rest, lambda i: index,
                        pipeline_mode=pl.Buffered(1))


def _tile_specs(width):
    return [pl.BlockSpec((ROW_TILE, width), lambda i: (i, 0))]


def _real_specs(width, lp, seq):
    tiles = seq // ROW_TILE
    blocks = lp // SWA_BLOCK

    def spec(j):
        return pl.BlockSpec((SWA_BLOCK, width),
                            lambda i: ((i // tiles) * blocks + 1 + (i % tiles) * SUB + j, 0))
    return [spec(j) for j in range(SUB)]


def _embed_specs(width, lp, seq):
    blocks = lp // SWA_BLOCK
    x_blocks = seq // SWA_BLOCK

    def spec(j):
        def index(i):
            blk = i * SUB + j
            return ((blk // blocks) * x_blocks + jnp.maximum(blk % blocks - 1, 0), 0)
        return pl.BlockSpec((SWA_BLOCK, width), index)
    return [spec(j) for j in range(SUB)]


def _load_rows(refs):
    if len(refs) == 1:
        return refs[0][...]
    return jnp.concatenate([r[...] for r in refs], axis=0)


def _ffn_core(x, nwa, nwb, wg_ref, wu_ref, wd_ref, a_ref):
    hn = _rms(x, nwa).astype(BF16)
    d_ff = wg_ref.shape[1]
    for c in range(d_ff // FF_TILE):
        cols = slice(c * FF_TILE, (c + 1) * FF_TILE)
        g = _dot(hn, wg_ref[:, cols])
        u = _dot(hn, wu_ref[:, cols])
        a_ref[:, cols] = (_silu(g) * u).astype(BF16)
    f = _dot(a_ref[...], wd_ref[...])
    return x + 0.5 * _rms(f, nwb)


CONV_HALO = 8


def _ffn_proj_body(*refs, n_rows, embed_blocks, n_out, conv_out, lp):
    row_refs = refs[:n_rows]
    refs = refs[n_rows:]
    if embed_blocks:
        meta_ref, refs = refs[0], refs[1:]
        first = pl.program_id(0) * SUB
        x = jnp.concatenate(
            [jnp.where((first + j) % embed_blocks == 0, meta_ref[...], r[...])
             for j, r in enumerate(row_refs)], axis=0)
    else:
        x = _load_rows(row_refs)
    if conv_out is not None:
        convw_ref, refs = refs[0], refs[1:]
    nwa_ref, nwb_ref, wg_ref, wu_ref, wd_ref, nwm_ref, win_ref = refs[:7]
    h_ref = refs[7]
    out_refs = refs[8:8 + n_out]
    a_ref = refs[8 + n_out]
    h = _ffn_core(x, nwa_ref[...], nwb_ref[...], wg_ref, wu_ref, wd_ref, a_ref)
    h_ref[...] = h
    hn = _rms(h, nwm_ref[...]).astype(BF16)
    offsets = np.cumsum([0] + [o.shape[1] for o in out_refs])

    def project(i):
        return _dot(hn, win_ref[:, offsets[i]:offsets[i + 1]])

    if conv_out is not None:
        xpad_ref = refs[9 + n_out]
        taps = convw_ref.shape[0]

        @pl.when(pl.program_id(0) == 0)
        def _():
            xpad_ref[0:CONV_HALO, :] = jnp.zeros((CONV_HALO, xpad_ref.shape[1]), F32)

        xpad_ref[CONV_HALO:, :] = project(conv_out)
        cw = convw_ref[...]
        y = cw[taps - 1:taps, :] * xpad_ref[CONV_HALO:, :]
        for j in range(taps - 1):
            lo = CONV_HALO - (taps - 1 - j)
            y = y + cw[j:j + 1, :] * xpad_ref[lo:lo + ROW_TILE, :]
        tail = xpad_ref[ROW_TILE:, :]
        row = (pl.program_id(0) * ROW_TILE) % lp + lax.broadcasted_iota(
            jnp.int32, (ROW_TILE, 1), 0)
        row = jnp.where(row >= lp, row - lp, row)
        out_refs[conv_out][...] = jnp.where(row >= PAD, _silu(y), 0.0).astype(
            out_refs[conv_out].dtype)
        xpad_ref[0:CONV_HALO, :] = tail
    for i, o_ref in enumerate(out_refs):
        if i != conv_out:
            o_ref[...] = project(i).astype(o_ref.dtype)


def _ffn_proj(src, meta_block, norm4, layer, wg, wu, wd, w_in, widths, dtypes, rows, lp, seq,
              conv_out=None, conv_w=None):
    d = norm4.shape[-1]
    d_ff = wg.shape[-1]
    assert sum(widths) == w_in.shape[1] and lp > ROW_TILE
    if meta_block is None:
        row_specs, extra, extra_specs, embed_blocks = _tile_specs(d), [], [], 0
    else:
        row_specs = _embed_specs(d, lp, seq)
        extra, extra_specs, embed_blocks = [meta_block], [_resident(meta_block.shape)], (
            lp // SWA_BLOCK)
    scratch = [pltpu.VMEM((ROW_TILE, d_ff), BF16)]
    if conv_out is not None:
        extra, extra_specs = extra + [conv_w], extra_specs + [_resident(conv_w.shape)]
        scratch.append(pltpu.VMEM((CONV_HALO + ROW_TILE, widths[conv_out]), F32))
    tile = lambda n: pl.BlockSpec((ROW_TILE, n), lambda i: (i, 0))
    return pl.pallas_call(
        functools.partial(_ffn_proj_body, n_rows=len(row_specs), embed_blocks=embed_blocks,
                          n_out=len(widths), conv_out=conv_out, lp=lp),
        grid=(rows // ROW_TILE,),
        in_specs=row_specs + extra_specs
        + [_pick(norm4, layer, 0), _pick(norm4, layer, 1), _pick(wg, layer, 0),
           _pick(wu, layer, 0), _pick(wd, layer, 0), _pick(norm4, layer, 2),
           _resident(w_in.shape)],
        out_specs=[tile(d)] + [tile(n) for n in widths],
        out_shape=[jax.ShapeDtypeStruct((rows, d), F32)]
        + [jax.ShapeDtypeStruct((rows, n), dt) for n, dt in zip(widths, dtypes)],
        scratch_shapes=scratch,
        compiler_params=_params("arbitrary"),
        name="ffn_proj",
    )(*([src] * len(row_specs)), *extra, norm4, norm4, wg, wu, wd, norm4, w_in)


def _proj_ffn_body(*refs, n_rows, n_x):
    x_groups = [refs[g * n_rows:(g + 1) * n_rows] for g in range(n_x)]
    refs = refs[n_x * n_rows:]
    h_refs = refs[:n_rows]
    wo_ref, nwo_ref, nwa_ref, nwb_ref, wg_ref, wu_ref, wd_ref, o_ref, a_ref = refs[n_rows:]
    acc = None
    off = 0
    for group in x_groups:
        xs = _load_rows(group)
        k = xs.shape[1]
        p = _dot(xs, wo_ref[off:off + k, :])
        acc = p if acc is None else acc + p
        off += k
    h = _load_rows(h_refs) + _rms(acc, nwo_ref[...])
    o_ref[...] = _ffn_core(h, nwa_ref[...], nwb_ref[...], wg_ref, wu_ref, wd_ref, a_ref)


def _proj_ffn(xs, h, w_out, norm4, layer, wg, wu, wd, final, batch, lp, seq):
    d = norm4.shape[-1]
    d_ff = wg.shape[-1]
    if final:
        specs = lambda width: _real_specs(width, lp, seq)
        out_rows = batch * seq
    else:
        specs = _tile_specs
        out_rows = batch * lp
    h_specs = specs(d)
    n_rows = len(h_specs)
    in_specs, args = [], []
    for x in xs:
        in_specs += specs(x.shape[1])
        args += [x] * n_rows
    in_specs += h_specs + [_resident(w_out.shape), _pick(norm4, layer, 3),
                           _pick(norm4, layer, 4), _pick(norm4, layer, 5),
                           _pick(wg, layer, 1), _pick(wu, layer, 1), _pick(wd, layer, 1)]
    args += [h] * n_rows + [w_out, norm4, norm4, norm4, wg, wu, wd]
    return pl.pallas_call(
        functools.partial(_proj_ffn_body, n_rows=n_rows, n_x=len(xs)),
        grid=(out_rows // ROW_TILE,),
        in_specs=in_specs,
        out_specs=pl.BlockSpec((ROW_TILE, d), lambda i: (i, 0)),
        out_shape=jax.ShapeDtypeStruct((out_rows, d), F32),
        scratch_shapes=[pltpu.VMEM((ROW_TILE, d_ff), BF16)],
        compiler_params=_params("parallel"),
        name="proj_ffn",
    )(*args)


def _t5_bucket_np(rel):
    n = np.maximum(rel, 0)
    max_exact = REL_BUCKETS // 2
    n_f = np.maximum(n, 1).astype(np.float32)
    large = max_exact + (np.log(n_f / np.float32(max_exact))
                         / np.float32(math.log(REL_MAX_DIST / max_exact))
                         * np.float32(REL_BUCKETS - max_exact)).astype(np.int32)
    large = np.minimum(large, REL_BUCKETS - 1)
    return np.where(n < max_exact, n, large).astype(np.int32)


def _swa_bucket_index():
    i = np.arange(SWA_BLOCK)[:, None]
    c = np.arange(SWA_BLOCK)[None, :]
    out = np.full((3, SWA_BLOCK, 2 * SWA_BLOCK), -1, np.int32)
    for variant in range(3):
        pos_q = variant * SWA_BLOCK + i - PAD
        pos_k = variant * SWA_BLOCK + c - PAD - np.where(c > i, SWA_BLOCK, 0)
        rel_b = pos_q - pos_k
        assert ((rel_b >= 0) & (rel_b < SWA_BLOCK)).all()
        out[variant, :, :SWA_BLOCK] = np.where(pos_k >= N_META, _t5_bucket_np(rel_b), -1)
        rel_m = pos_q - c
        meta = (c < N_META) & (rel_m >= 0)
        out[variant, :, SWA_BLOCK:] = np.where(meta, _t5_bucket_np(rel_m), -1)
    return out


_SWA_BUCKET_INDEX = _swa_bucket_index()


def _bias_body(tbl_ref, idx_ref, o_ref):
    head = pl.program_id(1)
    idx = idx_ref[0]
    acc = jnp.full(idx.shape, NEG_INF, F32)
    for b in range(REL_BUCKETS):
        acc = jnp.where(idx == b, tbl_ref[b, head], acc)
    o_ref[0, 0] = acc


def _swa_bias(rel_table):
    idx = jnp.asarray(_SWA_BUCKET_INDEX)
    group = SWA_Q_HEADS // SWA_KV_HEADS
    width = idx.shape[2]
    return pl.pallas_call(
        _bias_body,
        grid=(3, SWA_Q_HEADS),
        in_specs=[pl.BlockSpec(memory_space=pltpu.SMEM),
                  pl.BlockSpec((1, SWA_BLOCK, width), lambda v, h: (v, 0, 0))],
        out_specs=pl.BlockSpec((1, 1, SWA_BLOCK, width),
                               lambda v, h: (v, h // group, h % group, 0)),
        out_shape=jax.ShapeDtypeStruct((3, SWA_KV_HEADS, group * SWA_BLOCK, width), F32),
        compiler_params=_params("arbitrary", "arbitrary"),
        name="swa_bias",
    )(rel_table, idx)


def _swa_body(sink_ref, q_ref, kc_ref, kp_ref, km_ref, vc_ref, vp_ref, vm_ref, bias_ref, o_ref,
              *, blocks):
    dh = SWA_HEAD_DIM
    blk = SWA_BLOCK
    group = SWA_Q_HEADS // SWA_KV_HEADS
    scale = dh ** -0.5
    step = pl.program_id(1)
    zpad = jnp.zeros((blk - N_META, dh), BF16)
    row = lax.broadcasted_iota(jnp.int32, (group * blk, blk), 0) % blk
    col = lax.broadcasted_iota(jnp.int32, (group * blk, blk), 1)
    from_prev = col > row

    units = [(j, hk) for j in range(blocks) for hk in range(SWA_KV_HEADS)]
    scores, values = [], []
    for j, hk in units:
        r = slice(j * blk, (j + 1) * blk)
        kv = slice(hk * dh, (hk + 1) * dh)
        k_prev = kp_ref[:, kv] if j == 0 else kc_ref[(j - 1) * blk:j * blk, kv]
        v_prev = vp_ref[:, kv] if j == 0 else vc_ref[(j - 1) * blk:j * blk, kv]
        k_all = jnp.concatenate([k_prev, kc_ref[r, kv], km_ref[:, kv], zpad], axis=0)
        values.append(jnp.concatenate([v_prev, vc_ref[r, kv], vm_ref[:, kv], zpad], axis=0))
        q = jnp.concatenate([q_ref[r, (hk * group + g) * dh:(hk * group + g + 1) * dh]
                             for g in range(group)], axis=0) * scale
        scores.append(_dot_nt(q, k_all))

    probs, dens = [], []
    for (j, hk), s in zip(units, scores):
        bias = bias_ref[jnp.minimum(step * blocks + j, 2), hk]
        s_band = jnp.where(from_prev, s[:, :blk], s[:, blk:2 * blk]) + bias[:, :blk]
        s_meta = s[:, 2 * blk:] + bias[:, blk:]
        sink = jnp.concatenate([jnp.full((blk, 1), sink_ref[hk * group + g], F32)
                                for g in range(group)], axis=0)
        m = jnp.maximum(jnp.maximum(jnp.max(s_band, axis=-1, keepdims=True),
                                    jnp.max(s_meta, axis=-1, keepdims=True)), sink)
        e_band = jnp.exp(s_band - m)
        e_meta = jnp.exp(s_meta - m)
        dens.append(jnp.sum(e_band, axis=-1, keepdims=True)
                    + jnp.sum(e_meta, axis=-1, keepdims=True) + jnp.exp(sink - m))
        probs.append(jnp.concatenate([jnp.where(from_prev, e_band, 0.0),
                                      jnp.where(from_prev, 0.0, e_band), e_meta],
                                     axis=1).astype(BF16))

    for (j, hk), p, v_all, den in zip(units, probs, values, dens):
        o = _dot(p, v_all) / den
        for g in range(group):
            hq = hk * group + g
            o_ref[j * blk:(j + 1) * blk, hq * dh:(hq + 1) * dh] = (
                o[g * blk:(g + 1) * blk].astype(o_ref.dtype))


def _swa(q, k, v, bias, sinks, batch, lp, blocks):
    rows = q.shape[0]
    nb = lp // SWA_BLOCK
    steps = nb // blocks
    meta_blocks = lp // N_META
    cur = lambda b, s: (b * steps + s, 0)
    prev = lambda b, s: (b * nb + jnp.maximum(s * blocks - 1, 0), 0)
    meta = lambda b, s: (b * meta_blocks + PAD // N_META, 0)
    kv_w = k.shape[1]
    rs = blocks * SWA_BLOCK
    return pl.pallas_call(
        functools.partial(_swa_body, blocks=blocks),
        grid=(batch, steps),
        in_specs=[pl.BlockSpec(memory_space=pltpu.SMEM),
                  pl.BlockSpec((rs, q.shape[1]), cur),
                  pl.BlockSpec((rs, kv_w), cur), pl.BlockSpec((SWA_BLOCK, kv_w), prev),
                  pl.BlockSpec((N_META, kv_w), meta),
                  pl.BlockSpec((rs, kv_w), cur), pl.BlockSpec((SWA_BLOCK, kv_w), prev),
                  pl.BlockSpec((N_META, kv_w), meta),
                  _resident(bias.shape)],
        out_specs=pl.BlockSpec((rs, q.shape[1]), cur),
        out_shape=jax.ShapeDtypeStruct((rows, q.shape[1]), BF16),
        compiler_params=_params("parallel", "arbitrary"),
        name="swa",
    )(sinks, q, k, k, k, v, v, v, bias)


def _tri_masks():
    row = lax.broadcasted_iota(jnp.int32, (CHUNK, CHUNK), 0)
    col = lax.broadcasted_iota(jnp.int32, (CHUNK, CHUNK), 1)
    return row >= col, row > col, row == col


def _deltanet_body(act_ref, z_ref, gates_ref, alog_ref, dtb_ref, nw_ref, o_ref, s_ref, *,
                   chunks):
    nh, dh = DN_HEADS, DN_HEAD_DIM

    @pl.when(pl.program_id(1) == 0)
    def _():
        s_ref[...] = jnp.zeros_like(s_ref)

    gt = gates_ref[...]
    beta_full = jax.nn.sigmoid(gt)
    g_full = -jnp.exp(alog_ref[...]) * _softplus(gt + dtb_ref[...])
    incl, strict, diag = _tri_masks()
    eye = diag.astype(F32)
    tri = incl.astype(F32)
    gc_chunks = [_dot(tri, g_full[c * CHUNK:(c + 1) * CHUNK, :], HIGHEST)
                 for c in range(chunks)]
    items = [(c, h) for c in range(chunks) for h in range(nh)]

    neg_a, attn16, qd16, kdt16, rhs16, decay = [], [], [], [], [], []
    for c, h in items:
        r = slice(c * CHUNK, (c + 1) * CHUNK)
        q = act_ref[r, h * dh:(h + 1) * dh].astype(F32)
        k = act_ref[r, DN_DIM + h * dh:DN_DIM + (h + 1) * dh].astype(F32)
        v = act_ref[r, 2 * DN_DIM + h * dh:2 * DN_DIM + (h + 1) * dh].astype(F32)
        q = q * lax.rsqrt(jnp.sum(q * q, axis=-1, keepdims=True) + 1e-6) * dh ** -0.5
        k = k * lax.rsqrt(jnp.sum(k * k, axis=-1, keepdims=True) + 1e-6)
        beta = jnp.broadcast_to(beta_full[r, h:h + 1], (CHUNK, dh))
        gc = jnp.broadcast_to(gc_chunks[c][:, nh + h:nh + h + 1], (CHUNK, dh))
        gc_row = jnp.transpose(gc)[:CHUNK, :]
        gamma = jnp.where(incl, jnp.exp(jnp.where(incl, gc[:, :CHUNK] - gc_row, 0.0)), 0.0)
        k_beta = k * beta
        k16 = k.astype(BF16)
        neg_a.append(jnp.where(strict, -(_dot_nt(k_beta.astype(BF16), k16) * gamma), 0.0))
        attn16.append(jnp.where(incl, _dot_nt(q.astype(BF16), k16) * gamma, 0.0).astype(BF16))
        gc_last = gc[CHUNK - 1:CHUNK, :]
        qd16.append((q * jnp.exp(gc)).astype(BF16))
        kdt16.append(jnp.transpose(k * jnp.exp(gc_last - gc)).astype(BF16))
        rhs16.append(jnp.concatenate([v * beta, k_beta * jnp.exp(gc)], axis=1).astype(BF16))
        decay.append(jnp.exp(gc_last))

    p = neg_a
    t = [eye + x for x in p]
    for _ in range(int(math.log2(CHUNK)) - 1):
        p16 = [x.astype(BF16) for x in p]
        p = [_dot(x, x) for x in p16]
        t = [y + _dot(y.astype(BF16), x.astype(BF16)) for y, x in zip(t, p)]
    uw = [_dot(y.astype(BF16), b) for y, b in zip(t, rhs16)]

    nw = nw_ref[...]
    state = [s_ref[h] for h in range(nh)]
    for c in range(chunks):
        r = slice(c * CHUNK, (c + 1) * CHUNK)
        pair = [c * nh + h for h in range(nh)]
        s16 = [s.astype(BF16) for s in state]
        ws = [_dot(uw[i][:, dh:].astype(BF16), s16[h]) for h, i in enumerate(pair)]
        qs = [_dot(qd16[i], s16[h]) for h, i in enumerate(pair)]
        vn16 = [(uw[i][:, :dh] - ws[h]).astype(BF16) for h, i in enumerate(pair)]
        out = [qs[h] + _dot(attn16[i], vn16[h]) for h, i in enumerate(pair)]
        state = [state[h] * decay[i] + _dot(kdt16[i], vn16[h]) for h, i in enumerate(pair)]
        for h in range(nh):
            hc = slice(h * dh, (h + 1) * dh)
            o = _rms(out[h], nw) * _silu(z_ref[r, hc].astype(F32))
            o_ref[r, hc] = o.astype(o_ref.dtype)
    for h in range(nh):
        s_ref[h] = state[h]


def _deltanet(act, z, gates, alog_vec, dtb_vec, nw, batch, lp, chunks):
    rows = act.shape[0]
    rs = chunks * CHUNK
    steps = lp // rs
    idx = lambda b, s: (b * steps + s, 0)
    return pl.pallas_call(
        functools.partial(_deltanet_body, chunks=chunks),
        grid=(batch, steps),
        in_specs=[pl.BlockSpec((rs, act.shape[1]), idx), pl.BlockSpec((rs, z.shape[1]), idx),
                  pl.BlockSpec((rs, gates.shape[1]), idx),
                  _resident(alog_vec.shape), _resident(dtb_vec.shape), _resident(nw.shape)],
        out_specs=pl.BlockSpec((rs, DN_DIM), idx),
        out_shape=jax.ShapeDtypeStruct((rows, DN_DIM), BF16),
        scratch_shapes=[pltpu.VMEM((DN_HEADS, DN_HEAD_DIM, DN_HEAD_DIM), F32)],
        compiler_params=_params("parallel", "arbitrary"),
        name="deltanet",
    )(act, z, gates, alog_vec, dtb_vec, nw)


def _gla_body(q_ref, k_ref, v_ref, g_ref, gk_ref, wgu_ref, bg_ref, nw_ref, o_ref, s_ref, *,
              chunks):
    nh = GLA_HEADS
    dk = q_ref.shape[1] // nh
    dv = v_ref.shape[1] // nh

    @pl.when(pl.program_id(1) == 0)
    def _():
        s_ref[...] = jnp.zeros_like(s_ref)

    logits = _dot(gk_ref[...].astype(BF16), wgu_ref[...]) + bg_ref[...]
    glog = (jnp.minimum(logits, 0.0) - jnp.log(1.0 + jnp.exp(-jnp.abs(logits)))) / GLA_GATE_NORM
    incl, _, _ = _tri_masks()
    tri = incl.astype(F32)
    bcum = [_dot(tri, glog[c * CHUNK:(c + 1) * CHUNK, :], HIGHEST) for c in range(chunks)]

    items = [(c, h) for c in range(chunks) for h in range(nh)]
    qd16, kn16, kdt16, decay = [], [], [], []
    for c, h in items:
        r = slice(c * CHUNK, (c + 1) * CHUNK)
        kc = slice(h * dk, (h + 1) * dk)
        bc = bcum[c][:, kc]
        q = q_ref[r, kc].astype(F32) * dk ** -0.5
        k = k_ref[r, kc].astype(F32)
        b_last = bc[CHUNK - 1:CHUNK, :]
        qd16.append((q * jnp.exp(bc)).astype(BF16))
        kn16.append((k * jnp.exp(-bc)).astype(BF16))
        kdt16.append(jnp.transpose(k * jnp.exp(b_last - bc)).astype(BF16))
        decay_rows = jnp.broadcast_to(jnp.exp(b_last), (8, dk))
        decay.append(jnp.transpose(decay_rows)[:, :1])
    values = [v_ref[c * CHUNK:(c + 1) * CHUNK, h * dv:(h + 1) * dv] for c, h in items]
    attn16 = [jnp.where(incl, _dot_nt(a, b), 0.0).astype(BF16) for a, b in zip(qd16, kn16)]
    intra = [_dot(a, v) for a, v in zip(attn16, values)]
    update = [_dot(kt, v) for kt, v in zip(kdt16, values)]

    nw = nw_ref[...]
    state = [s_ref[h] for h in range(nh)]
    for c in range(chunks):
        r = slice(c * CHUNK, (c + 1) * CHUNK)
        for h in range(nh):
            i = c * nh + h
            vc = slice(h * dv, (h + 1) * dv)
            o = intra[i] + _dot(qd16[i], state[h].astype(BF16))
            state[h] = state[h] * decay[i] + update[i]
            o = _rms(o, nw) * _silu(g_ref[r, vc].astype(F32))
            o_ref[r, vc] = o.astype(o_ref.dtype)
    for h in range(nh):
        s_ref[h] = state[h]


def _gla(q, k, v, g, gk, wgu, bg, nw, batch, lp, chunks):
    rows = q.shape[0]
    rs = chunks * CHUNK
    steps = lp // rs
    idx = lambda b, s: (b * steps + s, 0)
    dk = q.shape[1] // GLA_HEADS
    dv = v.shape[1] // GLA_HEADS
    return pl.pallas_call(
        functools.partial(_gla_body, chunks=chunks),
        grid=(batch, steps),
        in_specs=[pl.BlockSpec((rs, q.shape[1]), idx), pl.BlockSpec((rs, k.shape[1]), idx),
                  pl.BlockSpec((rs, v.shape[1]), idx), pl.BlockSpec((rs, g.shape[1]), idx),
                  pl.BlockSpec((rs, gk.shape[1]), idx), _resident(wgu.shape),
                  _resident(bg.shape), _resident(nw.shape)],
        out_specs=pl.BlockSpec((rs, v.shape[1]), idx),
        out_shape=jax.ShapeDtypeStruct((rows, v.shape[1]), BF16),
        scratch_shapes=[pltpu.VMEM((GLA_HEADS, dk, dv), F32)],
        compiler_params=_params("parallel", "arbitrary"),
        name="gla",
    )(q, k, v, g, gk, wgu, bg, nw)


def _pad_cols(w, width):
    return jnp.pad(w, ((0, 0), (0, width - w.shape[1])))


def _per_step(n, preferred):
    for c in preferred:
        if n % c == 0:
            return c
    return 1


def kernel(x, meta_tokens, norm_w, ffn_w_gate, ffn_w_up, ffn_w_down, rel_bias_table,
           even_w_in, even_conv_w, swa_sinks, dn_a_log, dn_dt_bias, dn_norm_w, even_w_out,
           odd_w_in, gla_w_gate_up, gla_b_gate, gla_norm_w, odd_w_out):
    batch, seq, d = x.shape
    depth = norm_w.shape[0]
    lp = SWA_BLOCK + seq
    assert seq % ROW_TILE == 0 and (batch * lp) % ROW_TILE == 0
    rows = batch * lp
    chunks = _per_step(lp // CHUNK, (6, 3, 2))
    swa_blocks = _per_step(lp // SWA_BLOCK, (3, 2))

    meta_block = jnp.pad(meta_tokens.astype(x.dtype), ((PAD, 0), (0, 0)))
    norm4 = norm_w[:, :, None, :]
    wg = ffn_w_gate.astype(BF16)
    wu = ffn_w_up.astype(BF16)
    wd = ffn_w_down.astype(BF16)

    swa_q = SWA_Q_HEADS * SWA_HEAD_DIM
    swa_kv = SWA_KV_HEADS * SWA_HEAD_DIM
    bias = None
    h = x.reshape(batch * seq, d)
    for layer in range(depth):
        i = layer // 2
        embed = meta_block if layer == 0 else None
        final = layer == depth - 1
        if layer % 2 == 0:
            assert even_w_in.shape[2] == swa_q + 2 * swa_kv + 4 * DN_DIM + 2 * DN_HEADS
            widths = (swa_q, swa_kv, swa_kv, 3 * DN_DIM, DN_DIM, LANE)
            w_in = _pad_cols(even_w_in[i], sum(widths)).astype(BF16)
            h, qa, ka, va, act_b, z_b, gates = _ffn_proj(
                h, embed, norm4, layer, wg, wu, wd, w_in, widths,
                (BF16, BF16, BF16, BF16, BF16, F32), rows, lp, seq,
                conv_out=3, conv_w=even_conv_w[i])
            if bias is None:
                bias = _swa_bias(rel_bias_table)
            o_a = _swa(qa, ka, va, bias, swa_sinks[i], batch, lp, swa_blocks)
            lane_pad = (DN_HEADS, LANE - 2 * DN_HEADS)
            alog_vec = jnp.pad(dn_a_log[i], lane_pad)[None, :]
            dtb_vec = jnp.pad(dn_dt_bias[i], lane_pad)[None, :]
            o_b = _deltanet(act_b, z_b, gates, alog_vec, dtb_vec, dn_norm_w[i][None, :],
                            batch, lp, chunks)
            mixed, w_out = [o_a, o_b], even_w_out[i]
        else:
            key_dim = gla_w_gate_up.shape[2]
            val_dim = odd_w_out.shape[1]
            widths = (key_dim, key_dim, val_dim, val_dim, LANE)
            w_in = _pad_cols(odd_w_in[i], sum(widths)).astype(BF16)
            h, q, k, v, g, gk = _ffn_proj(
                h, embed, norm4, layer, wg, wu, wd, w_in, widths,
                (BF16, BF16, BF16, BF16, F32), rows, lp, seq)
            wgu = jnp.pad(gla_w_gate_up[i], ((0, LANE - GLA_GATE_RANK), (0, 0))).astype(BF16)
            o = _gla(q, k, v, g, gk, wgu, gla_b_gate[i][None, :], gla_norm_w[i][None, :],
                     batch, lp, chunks)
            mixed, w_out = [o], odd_w_out[i]
        h = _proj_ffn(mixed, h, w_out.astype(BF16), norm4, layer, wg, wu, wd, final,
                      batch, lp, seq)
    return h.reshape(batch, seq, d)
```

```python
import functools
import math

import numpy as np
import jax
import jax.numpy as jnp
from jax import lax
from jax.experimental import pallas as pl
from jax.experimental.pallas import tpu as pltpu

F32 = jnp.float32
BF16 = jnp.bfloat16
HIGHEST = lax.Precision.HIGHEST

N_META = 16
NORM_EPS = 1e-6
NEG_INF = -1e30
SWA_Q_HEADS = 8
SWA_KV_HEADS = 2
SWA_HEAD_DIM = 64
SWA_BLOCK = 128
REL_BUCKETS = 32
REL_MAX_DIST = 128
DN_HEADS = 4
DN_HEAD_DIM = 128
DN_DIM = DN_HEADS * DN_HEAD_DIM
DN_CONV = 4
GLA_HEADS = 4
GLA_GATE_RANK = 16
GLA_GATE_NORM = 16.0
CHUNK = 64

PAD = SWA_BLOCK - N_META
LANE = 128
ROW_TILE = 512
FF_TILE = 256
VMEM_LIMIT = 56 * 1024 * 1024


def _rms(x, w):
    return x * lax.rsqrt(jnp.mean(x * x, axis=-1, keepdims=True) + NORM_EPS) * w


def _silu(x):
    return x * jax.nn.sigmoid(x)


def _softplus(x):
    return jnp.maximum(x, 0.0) + jnp.log(1.0 + jnp.exp(-jnp.abs(x)))


def _dot(a, b, precision=None):
    return jnp.dot(a, b, preferred_element_type=F32, precision=precision)


def _dot_nt(a, b, precision=None):
    return lax.dot_general(a, b, (((1,), (1,)), ((), ())), preferred_element_type=F32,
                           precision=precision)


def _dot_tn(a, b, precision=None):
    return lax.dot_general(a, b, (((0,), (0,)), ((), ())), preferred_element_type=F32,
                           precision=precision)


def _resident(shape):
    nd = len(shape)
    return pl.BlockSpec(shape, lambda *_: (0,) * nd, pipeline_mode=pl.Buffered(1))


def _params(*semantics):
    return pltpu.CompilerParams(dimension_semantics=semantics, vmem_limit_bytes=VMEM_LIMIT)


SUB = ROW_TILE // SWA_BLOCK


def _pick(arr, *lead):
    rest = arr.shape[len(lead):]
    index = tuple(lead) + (0,) * len(rest)
    return pl.BlockSpec((None,) * len(lead) + rest, lambda i: index,
                        pipeline_mode=pl.Buffered(1))


def _tile_specs(width):
    return [pl.BlockSpec((ROW_TILE, width), lambda i: (i, 0))]


def _real_specs(width, lp, seq):
    tiles = seq // ROW_TILE
    blocks = lp // SWA_BLOCK

    def spec(j):
        return pl.BlockSpec((SWA_BLOCK, width),
                            lambda i: ((i // tiles) * blocks + 1 + (i % tiles) * SUB + j, 0))
    return [spec(j) for j in range(SUB)]


def _embed_specs(width, lp, seq):
    blocks = lp // SWA_BLOCK
    x_blocks = seq // SWA_BLOCK

    def spec(j):
        def index(i):
            blk = i * SUB + j
            return ((blk // blocks) * x_blocks + jnp.maximum(blk % blocks - 1, 0), 0)
        return pl.BlockSpec((SWA_BLOCK, width), index)
    return [spec(j) for j in range(SUB)]


def _load_rows(refs):
    if len(refs) == 1:
        return refs[0][...]
    return jnp.concatenate([r[...] for r in refs], axis=0)


def _ffn_core(x, nwa, nwb, wg_ref, wu_ref, wd_ref, a_ref):
    hn = _rms(x, nwa).astype(BF16)
    d_ff = wg_ref.shape[1]
    for c in range(d_ff // FF_TILE):
        cols = slice(c * FF_TILE, (c + 1) * FF_TILE)
        g = _dot(hn, wg_ref[:, cols])
        u = _dot(hn, wu_ref[:, cols])
        a_ref[:, cols] = (_silu(g) * u).astype(BF16)
    f = _dot(a_ref[...], wd_ref[...])
    return x + 0.5 * _rms(f, nwb)


CONV_HALO = 8


def _ffn_proj_body(*refs, n_rows, embed_blocks, n_out, conv_out, lp):
    row_refs = refs[:n_rows]
    refs = refs[n_rows:]
    if embed_blocks:
        meta_ref, refs = refs[0], refs[1:]
        first = pl.program_id(0) * SUB
        x = jnp.concatenate(
            [jnp.where((first + j) % embed_blocks == 0, meta_ref[...], r[...])
             for j, r in enumerate(row_refs)], axis=0)
    else:
        x = _load_rows(row_refs)
    if conv_out is not None:
        convw_ref, refs = refs[0], refs[1:]
    nwa_ref, nwb_ref, wg_ref, wu_ref, wd_ref, nwm_ref, win_ref = refs[:7]
    h_ref = refs[7]
    out_refs = refs[8:8 + n_out]
    a_ref = refs[8 + n_out]
    h = _ffn_core(x, nwa_ref[...], nwb_ref[...], wg_ref, wu_ref, wd_ref, a_ref)
    h_ref[...] = h
    hn = _rms(h, nwm_ref[...]).astype(BF16)
    offsets = np.cumsum([0] + [o.shape[1] for o in out_refs])

    def project(i):
        return _dot(hn, win_ref[:, offsets[i]:offsets[i + 1]])

    if conv_out is not None:
        xpad_ref = refs[9 + n_out]
        taps = convw_ref.shape[0]

        @pl.when(pl.program_id(0) == 0)
        def _():
            xpad_ref[0:CONV_HALO, :] = jnp.zeros((CONV_HALO, xpad_ref.shape[1]), F32)

        xpad_ref[CONV_HALO:, :] = project(conv_out)
        cw = convw_ref[...]
        y = cw[taps - 1:taps, :] * xpad_ref[CONV_HALO:, :]
        for j in range(taps - 1):
            lo = CONV_HALO - (taps - 1 - j)
            y = y + cw[j:j + 1, :] * xpad_ref[lo:lo + ROW_TILE, :]
        tail = xpad_ref[ROW_TILE:, :]
        row = (pl.program_id(0) * ROW_TILE) % lp + lax.broadcasted_iota(
            jnp.int32, (ROW_TILE, 1), 0)
        row = jnp.where(row >= lp, row - lp, row)
        out_refs[conv_out][...] = jnp.where(row >= PAD, _silu(y), 0.0).astype(
            out_refs[conv_out].dtype)
        xpad_ref[0:CONV_HALO, :] = tail
    for i, o_ref in enumerate(out_refs):
        if i != conv_out:
            o_ref[...] = project(i).astype(o_ref.dtype)


def _ffn_proj(src, meta_block, norm4, layer, wg, wu, wd, w_in, widths, dtypes, rows, lp, seq,
              conv_out=None, conv_w=None):
    d = norm4.shape[-1]
    d_ff = wg.shape[-1]
    assert sum(widths) == w_in.shape[1] and lp > ROW_TILE
    if meta_block is None:
        row_specs, extra, extra_specs, embed_blocks = _tile_specs(d), [], [], 0
    else:
        row_specs = _embed_specs(d, lp, seq)
        extra, extra_specs, embed_blocks = [meta_block], [_resident(meta_block.shape)], (
            lp // SWA_BLOCK)
    scratch = [pltpu.VMEM((ROW_TILE, d_ff), BF16)]
    if conv_out is not None:
        extra, extra_specs = extra + [conv_w], extra_specs + [_resident(conv_w.shape)]
        scratch.append(pltpu.VMEM((CONV_HALO + ROW_TILE, widths[conv_out]), F32))
    tile = lambda n: pl.BlockSpec((ROW_TILE, n), lambda i: (i, 0))
    return pl.pallas_call(
        functools.partial(_ffn_proj_body, n_rows=len(row_specs), embed_blocks=embed_blocks,
                          n_out=len(widths), conv_out=conv_out, lp=lp),
        grid=(rows // ROW_TILE,),
        in_specs=row_specs + extra_specs
        + [_pick(norm4, layer, 0), _pick(norm4, layer, 1), _pick(wg, layer, 0),
           _pick(wu, layer, 0), _pick(wd, layer, 0), _pick(norm4, layer, 2),
           _resident(w_in.shape)],
        out_specs=[tile(d)] + [tile(n) for n in widths],
        out_shape=[jax.ShapeDtypeStruct((rows, d), F32)]
        + [jax.ShapeDtypeStruct((rows, n), dt) for n, dt in zip(widths, dtypes)],
        scratch_shapes=scratch,
        compiler_params=_params("arbitrary"),
        name="ffn_proj",
    )(*([src] * len(row_specs)), *extra, norm4, norm4, wg, wu, wd, norm4, w_in)


def _proj_ffn_body(*refs, n_rows, n_x):
    x_groups = [refs[g * n_rows:(g + 1) * n_rows] for g in range(n_x)]
    refs = refs[n_x * n_rows:]
    h_refs = refs[:n_rows]
    wo_ref, nwo_ref, nwa_ref, nwb_ref, wg_ref, wu_ref, wd_ref, o_ref, a_ref = refs[n_rows:]
    acc = None
    off = 0
    for group in x_groups:
        xs = _load_rows(group)
        k = xs.shape[1]
        p = _dot(xs, wo_ref[off:off + k, :])
        acc = p if acc is None else acc + p
        off += k
    h = _load_rows(h_refs) + _rms(acc, nwo_ref[...])
    o_ref[...] = _ffn_core(h, nwa_ref[...], nwb_ref[...], wg_ref, wu_ref, wd_ref, a_ref)


def _proj_ffn(xs, h, w_out, norm4, layer, wg, wu, wd, final, batch, lp, seq):
    d = norm4.shape[-1]
    d_ff = wg.shape[-1]
    if final:
        specs = lambda width: _real_specs(width, lp, seq)
        out_rows = batch * seq
    else:
        specs = _tile_specs
        out_rows = batch * lp
    h_specs = specs(d)
    n_rows = len(h_specs)
    in_specs, args = [], []
    for x in xs:
        in_specs += specs(x.shape[1])
        args += [x] * n_rows
    in_specs += h_specs + [_resident(w_out.shape), _pick(norm4, layer, 3),
                           _pick(norm4, layer, 4), _pick(norm4, layer, 5),
                           _pick(wg, layer, 1), _pick(wu, layer, 1), _pick(wd, layer, 1)]
    args += [h] * n_rows + [w_out, norm4, norm4, norm4, wg, wu, wd]
    return pl.pallas_call(
        functools.partial(_proj_ffn_body, n_rows=n_rows, n_x=len(xs)),
        grid=(out_rows // ROW_TILE,),
        in_specs=in_specs,
        out_specs=pl.BlockSpec((ROW_TILE, d), lambda i: (i, 0)),
        out_shape=jax.ShapeDtypeStruct((out_rows, d), F32),
        scratch_shapes=[pltpu.VMEM((ROW_TILE, d_ff), BF16)],
        compiler_params=_params("parallel"),
        name="proj_ffn",
    )(*args)


def _t5_bucket_np(rel):
    n = np.maximum(rel, 0)
    max_exact = REL_BUCKETS // 2
    n_f = np.maximum(n, 1).astype(np.float32)
    large = max_exact + (np.log(n_f / np.float32(max_exact))
                         / np.float32(math.log(REL_MAX_DIST / max_exact))
                         * np.float32(REL_BUCKETS - max_exact)).astype(np.int32)
    large = np.minimum(large, REL_BUCKETS - 1)
    return np.where(n < max_exact, n, large).astype(np.int32)


def _swa_bucket_index():
    i = np.arange(SWA_BLOCK)[:, None]
    c = np.arange(SWA_BLOCK)[None, :]
    out = np.full((3, SWA_BLOCK, 2 * SWA_BLOCK), -1, np.int32)
    for variant in range(3):
        pos_q = variant * SWA_BLOCK + i - PAD
        pos_k = variant * SWA_BLOCK + c - PAD - np.where(c > i, SWA_BLOCK, 0)
        rel_b = pos_q - pos_k
        assert ((rel_b >= 0) & (rel_b < SWA_BLOCK)).all()
        out[variant, :, :SWA_BLOCK] = np.where(pos_k >= N_META, _t5_bucket_np(rel_b), -1)
        rel_m = pos_q - c
        meta = (c < N_META) & (rel_m >= 0)
        out[variant, :, SWA_BLOCK:] = np.where(meta, _t5_bucket_np(rel_m), -1)
    return out


_SWA_BUCKET_INDEX = _swa_bucket_index()


def _bias_body(tbl_ref, idx_ref, o_ref):
    head = pl.program_id(1)
    idx = idx_ref[0]
    acc = jnp.full(idx.shape, NEG_INF, F32)
    for b in range(REL_BUCKETS):
        acc = jnp.where(idx == b, tbl_ref[b, head], acc)
    o_ref[0, 0] = acc


def _swa_bias(rel_table):
    idx = jnp.asarray(_SWA_BUCKET_INDEX)
    group = SWA_Q_HEADS // SWA_KV_HEADS
    width = idx.shape[2]
    return pl.pallas_call(
        _bias_body,
        grid=(3, SWA_Q_HEADS),
        in_specs=[pl.BlockSpec(memory_space=pltpu.SMEM),
                  pl.BlockSpec((1, SWA_BLOCK, width), lambda v, h: (v, 0, 0))],
        out_specs=pl.BlockSpec((1, 1, SWA_BLOCK, width),
                               lambda v, h: (v, h // group, h % group, 0)),
        out_shape=jax.ShapeDtypeStruct((3, SWA_KV_HEADS, group * SWA_BLOCK, width), F32),
        compiler_params=_params("arbitrary", "arbitrary"),
        name="swa_bias",
    )(rel_table, idx)


def _swa_body(sink_ref, q_ref, kc_ref, kp_ref, km_ref, vc_ref, vp_ref, vm_ref, bias_ref, o_ref,
              *, blocks):
    dh = SWA_HEAD_DIM
    blk = SWA_BLOCK
    group = SWA_Q_HEADS // SWA_KV_HEADS
    scale = dh ** -0.5
    step = pl.program_id(1)
    zpad = jnp.zeros((blk - N_META, dh), BF16)
    row = lax.broadcasted_iota(jnp.int32, (group * blk, blk), 0) % blk
    col = lax.broadcasted_iota(jnp.int32, (group * blk, blk), 1)
    from_prev = col > row

    ones_col = (lax.broadcasted_iota(jnp.int32, (3 * blk, dh), 1) == 0).astype(BF16)

    units = [(j, hk) for j in range(blocks) for hk in range(SWA_KV_HEADS)]
    scores, values = [], []
    for j, hk in units:
        r = slice(j * blk, (j + 1) * blk)
        kv = slice(hk * dh, (hk + 1) * dh)
        k_prev = kp_ref[:, kv] if j == 0 else kc_ref[(j - 1) * blk:j * blk, kv]
        v_prev = vp_ref[:, kv] if j == 0 else vc_ref[(j - 1) * blk:j * blk, kv]
        k_all = jnp.concatenate([k_prev, kc_ref[r, kv], km_ref[:, kv], zpad], axis=0)
        v_all = jnp.concatenate([v_prev, vc_ref[r, kv], vm_ref[:, kv], zpad], axis=0)
        values.append(jnp.concatenate([v_all, ones_col], axis=1))
        q = jnp.concatenate([q_ref[r, (hk * group + g) * dh:(hk * group + g + 1) * dh]
                             for g in range(group)], axis=0) * scale
        scores.append(_dot_nt(q, k_all))

    probs, sink_terms = [], []
    for (j, hk), s in zip(units, scores):
        bias = bias_ref[jnp.minimum(step * blocks + j, 2), hk]
        s_band = jnp.where(from_prev, s[:, :blk], s[:, blk:2 * blk]) + bias[:, :blk]
        s_meta = s[:, 2 * blk:] + bias[:, blk:]
        sink = jnp.concatenate([jnp.full((blk, 1), sink_ref[hk * group + g], F32)
                                for g in range(group)], axis=0)
        m = jnp.maximum(jnp.max(jnp.maximum(s_band, s_meta), axis=-1, keepdims=True), sink)
        e_band = jnp.exp(s_band - m)
        e_meta = jnp.exp(s_meta - m)
        sink_terms.append(jnp.exp(sink - m))
        probs.append(jnp.concatenate([jnp.where(from_prev, e_band, 0.0),
                                      jnp.where(from_prev, 0.0, e_band), e_meta],
                                     axis=1).astype(BF16))

    for (j, hk), p, v_ext, sink_term in zip(units, probs, values, sink_terms):
        pv = _dot(p, v_ext)
        o = pv[:, :dh] / (pv[:, dh:dh + 1] + sink_term)
        for g in range(group):
            hq = hk * group + g
            o_ref[j * blk:(j + 1) * blk, hq * dh:(hq + 1) * dh] = (
                o[g * blk:(g + 1) * blk].astype(o_ref.dtype))


def _swa(q, k, v, bias, sinks, batch, lp, blocks):
    rows = q.shape[0]
    nb = lp // SWA_BLOCK
    steps = nb // blocks
    meta_blocks = lp // N_META
    cur = lambda b, s: (b * steps + s, 0)
    prev = lambda b, s: (b * nb + jnp.maximum(s * blocks - 1, 0), 0)
    meta = lambda b, s: (b * meta_blocks + PAD // N_META, 0)
    kv_w = k.shape[1]
    rs = blocks * SWA_BLOCK
    return pl.pallas_call(
        functools.partial(_swa_body, blocks=blocks),
        grid=(batch, steps),
        in_specs=[pl.BlockSpec(memory_space=pltpu.SMEM),
                  pl.BlockSpec((rs, q.shape[1]), cur),
                  pl.BlockSpec((rs, kv_w), cur), pl.BlockSpec((SWA_BLOCK, kv_w), prev),
                  pl.BlockSpec((N_META, kv_w), meta),
                  pl.BlockSpec((rs, kv_w), cur), pl.BlockSpec((SWA_BLOCK, kv_w), prev),
                  pl.BlockSpec((N_META, kv_w), meta),
                  _resident(bias.shape)],
        out_specs=pl.BlockSpec((rs, q.shape[1]), cur),
        out_shape=jax.ShapeDtypeStruct((rows, q.shape[1]), BF16),
        compiler_params=_params("parallel", "arbitrary"),
        name="swa",
    )(sinks, q, k, k, k, v, v, v, bias)


def _tri_masks():
    row = lax.broadcasted_iota(jnp.int32, (CHUNK, CHUNK), 0)
    col = lax.broadcasted_iota(jnp.int32, (CHUNK, CHUNK), 1)
    return row >= col, row > col, row == col


def _deltanet_body(act_ref, z_ref, gates_ref, alog_ref, dtb_ref, nw_ref, o_ref, s_ref, *,
                   chunks):
    nh, dh = DN_HEADS, DN_HEAD_DIM

    @pl.when(pl.program_id(1) == 0)
    def _():
        s_ref[...] = jnp.zeros_like(s_ref)

    gt = gates_ref[...]
    beta_full = jax.nn.sigmoid(gt)
    g_full = -jnp.exp(alog_ref[...]) * _softplus(gt + dtb_ref[...])
    incl, strict, diag = _tri_masks()
    eye = diag.astype(F32)
    tri = incl.astype(F32)
    gc_chunks = [_dot(tri, g_full[c * CHUNK:(c + 1) * CHUNK, :], HIGHEST)
                 for c in range(chunks)]
    items = [(c, h) for c in range(chunks) for h in range(nh)]

    neg_a, attn16, qd16, kdt16, rhs16, decay = [], [], [], [], [], []
    for c, h in items:
        r = slice(c * CHUNK, (c + 1) * CHUNK)
        q = act_ref[r, h * dh:(h + 1) * dh].astype(F32)
        k = act_ref[r, DN_DIM + h * dh:DN_DIM + (h + 1) * dh].astype(F32)
        v = act_ref[r, 2 * DN_DIM + h * dh:2 * DN_DIM + (h + 1) * dh].astype(F32)
        q = q * lax.rsqrt(jnp.sum(q * q, axis=-1, keepdims=True) + 1e-6) * dh ** -0.5
        k = k * lax.rsqrt(jnp.sum(k * k, axis=-1, keepdims=True) + 1e-6)
        beta = jnp.broadcast_to(beta_full[r, h:h + 1], (CHUNK, dh))
        gc = jnp.broadcast_to(gc_chunks[c][:, nh + h:nh + h + 1], (CHUNK, dh))
        gc_row = jnp.transpose(gc)[:CHUNK, :]
        gamma = jnp.where(incl, jnp.exp(jnp.where(incl, gc[:, :CHUNK] - gc_row, 0.0)), 0.0)
        k_beta = k * beta
        k16 = k.astype(BF16)
        neg_a.append(jnp.where(strict, -(_dot_nt(k_beta.astype(BF16), k16) * gamma), 0.0))
        attn16.append(jnp.where(incl, _dot_nt(q.astype(BF16), k16) * gamma, 0.0).astype(BF16))
        gc_last = gc[CHUNK - 1:CHUNK, :]
        qd16.append((q * jnp.exp(gc)).astype(BF16))
        kdt16.append(jnp.transpose(k * jnp.exp(gc_last - gc)).astype(BF16))
        rhs16.append(jnp.concatenate([v * beta, k_beta * jnp.exp(gc)], axis=1).astype(BF16))
        decay.append(jnp.exp(gc_last))

    p = neg_a
    t = [eye + x for x in p]
    for _ in range(int(math.log2(CHUNK)) - 1):
        p16 = [x.astype(BF16) for x in p]
        p = [_dot(x, x) for x in p16]
        t = [y + _dot(y.astype(BF16), x.astype(BF16)) for y, x in zip(t, p)]
    uw = [_dot(y.astype(BF16), b) for y, b in zip(t, rhs16)]

    nw = nw_ref[...]
    state = [s_ref[h] for h in range(nh)]
    for c in range(chunks):
        r = slice(c * CHUNK, (c + 1) * CHUNK)
        pair = [c * nh + h for h in range(nh)]
        s16 = [s.astype(BF16) for s in state]
        ws = [_dot(uw[i][:, dh:].astype(BF16), s16[h]) for h, i in enumerate(pair)]
        qs = [_dot(qd16[i], s16[h]) for h, i in enumerate(pair)]
        vn16 = [(uw[i][:, :dh] - ws[h]).astype(BF16) for h, i in enumerate(pair)]
        out = [qs[h] + _dot(attn16[i], vn16[h]) for h, i in enumerate(pair)]
        state = [state[h] * decay[i] + _dot(kdt16[i], vn16[h]) for h, i in enumerate(pair)]
        for h in range(nh):
            hc = slice(h * dh, (h + 1) * dh)
            o = _rms(out[h], nw) * _silu(z_ref[r, hc].astype(F32))
            o_ref[r, hc] = o.astype(o_ref.dtype)
    for h in range(nh):
        s_ref[h] = state[h]


def _deltanet(act, z, gates, alog_vec, dtb_vec, nw, batch, lp, chunks):
    rows = act.shape[0]
    rs = chunks * CHUNK
    steps = lp // rs
    idx = lambda b, s: (b * steps + s, 0)
    return pl.pallas_call(
        functools.partial(_deltanet_body, chunks=chunks),
        grid=(batch, steps),
        in_specs=[pl.BlockSpec((rs, act.shape[1]), idx), pl.BlockSpec((rs, z.shape[1]), idx),
                  pl.BlockSpec((rs, gates.shape[1]), idx),
                  _resident(alog_vec.shape), _resident(dtb_vec.shape), _resident(nw.shape)],
        out_specs=pl.BlockSpec((rs, DN_DIM), idx),
        out_shape=jax.ShapeDtypeStruct((rows, DN_DIM), BF16),
        scratch_shapes=[pltpu.VMEM((DN_HEADS, DN_HEAD_DIM, DN_HEAD_DIM), F32)],
        compiler_params=_params("parallel", "arbitrary"),
        name="deltanet",
    )(act, z, gates, alog_vec, dtb_vec, nw)


def _gla_body(q_ref, k_ref, v_ref, g_ref, gk_ref, wgu_ref, bg_ref, nw_ref, o_ref, s_ref, *,
              chunks):
    nh = GLA_HEADS
    dk = q_ref.shape[1] // nh
    dv = v_ref.shape[1] // nh

    @pl.when(pl.program_id(1) == 0)
    def _():
        s_ref[...] = jnp.zeros_like(s_ref)

    logits = _dot(gk_ref[...].astype(BF16), wgu_ref[...]) + bg_ref[...]
    glog = (jnp.minimum(logits, 0.0) - jnp.log(1.0 + jnp.exp(-jnp.abs(logits)))) / GLA_GATE_NORM
    incl, _, _ = _tri_masks()
    tri = incl.astype(F32)
    bcum = [_dot(tri, glog[c * CHUNK:(c + 1) * CHUNK, :], HIGHEST) for c in range(chunks)]

    items = [(c, h) for c in range(chunks) for h in range(nh)]
    qd16, kn16, kdt16, decay = [], [], [], []
    for c, h in items:
        r = slice(c * CHUNK, (c + 1) * CHUNK)
        kc = slice(h * dk, (h + 1) * dk)
        bc = bcum[c][:, kc]
        q = q_ref[r, kc].astype(F32) * dk ** -0.5
        k = k_ref[r, kc].astype(F32)
        b_last = bc[CHUNK - 1:CHUNK, :]
        qd16.append((q * jnp.exp(bc)).astype(BF16))
        kn16.append((k * jnp.exp(-bc)).astype(BF16))
        kdt16.append(jnp.transpose(k * jnp.exp(b_last - bc)).astype(BF16))
        decay_rows = jnp.broadcast_to(jnp.exp(b_last), (8, dk))
        decay.append(jnp.transpose(decay_rows)[:, :1])
    values = [v_ref[c * CHUNK:(c + 1) * CHUNK, h * dv:(h + 1) * dv] for c, h in items]
    attn16 = [jnp.where(incl, _dot_nt(a, b), 0.0).astype(BF16) for a, b in zip(qd16, kn16)]
    intra = [_dot(a, v) for a, v in zip(attn16, values)]
    update = [_dot(kt, v) for kt, v in zip(kdt16, values)]

    nw = nw_ref[...]
    state = [s_ref[h] for h in range(nh)]
    for c in range(chunks):
        r = slice(c * CHUNK, (c + 1) * CHUNK)
        for h in range(nh):
            i = c * nh + h
            vc = slice(h * dv, (h + 1) * dv)
            o = intra[i] + _dot(qd16[i], state[h].astype(BF16))
            state[h] = state[h] * decay[i] + update[i]
            o = _rms(o, nw) * _silu(g_ref[r, vc].astype(F32))
            o_ref[r, vc] = o.astype(o_ref.dtype)
    for h in range(nh):
        s_ref[h] = state[h]


def _gla(q, k, v, g, gk, wgu, bg, nw, batch, lp, chunks):
    rows = q.shape[0]
    rs = chunks * CHUNK
    steps = lp // rs
    idx = lambda b, s: (b * steps + s, 0)
    dk = q.shape[1] // GLA_HEADS
    dv = v.shape[1] // GLA_HEADS
    return pl.pallas_call(
        functools.partial(_gla_body, chunks=chunks),
        grid=(batch, steps),
        in_specs=[pl.BlockSpec((rs, q.shape[1]), idx), pl.BlockSpec((rs, k.shape[1]), idx),
                  pl.BlockSpec((rs, v.shape[1]), idx), pl.BlockSpec((rs, g.shape[1]), idx),
                  pl.BlockSpec((rs, gk.shape[1]), idx), _resident(wgu.shape),
                  _resident(bg.shape), _resident(nw.shape)],
        out_specs=pl.BlockSpec((rs, v.shape[1]), idx),
        out_shape=jax.ShapeDtypeStruct((rows, v.shape[1]), BF16),
        scratch_shapes=[pltpu.VMEM((GLA_HEADS, dk, dv), F32)],
        compiler_params=_params("parallel", "arbitrary"),
        name="gla",
    )(q, k, v, g, gk, wgu, bg, nw)


def _pad_cols(w, width):
    return jnp.pad(w, ((0, 0), (0, width - w.shape[1])))


def _per_step(n, preferred):
    for c in preferred:
        if n % c == 0:
            return c
    return 1


def kernel(x, meta_tokens, norm_w, ffn_w_gate, ffn_w_up, ffn_w_down, rel_bias_table,
           even_w_in, even_conv_w, swa_sinks, dn_a_log, dn_dt_bias, dn_norm_w, even_w_out,
           odd_w_in, gla_w_gate_up, gla_b_gate, gla_norm_w, odd_w_out):
    batch, seq, d = x.shape
    depth = norm_w.shape[0]
    lp = SWA_BLOCK + seq
    assert seq % ROW_TILE == 0 and (batch * lp) % ROW_TILE == 0
    rows = batch * lp
    chunks = _per_step(lp // CHUNK, (6, 3, 2))
    swa_blocks = _per_step(lp // SWA_BLOCK, (3, 2))

    meta_block = jnp.pad(meta_tokens.astype(x.dtype), ((PAD, 0), (0, 0)))
    norm4 = norm_w[:, :, None, :]
    wg = ffn_w_gate.astype(BF16)
    wu = ffn_w_up.astype(BF16)
    wd = ffn_w_down.astype(BF16)

    swa_q = SWA_Q_HEADS * SWA_HEAD_DIM
    swa_kv = SWA_KV_HEADS * SWA_HEAD_DIM
    bias = None
    h = x.reshape(batch * seq, d)
    for layer in range(depth):
        i = layer // 2
        embed = meta_block if layer == 0 else None
        final = layer == depth - 1
        if layer % 2 == 0:
            assert even_w_in.shape[2] == swa_q + 2 * swa_kv + 4 * DN_DIM + 2 * DN_HEADS
            widths = (swa_q, swa_kv, swa_kv, 3 * DN_DIM, DN_DIM, LANE)
            w_in = _pad_cols(even_w_in[i], sum(widths)).astype(BF16)
            h, qa, ka, va, act_b, z_b, gates = _ffn_proj(
                h, embed, norm4, layer, wg, wu, wd, w_in, widths,
                (BF16, BF16, BF16, BF16, BF16, F32), rows, lp, seq,
                conv_out=3, conv_w=even_conv_w[i])
            if bias is None:
                bias = _swa_bias(rel_bias_table)
            o_a = _swa(qa, ka, va, bias, swa_sinks[i], batch, lp, swa_blocks)
            lane_pad = (DN_HEADS, LANE - 2 * DN_HEADS)
            alog_vec = jnp.pad(dn_a_log[i], lane_pad)[None, :]
            dtb_vec = jnp.pad(dn_dt_bias[i], lane_pad)[None, :]
            o_b = _deltanet(act_b, z_b, gates, alog_vec, dtb_vec, dn_norm_w[i][None, :],
                            batch, lp, chunks)
            mixed, w_out = [o_a, o_b], even_w_out[i]
        else:
            key_dim = gla_w_gate_up.shape[2]
            val_dim = odd_w_out.shape[1]
            widths = (key_dim, key_dim, val_dim, val_dim, LANE)
            w_in = _pad_cols(odd_w_in[i], sum(widths)).astype(BF16)
            h, q, k, v, g, gk = _ffn_proj(
                h, embed, norm4, layer, wg, wu, wd, w_in, widths,
                (BF16, BF16, BF16, BF16, F32), rows, lp, seq)
            wgu = jnp.pad(gla_w_gate_up[i], ((0, LANE - GLA_GATE_RANK), (0, 0))).astype(BF16)
            o = _gla(q, k, v, g, gk, wgu, gla_b_gate[i][None, :], gla_norm_w[i][None, :],
                     batch, lp, chunks)
            mixed, w_out = [o], odd_w_out[i]
        h = _proj_ffn(mixed, h, w_out.astype(BF16), norm4, layer, wg, wu, wd, final,
                      batch, lp, seq)
    return h.reshape(batch, seq, d)
```

```python
import functools
import math

import numpy as np
import jax
import jax.numpy as jnp
from jax import lax
from jax.experimental import pallas as pl
from jax.experimental.pallas import tpu as pltpu

F32 = jnp.float32
BF16 = jnp.bfloat16
HIGHEST = lax.Precision.HIGHEST

N_META = 16
NORM_EPS = 1e-6
NEG_INF = -1e30
SWA_Q_HEADS = 8
SWA_KV_HEADS = 2
SWA_HEAD_DIM = 64
SWA_BLOCK = 128
REL_BUCKETS = 32
REL_MAX_DIST = 128
DN_HEADS = 4
DN_HEAD_DIM = 128
DN_DIM = DN_HEADS * DN_HEAD_DIM
DN_CONV = 4
GLA_HEADS = 4
GLA_GATE_RANK = 16
GLA_GATE_NORM = 16.0
CHUNK = 64

PAD = SWA_BLOCK - N_META
LANE = 128
ROW_TILE = 512
FF_TILE = 256
VMEM_LIMIT = 56 * 1024 * 1024


def _rms(x, w):
    return x * lax.rsqrt(jnp.mean(x * x, axis=-1, keepdims=True) + NORM_EPS) * w


def _silu(x):
    return x * jax.nn.sigmoid(x)


def _softplus(x):
    return jnp.maximum(x, 0.0) + jnp.log(1.0 + jnp.exp(-jnp.abs(x)))


def _dot(a, b, precision=None):
    return jnp.dot(a, b, preferred_element_type=F32, precision=precision)


def _dot_nt(a, b, precision=None):
    return lax.dot_general(a, b, (((1,), (1,)), ((), ())), preferred_element_type=F32,
                           precision=precision)


def _dot_tn(a, b, precision=None):
    return lax.dot_general(a, b, (((0,), (0,)), ((), ())), preferred_element_type=F32,
                           precision=precision)


def _resident(shape):
    nd = len(shape)
    return pl.BlockSpec(shape, lambda *_: (0,) * nd, pipeline_mode=pl.Buffered(1))


def _params(*semantics):
    return pltpu.CompilerParams(dimension_semantics=semantics, vmem_limit_bytes=VMEM_LIMIT)


SUB = ROW_TILE // SWA_BLOCK


def _pick(arr, *lead):
    rest = arr.shape[len(lead):]
    index = tuple(lead) + (0,) * len(rest)
    return pl.BlockSpec((None,) * len(lead) + rest, lambda i: index,
                        pipeline_mode=pl.Buffered(1))


def _tile_specs(width):
    return [pl.BlockSpec((ROW_TILE, width), lambda i: (i, 0))]


def _real_specs(width, lp, seq):
    tiles = seq // ROW_TILE
    blocks = lp // SWA_BLOCK

    def spec(j):
        return pl.BlockSpec((SWA_BLOCK, width),
                            lambda i: ((i // tiles) * blocks + 1 + (i % tiles) * SUB + j, 0))
    return [spec(j) for j in range(SUB)]


def _embed_specs(width, lp, seq):
    blocks = lp // SWA_BLOCK
    x_blocks = seq // SWA_BLOCK

    def spec(j):
        def index(i):
            blk = i * SUB + j
            return ((blk // blocks) * x_blocks + jnp.maximum(blk % blocks - 1, 0), 0)
        return pl.BlockSpec((SWA_BLOCK, width), index)
    return [spec(j) for j in range(SUB)]


ROW_SPLIT = 4
GROUP_ROWS = ROW_TILE // ROW_SPLIT


def _group(part):
    return slice(part * GROUP_ROWS, (part + 1) * GROUP_ROWS)


def _load_rows(refs, part):
    if len(refs) == 1:
        return refs[0][_group(part), :]
    per = len(refs) // ROW_SPLIT
    return jnp.concatenate([r[...] for r in refs[part * per:(part + 1) * per]], axis=0)


def _ffn_core(xs, nwa, nwb, wg_ref, wu_ref, wd_ref, a_ref):
    hns = [_rms(x, nwa).astype(BF16) for x in xs]
    d_ff = wg_ref.shape[1]
    for part, hn in enumerate(hns):
        for c in range(d_ff // FF_TILE):
            cols = slice(c * FF_TILE, (c + 1) * FF_TILE)
            g = _dot(hn, wg_ref[:, cols])
            u = _dot(hn, wu_ref[:, cols])
            a_ref[_group(part), cols] = (_silu(g) * u).astype(BF16)
    fs = [_dot(a_ref[_group(part), :], wd_ref[...]) for part in range(len(xs))]
    return [x + 0.5 * _rms(f, nwb) for x, f in zip(xs, fs)]


CONV_HALO = 8


def _ffn_proj_body(*refs, n_rows, embed_blocks, n_out, conv_out, lp):
    row_refs = refs[:n_rows]
    refs = refs[n_rows:]
    if embed_blocks:
        meta_ref, refs = refs[0], refs[1:]
    if conv_out is not None:
        convw_ref, refs = refs[0], refs[1:]
        xpad_ref = refs[9 + n_out]

        @pl.when(pl.program_id(0) == 0)
        def _():
            xpad_ref[0:CONV_HALO, :] = jnp.zeros((CONV_HALO, xpad_ref.shape[1]), F32)

    nwa_ref, nwb_ref, wg_ref, wu_ref, wd_ref, nwm_ref, win_ref = refs[:7]
    h_ref = refs[7]
    out_refs = refs[8:8 + n_out]
    a_ref = refs[8 + n_out]
    offsets = np.cumsum([0] + [o.shape[1] for o in out_refs])
    xs = []
    for part in range(ROW_SPLIT):
        if embed_blocks:
            per = n_rows // ROW_SPLIT
            first = pl.program_id(0) * SUB + part * per
            xs.append(jnp.concatenate(
                [jnp.where((first + j) % embed_blocks == 0, meta_ref[...], r[...])
                 for j, r in enumerate(row_refs[part * per:(part + 1) * per])], axis=0))
        else:
            xs.append(_load_rows(row_refs, part))
    hs = _ffn_core(xs, nwa_ref[...], nwb_ref[...], wg_ref, wu_ref, wd_ref, a_ref)
    hns = [_rms(h, nwm_ref[...]).astype(BF16) for h in hs]
    for part, (h, hn) in enumerate(zip(hs, hns)):
        rows = _group(part)
        h_ref[rows, :] = h
        for i, o_ref in enumerate(out_refs):
            p = _dot(hn, win_ref[:, offsets[i]:offsets[i + 1]])
            if i == conv_out:
                xpad_ref[CONV_HALO + part * GROUP_ROWS:CONV_HALO + (part + 1) * GROUP_ROWS,
                         :] = p
            else:
                o_ref[rows, :] = p.astype(o_ref.dtype)

    if conv_out is not None:
        taps = convw_ref.shape[0]
        cw = convw_ref[...]
        y = cw[taps - 1:taps, :] * xpad_ref[CONV_HALO:, :]
        for j in range(taps - 1):
            lo = CONV_HALO - (taps - 1 - j)
            y = y + cw[j:j + 1, :] * xpad_ref[lo:lo + ROW_TILE, :]
        tail = xpad_ref[ROW_TILE:, :]
        row = (pl.program_id(0) * ROW_TILE) % lp + lax.broadcasted_iota(
            jnp.int32, (ROW_TILE, 1), 0)
        row = jnp.where(row >= lp, row - lp, row)
        out_refs[conv_out][...] = jnp.where(row >= PAD, _silu(y), 0.0).astype(
            out_refs[conv_out].dtype)
        xpad_ref[0:CONV_HALO, :] = tail


def _ffn_proj(src, meta_block, norm4, layer, wg, wu, wd, w_in, widths, dtypes, rows, lp, seq,
              conv_out=None, conv_w=None):
    d = norm4.shape[-1]
    d_ff = wg.shape[-1]
    assert sum(widths) == w_in.shape[1] and lp > ROW_TILE
    if meta_block is None:
        row_specs, extra, extra_specs, embed_blocks = _tile_specs(d), [], [], 0
    else:
        row_specs = _embed_specs(d, lp, seq)
        extra, extra_specs, embed_blocks = [meta_block], [_resident(meta_block.shape)], (
            lp // SWA_BLOCK)
    scratch = [pltpu.VMEM((ROW_TILE, d_ff), BF16)]
    if conv_out is not None:
        extra, extra_specs = extra + [conv_w], extra_specs + [_resident(conv_w.shape)]
        scratch.append(pltpu.VMEM((CONV_HALO + ROW_TILE, widths[conv_out]), F32))
    tile = lambda n: pl.BlockSpec((ROW_TILE, n), lambda i: (i, 0))
    return pl.pallas_call(
        functools.partial(_ffn_proj_body, n_rows=len(row_specs), embed_blocks=embed_blocks,
                          n_out=len(widths), conv_out=conv_out, lp=lp),
        grid=(rows // ROW_TILE,),
        in_specs=row_specs + extra_specs
        + [_pick(norm4, layer, 0), _pick(norm4, layer, 1), _pick(wg, layer, 0),
           _pick(wu, layer, 0), _pick(wd, layer, 0), _pick(norm4, layer, 2),
           _resident(w_in.shape)],
        out_specs=[tile(d)] + [tile(n) for n in widths],
        out_shape=[jax.ShapeDtypeStruct((rows, d), F32)]
        + [jax.ShapeDtypeStruct((rows, n), dt) for n, dt in zip(widths, dtypes)],
        scratch_shapes=scratch,
        compiler_params=_params("arbitrary"),
        name="ffn_proj",
    )(*([src] * len(row_specs)), *extra, norm4, norm4, wg, wu, wd, norm4, w_in)


def _proj_ffn_body(*refs, n_rows, n_x):
    x_groups = [refs[g * n_rows:(g + 1) * n_rows] for g in range(n_x)]
    refs = refs[n_x * n_rows:]
    h_refs = refs[:n_rows]
    wo_ref, nwo_ref, nwa_ref, nwb_ref, wg_ref, wu_ref, wd_ref, o_ref, a_ref = refs[n_rows:]
    accs = []
    for part in range(ROW_SPLIT):
        acc = None
        off = 0
        for group in x_groups:
            xs = _load_rows(group, part)
            k = xs.shape[1]
            p = _dot(xs, wo_ref[off:off + k, :])
            acc = p if acc is None else acc + p
            off += k
        accs.append(acc)
    hs = [_load_rows(h_refs, part) + _rms(acc, nwo_ref[...]) for part, acc in enumerate(accs)]
    outs = _ffn_core(hs, nwa_ref[...], nwb_ref[...], wg_ref, wu_ref, wd_ref, a_ref)
    for part, out in enumerate(outs):
        o_ref[_group(part), :] = out


def _proj_ffn(xs, h, w_out, norm4, layer, wg, wu, wd, final, batch, lp, seq):
    d = norm4.shape[-1]
    d_ff = wg.shape[-1]
    if final:
        specs = lambda width: _real_specs(width, lp, seq)
        out_rows = batch * seq
    else:
        specs = _tile_specs
        out_rows = batch * lp
    h_specs = specs(d)
    n_rows = len(h_specs)
    in_specs, args = [], []
    for x in xs:
        in_specs += specs(x.shape[1])
        args += [x] * n_rows
    in_specs += h_specs + [_resident(w_out.shape), _pick(norm4, layer, 3),
                           _pick(norm4, layer, 4), _pick(norm4, layer, 5),
                           _pick(wg, layer, 1), _pick(wu, layer, 1), _pick(wd, layer, 1)]
    args += [h] * n_rows + [w_out, norm4, norm4, norm4, wg, wu, wd]
    return pl.pallas_call(
        functools.partial(_proj_ffn_body, n_rows=n_rows, n_x=len(xs)),
        grid=(out_rows // ROW_TILE,),
        in_specs=in_specs,
        out_specs=pl.BlockSpec((ROW_TILE, d), lambda i: (i, 0)),
        out_shape=jax.ShapeDtypeStruct((out_rows, d), F32),
        scratch_shapes=[pltpu.VMEM((ROW_TILE, d_ff), BF16)],
        compiler_params=_params("parallel"),
        name="proj_ffn",
    )(*args)


def _t5_bucket_np(rel):
    n = np.maximum(rel, 0)
    max_exact = REL_BUCKETS // 2
    n_f = np.maximum(n, 1).astype(np.float32)
    large = max_exact + (np.log(n_f / np.float32(max_exact))
                         / np.float32(math.log(REL_MAX_DIST / max_exact))
                         * np.float32(REL_BUCKETS - max_exact)).astype(np.int32)
    large = np.minimum(large, REL_BUCKETS - 1)
    return np.where(n < max_exact, n, large).astype(np.int32)


def _swa_bucket_index():
    i = np.arange(SWA_BLOCK)[:, None]
    c = np.arange(SWA_BLOCK)[None, :]
    out = np.full((3, SWA_BLOCK, 2 * SWA_BLOCK), -1, np.int32)
    for variant in range(3):
        pos_q = variant * SWA_BLOCK + i - PAD
        pos_k = variant * SWA_BLOCK + c - PAD - np.where(c > i, SWA_BLOCK, 0)
        rel_b = pos_q - pos_k
        assert ((rel_b >= 0) & (rel_b < SWA_BLOCK)).all()
        out[variant, :, :SWA_BLOCK] = np.where(pos_k >= N_META, _t5_bucket_np(rel_b), -1)
        rel_m = pos_q - c
        meta = (c < N_META) & (rel_m >= 0)
        out[variant, :, SWA_BLOCK:] = np.where(meta, _t5_bucket_np(rel_m), -1)
    return out


_SWA_BUCKET_INDEX = _swa_bucket_index()


def _bias_body(tbl_ref, idx_ref, o_ref):
    head = pl.program_id(1)
    idx = idx_ref[0]
    acc = jnp.full(idx.shape, NEG_INF, F32)
    for b in range(REL_BUCKETS):
        acc = jnp.where(idx == b, tbl_ref[b, head], acc)
    o_ref[0, 0] = acc


def _swa_bias(rel_table):
    idx = jnp.asarray(_SWA_BUCKET_INDEX)
    group = SWA_Q_HEADS // SWA_KV_HEADS
    width = idx.shape[2]
    return pl.pallas_call(
        _bias_body,
        grid=(3, SWA_Q_HEADS),
        in_specs=[pl.BlockSpec(memory_space=pltpu.SMEM),
                  pl.BlockSpec((1, SWA_BLOCK, width), lambda v, h: (v, 0, 0))],
        out_specs=pl.BlockSpec((1, 1, SWA_BLOCK, width),
                               lambda v, h: (v, h // group, h % group, 0)),
        out_shape=jax.ShapeDtypeStruct((3, SWA_KV_HEADS, group * SWA_BLOCK, width), F32),
        compiler_params=_params("arbitrary", "arbitrary"),
        name="swa_bias",
    )(rel_table, idx)


def _swa_body(sink_ref, q_ref, kc_ref, kp_ref, km_ref, vc_ref, vp_ref, vm_ref, bias_ref, o_ref,
              *, blocks):
    dh = SWA_HEAD_DIM
    blk = SWA_BLOCK
    group = SWA_Q_HEADS // SWA_KV_HEADS
    scale = dh ** -0.5
    step = pl.program_id(1)
    zpad = jnp.zeros((blk - N_META, dh), BF16)
    row = lax.broadcasted_iota(jnp.int32, (group * blk, blk), 0) % blk
    col = lax.broadcasted_iota(jnp.int32, (group * blk, blk), 1)
    from_prev = col > row

    ones_col = (lax.broadcasted_iota(jnp.int32, (3 * blk, dh), 1) == 0).astype(BF16)

    units = [(j, hk) for j in range(blocks) for hk in range(SWA_KV_HEADS)]
    scores, values = [], []
    for j, hk in units:
        r = slice(j * blk, (j + 1) * blk)
        kv = slice(hk * dh, (hk + 1) * dh)
        k_prev = kp_ref[:, kv] if j == 0 else kc_ref[(j - 1) * blk:j * blk, kv]
        v_prev = vp_ref[:, kv] if j == 0 else vc_ref[(j - 1) * blk:j * blk, kv]
        k_all = jnp.concatenate([k_prev, kc_ref[r, kv], km_ref[:, kv], zpad], axis=0)
        v_all = jnp.concatenate([v_prev, vc_ref[r, kv], vm_ref[:, kv], zpad], axis=0)
        values.append(jnp.concatenate([v_all, ones_col], axis=1))
        q = jnp.concatenate([q_ref[r, (hk * group + g) * dh:(hk * group + g + 1) * dh]
                             for g in range(group)], axis=0) * scale
        scores.append(_dot_nt(q, k_all))

    probs, sink_terms = [], []
    for (j, hk), s in zip(units, scores):
        bias = bias_ref[jnp.minimum(step * blocks + j, 2), hk]
        s_band = jnp.where(from_prev, s[:, :blk], s[:, blk:2 * blk]) + bias[:, :blk]
        s_meta = s[:, 2 * blk:] + bias[:, blk:]
        sink = jnp.concatenate([jnp.full((blk, 1), sink_ref[hk * group + g], F32)
                                for g in range(group)], axis=0)
        m = jnp.maximum(jnp.max(jnp.maximum(s_band, s_meta), axis=-1, keepdims=True), sink)
        e_band = jnp.exp(s_band - m)
        e_meta = jnp.exp(s_meta - m)
        sink_terms.append(jnp.exp(sink - m))
        probs.append(jnp.concatenate([jnp.where(from_prev, e_band, 0.0),
                                      jnp.where(from_prev, 0.0, e_band), e_meta],
                                     axis=1).astype(BF16))

    for (j, hk), p, v_ext, sink_term in zip(units, probs, values, sink_terms):
        pv = _dot(p, v_ext)
        o = pv[:, :dh] / (pv[:, dh:dh + 1] + sink_term)
        for g in range(group):
            hq = hk * group + g
            o_ref[j * blk:(j + 1) * blk, hq * dh:(hq + 1) * dh] = (
                o[g * blk:(g + 1) * blk].astype(o_ref.dtype))


def _swa(q, k, v, bias, sinks, batch, lp, blocks):
    rows = q.shape[0]
    nb = lp // SWA_BLOCK
    steps = nb // blocks
    meta_blocks = lp // N_META
    cur = lambda b, s: (b * steps + s, 0)
    prev = lambda b, s: (b * nb + jnp.maximum(s * blocks - 1, 0), 0)
    meta = lambda b, s: (b * meta_blocks + PAD // N_META, 0)
    kv_w = k.shape[1]
    rs = blocks * SWA_BLOCK
    return pl.pallas_call(
        functools.partial(_swa_body, blocks=blocks),
        grid=(batch, steps),
        in_specs=[pl.BlockSpec(memory_space=pltpu.SMEM),
                  pl.BlockSpec((rs, q.shape[1]), cur),
                  pl.BlockSpec((rs, kv_w), cur), pl.BlockSpec((SWA_BLOCK, kv_w), prev),
                  pl.BlockSpec((N_META, kv_w), meta),
                  pl.BlockSpec((rs, kv_w), cur), pl.BlockSpec((SWA_BLOCK, kv_w), prev),
                  pl.BlockSpec((N_META, kv_w), meta),
                  _resident(bias.shape)],
        out_specs=pl.BlockSpec((rs, q.shape[1]), cur),
        out_shape=jax.ShapeDtypeStruct((rows, q.shape[1]), BF16),
        compiler_params=_params("parallel", "arbitrary"),
        name="swa",
    )(sinks, q, k, k, k, v, v, v, bias)


def _tri_masks():
    row = lax.broadcasted_iota(jnp.int32, (CHUNK, CHUNK), 0)
    col = lax.broadcasted_iota(jnp.int32, (CHUNK, CHUNK), 1)
    return row >= col, row > col, row == col


def _deltanet_body(act_ref, z_ref, gates_ref, alog_ref, dtb_ref, nw_ref, o_ref, s_ref, *,
                   chunks):
    nh, dh = DN_HEADS, DN_HEAD_DIM

    @pl.when(pl.program_id(1) == 0)
    def _():
        s_ref[...] = jnp.zeros_like(s_ref)

    gt = gates_ref[...]
    beta_full = jax.nn.sigmoid(gt)
    g_full = -jnp.exp(alog_ref[...]) * _softplus(gt + dtb_ref[...])
    incl, strict, diag = _tri_masks()
    eye = diag.astype(F32)
    tri = incl.astype(F32)
    gc_chunks = [_dot(tri, g_full[c * CHUNK:(c + 1) * CHUNK, :], HIGHEST)
                 for c in range(chunks)]
    items = [(c, h) for c in range(chunks) for h in range(nh)]

    neg_a, attn16, qd16, kdt16, rhs16, decay = [], [], [], [], [], []
    for c, h in items:
        r = slice(c * CHUNK, (c + 1) * CHUNK)
        q = act_ref[r, h * dh:(h + 1) * dh].astype(F32)
        k = act_ref[r, DN_DIM + h * dh:DN_DIM + (h + 1) * dh].astype(F32)
        v = act_ref[r, 2 * DN_DIM + h * dh:2 * DN_DIM + (h + 1) * dh].astype(F32)
        q = q * lax.rsqrt(jnp.sum(q * q, axis=-1, keepdims=True) + 1e-6) * dh ** -0.5
        k = k * lax.rsqrt(jnp.sum(k * k, axis=-1, keepdims=True) + 1e-6)
        beta = jnp.broadcast_to(beta_full[r, h:h + 1], (CHUNK, dh))
        gc = jnp.broadcast_to(gc_chunks[c][:, nh + h:nh + h + 1], (CHUNK, dh))
        gc_row = jnp.transpose(gc)[:CHUNK, :]
        gamma = jnp.where(incl, jnp.exp(jnp.where(incl, gc[:, :CHUNK] - gc_row, 0.0)), 0.0)
        k_beta = k * beta
        k16 = k.astype(BF16)
        neg_a.append(jnp.where(strict, -(_dot_nt(k_beta.astype(BF16), k16) * gamma), 0.0))
        attn16.append(jnp.where(incl, _dot_nt(q.astype(BF16), k16) * gamma, 0.0).astype(BF16))
        gc_last = gc[CHUNK - 1:CHUNK, :]
        qd16.append((q * jnp.exp(gc)).astype(BF16))
        kdt16.append(jnp.transpose(k * jnp.exp(gc_last - gc)).astype(BF16))
        rhs16.append(jnp.concatenate([v * beta, k_beta * jnp.exp(gc)], axis=1).astype(BF16))
        decay.append(jnp.exp(gc_last))

    p = neg_a
    t = [eye + x for x in p]
    for _ in range(int(math.log2(CHUNK)) - 1):
        p16 = [x.astype(BF16) for x in p]
        p = [_dot(x, x) for x in p16]
        t = [y + _dot(y.astype(BF16), x.astype(BF16)) for y, x in zip(t, p)]
    uw = [_dot(y.astype(BF16), b) for y, b in zip(t, rhs16)]

    nw = nw_ref[...]
    state = [s_ref[h] for h in range(nh)]
    for c in range(chunks):
        r = slice(c * CHUNK, (c + 1) * CHUNK)
        pair = [c * nh + h for h in range(nh)]
        s16 = [s.astype(BF16) for s in state]
        ws = [_dot(uw[i][:, dh:].astype(BF16), s16[h]) for h, i in enumerate(pair)]
        qs = [_dot(qd16[i], s16[h]) for h, i in enumerate(pair)]
        vn16 = [(uw[i][:, :dh] - ws[h]).astype(BF16) for h, i in enumerate(pair)]
        out = [qs[h] + _dot(attn16[i], vn16[h]) for h, i in enumerate(pair)]
        state = [state[h] * decay[i] + _dot(kdt16[i], vn16[h]) for h, i in enumerate(pair)]
        for h in range(nh):
            hc = slice(h * dh, (h + 1) * dh)
            o = _rms(out[h], nw) * _silu(z_ref[r, hc].astype(F32))
            o_ref[r, hc] = o.astype(o_ref.dtype)
    for h in range(nh):
        s_ref[h] = state[h]


def _deltanet(act, z, gates, alog_vec, dtb_vec, nw, batch, lp, chunks):
    rows = act.shape[0]
    rs = chunks * CHUNK
    steps = lp // rs
    idx = lambda b, s: (b * steps + s, 0)
    return pl.pallas_call(
        functools.partial(_deltanet_body, chunks=chunks),
        grid=(batch, steps),
        in_specs=[pl.BlockSpec((rs, act.shape[1]), idx), pl.BlockSpec((rs, z.shape[1]), idx),
                  pl.BlockSpec((rs, gates.shape[1]), idx),
                  _resident(alog_vec.shape), _resident(dtb_vec.shape), _resident(nw.shape)],
        out_specs=pl.BlockSpec((rs, DN_DIM), idx),
        out_shape=jax.ShapeDtypeStruct((rows, DN_DIM), BF16),
        scratch_shapes=[pltpu.VMEM((DN_HEADS, DN_HEAD_DIM, DN_HEAD_DIM), F32)],
        compiler_params=_params("parallel", "arbitrary"),
        name="deltanet",
    )(act, z, gates, alog_vec, dtb_vec, nw)


def _gla_body(q_ref, k_ref, v_ref, g_ref, gk_ref, wgu_ref, bg_ref, nw_ref, o_ref, s_ref, *,
              chunks):
    nh = GLA_HEADS
    dk = q_ref.shape[1] // nh
    dv = v_ref.shape[1] // nh

    @pl.when(pl.program_id(1) == 0)
    def _():
        s_ref[...] = jnp.zeros_like(s_ref)

    logits = _dot(gk_ref[...].astype(BF16), wgu_ref[...]) + bg_ref[...]
    glog = (jnp.minimum(logits, 0.0) - jnp.log(1.0 + jnp.exp(-jnp.abs(logits)))) / GLA_GATE_NORM
    incl, _, _ = _tri_masks()
    tri = incl.astype(F32)
    bcum = [_dot(tri, glog[c * CHUNK:(c + 1) * CHUNK, :], HIGHEST) for c in range(chunks)]

    items = [(c, h) for c in range(chunks) for h in range(nh)]
    qd16, kn16, kdt16, decay = [], [], [], []
    for c, h in items:
        r = slice(c * CHUNK, (c + 1) * CHUNK)
        kc = slice(h * dk, (h + 1) * dk)
        bc = bcum[c][:, kc]
        q = q_ref[r, kc].astype(F32) * dk ** -0.5
        k = k_ref[r, kc].astype(F32)
        b_last = bc[CHUNK - 1:CHUNK, :]
        qd16.append((q * jnp.exp(bc)).astype(BF16))
        kn16.append((k * jnp.exp(-bc)).astype(BF16))
        kdt16.append(jnp.transpose(k * jnp.exp(b_last - bc)).astype(BF16))
        decay_rows = jnp.broadcast_to(jnp.exp(b_last), (8, dk))
        decay.append(jnp.transpose(decay_rows)[:, :1])
    values = [v_ref[c * CHUNK:(c + 1) * CHUNK, h * dv:(h + 1) * dv] for c, h in items]
    attn16 = [jnp.where(incl, _dot_nt(a, b), 0.0).astype(BF16) for a, b in zip(qd16, kn16)]
    intra = [_dot(a, v) for a, v in zip(attn16, values)]
    update = [_dot(kt, v) for kt, v in zip(kdt16, values)]

    nw = nw_ref[...]
    state = [s_ref[h] for h in range(nh)]
    for c in range(chunks):
        r = slice(c * CHUNK, (c + 1) * CHUNK)
        for h in range(nh):
            i = c * nh + h
            vc = slice(h * dv, (h + 1) * dv)
            o = intra[i] + _dot(qd16[i], state[h].astype(BF16))
            state[h] = state[h] * decay[i] + update[i]
            o = _rms(o, nw) * _silu(g_ref[r, vc].astype(F32))
            o_ref[r, vc] = o.astype(o_ref.dtype)
    for h in range(nh):
        s_ref[h] = state[h]


def _gla(q, k, v, g, gk, wgu, bg, nw, batch, lp, chunks):
    rows = q.shape[0]
    rs = chunks * CHUNK
    steps = lp // rs
    idx = lambda b, s: (b * steps + s, 0)
    dk = q.shape[1] // GLA_HEADS
    dv = v.shape[1] // GLA_HEADS
    return pl.pallas_call(
        functools.partial(_gla_body, chunks=chunks),
        grid=(batch, steps),
        in_specs=[pl.BlockSpec((rs, q.shape[1]), idx), pl.BlockSpec((rs, k.shape[1]), idx),
                  pl.BlockSpec((rs, v.shape[1]), idx), pl.BlockSpec((rs, g.shape[1]), idx),
                  pl.BlockSpec((rs, gk.shape[1]), idx), _resident(wgu.shape),
                  _resident(bg.shape), _resident(nw.shape)],
        out_specs=pl.BlockSpec((rs, v.shape[1]), idx),
        out_shape=jax.ShapeDtypeStruct((rows, v.shape[1]), BF16),
        scratch_shapes=[pltpu.VMEM((GLA_HEADS, dk, dv), F32)],
        compiler_params=_params("parallel", "arbitrary"),
        name="gla",
    )(q, k, v, g, gk, wgu, bg, nw)


def _pad_cols(w, width):
    return jnp.pad(w, ((0, 0), (0, width - w.shape[1])))


def _per_step(n, preferred):
    for c in preferred:
        if n % c == 0:
            return c
    return 1


def kernel(x, meta_tokens, norm_w, ffn_w_gate, ffn_w_up, ffn_w_down, rel_bias_table,
           even_w_in, even_conv_w, swa_sinks, dn_a_log, dn_dt_bias, dn_norm_w, even_w_out,
           odd_w_in, gla_w_gate_up, gla_b_gate, gla_norm_w, odd_w_out):
    batch, seq, d = x.shape
    depth = norm_w.shape[0]
    lp = SWA_BLOCK + seq
    assert seq % ROW_TILE == 0 and (batch * lp) % ROW_TILE == 0
    rows = batch * lp
    chunks = _per_step(lp // CHUNK, (6, 3, 2))
    swa_blocks = _per_step(lp // SWA_BLOCK, (3, 2))

    meta_block = jnp.pad(meta_tokens.astype(x.dtype), ((PAD, 0), (0, 0)))
    norm4 = norm_w[:, :, None, :]
    wg = ffn_w_gate.astype(BF16)
    wu = ffn_w_up.astype(BF16)
    wd = ffn_w_down.astype(BF16)

    swa_q = SWA_Q_HEADS * SWA_HEAD_DIM
    swa_kv = SWA_KV_HEADS * SWA_HEAD_DIM
    bias = None
    h = x.reshape(batch * seq, d)
    for layer in range(depth):
        i = layer // 2
        embed = meta_block if layer == 0 else None
        final = layer == depth - 1
        if layer % 2 == 0:
            assert even_w_in.shape[2] == swa_q + 2 * swa_kv + 4 * DN_DIM + 2 * DN_HEADS
            widths = (swa_q, swa_kv, swa_kv, 3 * DN_DIM, DN_DIM, LANE)
            w_in = _pad_cols(even_w_in[i], sum(widths)).astype(BF16)
            h, qa, ka, va, act_b, z_b, gates = _ffn_proj(
                h, embed, norm4, layer, wg, wu, wd, w_in, widths,
                (BF16, BF16, BF16, BF16, BF16, F32), rows, lp, seq,
                conv_out=3, conv_w=even_conv_w[i])
            if bias is None:
                bias = _swa_bias(rel_bias_table)
            o_a = _swa(qa, ka, va, bias, swa_sinks[i], batch, lp, swa_blocks)
            lane_pad = (DN_HEADS, LANE - 2 * DN_HEADS)
            alog_vec = jnp.pad(dn_a_log[i], lane_pad)[None, :]
            dtb_vec = jnp.pad(dn_dt_bias[i], lane_pad)[None, :]
            o_b = _deltanet(act_b, z_b, gates, alog_vec, dtb_vec, dn_norm_w[i][None, :],
                            batch, lp, chunks)
            mixed, w_out = [o_a, o_b], even_w_out[i]
        else:
            key_dim = gla_w_gate_up.shape[2]
            val_dim = odd_w_out.shape[1]
            widths = (key_dim, key_dim, val_dim, val_dim, LANE)
            w_in = _pad_cols(odd_w_in[i], sum(widths)).astype(BF16)
            h, q, k, v, g, gk = _ffn_proj(
                h, embed, norm4, layer, wg, wu, wd, w_in, widths,
                (BF16, BF16, BF16, BF16, F32), rows, lp, seq)
            wgu = jnp.pad(gla_w_gate_up[i], ((0, LANE - GLA_GATE_RANK), (0, 0))).astype(BF16)
            o = _gla(q, k, v, g, gk, wgu, gla_b_gate[i][None, :], gla_norm_w[i][None, :],
                     batch, lp, chunks)
            mixed, w_out = [o], odd_w_out[i]
        h = _proj_ffn(mixed, h, w_out.astype(BF16), norm4, layer, wg, wu, wd, final,
                      batch, lp, seq)
    return h.reshape(batch, seq, d)
```

```python
import functools
import math

import numpy as np
import jax
import jax.numpy as jnp
from jax import lax
from jax.experimental import pallas as pl
from jax.experimental.pallas import tpu as pltpu

F32 = jnp.float32
BF16 = jnp.bfloat16
HIGHEST = lax.Precision.HIGHEST

N_META = 16
NORM_EPS = 1e-6
NEG_INF = -1e30
SWA_Q_HEADS = 8
SWA_KV_HEADS = 2
SWA_HEAD_DIM = 64
SWA_BLOCK = 128
REL_BUCKETS = 32
REL_MAX_DIST = 128
DN_HEADS = 4
DN_HEAD_DIM = 128
DN_DIM = DN_HEADS * DN_HEAD_DIM
DN_CONV = 4
GLA_HEADS = 4
GLA_GATE_RANK = 16
GLA_GATE_NORM = 16.0
CHUNK = 64

PAD = SWA_BLOCK - N_META
LANE = 128
ROW_TILE = 512
FF_TILE = 256
VMEM_LIMIT = 56 * 1024 * 1024


def _rms(x, w):
    return x * lax.rsqrt(jnp.mean(x * x, axis=-1, keepdims=True) + NORM_EPS) * w


def _silu(x):
    return x * jax.nn.sigmoid(x)


def _softplus(x):
    return jnp.maximum(x, 0.0) + jnp.log(1.0 + jnp.exp(-jnp.abs(x)))


def _dot(a, b, precision=None):
    return jnp.dot(a, b, preferred_element_type=F32, precision=precision)


def _dot_nt(a, b, precision=None):
    return lax.dot_general(a, b, (((1,), (1,)), ((), ())), preferred_element_type=F32,
                           precision=precision)


def _dot_tn(a, b, precision=None):
    return lax.dot_general(a, b, (((0,), (0,)), ((), ())), preferred_element_type=F32,
                           precision=precision)


def _resident(shape):
    nd = len(shape)
    return pl.BlockSpec(shape, lambda *_: (0,) * nd, pipeline_mode=pl.Buffered(1))


def _params(*semantics):
    return pltpu.CompilerParams(dimension_semantics=semantics, vmem_limit_bytes=VMEM_LIMIT)


SUB = ROW_TILE // SWA_BLOCK


def _pick(arr, *lead):
    rest = arr.shape[len(lead):]
    index = tuple(lead) + (0,) * len(rest)
    return pl.BlockSpec((None,) * len(lead) + rest, lambda i: index,
                        pipeline_mode=pl.Buffered(1))


def _tile_specs(width):
    return [pl.BlockSpec((ROW_TILE, width), lambda i: (i, 0))]


def _real_specs(width, lp, seq):
    tiles = seq // ROW_TILE
    blocks = lp // SWA_BLOCK

    def spec(j):
        return pl.BlockSpec((SWA_BLOCK, width),
                            lambda i: ((i // tiles) * blocks + 1 + (i % tiles) * SUB + j, 0))
    return [spec(j) for j in range(SUB)]


def _embed_specs(width, lp, seq):
    blocks = lp // SWA_BLOCK
    x_blocks = seq // SWA_BLOCK

    def spec(j):
        def index(i):
            blk = i * SUB + j
            return ((blk // blocks) * x_blocks + jnp.maximum(blk % blocks - 1, 0), 0)
        return pl.BlockSpec((SWA_BLOCK, width), index)
    return [spec(j) for j in range(SUB)]


ROW_SPLIT = 4
GROUP_ROWS = ROW_TILE // ROW_SPLIT


def _group(part):
    return slice(part * GROUP_ROWS, (part + 1) * GROUP_ROWS)


def _load_rows(refs, part):
    if len(refs) == 1:
        return refs[0][_group(part), :]
    per = len(refs) // ROW_SPLIT
    return jnp.concatenate([r[...] for r in refs[part * per:(part + 1) * per]], axis=0)


def _drain(stages):
    for _ in stages:
        pass


def _interleave(main, side):
    for _ in main:
        next(side, None)
    _drain(side)


def _ffn_stages(xs, nwa, nwb, wg_ref, wu_ref, wd_ref, a_ref, outs):
    hns = [_rms(x, nwa).astype(BF16) for x in xs]
    d_ff = wg_ref.shape[1]
    for part, hn in enumerate(hns):
        for c in range(d_ff // FF_TILE):
            cols = slice(c * FF_TILE, (c + 1) * FF_TILE)
            g = _dot(hn, wg_ref[:, cols])
            u = _dot(hn, wu_ref[:, cols])
            a_ref[_group(part), cols] = (_silu(g) * u).astype(BF16)
            yield
    fs = []
    for part in range(len(xs)):
        fs.append(_dot(a_ref[_group(part), :], wd_ref[...]))
        yield
    outs.extend(x + 0.5 * _rms(f, nwb) for x, f in zip(xs, fs))


def _ffn_core(xs, nwa, nwb, wg_ref, wu_ref, wd_ref, a_ref):
    outs = []
    _drain(_ffn_stages(xs, nwa, nwb, wg_ref, wu_ref, wd_ref, a_ref, outs))
    return outs


CONV_HALO = 8


def _ffn_proj_body(*refs, n_rows, embed_blocks, n_out, conv_out, lp):
    row_refs = refs[:n_rows]
    refs = refs[n_rows:]
    if embed_blocks:
        meta_ref, refs = refs[0], refs[1:]
    if conv_out is not None:
        convw_ref, refs = refs[0], refs[1:]
        xpad_ref = refs[9 + n_out]

        @pl.when(pl.program_id(0) == 0)
        def _():
            xpad_ref[0:CONV_HALO, :] = jnp.zeros((CONV_HALO, xpad_ref.shape[1]), F32)

    nwa_ref, nwb_ref, wg_ref, wu_ref, wd_ref, nwm_ref, win_ref = refs[:7]
    h_ref = refs[7]
    out_refs = refs[8:8 + n_out]
    a_ref = refs[8 + n_out]
    offsets = np.cumsum([0] + [o.shape[1] for o in out_refs])
    xs = []
    for part in range(ROW_SPLIT):
        if embed_blocks:
            per = n_rows // ROW_SPLIT
            first = pl.program_id(0) * SUB + part * per
            xs.append(jnp.concatenate(
                [jnp.where((first + j) % embed_blocks == 0, meta_ref[...], r[...])
                 for j, r in enumerate(row_refs[part * per:(part + 1) * per])], axis=0))
        else:
            xs.append(_load_rows(row_refs, part))
    hs = _ffn_core(xs, nwa_ref[...], nwb_ref[...], wg_ref, wu_ref, wd_ref, a_ref)
    hns = [_rms(h, nwm_ref[...]).astype(BF16) for h in hs]
    for part, (h, hn) in enumerate(zip(hs, hns)):
        rows = _group(part)
        h_ref[rows, :] = h
        for i, o_ref in enumerate(out_refs):
            p = _dot(hn, win_ref[:, offsets[i]:offsets[i + 1]])
            if i == conv_out:
                xpad_ref[CONV_HALO + part * GROUP_ROWS:CONV_HALO + (part + 1) * GROUP_ROWS,
                         :] = p
            else:
                o_ref[rows, :] = p.astype(o_ref.dtype)

    if conv_out is not None:
        taps = convw_ref.shape[0]
        cw = convw_ref[...]
        y = cw[taps - 1:taps, :] * xpad_ref[CONV_HALO:, :]
        for j in range(taps - 1):
            lo = CONV_HALO - (taps - 1 - j)
            y = y + cw[j:j + 1, :] * xpad_ref[lo:lo + ROW_TILE, :]
        tail = xpad_ref[ROW_TILE:, :]
        row = (pl.program_id(0) * ROW_TILE) % lp + lax.broadcasted_iota(
            jnp.int32, (ROW_TILE, 1), 0)
        row = jnp.where(row >= lp, row - lp, row)
        out_refs[conv_out][...] = jnp.where(row >= PAD, _silu(y), 0.0).astype(
            out_refs[conv_out].dtype)
        xpad_ref[0:CONV_HALO, :] = tail


def _ffn_proj(src, meta_block, norm4, layer, wg, wu, wd, w_in, widths, dtypes, rows, lp, seq,
              conv_out=None, conv_w=None):
    d = norm4.shape[-1]
    d_ff = wg.shape[-1]
    assert sum(widths) == w_in.shape[1] and lp > ROW_TILE
    if meta_block is None:
        row_specs, extra, extra_specs, embed_blocks = _tile_specs(d), [], [], 0
    else:
        row_specs = _embed_specs(d, lp, seq)
        extra, extra_specs, embed_blocks = [meta_block], [_resident(meta_block.shape)], (
            lp // SWA_BLOCK)
    scratch = [pltpu.VMEM((ROW_TILE, d_ff), BF16)]
    if conv_out is not None:
        extra, extra_specs = extra + [conv_w], extra_specs + [_resident(conv_w.shape)]
        scratch.append(pltpu.VMEM((CONV_HALO + ROW_TILE, widths[conv_out]), F32))
    tile = lambda n: pl.BlockSpec((ROW_TILE, n), lambda i: (i, 0))
    return pl.pallas_call(
        functools.partial(_ffn_proj_body, n_rows=len(row_specs), embed_blocks=embed_blocks,
                          n_out=len(widths), conv_out=conv_out, lp=lp),
        grid=(rows // ROW_TILE,),
        in_specs=row_specs + extra_specs
        + [_pick(norm4, layer, 0), _pick(norm4, layer, 1), _pick(wg, layer, 0),
           _pick(wu, layer, 0), _pick(wd, layer, 0), _pick(norm4, layer, 2),
           _resident(w_in.shape)],
        out_specs=[tile(d)] + [tile(n) for n in widths],
        out_shape=[jax.ShapeDtypeStruct((rows, d), F32)]
        + [jax.ShapeDtypeStruct((rows, n), dt) for n, dt in zip(widths, dtypes)],
        scratch_shapes=scratch,
        compiler_params=_params("arbitrary"),
        name="ffn_proj",
    )(*([src] * len(row_specs)), *extra, norm4, norm4, wg, wu, wd, norm4, w_in)


def _proj_ffn_body(*refs, n_rows, n_x):
    x_groups = [refs[g * n_rows:(g + 1) * n_rows] for g in range(n_x)]
    refs = refs[n_x * n_rows:]
    h_refs = refs[:n_rows]
    wo_ref, nwo_ref, nwa_ref, nwb_ref, wg_ref, wu_ref, wd_ref, o_ref, a_ref = refs[n_rows:]
    accs = []
    for part in range(ROW_SPLIT):
        acc = None
        off = 0
        for group in x_groups:
            xs = _load_rows(group, part)
            k = xs.shape[1]
            p = _dot(xs, wo_ref[off:off + k, :])
            acc = p if acc is None else acc + p
            off += k
        accs.append(acc)
    hs = [_load_rows(h_refs, part) + _rms(acc, nwo_ref[...]) for part, acc in enumerate(accs)]
    outs = _ffn_core(hs, nwa_ref[...], nwb_ref[...], wg_ref, wu_ref, wd_ref, a_ref)
    for part, out in enumerate(outs):
        o_ref[_group(part), :] = out


def _proj_ffn(xs, h, w_out, norm4, layer, wg, wu, wd, final, batch, lp, seq):
    d = norm4.shape[-1]
    d_ff = wg.shape[-1]
    if final:
        specs = lambda width: _real_specs(width, lp, seq)
        out_rows = batch * seq
    else:
        specs = _tile_specs
        out_rows = batch * lp
    h_specs = specs(d)
    n_rows = len(h_specs)
    in_specs, args = [], []
    for x in xs:
        in_specs += specs(x.shape[1])
        args += [x] * n_rows
    in_specs += h_specs + [_resident(w_out.shape), _pick(norm4, layer, 3),
                           _pick(norm4, layer, 4), _pick(norm4, layer, 5),
                           _pick(wg, layer, 1), _pick(wu, layer, 1), _pick(wd, layer, 1)]
    args += [h] * n_rows + [w_out, norm4, norm4, norm4, wg, wu, wd]
    return pl.pallas_call(
        functools.partial(_proj_ffn_body, n_rows=n_rows, n_x=len(xs)),
        grid=(out_rows // ROW_TILE,),
        in_specs=in_specs,
        out_specs=pl.BlockSpec((ROW_TILE, d), lambda i: (i, 0)),
        out_shape=jax.ShapeDtypeStruct((out_rows, d), F32),
        scratch_shapes=[pltpu.VMEM((ROW_TILE, d_ff), BF16)],
        compiler_params=_params("parallel"),
        name="proj_ffn",
    )(*args)


def _t5_bucket_np(rel):
    n = np.maximum(rel, 0)
    max_exact = REL_BUCKETS // 2
    n_f = np.maximum(n, 1).astype(np.float32)
    large = max_exact + (np.log(n_f / np.float32(max_exact))
                         / np.float32(math.log(REL_MAX_DIST / max_exact))
                         * np.float32(REL_BUCKETS - max_exact)).astype(np.int32)
    large = np.minimum(large, REL_BUCKETS - 1)
    return np.where(n < max_exact, n, large).astype(np.int32)


def _swa_bucket_index():
    i = np.arange(SWA_BLOCK)[:, None]
    c = np.arange(SWA_BLOCK)[None, :]
    out = np.full((3, SWA_BLOCK, 2 * SWA_BLOCK), -1, np.int32)
    for variant in range(3):
        pos_q = variant * SWA_BLOCK + i - PAD
        pos_k = variant * SWA_BLOCK + c - PAD - np.where(c > i, SWA_BLOCK, 0)
        rel_b = pos_q - pos_k
        assert ((rel_b >= 0) & (rel_b < SWA_BLOCK)).all()
        out[variant, :, :SWA_BLOCK] = np.where(pos_k >= N_META, _t5_bucket_np(rel_b), -1)
        rel_m = pos_q - c
        meta = (c < N_META) & (rel_m >= 0)
        out[variant, :, SWA_BLOCK:] = np.where(meta, _t5_bucket_np(rel_m), -1)
    return out


_SWA_BUCKET_INDEX = _swa_bucket_index()


def _bias_body(tbl_ref, idx_ref, o_ref):
    head = pl.program_id(1)
    idx = idx_ref[0]
    acc = jnp.full(idx.shape, NEG_INF, F32)
    for b in range(REL_BUCKETS):
        acc = jnp.where(idx == b, tbl_ref[b, head], acc)
    o_ref[0, 0] = acc


def _swa_bias(rel_table):
    idx = jnp.asarray(_SWA_BUCKET_INDEX)
    group = SWA_Q_HEADS // SWA_KV_HEADS
    width = idx.shape[2]
    return pl.pallas_call(
        _bias_body,
        grid=(3, SWA_Q_HEADS),
        in_specs=[pl.BlockSpec(memory_space=pltpu.SMEM),
                  pl.BlockSpec((1, SWA_BLOCK, width), lambda v, h: (v, 0, 0))],
        out_specs=pl.BlockSpec((1, 1, SWA_BLOCK, width),
                               lambda v, h: (v, h // group, h % group, 0)),
        out_shape=jax.ShapeDtypeStruct((3, SWA_KV_HEADS, group * SWA_BLOCK, width), F32),
        compiler_params=_params("arbitrary", "arbitrary"),
        name="swa_bias",
    )(rel_table, idx)


def _swa_stages(sink_ref, q_ref, kc_ref, kp_ref, km_ref, vc_ref, vp_ref, vm_ref, bias_ref,
                step, blocks, store):
    dh = SWA_HEAD_DIM
    blk = SWA_BLOCK
    group = SWA_Q_HEADS // SWA_KV_HEADS
    scale = dh ** -0.5
    zpad = jnp.zeros((blk - N_META, dh), BF16)
    row = lax.broadcasted_iota(jnp.int32, (group * blk, blk), 0) % blk
    col = lax.broadcasted_iota(jnp.int32, (group * blk, blk), 1)
    from_prev = col > row

    ones_col = (lax.broadcasted_iota(jnp.int32, (3 * blk, dh), 1) == 0).astype(BF16)

    units = [(j, hk) for j in range(blocks) for hk in range(SWA_KV_HEADS)]
    scores, values = [], []
    for j, hk in units:
        r = slice(j * blk, (j + 1) * blk)
        kv = slice(hk * dh, (hk + 1) * dh)
        k_prev = kp_ref[:, kv] if j == 0 else kc_ref[(j - 1) * blk:j * blk, kv]
        v_prev = vp_ref[:, kv] if j == 0 else vc_ref[(j - 1) * blk:j * blk, kv]
        k_all = jnp.concatenate([k_prev, kc_ref[r, kv], km_ref[:, kv], zpad], axis=0)
        v_all = jnp.concatenate([v_prev, vc_ref[r, kv], vm_ref[:, kv], zpad], axis=0)
        values.append(jnp.concatenate([v_all, ones_col], axis=1))
        q = jnp.concatenate([q_ref[r, (hk * group + g) * dh:(hk * group + g + 1) * dh]
                             for g in range(group)], axis=0) * scale
        scores.append(_dot_nt(q, k_all))
        yield

    probs, sink_terms = [], []
    for (j, hk), s in zip(units, scores):
        bias = bias_ref[jnp.minimum(step * blocks + j, 2), hk]
        s_band = jnp.where(from_prev, s[:, :blk], s[:, blk:2 * blk]) + bias[:, :blk]
        s_meta = s[:, 2 * blk:] + bias[:, blk:]
        sink = jnp.concatenate([jnp.full((blk, 1), sink_ref[hk * group + g], F32)
                                for g in range(group)], axis=0)
        m = jnp.maximum(jnp.max(jnp.maximum(s_band, s_meta), axis=-1, keepdims=True), sink)
        e_band = jnp.exp(s_band - m)
        e_meta = jnp.exp(s_meta - m)
        sink_terms.append(jnp.exp(sink - m))
        probs.append(jnp.concatenate([jnp.where(from_prev, e_band, 0.0),
                                      jnp.where(from_prev, 0.0, e_band), e_meta],
                                     axis=1).astype(BF16))

    for (j, hk), p, v_ext, sink_term in zip(units, probs, values, sink_terms):
        pv = _dot(p, v_ext)
        o = pv[:, :dh] / (pv[:, dh:dh + 1] + sink_term)
        for g in range(group):
            hq = hk * group + g
            store(slice(j * blk, (j + 1) * blk), slice(hq * dh, (hq + 1) * dh),
                  o[g * blk:(g + 1) * blk])
        yield


def _swa_body(sink_ref, q_ref, kc_ref, kp_ref, km_ref, vc_ref, vp_ref, vm_ref, bias_ref, o_ref,
              *, blocks):
    def store(rows, cols, value):
        o_ref[rows, cols] = value.astype(o_ref.dtype)

    _drain(_swa_stages(sink_ref, q_ref, kc_ref, kp_ref, km_ref, vc_ref, vp_ref, vm_ref,
                       bias_ref, pl.program_id(1), blocks, store))


def _swa(q, k, v, bias, sinks, batch, lp, blocks):
    rows = q.shape[0]
    nb = lp // SWA_BLOCK
    steps = nb // blocks
    meta_blocks = lp // N_META
    cur = lambda b, s: (b * steps + s, 0)
    prev = lambda b, s: (b * nb + jnp.maximum(s * blocks - 1, 0), 0)
    meta = lambda b, s: (b * meta_blocks + PAD // N_META, 0)
    kv_w = k.shape[1]
    rs = blocks * SWA_BLOCK
    return pl.pallas_call(
        functools.partial(_swa_body, blocks=blocks),
        grid=(batch, steps),
        in_specs=[pl.BlockSpec(memory_space=pltpu.SMEM),
                  pl.BlockSpec((rs, q.shape[1]), cur),
                  pl.BlockSpec((rs, kv_w), cur), pl.BlockSpec((SWA_BLOCK, kv_w), prev),
                  pl.BlockSpec((N_META, kv_w), meta),
                  pl.BlockSpec((rs, kv_w), cur), pl.BlockSpec((SWA_BLOCK, kv_w), prev),
                  pl.BlockSpec((N_META, kv_w), meta),
                  _resident(bias.shape)],
        out_specs=pl.BlockSpec((rs, q.shape[1]), cur),
        out_shape=jax.ShapeDtypeStruct((rows, q.shape[1]), BF16),
        compiler_params=_params("parallel", "arbitrary"),
        name="swa",
    )(sinks, q, k, k, k, v, v, v, bias)


def _tri_masks():
    row = lax.broadcasted_iota(jnp.int32, (CHUNK, CHUNK), 0)
    col = lax.broadcasted_iota(jnp.int32, (CHUNK, CHUNK), 1)
    return row >= col, row > col, row == col


def _deltanet_stages(act_ref, z_ref, gates_ref, alog_ref, dtb_ref, nw_ref, s_ref, chunks,
                     store):
    nh, dh = DN_HEADS, DN_HEAD_DIM
    gt = gates_ref[...]
    beta_full = jax.nn.sigmoid(gt)
    g_full = -jnp.exp(alog_ref[...]) * _softplus(gt + dtb_ref[...])
    incl, strict, diag = _tri_masks()
    eye = diag.astype(F32)
    tri = incl.astype(F32)
    gc_chunks = [_dot(tri, g_full[c * CHUNK:(c + 1) * CHUNK, :], HIGHEST)
                 for c in range(chunks)]
    items = [(c, h) for c in range(chunks) for h in range(nh)]
    yield

    neg_a, attn16, qd16, kdt16, rhs16, decay = [], [], [], [], [], []
    for c, h in items:
        if h == 0 and c % 2 == 0 and c > 0:
            yield
        r = slice(c * CHUNK, (c + 1) * CHUNK)
        q = act_ref[r, h * dh:(h + 1) * dh].astype(F32)
        k = act_ref[r, DN_DIM + h * dh:DN_DIM + (h + 1) * dh].astype(F32)
        v = act_ref[r, 2 * DN_DIM + h * dh:2 * DN_DIM + (h + 1) * dh].astype(F32)
        q = q * lax.rsqrt(jnp.sum(q * q, axis=-1, keepdims=True) + 1e-6) * dh ** -0.5
        k = k * lax.rsqrt(jnp.sum(k * k, axis=-1, keepdims=True) + 1e-6)
        beta = jnp.broadcast_to(beta_full[r, h:h + 1], (CHUNK, dh))
        gc = jnp.broadcast_to(gc_chunks[c][:, nh + h:nh + h + 1], (CHUNK, dh))
        gc_row = jnp.transpose(gc)[:CHUNK, :]
        gamma = jnp.where(incl, jnp.exp(jnp.where(incl, gc[:, :CHUNK] - gc_row, 0.0)), 0.0)
        k_beta = k * beta
        k16 = k.astype(BF16)
        neg_a.append(jnp.where(strict, -(_dot_nt(k_beta.astype(BF16), k16) * gamma), 0.0))
        attn16.append(jnp.where(incl, _dot_nt(q.astype(BF16), k16) * gamma, 0.0).astype(BF16))
        gc_last = gc[CHUNK - 1:CHUNK, :]
        qd16.append((q * jnp.exp(gc)).astype(BF16))
        kdt16.append(jnp.transpose(k * jnp.exp(gc_last - gc)).astype(BF16))
        rhs16.append(jnp.concatenate([v * beta, k_beta * jnp.exp(gc)], axis=1).astype(BF16))
        decay.append(jnp.exp(gc_last))

    yield
    p = neg_a
    t = [eye + x for x in p]
    for _ in range(int(math.log2(CHUNK)) - 1):
        p16 = [x.astype(BF16) for x in p]
        p = [_dot(x, x) for x in p16]
        yield
        t = [y + _dot(y.astype(BF16), x.astype(BF16)) for y, x in zip(t, p)]
        yield
    uw = [_dot(y.astype(BF16), b) for y, b in zip(t, rhs16)]
    yield

    nw = nw_ref[...]
    state = [s_ref[h] for h in range(nh)]
    for c in range(chunks):
        r = slice(c * CHUNK, (c + 1) * CHUNK)
        pair = [c * nh + h for h in range(nh)]
        s16 = [s.astype(BF16) for s in state]
        ws = [_dot(uw[i][:, dh:].astype(BF16), s16[h]) for h, i in enumerate(pair)]
        qs = [_dot(qd16[i], s16[h]) for h, i in enumerate(pair)]
        yield
        vn16 = [(uw[i][:, :dh] - ws[h]).astype(BF16) for h, i in enumerate(pair)]
        out = [qs[h] + _dot(attn16[i], vn16[h]) for h, i in enumerate(pair)]
        state = [state[h] * decay[i] + _dot(kdt16[i], vn16[h]) for h, i in enumerate(pair)]
        for h in range(nh):
            hc = slice(h * dh, (h + 1) * dh)
            store(r, hc, _rms(out[h], nw) * _silu(z_ref[r, hc].astype(F32)))
        yield
    for h in range(nh):
        s_ref[h] = state[h]


def _deltanet_body(act_ref, z_ref, gates_ref, alog_ref, dtb_ref, nw_ref, o_ref, s_ref, *,
                   chunks):
    @pl.when(pl.program_id(1) == 0)
    def _():
        s_ref[...] = jnp.zeros_like(s_ref)

    def store(rows, cols, value):
        o_ref[rows, cols] = value.astype(o_ref.dtype)

    _drain(_deltanet_stages(act_ref, z_ref, gates_ref, alog_ref, dtb_ref, nw_ref, s_ref,
                            chunks, store))


def _deltanet(act, z, gates, alog_vec, dtb_vec, nw, batch, lp, chunks):
    rows = act.shape[0]
    rs = chunks * CHUNK
    steps = lp // rs
    idx = lambda b, s: (b * steps + s, 0)
    return pl.pallas_call(
        functools.partial(_deltanet_body, chunks=chunks),
        grid=(batch, steps),
        in_specs=[pl.BlockSpec((rs, act.shape[1]), idx), pl.BlockSpec((rs, z.shape[1]), idx),
                  pl.BlockSpec((rs, gates.shape[1]), idx),
                  _resident(alog_vec.shape), _resident(dtb_vec.shape), _resident(nw.shape)],
        out_specs=pl.BlockSpec((rs, DN_DIM), idx),
        out_shape=jax.ShapeDtypeStruct((rows, DN_DIM), BF16),
        scratch_shapes=[pltpu.VMEM((DN_HEADS, DN_HEAD_DIM, DN_HEAD_DIM), F32)],
        compiler_params=_params("parallel", "arbitrary"),
        name="deltanet",
    )(act, z, gates, alog_vec, dtb_vec, nw)


def _mix_ffn_body(sink_ref, q_ref, kc_ref, kp_ref, km_ref, vc_ref, vp_ref, vm_ref, bias_ref,
                  act_ref, z_ref, gates_ref, alog_ref, dtb_ref, dnw_ref,
                  h_ref, wo_ref, nwo_ref, nwa_ref, nwb_ref, wg_ref, wu_ref, wd_ref,
                  o_ref, s_ref, mix_ref, a_ref, *, blocks, chunks, steps, last):
    t = pl.program_id(0)
    seq_step = jnp.minimum(t, last) % steps
    write_slot = t % 2
    read_slot = 1 - write_slot
    swa_width = q_ref.shape[1]
    groups = q_ref.shape[0] // GROUP_ROWS

    @pl.when(t == 0)
    def _():
        mix_ref[...] = jnp.zeros_like(mix_ref)

    @pl.when(seq_step == 0)
    def _():
        s_ref[...] = jnp.zeros_like(s_ref)

    def store_swa(rows, cols, value):
        mix_ref[write_slot, rows, cols] = value.astype(mix_ref.dtype)

    def store_dn(rows, cols, value):
        cols = slice(swa_width + cols.start, swa_width + cols.stop)
        mix_ref[write_slot, rows, cols] = value.astype(mix_ref.dtype)

    def mixer_stages():
        yield from _deltanet_stages(act_ref, z_ref, gates_ref, alog_ref, dtb_ref, dnw_ref,
                                    s_ref, chunks, store_dn)
        yield from _swa_stages(sink_ref, q_ref, kc_ref, kp_ref, km_ref, vc_ref, vp_ref,
                               vm_ref, bias_ref, seq_step, blocks, store_swa)

    def ffn_stages():
        accs = []
        for part in range(groups):
            accs.append(_dot(mix_ref[read_slot, _group(part), :], wo_ref[...]))
            yield
        hs = [h_ref[_group(part), :] + _rms(acc, nwo_ref[...]) for part, acc in enumerate(accs)]
        outs = []
        yield from _ffn_stages(hs, nwa_ref[...], nwb_ref[...], wg_ref, wu_ref, wd_ref, a_ref,
                               outs)
        for part, out in enumerate(outs):
            o_ref[_group(part), :] = out

    _interleave(ffn_stages(), mixer_stages())


def _mix_ffn(qa, ka, va, bias, sinks, act, z, gates, alog_vec, dtb_vec, dn_nw, h, w_out, norm4,
             layer, wg, wu, wd, batch, lp, blocks, chunks):
    d = norm4.shape[-1]
    d_ff = wg.shape[-1]
    rs = blocks * SWA_BLOCK
    assert rs == chunks * CHUNK and rs % GROUP_ROWS == 0
    nb = lp // SWA_BLOCK
    steps = lp // rs
    last = batch * steps - 1
    meta_blocks = lp // N_META

    def tile(t):
        return jnp.minimum(t, last)

    cur = lambda t: (tile(t), 0)
    prev = lambda t: ((tile(t) // steps) * nb + jnp.maximum((tile(t) % steps) * blocks - 1, 0), 0)
    meta = lambda t: ((tile(t) // steps) * meta_blocks + PAD // N_META, 0)
    behind = lambda t: (jnp.maximum(t - 1, 0), 0)
    kv_w = ka.shape[1]
    return pl.pallas_call(
        functools.partial(_mix_ffn_body, blocks=blocks, chunks=chunks, steps=steps, last=last),
        grid=(last + 2,),
        in_specs=[pl.BlockSpec(memory_space=pltpu.SMEM),
                  pl.BlockSpec((rs, qa.shape[1]), cur),
                  pl.BlockSpec((rs, kv_w), cur), pl.BlockSpec((SWA_BLOCK, kv_w), prev),
                  pl.BlockSpec((N_META, kv_w), meta),
                  pl.BlockSpec((rs, kv_w), cur), pl.BlockSpec((SWA_BLOCK, kv_w), prev),
                  pl.BlockSpec((N_META, kv_w), meta),
                  _resident(bias.shape),
                  pl.BlockSpec((rs, act.shape[1]), cur), pl.BlockSpec((rs, z.shape[1]), cur),
                  pl.BlockSpec((rs, gates.shape[1]), cur),
                  _resident(alog_vec.shape), _resident(dtb_vec.shape), _resident(dn_nw.shape),
                  pl.BlockSpec((rs, d), behind), _resident(w_out.shape),
                  _pick(norm4, layer, 3), _pick(norm4, layer, 4), _pick(norm4, layer, 5),
                  _pick(wg, layer, 1), _pick(wu, layer, 1), _pick(wd, layer, 1)],
        out_specs=pl.BlockSpec((rs, d), behind),
        out_shape=jax.ShapeDtypeStruct(h.shape, F32),
        scratch_shapes=[pltpu.VMEM((DN_HEADS, DN_HEAD_DIM, DN_HEAD_DIM), F32),
                        pltpu.VMEM((2, rs, qa.shape[1] + DN_DIM), BF16),
                        pltpu.VMEM((rs, d_ff), BF16)],
        compiler_params=_params("arbitrary"),
        name="mix_ffn",
    )(sinks, qa, ka, ka, ka, va, va, va, bias, act, z, gates, alog_vec, dtb_vec, dn_nw,
      h, w_out, norm4, norm4, norm4, wg, wu, wd)


def _gla_body(q_ref, k_ref, v_ref, g_ref, gk_ref, wgu_ref, bg_ref, nw_ref, o_ref, s_ref, *,
              chunks):
    nh = GLA_HEADS
    dk = q_ref.shape[1] // nh
    dv = v_ref.shape[1] // nh

    @pl.when(pl.program_id(1) == 0)
    def _():
        s_ref[...] = jnp.zeros_like(s_ref)

    logits = _dot(gk_ref[...].astype(BF16), wgu_ref[...]) + bg_ref[...]
    glog = (jnp.minimum(logits, 0.0) - jnp.log(1.0 + jnp.exp(-jnp.abs(logits)))) / GLA_GATE_NORM
    incl, _, _ = _tri_masks()
    tri = incl.astype(F32)
    bcum = [_dot(tri, glog[c * CHUNK:(c + 1) * CHUNK, :], HIGHEST) for c in range(chunks)]

    items = [(c, h) for c in range(chunks) for h in range(nh)]
    qd16, kn16, kdt16, decay = [], [], [], []
    for c, h in items:
        r = slice(c * CHUNK, (c + 1) * CHUNK)
        kc = slice(h * dk, (h + 1) * dk)
        bc = bcum[c][:, kc]
        q = q_ref[r, kc].astype(F32) * dk ** -0.5
        k = k_ref[r, kc].astype(F32)
        b_last = bc[CHUNK - 1:CHUNK, :]
        qd16.append((q * jnp.exp(bc)).astype(BF16))
        kn16.append((k * jnp.exp(-bc)).astype(BF16))
        kdt16.append(jnp.transpose(k * jnp.exp(b_last - bc)).astype(BF16))
        decay_rows = jnp.broadcast_to(jnp.exp(b_last), (8, dk))
        decay.append(jnp.transpose(decay_rows)[:, :1])
    values = [v_ref[c * CHUNK:(c + 1) * CHUNK, h * dv:(h + 1) * dv] for c, h in items]
    attn16 = [jnp.where(incl, _dot_nt(a, b), 0.0).astype(BF16) for a, b in zip(qd16, kn16)]
    intra = [_dot(a, v) for a, v in zip(attn16, values)]
    update = [_dot(kt, v) for kt, v in zip(kdt16, values)]

    nw = nw_ref[...]
    state = [s_ref[h] for h in range(nh)]
    for c in range(chunks):
        r = slice(c * CHUNK, (c + 1) * CHUNK)
        for h in range(nh):
            i = c * nh + h
            vc = slice(h * dv, (h + 1) * dv)
            o = intra[i] + _dot(qd16[i], state[h].astype(BF16))
            state[h] = state[h] * decay[i] + update[i]
            o = _rms(o, nw) * _silu(g_ref[r, vc].astype(F32))
            o_ref[r, vc] = o.astype(o_ref.dtype)
    for h in range(nh):
        s_ref[h] = state[h]


def _gla(q, k, v, g, gk, wgu, bg, nw, batch, lp, chunks):
    rows = q.shape[0]
    rs = chunks * CHUNK
    steps = lp // rs
    idx = lambda b, s: (b * steps + s, 0)
    dk = q.shape[1] // GLA_HEADS
    dv = v.shape[1] // GLA_HEADS
    return pl.pallas_call(
        functools.partial(_gla_body, chunks=chunks),
        grid=(batch, steps),
        in_specs=[pl.BlockSpec((rs, q.shape[1]), idx), pl.BlockSpec((rs, k.shape[1]), idx),
                  pl.BlockSpec((rs, v.shape[1]), idx), pl.BlockSpec((rs, g.shape[1]), idx),
                  pl.BlockSpec((rs, gk.shape[1]), idx), _resident(wgu.shape),
                  _resident(bg.shape), _resident(nw.shape)],
        out_specs=pl.BlockSpec((rs, v.shape[1]), idx),
        out_shape=jax.ShapeDtypeStruct((rows, v.shape[1]), BF16),
        scratch_shapes=[pltpu.VMEM((GLA_HEADS, dk, dv), F32)],
        compiler_params=_params("parallel", "arbitrary"),
        name="gla",
    )(q, k, v, g, gk, wgu, bg, nw)


def _pad_cols(w, width):
    return jnp.pad(w, ((0, 0), (0, width - w.shape[1])))


def _per_step(n, preferred):
    for c in preferred:
        if n % c == 0:
            return c
    return 1


def kernel(x, meta_tokens, norm_w, ffn_w_gate, ffn_w_up, ffn_w_down, rel_bias_table,
           even_w_in, even_conv_w, swa_sinks, dn_a_log, dn_dt_bias, dn_norm_w, even_w_out,
           odd_w_in, gla_w_gate_up, gla_b_gate, gla_norm_w, odd_w_out):
    batch, seq, d = x.shape
    depth = norm_w.shape[0]
    lp = SWA_BLOCK + seq
    assert seq % ROW_TILE == 0 and (batch * lp) % ROW_TILE == 0
    rows = batch * lp
    chunks = _per_step(lp // CHUNK, (6, 3, 2))
    swa_blocks = _per_step(lp // SWA_BLOCK, (3, 2))

    meta_block = jnp.pad(meta_tokens.astype(x.dtype), ((PAD, 0), (0, 0)))
    norm4 = norm_w[:, :, None, :]
    wg = ffn_w_gate.astype(BF16)
    wu = ffn_w_up.astype(BF16)
    wd = ffn_w_down.astype(BF16)

    swa_q = SWA_Q_HEADS * SWA_HEAD_DIM
    swa_kv = SWA_KV_HEADS * SWA_HEAD_DIM
    bias = None
    h = x.reshape(batch * seq, d)
    for layer in range(depth):
        i = layer // 2
        embed = meta_block if layer == 0 else None
        final = layer == depth - 1
        if layer % 2 == 0:
            assert even_w_in.shape[2] == swa_q + 2 * swa_kv + 4 * DN_DIM + 2 * DN_HEADS
            widths = (swa_q, swa_kv, swa_kv, 3 * DN_DIM, DN_DIM, LANE)
            w_in = _pad_cols(even_w_in[i], sum(widths)).astype(BF16)
            h, qa, ka, va, act_b, z_b, gates = _ffn_proj(
                h, embed, norm4, layer, wg, wu, wd, w_in, widths,
                (BF16, BF16, BF16, BF16, BF16, F32), rows, lp, seq,
                conv_out=3, conv_w=even_conv_w[i])
            if bias is None:
                bias = _swa_bias(rel_bias_table)
            lane_pad = (DN_HEADS, LANE - 2 * DN_HEADS)
            alog_vec = jnp.pad(dn_a_log[i], lane_pad)[None, :]
            dtb_vec = jnp.pad(dn_dt_bias[i], lane_pad)[None, :]
            dn_nw = dn_norm_w[i][None, :]
            w_out = even_w_out[i]
            if not final and swa_blocks * SWA_BLOCK == chunks * CHUNK:
                h = _mix_ffn(qa, ka, va, bias, swa_sinks[i], act_b, z_b, gates, alog_vec,
                             dtb_vec, dn_nw, h, w_out.astype(BF16), norm4, layer, wg, wu, wd,
                             batch, lp, swa_blocks, chunks)
                continue
            o_a = _swa(qa, ka, va, bias, swa_sinks[i], batch, lp, swa_blocks)
            o_b = _deltanet(act_b, z_b, gates, alog_vec, dtb_vec, dn_nw, batch, lp, chunks)
            mixed = [o_a, o_b]
        else:
            key_dim = gla_w_gate_up.shape[2]
            val_dim = odd_w_out.shape[1]
            widths = (key_dim, key_dim, val_dim, val_dim, LANE)
            w_in = _pad_cols(odd_w_in[i], sum(widths)).astype(BF16)
            h, q, k, v, g, gk = _ffn_proj(
                h, embed, norm4, layer, wg, wu, wd, w_in, widths,
                (BF16, BF16, BF16, BF16, F32), rows, lp, seq)
            wgu = jnp.pad(gla_w_gate_up[i], ((0, LANE - GLA_GATE_RANK), (0, 0))).astype(BF16)
            o = _gla(q, k, v, g, gk, wgu, gla_b_gate[i][None, :], gla_norm_w[i][None, :],
                     batch, lp, chunks)
            mixed, w_out = [o], odd_w_out[i]
        h = _proj_ffn(mixed, h, w_out.astype(BF16), norm4, layer, wg, wu, wd, final,
                      batch, lp, seq)
    return h.reshape(batch, seq, d)
```

```python
import functools
import math

import numpy as np
import jax
import jax.numpy as jnp
from jax import lax
from jax.experimental import pallas as pl
from jax.experimental.pallas import tpu as pltpu

F32 = jnp.float32
BF16 = jnp.bfloat16
HIGHEST = lax.Precision.HIGHEST

N_META = 16
NORM_EPS = 1e-6
NEG_INF = -1e30
SWA_Q_HEADS = 8
SWA_KV_HEADS = 2
SWA_HEAD_DIM = 64
SWA_BLOCK = 128
REL_BUCKETS = 32
REL_MAX_DIST = 128
DN_HEADS = 4
DN_HEAD_DIM = 128
DN_DIM = DN_HEADS * DN_HEAD_DIM
DN_CONV = 4
GLA_HEADS = 4
GLA_GATE_RANK = 16
GLA_GATE_NORM = 16.0
CHUNK = 64

PAD = SWA_BLOCK - N_META
LANE = 128
ROW_TILE = 512
FF_TILE = 256
VMEM_LIMIT = 56 * 1024 * 1024


def _rms(x, w):
    return x * lax.rsqrt(jnp.mean(x * x, axis=-1, keepdims=True) + NORM_EPS) * w


def _silu(x):
    return x * jax.nn.sigmoid(x)


def _softplus(x):
    return jnp.maximum(x, 0.0) + jnp.log(1.0 + jnp.exp(-jnp.abs(x)))


def _dot(a, b, precision=None):
    return jnp.dot(a, b, preferred_element_type=F32, precision=precision)


def _dot_nt(a, b, precision=None):
    return lax.dot_general(a, b, (((1,), (1,)), ((), ())), preferred_element_type=F32,
                           precision=precision)


def _dot_tn(a, b, precision=None):
    return lax.dot_general(a, b, (((0,), (0,)), ((), ())), preferred_element_type=F32,
                           precision=precision)


def _resident(shape):
    nd = len(shape)
    return pl.BlockSpec(shape, lambda *_: (0,) * nd, pipeline_mode=pl.Buffered(1))


def _params(*semantics):
    return pltpu.CompilerParams(dimension_semantics=semantics, vmem_limit_bytes=VMEM_LIMIT)


SUB = ROW_TILE // SWA_BLOCK


def _pick(arr, *lead):
    rest = arr.shape[len(lead):]
    index = tuple(lead) + (0,) * len(rest)
    return pl.BlockSpec((None,) * len(lead) + rest, lambda i: index,
                        pipeline_mode=pl.Buffered(1))


def _tile_specs(width):
    return [pl.BlockSpec((ROW_TILE, width), lambda i: (i, 0))]


def _real_specs(width, lp, seq):
    tiles = seq // ROW_TILE
    blocks = lp // SWA_BLOCK

    def spec(j):
        return pl.BlockSpec((SWA_BLOCK, width),
                            lambda i: ((i // tiles) * blocks + 1 + (i % tiles) * SUB + j, 0))
    return [spec(j) for j in range(SUB)]


def _embed_specs(width, lp, seq):
    blocks = lp // SWA_BLOCK
    x_blocks = seq // SWA_BLOCK

    def spec(j):
        def index(i):
            blk = i * SUB + j
            return ((blk // blocks) * x_blocks + jnp.maximum(blk % blocks - 1, 0), 0)
        return pl.BlockSpec((SWA_BLOCK, width), index)
    return [spec(j) for j in range(SUB)]


ROW_SPLIT = 4
GROUP_ROWS = ROW_TILE // ROW_SPLIT


def _group(part):
    return slice(part * GROUP_ROWS, (part + 1) * GROUP_ROWS)


def _load_rows(refs, part):
    if len(refs) == 1:
        return refs[0][_group(part), :]
    per = len(refs) // ROW_SPLIT
    return jnp.concatenate([r[...] for r in refs[part * per:(part + 1) * per]], axis=0)


def _drain(stages):
    for _ in stages:
        pass


def _interleave(main, side):
    for _ in main:
        next(side, None)
    _drain(side)


def _ffn_stages(xs, nwa, nwb, wg_ref, wu_ref, wd_ref, a_ref, outs):
    hns = [_rms(x, nwa).astype(BF16) for x in xs]
    d_ff = wg_ref.shape[1]
    for part, hn in enumerate(hns):
        for c in range(d_ff // FF_TILE):
            cols = slice(c * FF_TILE, (c + 1) * FF_TILE)
            g = _dot(hn, wg_ref[:, cols])
            u = _dot(hn, wu_ref[:, cols])
            a_ref[_group(part), cols] = (_silu(g) * u).astype(BF16)
            yield
    fs = []
    for part in range(len(xs)):
        fs.append(_dot(a_ref[_group(part), :], wd_ref[...]))
        yield
    outs.extend(x + 0.5 * _rms(f, nwb) for x, f in zip(xs, fs))


def _ffn_core(xs, nwa, nwb, wg_ref, wu_ref, wd_ref, a_ref):
    outs = []
    _drain(_ffn_stages(xs, nwa, nwb, wg_ref, wu_ref, wd_ref, a_ref, outs))
    return outs


CONV_HALO = 8


def _ffn_proj_body(*refs, n_rows, embed_blocks, n_out, conv_out, lp):
    row_refs = refs[:n_rows]
    refs = refs[n_rows:]
    if embed_blocks:
        meta_ref, refs = refs[0], refs[1:]
    if conv_out is not None:
        convw_ref, refs = refs[0], refs[1:]
        xpad_ref = refs[9 + n_out]

        @pl.when(pl.program_id(0) == 0)
        def _():
            xpad_ref[0:CONV_HALO, :] = jnp.zeros((CONV_HALO, xpad_ref.shape[1]), F32)

    nwa_ref, nwb_ref, wg_ref, wu_ref, wd_ref, nwm_ref, win_ref = refs[:7]
    h_ref = refs[7]
    out_refs = refs[8:8 + n_out]
    a_ref = refs[8 + n_out]
    offsets = np.cumsum([0] + [o.shape[1] for o in out_refs])
    xs = []
    for part in range(ROW_SPLIT):
        if embed_blocks:
            per = n_rows // ROW_SPLIT
            first = pl.program_id(0) * SUB + part * per
            xs.append(jnp.concatenate(
                [jnp.where((first + j) % embed_blocks == 0, meta_ref[...], r[...])
                 for j, r in enumerate(row_refs[part * per:(part + 1) * per])], axis=0))
        else:
            xs.append(_load_rows(row_refs, part))
    hs = _ffn_core(xs, nwa_ref[...], nwb_ref[...], wg_ref, wu_ref, wd_ref, a_ref)
    hns = [_rms(h, nwm_ref[...]).astype(BF16) for h in hs]
    for part, (h, hn) in enumerate(zip(hs, hns)):
        rows = _group(part)
        h_ref[rows, :] = h
        for i, o_ref in enumerate(out_refs):
            p = _dot(hn, win_ref[:, offsets[i]:offsets[i + 1]])
            if i == conv_out:
                xpad_ref[CONV_HALO + part * GROUP_ROWS:CONV_HALO + (part + 1) * GROUP_ROWS,
                         :] = p
            else:
                o_ref[rows, :] = p.astype(o_ref.dtype)

    if conv_out is not None:
        taps = convw_ref.shape[0]
        cw = convw_ref[...]
        y = cw[taps - 1:taps, :] * xpad_ref[CONV_HALO:, :]
        for j in range(taps - 1):
            lo = CONV_HALO - (taps - 1 - j)
            y = y + cw[j:j + 1, :] * xpad_ref[lo:lo + ROW_TILE, :]
        tail = xpad_ref[ROW_TILE:, :]
        row = (pl.program_id(0) * ROW_TILE) % lp + lax.broadcasted_iota(
            jnp.int32, (ROW_TILE, 1), 0)
        row = jnp.where(row >= lp, row - lp, row)
        out_refs[conv_out][...] = jnp.where(row >= PAD, _silu(y), 0.0).astype(
            out_refs[conv_out].dtype)
        xpad_ref[0:CONV_HALO, :] = tail


def _ffn_proj(src, meta_block, norm4, layer, wg, wu, wd, w_in, widths, dtypes, rows, lp, seq,
              conv_out=None, conv_w=None):
    d = norm4.shape[-1]
    d_ff = wg.shape[-1]
    assert sum(widths) == w_in.shape[1] and lp > ROW_TILE
    if meta_block is None:
        row_specs, extra, extra_specs, embed_blocks = _tile_specs(d), [], [], 0
    else:
        row_specs = _embed_specs(d, lp, seq)
        extra, extra_specs, embed_blocks = [meta_block], [_resident(meta_block.shape)], (
            lp // SWA_BLOCK)
    scratch = [pltpu.VMEM((ROW_TILE, d_ff), BF16)]
    if conv_out is not None:
        extra, extra_specs = extra + [conv_w], extra_specs + [_resident(conv_w.shape)]
        scratch.append(pltpu.VMEM((CONV_HALO + ROW_TILE, widths[conv_out]), F32))
    tile = lambda n: pl.BlockSpec((ROW_TILE, n), lambda i: (i, 0))
    return pl.pallas_call(
        functools.partial(_ffn_proj_body, n_rows=len(row_specs), embed_blocks=embed_blocks,
                          n_out=len(widths), conv_out=conv_out, lp=lp),
        grid=(rows // ROW_TILE,),
        in_specs=row_specs + extra_specs
        + [_pick(norm4, layer, 0), _pick(norm4, layer, 1), _pick(wg, layer, 0),
           _pick(wu, layer, 0), _pick(wd, layer, 0), _pick(norm4, layer, 2),
           _resident(w_in.shape)],
        out_specs=[tile(d)] + [tile(n) for n in widths],
        out_shape=[jax.ShapeDtypeStruct((rows, d), F32)]
        + [jax.ShapeDtypeStruct((rows, n), dt) for n, dt in zip(widths, dtypes)],
        scratch_shapes=scratch,
        compiler_params=_params("arbitrary"),
        name="ffn_proj",
    )(*([src] * len(row_specs)), *extra, norm4, norm4, wg, wu, wd, norm4, w_in)


def _proj_ffn_body(*refs, n_rows, n_x):
    x_groups = [refs[g * n_rows:(g + 1) * n_rows] for g in range(n_x)]
    refs = refs[n_x * n_rows:]
    h_refs = refs[:n_rows]
    wo_ref, nwo_ref, nwa_ref, nwb_ref, wg_ref, wu_ref, wd_ref, o_ref, a_ref = refs[n_rows:]
    accs = []
    for part in range(ROW_SPLIT):
        acc = None
        off = 0
        for group in x_groups:
            xs = _load_rows(group, part)
            k = xs.shape[1]
            p = _dot(xs, wo_ref[off:off + k, :])
            acc = p if acc is None else acc + p
            off += k
        accs.append(acc)
    hs = [_load_rows(h_refs, part) + _rms(acc, nwo_ref[...]) for part, acc in enumerate(accs)]
    outs = _ffn_core(hs, nwa_ref[...], nwb_ref[...], wg_ref, wu_ref, wd_ref, a_ref)
    for part, out in enumerate(outs):
        o_ref[_group(part), :] = out


def _proj_ffn(xs, h, w_out, norm4, layer, wg, wu, wd, final, batch, lp, seq):
    d = norm4.shape[-1]
    d_ff = wg.shape[-1]
    if final:
        specs = lambda width: _real_specs(width, lp, seq)
        out_rows = batch * seq
    else:
        specs = _tile_specs
        out_rows = batch * lp
    h_specs = specs(d)
    n_rows = len(h_specs)
    in_specs, args = [], []
    for x in xs:
        in_specs += specs(x.shape[1])
        args += [x] * n_rows
    in_specs += h_specs + [_resident(w_out.shape), _pick(norm4, layer, 3),
                           _pick(norm4, layer, 4), _pick(norm4, layer, 5),
                           _pick(wg, layer, 1), _pick(wu, layer, 1), _pick(wd, layer, 1)]
    args += [h] * n_rows + [w_out, norm4, norm4, norm4, wg, wu, wd]
    return pl.pallas_call(
        functools.partial(_proj_ffn_body, n_rows=n_rows, n_x=len(xs)),
        grid=(out_rows // ROW_TILE,),
        in_specs=in_specs,
        out_specs=pl.BlockSpec((ROW_TILE, d), lambda i: (i, 0)),
        out_shape=jax.ShapeDtypeStruct((out_rows, d), F32),
        scratch_shapes=[pltpu.VMEM((ROW_TILE, d_ff), BF16)],
        compiler_params=_params("parallel"),
        name="proj_ffn",
    )(*args)


def _t5_bucket_np(rel):
    n = np.maximum(rel, 0)
    max_exact = REL_BUCKETS // 2
    n_f = np.maximum(n, 1).astype(np.float32)
    large = max_exact + (np.log(n_f / np.float32(max_exact))
                         / np.float32(math.log(REL_MAX_DIST / max_exact))
                         * np.float32(REL_BUCKETS - max_exact)).astype(np.int32)
    large = np.minimum(large, REL_BUCKETS - 1)
    return np.where(n < max_exact, n, large).astype(np.int32)


def _swa_bucket_index():
    i = np.arange(SWA_BLOCK)[:, None]
    c = np.arange(SWA_BLOCK)[None, :]
    out = np.full((3, SWA_BLOCK, 2 * SWA_BLOCK), -1, np.int32)
    for variant in range(3):
        pos_q = variant * SWA_BLOCK + i - PAD
        pos_k = variant * SWA_BLOCK + c - PAD - np.where(c > i, SWA_BLOCK, 0)
        rel_b = pos_q - pos_k
        assert ((rel_b >= 0) & (rel_b < SWA_BLOCK)).all()
        out[variant, :, :SWA_BLOCK] = np.where(pos_k >= N_META, _t5_bucket_np(rel_b), -1)
        rel_m = pos_q - c
        meta = (c < N_META) & (rel_m >= 0)
        out[variant, :, SWA_BLOCK:] = np.where(meta, _t5_bucket_np(rel_m), -1)
    return out


_SWA_BUCKET_INDEX = _swa_bucket_index()


def _bias_body(tbl_ref, idx_ref, o_ref):
    head = pl.program_id(1)
    idx = idx_ref[0]
    acc = jnp.full(idx.shape, NEG_INF, F32)
    for b in range(REL_BUCKETS):
        acc = jnp.where(idx == b, tbl_ref[b, head], acc)
    o_ref[0, 0] = acc


def _swa_bias(rel_table):
    idx = jnp.asarray(_SWA_BUCKET_INDEX)
    group = SWA_Q_HEADS // SWA_KV_HEADS
    width = idx.shape[2]
    return pl.pallas_call(
        _bias_body,
        grid=(3, SWA_Q_HEADS),
        in_specs=[pl.BlockSpec(memory_space=pltpu.SMEM),
                  pl.BlockSpec((1, SWA_BLOCK, width), lambda v, h: (v, 0, 0))],
        out_specs=pl.BlockSpec((1, 1, SWA_BLOCK, width),
                               lambda v, h: (v, h // group, h % group, 0)),
        out_shape=jax.ShapeDtypeStruct((3, SWA_KV_HEADS, group * SWA_BLOCK, width), F32),
        compiler_params=_params("arbitrary", "arbitrary"),
        name="swa_bias",
    )(rel_table, idx)


def _swa_stages(sink_ref, q_ref, kc_ref, kp_ref, km_ref, vc_ref, vp_ref, vm_ref, bias_ref,
                step, blocks, store):
    dh = SWA_HEAD_DIM
    blk = SWA_BLOCK
    group = SWA_Q_HEADS // SWA_KV_HEADS
    scale = dh ** -0.5
    zpad = jnp.zeros((blk - N_META, dh), BF16)
    row = lax.broadcasted_iota(jnp.int32, (group * blk, blk), 0) % blk
    col = lax.broadcasted_iota(jnp.int32, (group * blk, blk), 1)
    from_prev = col > row

    ones_col = (lax.broadcasted_iota(jnp.int32, (3 * blk, dh), 1) == 0).astype(BF16)

    units = [(j, hk) for j in range(blocks) for hk in range(SWA_KV_HEADS)]
    scores, values = [], []
    for j, hk in units:
        r = slice(j * blk, (j + 1) * blk)
        kv = slice(hk * dh, (hk + 1) * dh)
        k_prev = kp_ref[:, kv] if j == 0 else kc_ref[(j - 1) * blk:j * blk, kv]
        v_prev = vp_ref[:, kv] if j == 0 else vc_ref[(j - 1) * blk:j * blk, kv]
        k_all = jnp.concatenate([k_prev, kc_ref[r, kv], km_ref[:, kv], zpad], axis=0)
        v_all = jnp.concatenate([v_prev, vc_ref[r, kv], vm_ref[:, kv], zpad], axis=0)
        values.append(jnp.concatenate([v_all, ones_col], axis=1))
        q = jnp.concatenate([q_ref[r, (hk * group + g) * dh:(hk * group + g + 1) * dh]
                             for g in range(group)], axis=0) * scale
        scores.append(_dot_nt(q, k_all))
        yield

    probs, sink_terms = [], []
    for (j, hk), s in zip(units, scores):
        bias = bias_ref[jnp.minimum(step * blocks + j, 2), hk]
        s_band = jnp.where(from_prev, s[:, :blk], s[:, blk:2 * blk]) + bias[:, :blk]
        s_meta = s[:, 2 * blk:] + bias[:, blk:]
        sink = jnp.concatenate([jnp.full((blk, 1), sink_ref[hk * group + g], F32)
                                for g in range(group)], axis=0)
        m = jnp.maximum(jnp.max(jnp.maximum(s_band, s_meta), axis=-1, keepdims=True), sink)
        e_band = jnp.exp(s_band - m)
        e_meta = jnp.exp(s_meta - m)
        sink_terms.append(jnp.exp(sink - m))
        probs.append(jnp.concatenate([jnp.where(from_prev, e_band, 0.0),
                                      jnp.where(from_prev, 0.0, e_band), e_meta],
                                     axis=1).astype(BF16))

    for (j, hk), p, v_ext, sink_term in zip(units, probs, values, sink_terms):
        pv = _dot(p, v_ext)
        o = pv[:, :dh] / (pv[:, dh:dh + 1] + sink_term)
        for g in range(group):
            hq = hk * group + g
            store(slice(j * blk, (j + 1) * blk), slice(hq * dh, (hq + 1) * dh),
                  o[g * blk:(g + 1) * blk])
        yield


def _swa_body(sink_ref, q_ref, kc_ref, kp_ref, km_ref, vc_ref, vp_ref, vm_ref, bias_ref, o_ref,
              *, blocks):
    def store(rows, cols, value):
        o_ref[rows, cols] = value.astype(o_ref.dtype)

    _drain(_swa_stages(sink_ref, q_ref, kc_ref, kp_ref, km_ref, vc_ref, vp_ref, vm_ref,
                       bias_ref, pl.program_id(1), blocks, store))


def _swa(q, k, v, bias, sinks, batch, lp, blocks):
    rows = q.shape[0]
    nb = lp // SWA_BLOCK
    steps = nb // blocks
    meta_blocks = lp // N_META
    cur = lambda b, s: (b * steps + s, 0)
    prev = lambda b, s: (b * nb + jnp.maximum(s * blocks - 1, 0), 0)
    meta = lambda b, s: (b * meta_blocks + PAD // N_META, 0)
    kv_w = k.shape[1]
    rs = blocks * SWA_BLOCK
    return pl.pallas_call(
        functools.partial(_swa_body, blocks=blocks),
        grid=(batch, steps),
        in_specs=[pl.BlockSpec(memory_space=pltpu.SMEM),
                  pl.BlockSpec((rs, q.shape[1]), cur),
                  pl.BlockSpec((rs, kv_w), cur), pl.BlockSpec((SWA_BLOCK, kv_w), prev),
                  pl.BlockSpec((N_META, kv_w), meta),
                  pl.BlockSpec((rs, kv_w), cur), pl.BlockSpec((SWA_BLOCK, kv_w), prev),
                  pl.BlockSpec((N_META, kv_w), meta),
                  _resident(bias.shape)],
        out_specs=pl.BlockSpec((rs, q.shape[1]), cur),
        out_shape=jax.ShapeDtypeStruct((rows, q.shape[1]), BF16),
        compiler_params=_params("parallel", "arbitrary"),
        name="swa",
    )(sinks, q, k, k, k, v, v, v, bias)


def _tri_masks():
    row = lax.broadcasted_iota(jnp.int32, (CHUNK, CHUNK), 0)
    col = lax.broadcasted_iota(jnp.int32, (CHUNK, CHUNK), 1)
    return row >= col, row > col, row == col


def _deltanet_stages(act_ref, z_ref, gates_ref, alog_ref, dtb_ref, nw_ref, s_ref, chunks,
                     store):
    nh, dh = DN_HEADS, DN_HEAD_DIM
    gt = gates_ref[...]
    beta_full = jax.nn.sigmoid(gt)
    g_full = -jnp.exp(alog_ref[...]) * _softplus(gt + dtb_ref[...])
    incl, strict, diag = _tri_masks()
    eye = diag.astype(F32)
    tri = incl.astype(F32)
    gc_chunks = [_dot(tri, g_full[c * CHUNK:(c + 1) * CHUNK, :], HIGHEST)
                 for c in range(chunks)]
    items = [(c, h) for c in range(chunks) for h in range(nh)]
    yield

    neg_a, attn16, qd16, kdt16, rhs16, decay = [], [], [], [], [], []
    for c, h in items:
        if h == 0 and c % 2 == 0 and c > 0:
            yield
        r = slice(c * CHUNK, (c + 1) * CHUNK)
        q = act_ref[r, h * dh:(h + 1) * dh].astype(F32)
        k = act_ref[r, DN_DIM + h * dh:DN_DIM + (h + 1) * dh].astype(F32)
        v = act_ref[r, 2 * DN_DIM + h * dh:2 * DN_DIM + (h + 1) * dh].astype(F32)
        q = q * lax.rsqrt(jnp.sum(q * q, axis=-1, keepdims=True) + 1e-6) * dh ** -0.5
        k = k * lax.rsqrt(jnp.sum(k * k, axis=-1, keepdims=True) + 1e-6)
        beta = jnp.broadcast_to(beta_full[r, h:h + 1], (CHUNK, dh))
        gc = jnp.broadcast_to(gc_chunks[c][:, nh + h:nh + h + 1], (CHUNK, dh))
        gc_row = jnp.transpose(gc)[:CHUNK, :]
        gamma = jnp.where(incl, jnp.exp(jnp.where(incl, gc[:, :CHUNK] - gc_row, 0.0)), 0.0)
        k_beta = k * beta
        k16 = k.astype(BF16)
        neg_a.append(jnp.where(strict, -(_dot_nt(k_beta.astype(BF16), k16) * gamma), 0.0))
        attn16.append(jnp.where(incl, _dot_nt(q.astype(BF16), k16) * gamma, 0.0).astype(BF16))
        gc_last = gc[CHUNK - 1:CHUNK, :]
        qd16.append((q * jnp.exp(gc)).astype(BF16))
        kdt16.append(jnp.transpose(k * jnp.exp(gc_last - gc)).astype(BF16))
        rhs16.append(jnp.concatenate([v * beta, k_beta * jnp.exp(gc)], axis=1).astype(BF16))
        decay.append(jnp.exp(gc_last))

    yield
    levels = int(math.log2(CHUNK)) - 1
    t = [eye + x for x in neg_a]
    p16 = [x.astype(BF16) for x in neg_a]
    p16 = [_dot(x, x).astype(BF16) for x in p16]
    yield
    for level in range(levels):
        t = [y + _dot(y.astype(BF16), x) for y, x in zip(t, p16)]
        if level + 1 < levels:
            p16 = [_dot(x, x).astype(BF16) for x in p16]
        yield
    uw = [_dot(y.astype(BF16), b) for y, b in zip(t, rhs16)]
    yield

    nw = nw_ref[...]
    state = [s_ref[h] for h in range(nh)]
    for c in range(chunks):
        r = slice(c * CHUNK, (c + 1) * CHUNK)
        pair = [c * nh + h for h in range(nh)]
        s16 = [s.astype(BF16) for s in state]
        ws = [_dot(uw[i][:, dh:].astype(BF16), s16[h]) for h, i in enumerate(pair)]
        qs = [_dot(qd16[i], s16[h]) for h, i in enumerate(pair)]
        yield
        vn16 = [(uw[i][:, :dh] - ws[h]).astype(BF16) for h, i in enumerate(pair)]
        out = [qs[h] + _dot(attn16[i], vn16[h]) for h, i in enumerate(pair)]
        state = [state[h] * decay[i] + _dot(kdt16[i], vn16[h]) for h, i in enumerate(pair)]
        for h in range(nh):
            hc = slice(h * dh, (h + 1) * dh)
            store(r, hc, _rms(out[h], nw) * _silu(z_ref[r, hc].astype(F32)))
        yield
    for h in range(nh):
        s_ref[h] = state[h]


def _deltanet_body(act_ref, z_ref, gates_ref, alog_ref, dtb_ref, nw_ref, o_ref, s_ref, *,
                   chunks):
    @pl.when(pl.program_id(1) == 0)
    def _():
        s_ref[...] = jnp.zeros_like(s_ref)

    def store(rows, cols, value):
        o_ref[rows, cols] = value.astype(o_ref.dtype)

    _drain(_deltanet_stages(act_ref, z_ref, gates_ref, alog_ref, dtb_ref, nw_ref, s_ref,
                            chunks, store))


def _deltanet(act, z, gates, alog_vec, dtb_vec, nw, batch, lp, chunks):
    rows = act.shape[0]
    rs = chunks * CHUNK
    steps = lp // rs
    idx = lambda b, s: (b * steps + s, 0)
    return pl.pallas_call(
        functools.partial(_deltanet_body, chunks=chunks),
        grid=(batch, steps),
        in_specs=[pl.BlockSpec((rs, act.shape[1]), idx), pl.BlockSpec((rs, z.shape[1]), idx),
                  pl.BlockSpec((rs, gates.shape[1]), idx),
                  _resident(alog_vec.shape), _resident(dtb_vec.shape), _resident(nw.shape)],
        out_specs=pl.BlockSpec((rs, DN_DIM), idx),
        out_shape=jax.ShapeDtypeStruct((rows, DN_DIM), BF16),
        scratch_shapes=[pltpu.VMEM((DN_HEADS, DN_HEAD_DIM, DN_HEAD_DIM), F32)],
        compiler_params=_params("parallel", "arbitrary"),
        name="deltanet",
    )(act, z, gates, alog_vec, dtb_vec, nw)


def _mix_ffn_body(sink_ref, q_ref, kc_ref, kp_ref, km_ref, vc_ref, vp_ref, vm_ref, bias_ref,
                  act_ref, z_ref, gates_ref, alog_ref, dtb_ref, dnw_ref,
                  h_ref, wo_ref, nwo_ref, nwa_ref, nwb_ref, wg_ref, wu_ref, wd_ref,
                  o_ref, s_ref, mix_ref, a_ref, *, blocks, chunks, steps, last):
    t = pl.program_id(0)
    seq_step = jnp.minimum(t, last) % steps
    write_slot = t % 2
    read_slot = 1 - write_slot
    swa_width = q_ref.shape[1]
    groups = q_ref.shape[0] // GROUP_ROWS

    @pl.when(t == 0)
    def _():
        mix_ref[...] = jnp.zeros_like(mix_ref)

    @pl.when(seq_step == 0)
    def _():
        s_ref[...] = jnp.zeros_like(s_ref)

    def store_swa(rows, cols, value):
        mix_ref[write_slot, rows, cols] = value.astype(mix_ref.dtype)

    def store_dn(rows, cols, value):
        cols = slice(swa_width + cols.start, swa_width + cols.stop)
        mix_ref[write_slot, rows, cols] = value.astype(mix_ref.dtype)

    def mixer_stages():
        dn = _deltanet_stages(act_ref, z_ref, gates_ref, alog_ref, dtb_ref, dnw_ref,
                              s_ref, chunks, store_dn)
        swa = _swa_stages(sink_ref, q_ref, kc_ref, kp_ref, km_ref, vc_ref, vp_ref,
                          vm_ref, bias_ref, seq_step, blocks, store_swa)
        for _ in dn:
            yield
            if next(swa, StopIteration) is not StopIteration:
                yield
        yield from swa

    def ffn_stages():
        accs = []
        for part in range(groups):
            accs.append(_dot(mix_ref[read_slot, _group(part), :], wo_ref[...]))
            yield
        hs = [h_ref[_group(part), :] + _rms(acc, nwo_ref[...]) for part, acc in enumerate(accs)]
        outs = []
        yield from _ffn_stages(hs, nwa_ref[...], nwb_ref[...], wg_ref, wu_ref, wd_ref, a_ref,
                               outs)
        for part, out in enumerate(outs):
            o_ref[_group(part), :] = out

    _interleave(ffn_stages(), mixer_stages())


def _mix_ffn(qa, ka, va, bias, sinks, act, z, gates, alog_vec, dtb_vec, dn_nw, h, w_out, norm4,
             layer, wg, wu, wd, batch, lp, blocks, chunks):
    d = norm4.shape[-1]
    d_ff = wg.shape[-1]
    rs = blocks * SWA_BLOCK
    assert rs == chunks * CHUNK and rs % GROUP_ROWS == 0
    nb = lp // SWA_BLOCK
    steps = lp // rs
    last = batch * steps - 1
    meta_blocks = lp // N_META

    def tile(t):
        return jnp.minimum(t, last)

    cur = lambda t: (tile(t), 0)
    prev = lambda t: ((tile(t) // steps) * nb + jnp.maximum((tile(t) % steps) * blocks - 1, 0), 0)
    meta = lambda t: ((tile(t) // steps) * meta_blocks + PAD // N_META, 0)
    behind = lambda t: (jnp.maximum(t - 1, 0), 0)
    kv_w = ka.shape[1]
    return pl.pallas_call(
        functools.partial(_mix_ffn_body, blocks=blocks, chunks=chunks, steps=steps, last=last),
        grid=(last + 2,),
        in_specs=[pl.BlockSpec(memory_space=pltpu.SMEM),
                  pl.BlockSpec((rs, qa.shape[1]), cur),
                  pl.BlockSpec((rs, kv_w), cur), pl.BlockSpec((SWA_BLOCK, kv_w), prev),
                  pl.BlockSpec((N_META, kv_w), meta),
                  pl.BlockSpec((rs, kv_w), cur), pl.BlockSpec((SWA_BLOCK, kv_w), prev),
                  pl.BlockSpec((N_META, kv_w), meta),
                  _resident(bias.shape),
                  pl.BlockSpec((rs, act.shape[1]), cur), pl.BlockSpec((rs, z.shape[1]), cur),
                  pl.BlockSpec((rs, gates.shape[1]), cur),
                  _resident(alog_vec.shape), _resident(dtb_vec.shape), _resident(dn_nw.shape),
                  pl.BlockSpec((rs, d), behind), _resident(w_out.shape),
                  _pick(norm4, layer, 3), _pick(norm4, layer, 4), _pick(norm4, layer, 5),
                  _pick(wg, layer, 1), _pick(wu, layer, 1), _pick(wd, layer, 1)],
        out_specs=pl.BlockSpec((rs, d), behind),
        out_shape=jax.ShapeDtypeStruct(h.shape, F32),
        scratch_shapes=[pltpu.VMEM((DN_HEADS, DN_HEAD_DIM, DN_HEAD_DIM), F32),
                        pltpu.VMEM((2, rs, qa.shape[1] + DN_DIM), BF16),
                        pltpu.VMEM((rs, d_ff), BF16)],
        compiler_params=_params("arbitrary"),
        name="mix_ffn",
    )(sinks, qa, ka, ka, ka, va, va, va, bias, act, z, gates, alog_vec, dtb_vec, dn_nw,
      h, w_out, norm4, norm4, norm4, wg, wu, wd)


def _gla_body(q_ref, k_ref, v_ref, g_ref, gk_ref, wgu_ref, bg_ref, nw_ref, o_ref, s_ref, *,
              chunks):
    nh = GLA_HEADS
    dk = q_ref.shape[1] // nh
    dv = v_ref.shape[1] // nh

    @pl.when(pl.program_id(1) == 0)
    def _():
        s_ref[...] = jnp.zeros_like(s_ref)

    logits = _dot(gk_ref[...].astype(BF16), wgu_ref[...]) + bg_ref[...]
    glog = (jnp.minimum(logits, 0.0) - jnp.log(1.0 + jnp.exp(-jnp.abs(logits)))) / GLA_GATE_NORM
    incl, _, _ = _tri_masks()
    tri = incl.astype(F32)
    bcum = [_dot(tri, glog[c * CHUNK:(c + 1) * CHUNK, :], HIGHEST) for c in range(chunks)]

    items = [(c, h) for c in range(chunks) for h in range(nh)]
    qd16, kn16, kdt16, decay = [], [], [], []
    for c, h in items:
        r = slice(c * CHUNK, (c + 1) * CHUNK)
        kc = slice(h * dk, (h + 1) * dk)
        bc = bcum[c][:, kc]
        q = q_ref[r, kc].astype(F32) * dk ** -0.5
        k = k_ref[r, kc].astype(F32)
        b_last = bc[CHUNK - 1:CHUNK, :]
        qd16.append((q * jnp.exp(bc)).astype(BF16))
        kn16.append((k * jnp.exp(-bc)).astype(BF16))
        kdt16.append(jnp.transpose(k * jnp.exp(b_last - bc)).astype(BF16))
        decay_rows = jnp.broadcast_to(jnp.exp(b_last), (8, dk))
        decay.append(jnp.transpose(decay_rows)[:, :1])
    values = [v_ref[c * CHUNK:(c + 1) * CHUNK, h * dv:(h + 1) * dv] for c, h in items]
    attn16 = [jnp.where(incl, _dot_nt(a, b), 0.0).astype(BF16) for a, b in zip(qd16, kn16)]
    intra = [_dot(a, v) for a, v in zip(attn16, values)]
    update = [_dot(kt, v) for kt, v in zip(kdt16, values)]

    nw = nw_ref[...]
    state = [s_ref[h] for h in range(nh)]
    for c in range(chunks):
        r = slice(c * CHUNK, (c + 1) * CHUNK)
        for h in range(nh):
            i = c * nh + h
            vc = slice(h * dv, (h + 1) * dv)
            o = intra[i] + _dot(qd16[i], state[h].astype(BF16))
            state[h] = state[h] * decay[i] + update[i]
            o = _rms(o, nw) * _silu(g_ref[r, vc].astype(F32))
            o_ref[r, vc] = o.astype(o_ref.dtype)
    for h in range(nh):
        s_ref[h] = state[h]


def _gla(q, k, v, g, gk, wgu, bg, nw, batch, lp, chunks):
    rows = q.shape[0]
    rs = chunks * CHUNK
    steps = lp // rs
    idx = lambda b, s: (b * steps + s, 0)
    dk = q.shape[1] // GLA_HEADS
    dv = v.shape[1] // GLA_HEADS
    return pl.pallas_call(
        functools.partial(_gla_body, chunks=chunks),
        grid=(batch, steps),
        in_specs=[pl.BlockSpec((rs, q.shape[1]), idx), pl.BlockSpec((rs, k.shape[1]), idx),
                  pl.BlockSpec((rs, v.shape[1]), idx), pl.BlockSpec((rs, g.shape[1]), idx),
                  pl.BlockSpec((rs, gk.shape[1]), idx), _resident(wgu.shape),
                  _resident(bg.shape), _resident(nw.shape)],
        out_specs=pl.BlockSpec((rs, v.shape[1]), idx),
        out_shape=jax.ShapeDtypeStruct((rows, v.shape[1]), BF16),
        scratch_shapes=[pltpu.VMEM((GLA_HEADS, dk, dv), F32)],
        compiler_params=_params("parallel", "arbitrary"),
        name="gla",
    )(q, k, v, g, gk, wgu, bg, nw)


def _pad_cols(w, width):
    return jnp.pad(w, ((0, 0), (0, width - w.shape[1])))


def _per_step(n, preferred):
    for c in preferred:
        if n % c == 0:
            return c
    return 1


def kernel(x, meta_tokens, norm_w, ffn_w_gate, ffn_w_up, ffn_w_down, rel_bias_table,
           even_w_in, even_conv_w, swa_sinks, dn_a_log, dn_dt_bias, dn_norm_w, even_w_out,
           odd_w_in, gla_w_gate_up, gla_b_gate, gla_norm_w, odd_w_out):
    batch, seq, d = x.shape
    depth = norm_w.shape[0]
    lp = SWA_BLOCK + seq
    assert seq % ROW_TILE == 0 and (batch * lp) % ROW_TILE == 0
    rows = batch * lp
    chunks = _per_step(lp // CHUNK, (6, 3, 2))
    swa_blocks = _per_step(lp // SWA_BLOCK, (3, 2))

    meta_block = jnp.pad(meta_tokens.astype(x.dtype), ((PAD, 0), (0, 0)))
    norm4 = norm_w[:, :, None, :]
    wg = ffn_w_gate.astype(BF16)
    wu = ffn_w_up.astype(BF16)
    wd = ffn_w_down.astype(BF16)

    swa_q = SWA_Q_HEADS * SWA_HEAD_DIM
    swa_kv = SWA_KV_HEADS * SWA_HEAD_DIM
    bias = None
    h = x.reshape(batch * seq, d)
    for layer in range(depth):
        i = layer // 2
        embed = meta_block if layer == 0 else None
        final = layer == depth - 1
        if layer % 2 == 0:
            assert even_w_in.shape[2] == swa_q + 2 * swa_kv + 4 * DN_DIM + 2 * DN_HEADS
            widths = (swa_q, swa_kv, swa_kv, 3 * DN_DIM, DN_DIM, LANE)
            w_in = _pad_cols(even_w_in[i], sum(widths)).astype(BF16)
            h, qa, ka, va, act_b, z_b, gates = _ffn_proj(
                h, embed, norm4, layer, wg, wu, wd, w_in, widths,
                (BF16, BF16, BF16, BF16, BF16, F32), rows, lp, seq,
                conv_out=3, conv_w=even_conv_w[i])
            if bias is None:
                bias = _swa_bias(rel_bias_table)
            lane_pad = (DN_HEADS, LANE - 2 * DN_HEADS)
            alog_vec = jnp.pad(dn_a_log[i], lane_pad)[None, :]
            dtb_vec = jnp.pad(dn_dt_bias[i], lane_pad)[None, :]
            dn_nw = dn_norm_w[i][None, :]
            w_out = even_w_out[i]
            if not final and swa_blocks * SWA_BLOCK == chunks * CHUNK:
                h = _mix_ffn(qa, ka, va, bias, swa_sinks[i], act_b, z_b, gates, alog_vec,
                             dtb_vec, dn_nw, h, w_out.astype(BF16), norm4, layer, wg, wu, wd,
                             batch, lp, swa_blocks, chunks)
                continue
            o_a = _swa(qa, ka, va, bias, swa_sinks[i], batch, lp, swa_blocks)
            o_b = _deltanet(act_b, z_b, gates, alog_vec, dtb_vec, dn_nw, batch, lp, chunks)
            mixed = [o_a, o_b]
        else:
            key_dim = gla_w_gate_up.shape[2]
            val_dim = odd_w_out.shape[1]
            widths = (key_dim, key_dim, val_dim, val_dim, LANE)
            w_in = _pad_cols(odd_w_in[i], sum(widths)).astype(BF16)
            h, q, k, v, g, gk = _ffn_proj(
                h, embed, norm4, layer, wg, wu, wd, w_in, widths,
                (BF16, BF16, BF16, BF16, F32), rows, lp, seq)
            wgu = jnp.pad(gla_w_gate_up[i], ((0, LANE - GLA_GATE_RANK), (0, 0))).astype(BF16)
            o = _gla(q, k, v, g, gk, wgu, gla_b_gate[i][None, :], gla_norm_w[i][None, :],
                     batch, lp, chunks)
            mixed, w_out = [o], odd_w_out[i]
        h = _proj_ffn(mixed, h, w_out.astype(BF16), norm4, layer, wg, wu, wd, final,
                      batch, lp, seq)
    return h.reshape(batch, seq, d)
```

```python
import functools
import math

import numpy as np
import jax
import jax.numpy as jnp
from jax import lax
from jax.experimental import pallas as pl
from jax.experimental.pallas import tpu as pltpu

F32 = jnp.float32
BF16 = jnp.bfloat16
HIGHEST = lax.Precision.HIGHEST

N_META = 16
NORM_EPS = 1e-6
NEG_INF = -1e30
SWA_Q_HEADS = 8
SWA_KV_HEADS = 2
SWA_HEAD_DIM = 64
SWA_BLOCK = 128
REL_BUCKETS = 32
REL_MAX_DIST = 128
DN_HEADS = 4
DN_HEAD_DIM = 128
DN_DIM = DN_HEADS * DN_HEAD_DIM
DN_CONV = 4
GLA_HEADS = 4
GLA_GATE_RANK = 16
GLA_GATE_NORM = 16.0
CHUNK = 64

PAD = SWA_BLOCK - N_META
LANE = 128
ROW_TILE = 512
FF_TILE = 256
VMEM_LIMIT = 56 * 1024 * 1024


def _rms(x, w):
    return x * lax.rsqrt(jnp.mean(x * x, axis=-1, keepdims=True) + NORM_EPS) * w


def _silu(x):
    return x * jax.nn.sigmoid(x)


def _softplus(x):
    return jnp.maximum(x, 0.0) + jnp.log(1.0 + jnp.exp(-jnp.abs(x)))


def _dot(a, b, precision=None):
    return jnp.dot(a, b, preferred_element_type=F32, precision=precision)


def _dot_nt(a, b, precision=None):
    return lax.dot_general(a, b, (((1,), (1,)), ((), ())), preferred_element_type=F32,
                           precision=precision)


def _dot_tn(a, b, precision=None):
    return lax.dot_general(a, b, (((0,), (0,)), ((), ())), preferred_element_type=F32,
                           precision=precision)


def _resident(shape):
    nd = len(shape)
    return pl.BlockSpec(shape, lambda *_: (0,) * nd, pipeline_mode=pl.Buffered(1))


def _params(*semantics):
    return pltpu.CompilerParams(dimension_semantics=semantics, vmem_limit_bytes=VMEM_LIMIT)


SUB = ROW_TILE // SWA_BLOCK


def _pick(arr, *lead):
    rest = arr.shape[len(lead):]
    index = tuple(lead) + (0,) * len(rest)
    return pl.BlockSpec((None,) * len(lead) + rest, lambda i: index,
                        pipeline_mode=pl.Buffered(1))


def _tile_specs(width):
    return [pl.BlockSpec((ROW_TILE, width), lambda i: (i, 0))]


def _real_specs(width, lp, seq):
    tiles = seq // ROW_TILE
    blocks = lp // SWA_BLOCK

    def spec(j):
        return pl.BlockSpec((SWA_BLOCK, width),
                            lambda i: ((i // tiles) * blocks + 1 + (i % tiles) * SUB + j, 0))
    return [spec(j) for j in range(SUB)]


def _embed_specs(width, lp, seq):
    blocks = lp // SWA_BLOCK
    x_blocks = seq // SWA_BLOCK

    def spec(j):
        def index(i):
            blk = i * SUB + j
            return ((blk // blocks) * x_blocks + jnp.maximum(blk % blocks - 1, 0), 0)
        return pl.BlockSpec((SWA_BLOCK, width), index)
    return [spec(j) for j in range(SUB)]


ROW_SPLIT = 4
GROUP_ROWS = ROW_TILE // ROW_SPLIT


def _group(part):
    return slice(part * GROUP_ROWS, (part + 1) * GROUP_ROWS)


def _load_rows(refs, part):
    if len(refs) == 1:
        return refs[0][_group(part), :]
    per = len(refs) // ROW_SPLIT
    return jnp.concatenate([r[...] for r in refs[part * per:(part + 1) * per]], axis=0)


def _drain(stages):
    for _ in stages:
        pass


def _interleave(main, side):
    for _ in main:
        next(side, None)
    _drain(side)


def _ffn_stages(xs, nwa, nwb, wg_ref, wu_ref, wd_ref, a_ref, outs):
    hns = [_rms(x, nwa).astype(BF16) for x in xs]
    d_ff = wg_ref.shape[1]
    for part, hn in enumerate(hns):
        for c in range(d_ff // FF_TILE):
            cols = slice(c * FF_TILE, (c + 1) * FF_TILE)
            g = _dot(hn, wg_ref[:, cols])
            u = _dot(hn, wu_ref[:, cols])
            a_ref[_group(part), cols] = (_silu(g) * u).astype(BF16)
            yield
    fs = []
    for part in range(len(xs)):
        fs.append(_dot(a_ref[_group(part), :], wd_ref[...]))
        yield
    outs.extend(x + 0.5 * _rms(f, nwb) for x, f in zip(xs, fs))


def _ffn_core(xs, nwa, nwb, wg_ref, wu_ref, wd_ref, a_ref):
    outs = []
    _drain(_ffn_stages(xs, nwa, nwb, wg_ref, wu_ref, wd_ref, a_ref, outs))
    return outs


CONV_HALO = 8


def _ffn_proj_body(*refs, n_rows, embed_blocks, n_out, conv_out, lp):
    row_refs = refs[:n_rows]
    refs = refs[n_rows:]
    if embed_blocks:
        meta_ref, refs = refs[0], refs[1:]
    if conv_out is not None:
        convw_ref, refs = refs[0], refs[1:]
        xpad_ref = refs[9 + n_out]

        @pl.when(pl.program_id(0) == 0)
        def _():
            xpad_ref[0:CONV_HALO, :] = jnp.zeros((CONV_HALO, xpad_ref.shape[1]), F32)

    nwa_ref, nwb_ref, wg_ref, wu_ref, wd_ref, nwm_ref, win_ref = refs[:7]
    h_ref = refs[7]
    out_refs = refs[8:8 + n_out]
    a_ref = refs[8 + n_out]
    offsets = np.cumsum([0] + [o.shape[1] for o in out_refs])
    xs = []
    for part in range(ROW_SPLIT):
        if embed_blocks:
            per = n_rows // ROW_SPLIT
            first = pl.program_id(0) * SUB + part * per
            xs.append(jnp.concatenate(
                [jnp.where((first + j) % embed_blocks == 0, meta_ref[...], r[...])
                 for j, r in enumerate(row_refs[part * per:(part + 1) * per])], axis=0))
        else:
            xs.append(_load_rows(row_refs, part))
    hs = _ffn_core(xs, nwa_ref[...], nwb_ref[...], wg_ref, wu_ref, wd_ref, a_ref)
    hns = [_rms(h, nwm_ref[...]).astype(BF16) for h in hs]
    for part, (h, hn) in enumerate(zip(hs, hns)):
        rows = _group(part)
        h_ref[rows, :] = h
        for i, o_ref in enumerate(out_refs):
            p = _dot(hn, win_ref[:, offsets[i]:offsets[i + 1]])
            if i == conv_out:
                xpad_ref[CONV_HALO + part * GROUP_ROWS:CONV_HALO + (part + 1) * GROUP_ROWS,
                         :] = p
            else:
                o_ref[rows, :] = p.astype(o_ref.dtype)

    if conv_out is not None:
        taps = convw_ref.shape[0]
        cw = convw_ref[...]
        y = cw[taps - 1:taps, :] * xpad_ref[CONV_HALO:, :]
        for j in range(taps - 1):
            lo = CONV_HALO - (taps - 1 - j)
            y = y + cw[j:j + 1, :] * xpad_ref[lo:lo + ROW_TILE, :]
        tail = xpad_ref[ROW_TILE:, :]
        row = (pl.program_id(0) * ROW_TILE) % lp + lax.broadcasted_iota(
            jnp.int32, (ROW_TILE, 1), 0)
        row = jnp.where(row >= lp, row - lp, row)
        out_refs[conv_out][...] = jnp.where(row >= PAD, _silu(y), 0.0).astype(
            out_refs[conv_out].dtype)
        xpad_ref[0:CONV_HALO, :] = tail


def _ffn_proj(src, meta_block, norm4, layer, wg, wu, wd, w_in, widths, dtypes, rows, lp, seq,
              conv_out=None, conv_w=None):
    d = norm4.shape[-1]
    d_ff = wg.shape[-1]
    assert sum(widths) == w_in.shape[1] and lp > ROW_TILE
    if meta_block is None:
        row_specs, extra, extra_specs, embed_blocks = _tile_specs(d), [], [], 0
    else:
        row_specs = _embed_specs(d, lp, seq)
        extra, extra_specs, embed_blocks = [meta_block], [_resident(meta_block.shape)], (
            lp // SWA_BLOCK)
    scratch = [pltpu.VMEM((ROW_TILE, d_ff), BF16)]
    if conv_out is not None:
        extra, extra_specs = extra + [conv_w], extra_specs + [_resident(conv_w.shape)]
        scratch.append(pltpu.VMEM((CONV_HALO + ROW_TILE, widths[conv_out]), F32))
    tile = lambda n: pl.BlockSpec((ROW_TILE, n), lambda i: (i, 0))
    return pl.pallas_call(
        functools.partial(_ffn_proj_body, n_rows=len(row_specs), embed_blocks=embed_blocks,
                          n_out=len(widths), conv_out=conv_out, lp=lp),
        grid=(rows // ROW_TILE,),
        in_specs=row_specs + extra_specs
        + [_pick(norm4, layer, 0), _pick(norm4, layer, 1), _pick(wg, layer, 0),
           _pick(wu, layer, 0), _pick(wd, layer, 0), _pick(norm4, layer, 2),
           _resident(w_in.shape)],
        out_specs=[tile(d)] + [tile(n) for n in widths],
        out_shape=[jax.ShapeDtypeStruct((rows, d), F32)]
        + [jax.ShapeDtypeStruct((rows, n), dt) for n, dt in zip(widths, dtypes)],
        scratch_shapes=scratch,
        compiler_params=_params("arbitrary"),
        name="ffn_proj",
    )(*([src] * len(row_specs)), *extra, norm4, norm4, wg, wu, wd, norm4, w_in)


def _proj_ffn_body(*refs, n_rows, n_x):
    x_groups = [refs[g * n_rows:(g + 1) * n_rows] for g in range(n_x)]
    refs = refs[n_x * n_rows:]
    h_refs = refs[:n_rows]
    wo_ref, nwo_ref, nwa_ref, nwb_ref, wg_ref, wu_ref, wd_ref, o_ref, a_ref = refs[n_rows:]
    accs = []
    for part in range(ROW_SPLIT):
        acc = None
        off = 0
        for group in x_groups:
            xs = _load_rows(group, part)
            k = xs.shape[1]
            p = _dot(xs, wo_ref[off:off + k, :])
            acc = p if acc is None else acc + p
            off += k
        accs.append(acc)
    hs = [_load_rows(h_refs, part) + _rms(acc, nwo_ref[...]) for part, acc in enumerate(accs)]
    outs = _ffn_core(hs, nwa_ref[...], nwb_ref[...], wg_ref, wu_ref, wd_ref, a_ref)
    for part, out in enumerate(outs):
        o_ref[_group(part), :] = out


def _proj_ffn(xs, h, w_out, norm4, layer, wg, wu, wd, final, batch, lp, seq):
    d = norm4.shape[-1]
    d_ff = wg.shape[-1]
    if final:
        specs = lambda width: _real_specs(width, lp, seq)
        out_rows = batch * seq
    else:
        specs = _tile_specs
        out_rows = batch * lp
    h_specs = specs(d)
    n_rows = len(h_specs)
    in_specs, args = [], []
    for x in xs:
        in_specs += specs(x.shape[1])
        args += [x] * n_rows
    in_specs += h_specs + [_resident(w_out.shape), _pick(norm4, layer, 3),
                           _pick(norm4, layer, 4), _pick(norm4, layer, 5),
                           _pick(wg, layer, 1), _pick(wu, layer, 1), _pick(wd, layer, 1)]
    args += [h] * n_rows + [w_out, norm4, norm4, norm4, wg, wu, wd]
    return pl.pallas_call(
        functools.partial(_proj_ffn_body, n_rows=n_rows, n_x=len(xs)),
        grid=(out_rows // ROW_TILE,),
        in_specs=in_specs,
        out_specs=pl.BlockSpec((ROW_TILE, d), lambda i: (i, 0)),
        out_shape=jax.ShapeDtypeStruct((out_rows, d), F32),
        scratch_shapes=[pltpu.VMEM((ROW_TILE, d_ff), BF16)],
        compiler_params=_params("parallel"),
        name="proj_ffn",
    )(*args)


def _t5_bucket_np(rel):
    n = np.maximum(rel, 0)
    max_exact = REL_BUCKETS // 2
    n_f = np.maximum(n, 1).astype(np.float32)
    large = max_exact + (np.log(n_f / np.float32(max_exact))
                         / np.float32(math.log(REL_MAX_DIST / max_exact))
                         * np.float32(REL_BUCKETS - max_exact)).astype(np.int32)
    large = np.minimum(large, REL_BUCKETS - 1)
    return np.where(n < max_exact, n, large).astype(np.int32)


def _swa_bucket_index():
    i = np.arange(SWA_BLOCK)[:, None]
    c = np.arange(SWA_BLOCK)[None, :]
    out = np.full((3, SWA_BLOCK, 2 * SWA_BLOCK), -1, np.int32)
    for variant in range(3):
        pos_q = variant * SWA_BLOCK + i - PAD
        pos_k = variant * SWA_BLOCK + c - PAD - np.where(c > i, SWA_BLOCK, 0)
        rel_b = pos_q - pos_k
        assert ((rel_b >= 0) & (rel_b < SWA_BLOCK)).all()
        out[variant, :, :SWA_BLOCK] = np.where(pos_k >= N_META, _t5_bucket_np(rel_b), -1)
        rel_m = pos_q - c
        meta = (c < N_META) & (rel_m >= 0)
        out[variant, :, SWA_BLOCK:] = np.where(meta, _t5_bucket_np(rel_m), -1)
    return out


_SWA_BUCKET_INDEX = _swa_bucket_index()


def _bias_body(tbl_ref, idx_ref, o_ref):
    head = pl.program_id(1)
    idx = idx_ref[0]
    acc = jnp.full(idx.shape, NEG_INF, F32)
    for b in range(REL_BUCKETS):
        acc = jnp.where(idx == b, tbl_ref[b, head], acc)
    o_ref[0, 0] = acc


def _swa_bias(rel_table):
    idx = jnp.asarray(_SWA_BUCKET_INDEX)
    group = SWA_Q_HEADS // SWA_KV_HEADS
    width = idx.shape[2]
    return pl.pallas_call(
        _bias_body,
        grid=(3, SWA_Q_HEADS),
        in_specs=[pl.BlockSpec(memory_space=pltpu.SMEM),
                  pl.BlockSpec((1, SWA_BLOCK, width), lambda v, h: (v, 0, 0))],
        out_specs=pl.BlockSpec((1, 1, SWA_BLOCK, width),
                               lambda v, h: (v, h // group, h % group, 0)),
        out_shape=jax.ShapeDtypeStruct((3, SWA_KV_HEADS, group * SWA_BLOCK, width), F32),
        compiler_params=_params("arbitrary", "arbitrary"),
        name="swa_bias",
    )(rel_table, idx)


def _swa_stages(sink_ref, q_ref, kc_ref, kp_ref, km_ref, vc_ref, vp_ref, vm_ref, bias_ref,
                step, blocks, store):
    dh = SWA_HEAD_DIM
    blk = SWA_BLOCK
    group = SWA_Q_HEADS // SWA_KV_HEADS
    scale = dh ** -0.5
    zpad = jnp.zeros((blk - N_META, dh), BF16)
    row = lax.broadcasted_iota(jnp.int32, (group * blk, blk), 0) % blk
    col = lax.broadcasted_iota(jnp.int32, (group * blk, blk), 1)
    from_prev = col > row

    ones_col = (lax.broadcasted_iota(jnp.int32, (3 * blk, dh), 1) == 0).astype(BF16)

    units = [(j, hk) for j in range(blocks) for hk in range(SWA_KV_HEADS)]
    scores, values = [], []
    for j, hk in units:
        r = slice(j * blk, (j + 1) * blk)
        kv = slice(hk * dh, (hk + 1) * dh)
        k_prev = kp_ref[:, kv] if j == 0 else kc_ref[(j - 1) * blk:j * blk, kv]
        v_prev = vp_ref[:, kv] if j == 0 else vc_ref[(j - 1) * blk:j * blk, kv]
        k_all = jnp.concatenate([k_prev, kc_ref[r, kv], km_ref[:, kv], zpad], axis=0)
        v_all = jnp.concatenate([v_prev, vc_ref[r, kv], vm_ref[:, kv], zpad], axis=0)
        values.append(jnp.concatenate([v_all, ones_col], axis=1))
        q = jnp.concatenate([q_ref[r, (hk * group + g) * dh:(hk * group + g + 1) * dh]
                             for g in range(group)], axis=0) * scale
        scores.append(_dot_nt(q, k_all))
        yield

    probs, sink_terms = [], []
    for (j, hk), s in zip(units, scores):
        bias = bias_ref[jnp.minimum(step * blocks + j, 2), hk]
        s_band = jnp.where(from_prev, s[:, :blk], s[:, blk:2 * blk]) + bias[:, :blk]
        s_meta = s[:, 2 * blk:] + bias[:, blk:]
        sink = jnp.concatenate([jnp.full((blk, 1), sink_ref[hk * group + g], F32)
                                for g in range(group)], axis=0)
        m = jnp.maximum(jnp.max(jnp.maximum(s_band, s_meta), axis=-1, keepdims=True), sink)
        e_band = jnp.exp(s_band - m)
        e_meta = jnp.exp(s_meta - m)
        sink_terms.append(jnp.exp(sink - m))
        probs.append(jnp.concatenate([jnp.where(from_prev, e_band, 0.0),
                                      jnp.where(from_prev, 0.0, e_band), e_meta],
                                     axis=1).astype(BF16))

    for (j, hk), p, v_ext, sink_term in zip(units, probs, values, sink_terms):
        pv = _dot(p, v_ext)
        o = pv[:, :dh] / (pv[:, dh:dh + 1] + sink_term)
        for g in range(group):
            hq = hk * group + g
            store(slice(j * blk, (j + 1) * blk), slice(hq * dh, (hq + 1) * dh),
                  o[g * blk:(g + 1) * blk])
        yield


def _swa_body(sink_ref, q_ref, kc_ref, kp_ref, km_ref, vc_ref, vp_ref, vm_ref, bias_ref, o_ref,
              *, blocks):
    def store(rows, cols, value):
        o_ref[rows, cols] = value.astype(o_ref.dtype)

    _drain(_swa_stages(sink_ref, q_ref, kc_ref, kp_ref, km_ref, vc_ref, vp_ref, vm_ref,
                       bias_ref, pl.program_id(1), blocks, store))


def _swa(q, k, v, bias, sinks, batch, lp, blocks):
    rows = q.shape[0]
    nb = lp // SWA_BLOCK
    steps = nb // blocks
    meta_blocks = lp // N_META
    cur = lambda b, s: (b * steps + s, 0)
    prev = lambda b, s: (b * nb + jnp.maximum(s * blocks - 1, 0), 0)
    meta = lambda b, s: (b * meta_blocks + PAD // N_META, 0)
    kv_w = k.shape[1]
    rs = blocks * SWA_BLOCK
    return pl.pallas_call(
        functools.partial(_swa_body, blocks=blocks),
        grid=(batch, steps),
        in_specs=[pl.BlockSpec(memory_space=pltpu.SMEM),
                  pl.BlockSpec((rs, q.shape[1]), cur),
                  pl.BlockSpec((rs, kv_w), cur), pl.BlockSpec((SWA_BLOCK, kv_w), prev),
                  pl.BlockSpec((N_META, kv_w), meta),
                  pl.BlockSpec((rs, kv_w), cur), pl.BlockSpec((SWA_BLOCK, kv_w), prev),
                  pl.BlockSpec((N_META, kv_w), meta),
                  _resident(bias.shape)],
        out_specs=pl.BlockSpec((rs, q.shape[1]), cur),
        out_shape=jax.ShapeDtypeStruct((rows, q.shape[1]), BF16),
        compiler_params=_params("parallel", "arbitrary"),
        name="swa",
    )(sinks, q, k, k, k, v, v, v, bias)


def _tri_masks():
    row = lax.broadcasted_iota(jnp.int32, (CHUNK, CHUNK), 0)
    col = lax.broadcasted_iota(jnp.int32, (CHUNK, CHUNK), 1)
    return row >= col, row > col, row == col


def _deltanet_stages(act_ref, z_ref, gates_ref, alog_ref, dtb_ref, nw_ref, s_ref, chunks,
                     store):
    nh, dh = DN_HEADS, DN_HEAD_DIM
    gt = gates_ref[...]
    beta_full = jax.nn.sigmoid(gt)
    g_full = -jnp.exp(alog_ref[...]) * _softplus(gt + dtb_ref[...])
    incl, strict, diag = _tri_masks()
    eye = diag.astype(F32)
    tri = incl.astype(F32)
    gc_chunks = [_dot(tri, g_full[c * CHUNK:(c + 1) * CHUNK, :], HIGHEST)
                 for c in range(chunks)]
    items = [(c, h) for c in range(chunks) for h in range(nh)]
    yield

    neg_a, akt16, qd16, rhs16, decay = [], [], [], [], []
    for c, h in items:
        if h == 0 and c % 2 == 0 and c > 0:
            yield
        r = slice(c * CHUNK, (c + 1) * CHUNK)
        q = act_ref[r, h * dh:(h + 1) * dh].astype(F32)
        k = act_ref[r, DN_DIM + h * dh:DN_DIM + (h + 1) * dh].astype(F32)
        v = act_ref[r, 2 * DN_DIM + h * dh:2 * DN_DIM + (h + 1) * dh].astype(F32)
        q = q * lax.rsqrt(jnp.sum(q * q, axis=-1, keepdims=True) + 1e-6) * dh ** -0.5
        k = k * lax.rsqrt(jnp.sum(k * k, axis=-1, keepdims=True) + 1e-6)
        beta = jnp.broadcast_to(beta_full[r, h:h + 1], (CHUNK, dh))
        gc = jnp.broadcast_to(gc_chunks[c][:, nh + h:nh + h + 1], (CHUNK, dh))
        gc_row = jnp.transpose(gc)[:CHUNK, :]
        gamma = jnp.where(incl, jnp.exp(jnp.where(incl, gc[:, :CHUNK] - gc_row, 0.0)), 0.0)
        k_beta = k * beta
        kq = _dot_nt(jnp.concatenate([k_beta, q], axis=0).astype(BF16), k.astype(BF16))
        neg_a.append(jnp.where(strict, -(kq[:CHUNK] * gamma), 0.0))
        attn16 = jnp.where(incl, kq[CHUNK:] * gamma, 0.0).astype(BF16)
        gc_last = gc[CHUNK - 1:CHUNK, :]
        qd16.append((q * jnp.exp(gc)).astype(BF16))
        kdt16 = jnp.transpose(k * jnp.exp(gc_last - gc)).astype(BF16)
        akt16.append(jnp.concatenate([attn16, kdt16], axis=0))
        rhs16.append(jnp.concatenate([v * beta, k_beta * jnp.exp(gc)], axis=1).astype(BF16))
        decay.append(jnp.exp(gc_last))

    yield
    levels = int(math.log2(CHUNK)) - 1
    t = [eye + x for x in neg_a]
    p16 = [x.astype(BF16) for x in neg_a]
    p16 = [_dot(x, x).astype(BF16) for x in p16]
    yield
    for level in range(levels):
        if level + 1 < levels:
            tp = [_dot(jnp.concatenate([y.astype(BF16), x], axis=0), x) for y, x in zip(t, p16)]
            t = [y + r[:CHUNK] for y, r in zip(t, tp)]
            p16 = [r[CHUNK:].astype(BF16) for r in tp]
        else:
            t = [y + _dot(y.astype(BF16), x) for y, x in zip(t, p16)]
        yield
    uw = [_dot(y.astype(BF16), b) for y, b in zip(t, rhs16)]
    wq16 = [jnp.concatenate([x[:, dh:].astype(BF16), y], axis=0)
            for x, y in zip(uw, qd16)]
    yield

    nw = nw_ref[...]
    state = [s_ref[h] for h in range(nh)]
    for c in range(chunks):
        r = slice(c * CHUNK, (c + 1) * CHUNK)
        pair = [c * nh + h for h in range(nh)]
        s16 = [s.astype(BF16) for s in state]
        wqs = [_dot(wq16[i], s16[h]) for h, i in enumerate(pair)]
        yield
        vn16 = [(uw[i][:, :dh] - wqs[h][:CHUNK]).astype(BF16) for h, i in enumerate(pair)]
        ak = [_dot(akt16[i], vn16[h]) for h, i in enumerate(pair)]
        state = [state[h] * decay[i] + ak[h][CHUNK:] for h, i in enumerate(pair)]
        for h in range(nh):
            hc = slice(h * dh, (h + 1) * dh)
            out = wqs[h][CHUNK:] + ak[h][:CHUNK]
            store(r, hc, _rms(out, nw) * _silu(z_ref[r, hc].astype(F32)))
        yield
    for h in range(nh):
        s_ref[h] = state[h]


def _deltanet_body(act_ref, z_ref, gates_ref, alog_ref, dtb_ref, nw_ref, o_ref, s_ref, *,
                   chunks):
    @pl.when(pl.program_id(1) == 0)
    def _():
        s_ref[...] = jnp.zeros_like(s_ref)

    def store(rows, cols, value):
        o_ref[rows, cols] = value.astype(o_ref.dtype)

    _drain(_deltanet_stages(act_ref, z_ref, gates_ref, alog_ref, dtb_ref, nw_ref, s_ref,
                            chunks, store))


def _deltanet(act, z, gates, alog_vec, dtb_vec, nw, batch, lp, chunks):
    rows = act.shape[0]
    rs = chunks * CHUNK
    steps = lp // rs
    idx = lambda b, s: (b * steps + s, 0)
    return pl.pallas_call(
        functools.partial(_deltanet_body, chunks=chunks),
        grid=(batch, steps),
        in_specs=[pl.BlockSpec((rs, act.shape[1]), idx), pl.BlockSpec((rs, z.shape[1]), idx),
                  pl.BlockSpec((rs, gates.shape[1]), idx),
                  _resident(alog_vec.shape), _resident(dtb_vec.shape), _resident(nw.shape)],
        out_specs=pl.BlockSpec((rs, DN_DIM), idx),
        out_shape=jax.ShapeDtypeStruct((rows, DN_DIM), BF16),
        scratch_shapes=[pltpu.VMEM((DN_HEADS, DN_HEAD_DIM, DN_HEAD_DIM), F32)],
        compiler_params=_params("parallel", "arbitrary"),
        name="deltanet",
    )(act, z, gates, alog_vec, dtb_vec, nw)


def _mix_ffn_body(sink_ref, q_ref, kc_ref, kp_ref, km_ref, vc_ref, vp_ref, vm_ref, bias_ref,
                  act_ref, z_ref, gates_ref, alog_ref, dtb_ref, dnw_ref,
                  h_ref, wo_ref, nwo_ref, nwa_ref, nwb_ref, wg_ref, wu_ref, wd_ref,
                  o_ref, s_ref, mix_ref, a_ref, *, blocks, chunks, steps, last):
    t = pl.program_id(0)
    seq_step = jnp.minimum(t, last) % steps
    write_slot = t % 2
    read_slot = 1 - write_slot
    swa_width = q_ref.shape[1]
    groups = q_ref.shape[0] // GROUP_ROWS

    @pl.when(t == 0)
    def _():
        mix_ref[...] = jnp.zeros_like(mix_ref)

    @pl.when(seq_step == 0)
    def _():
        s_ref[...] = jnp.zeros_like(s_ref)

    def store_swa(rows, cols, value):
        mix_ref[write_slot, rows, cols] = value.astype(mix_ref.dtype)

    def store_dn(rows, cols, value):
        cols = slice(swa_width + cols.start, swa_width + cols.stop)
        mix_ref[write_slot, rows, cols] = value.astype(mix_ref.dtype)

    def mixer_stages():
        dn = _deltanet_stages(act_ref, z_ref, gates_ref, alog_ref, dtb_ref, dnw_ref,
                              s_ref, chunks, store_dn)
        swa = _swa_stages(sink_ref, q_ref, kc_ref, kp_ref, km_ref, vc_ref, vp_ref,
                          vm_ref, bias_ref, seq_step, blocks, store_swa)
        for _ in dn:
            yield
            if next(swa, StopIteration) is not StopIteration:
                yield
        yield from swa

    def ffn_stages():
        accs = []
        for part in range(groups):
            accs.append(_dot(mix_ref[read_slot, _group(part), :], wo_ref[...]))
            yield
        hs = [h_ref[_group(part), :] + _rms(acc, nwo_ref[...]) for part, acc in enumerate(accs)]
        outs = []
        yield from _ffn_stages(hs, nwa_ref[...], nwb_ref[...], wg_ref, wu_ref, wd_ref, a_ref,
                               outs)
        for part, out in enumerate(outs):
            o_ref[_group(part), :] = out

    _interleave(ffn_stages(), mixer_stages())


def _mix_ffn(qa, ka, va, bias, sinks, act, z, gates, alog_vec, dtb_vec, dn_nw, h, w_out, norm4,
             layer, wg, wu, wd, batch, lp, blocks, chunks):
    d = norm4.shape[-1]
    d_ff = wg.shape[-1]
    rs = blocks * SWA_BLOCK
    assert rs == chunks * CHUNK and rs % GROUP_ROWS == 0
    nb = lp // SWA_BLOCK
    steps = lp // rs
    last = batch * steps - 1
    meta_blocks = lp // N_META

    def tile(t):
        return jnp.minimum(t, last)

    cur = lambda t: (tile(t), 0)
    prev = lambda t: ((tile(t) // steps) * nb + jnp.maximum((tile(t) % steps) * blocks - 1, 0), 0)
    meta = lambda t: ((tile(t) // steps) * meta_blocks + PAD // N_META, 0)
    behind = lambda t: (jnp.maximum(t - 1, 0), 0)
    kv_w = ka.shape[1]
    return pl.pallas_call(
        functools.partial(_mix_ffn_body, blocks=blocks, chunks=chunks, steps=steps, last=last),
        grid=(last + 2,),
        in_specs=[pl.BlockSpec(memory_space=pltpu.SMEM),
                  pl.BlockSpec((rs, qa.shape[1]), cur),
                  pl.BlockSpec((rs, kv_w), cur), pl.BlockSpec((SWA_BLOCK, kv_w), prev),
                  pl.BlockSpec((N_META, kv_w), meta),
                  pl.BlockSpec((rs, kv_w), cur), pl.BlockSpec((SWA_BLOCK, kv_w), prev),
                  pl.BlockSpec((N_META, kv_w), meta),
                  _resident(bias.shape),
                  pl.BlockSpec((rs, act.shape[1]), cur), pl.BlockSpec((rs, z.shape[1]), cur),
                  pl.BlockSpec((rs, gates.shape[1]), cur),
                  _resident(alog_vec.shape), _resident(dtb_vec.shape), _resident(dn_nw.shape),
                  pl.BlockSpec((rs, d), behind), _resident(w_out.shape),
                  _pick(norm4, layer, 3), _pick(norm4, layer, 4), _pick(norm4, layer, 5),
                  _pick(wg, layer, 1), _pick(wu, layer, 1), _pick(wd, layer, 1)],
        out_specs=pl.BlockSpec((rs, d), behind),
        out_shape=jax.ShapeDtypeStruct(h.shape, F32),
        scratch_shapes=[pltpu.VMEM((DN_HEADS, DN_HEAD_DIM, DN_HEAD_DIM), F32),
                        pltpu.VMEM((2, rs, qa.shape[1] + DN_DIM), BF16),
                        pltpu.VMEM((rs, d_ff), BF16)],
        compiler_params=_params("arbitrary"),
        name="mix_ffn",
    )(sinks, qa, ka, ka, ka, va, va, va, bias, act, z, gates, alog_vec, dtb_vec, dn_nw,
      h, w_out, norm4, norm4, norm4, wg, wu, wd)


def _gla_body(q_ref, k_ref, v_ref, g_ref, gk_ref, wgu_ref, bg_ref, nw_ref, o_ref, s_ref, *,
              chunks):
    nh = GLA_HEADS
    dk = q_ref.shape[1] // nh
    dv = v_ref.shape[1] // nh

    @pl.when(pl.program_id(1) == 0)
    def _():
        s_ref[...] = jnp.zeros_like(s_ref)

    logits = _dot(gk_ref[...].astype(BF16), wgu_ref[...]) + bg_ref[...]
    glog = (jnp.minimum(logits, 0.0) - jnp.log(1.0 + jnp.exp(-jnp.abs(logits)))) / GLA_GATE_NORM
    incl, _, _ = _tri_masks()
    tri = incl.astype(F32)
    bcum = [_dot(tri, glog[c * CHUNK:(c + 1) * CHUNK, :], HIGHEST) for c in range(chunks)]

    items = [(c, h) for c in range(chunks) for h in range(nh)]
    qd16, kn16, kdt16, decay = [], [], [], []
    for c, h in items:
        r = slice(c * CHUNK, (c + 1) * CHUNK)
        kc = slice(h * dk, (h + 1) * dk)
        bc = bcum[c][:, kc]
        q = q_ref[r, kc].astype(F32) * dk ** -0.5
        k = k_ref[r, kc].astype(F32)
        b_last = bc[CHUNK - 1:CHUNK, :]
        qd16.append((q * jnp.exp(bc)).astype(BF16))
        kn16.append((k * jnp.exp(-bc)).astype(BF16))
        kdt16.append(jnp.transpose(k * jnp.exp(b_last - bc)).astype(BF16))
        decay_rows = jnp.broadcast_to(jnp.exp(b_last), (8, dk))
        decay.append(jnp.transpose(decay_rows)[:, :1])
    values = [v_ref[c * CHUNK:(c + 1) * CHUNK, h * dv:(h + 1) * dv] for c, h in items]
    attn16 = [jnp.where(incl, _dot_nt(a, b), 0.0).astype(BF16) for a, b in zip(qd16, kn16)]
    intra = [_dot(a, v) for a, v in zip(attn16, values)]
    update = [_dot(kt, v) for kt, v in zip(kdt16, values)]

    nw = nw_ref[...]
    state = [s_ref[h] for h in range(nh)]
    for c in range(chunks):
        r = slice(c * CHUNK, (c + 1) * CHUNK)
        for h in range(nh):
            i = c * nh + h
            vc = slice(h * dv, (h + 1) * dv)
            o = intra[i] + _dot(qd16[i], state[h].astype(BF16))
            state[h] = state[h] * decay[i] + update[i]
            o = _rms(o, nw) * _silu(g_ref[r, vc].astype(F32))
            o_ref[r, vc] = o.astype(o_ref.dtype)
    for h in range(nh):
        s_ref[h] = state[h]


def _gla(q, k, v, g, gk, wgu, bg, nw, batch, lp, chunks):
    rows = q.shape[0]
    rs = chunks * CHUNK
    steps = lp // rs
    idx = lambda b, s: (b * steps + s, 0)
    dk = q.shape[1] // GLA_HEADS
    dv = v.shape[1] // GLA_HEADS
    return pl.pallas_call(
        functools.partial(_gla_body, chunks=chunks),
        grid=(batch, steps),
        in_specs=[pl.BlockSpec((rs, q.shape[1]), idx), pl.BlockSpec((rs, k.shape[1]), idx),
                  pl.BlockSpec((rs, v.shape[1]), idx), pl.BlockSpec((rs, g.shape[1]), idx),
                  pl.BlockSpec((rs, gk.shape[1]), idx), _resident(wgu.shape),
                  _resident(bg.shape), _resident(nw.shape)],
        out_specs=pl.BlockSpec((rs, v.shape[1]), idx),
        out_shape=jax.ShapeDtypeStruct((rows, v.shape[1]), BF16),
        scratch_shapes=[pltpu.VMEM((GLA_HEADS, dk, dv), F32)],
        compiler_params=_params("parallel", "arbitrary"),
        name="gla",
    )(q, k, v, g, gk, wgu, bg, nw)


def _pad_cols(w, width):
    return jnp.pad(w, ((0, 0), (0, width - w.shape[1])))


def _per_step(n, preferred):
    for c in preferred:
        if n % c == 0:
            return c
    return 1


def kernel(x, meta_tokens, norm_w, ffn_w_gate, ffn_w_up, ffn_w_down, rel_bias_table,
           even_w_in, even_conv_w, swa_sinks, dn_a_log, dn_dt_bias, dn_norm_w, even_w_out,
           odd_w_in, gla_w_gate_up, gla_b_gate, gla_norm_w, odd_w_out):
    batch, seq, d = x.shape
    depth = norm_w.shape[0]
    lp = SWA_BLOCK + seq
    assert seq % ROW_TILE == 0 and (batch * lp) % ROW_TILE == 0
    rows = batch * lp
    chunks = _per_step(lp // CHUNK, (6, 3, 2))
    swa_blocks = _per_step(lp // SWA_BLOCK, (3, 2))

    meta_block = jnp.pad(meta_tokens.astype(x.dtype), ((PAD, 0), (0, 0)))
    norm4 = norm_w[:, :, None, :]
    wg = ffn_w_gate.astype(BF16)
    wu = ffn_w_up.astype(BF16)
    wd = ffn_w_down.astype(BF16)

    swa_q = SWA_Q_HEADS * SWA_HEAD_DIM
    swa_kv = SWA_KV_HEADS * SWA_HEAD_DIM
    bias = None
    h = x.reshape(batch * seq, d)
    for layer in range(depth):
        i = layer // 2
        embed = meta_block if layer == 0 else None
        final = layer == depth - 1
        if layer % 2 == 0:
            assert even_w_in.shape[2] == swa_q + 2 * swa_kv + 4 * DN_DIM + 2 * DN_HEADS
            widths = (swa_q, swa_kv, swa_kv, 3 * DN_DIM, DN_DIM, LANE)
            w_in = _pad_cols(even_w_in[i], sum(widths)).astype(BF16)
            h, qa, ka, va, act_b, z_b, gates = _ffn_proj(
                h, embed, norm4, layer, wg, wu, wd, w_in, widths,
                (BF16, BF16, BF16, BF16, BF16, F32), rows, lp, seq,
                conv_out=3, conv_w=even_conv_w[i])
            if bias is None:
                bias = _swa_bias(rel_bias_table)
            lane_pad = (DN_HEADS, LANE - 2 * DN_HEADS)
            alog_vec = jnp.pad(dn_a_log[i], lane_pad)[None, :]
            dtb_vec = jnp.pad(dn_dt_bias[i], lane_pad)[None, :]
            dn_nw = dn_norm_w[i][None, :]
            w_out = even_w_out[i]
            if not final and swa_blocks * SWA_BLOCK == chunks * CHUNK:
                h = _mix_ffn(qa, ka, va, bias, swa_sinks[i], act_b, z_b, gates, alog_vec,
                             dtb_vec, dn_nw, h, w_out.astype(BF16), norm4, layer, wg, wu, wd,
                             batch, lp, swa_blocks, chunks)
                continue
            o_a = _swa(qa, ka, va, bias, swa_sinks[i], batch, lp, swa_blocks)
            o_b = _deltanet(act_b, z_b, gates, alog_vec, dtb_vec, dn_nw, batch, lp, chunks)
            mixed = [o_a, o_b]
        else:
            key_dim = gla_w_gate_up.shape[2]
            val_dim = odd_w_out.shape[1]
            widths = (key_dim, key_dim, val_dim, val_dim, LANE)
            w_in = _pad_cols(odd_w_in[i], sum(widths)).astype(BF16)
            h, q, k, v, g, gk = _ffn_proj(
                h, embed, norm4, layer, wg, wu, wd, w_in, widths,
                (BF16, BF16, BF16, BF16, F32), rows, lp, seq)
            wgu = jnp.pad(gla_w_gate_up[i], ((0, LANE - GLA_GATE_RANK), (0, 0))).astype(BF16)
            o = _gla(q, k, v, g, gk, wgu, gla_b_gate[i][None, :], gla_norm_w[i][None, :],
                     batch, lp, chunks)
            mixed, w_out = [o], odd_w_out[i]
        h = _proj_ffn(mixed, h, w_out.astype(BF16), norm4, layer, wg, wu, wd, final,
                      batch, lp, seq)
    return h.reshape(batch, seq, d)
```

```python
import functools
import math

import numpy as np
import jax
import jax.numpy as jnp
from jax import lax
from jax.experimental import pallas as pl
from jax.experimental.pallas import tpu as pltpu

F32 = jnp.float32
BF16 = jnp.bfloat16
HIGHEST = lax.Precision.HIGHEST

N_META = 16
NORM_EPS = 1e-6
NEG_INF = -1e30
SWA_Q_HEADS = 8
SWA_KV_HEADS = 2
SWA_HEAD_DIM = 64
SWA_BLOCK = 128
REL_BUCKETS = 32
REL_MAX_DIST = 128
DN_HEADS = 4
DN_HEAD_DIM = 128
DN_DIM = DN_HEADS * DN_HEAD_DIM
DN_CONV = 4
GLA_HEADS = 4
GLA_GATE_RANK = 16
GLA_GATE_NORM = 16.0
CHUNK = 64

PAD = SWA_BLOCK - N_META
LANE = 128
ROW_TILE = 512
FF_TILE = 256
VMEM_LIMIT = 56 * 1024 * 1024


def _rms(x, w):
    return x * lax.rsqrt(jnp.mean(x * x, axis=-1, keepdims=True) + NORM_EPS) * w


def _silu(x):
    return x * jax.nn.sigmoid(x)


def _softplus(x):
    return jnp.maximum(x, 0.0) + jnp.log(1.0 + jnp.exp(-jnp.abs(x)))


def _dot(a, b, precision=None):
    return jnp.dot(a, b, preferred_element_type=F32, precision=precision)


def _dot_nt(a, b, precision=None):
    return lax.dot_general(a, b, (((1,), (1,)), ((), ())), preferred_element_type=F32,
                           precision=precision)


def _dot_tn(a, b, precision=None):
    return lax.dot_general(a, b, (((0,), (0,)), ((), ())), preferred_element_type=F32,
                           precision=precision)


def _resident(shape):
    nd = len(shape)
    return pl.BlockSpec(shape, lambda *_: (0,) * nd, pipeline_mode=pl.Buffered(1))


def _params(*semantics):
    return pltpu.CompilerParams(dimension_semantics=semantics, vmem_limit_bytes=VMEM_LIMIT)


SUB = ROW_TILE // SWA_BLOCK


def _pick(arr, *lead):
    rest = arr.shape[len(lead):]
    index = tuple(lead) + (0,) * len(rest)
    return pl.BlockSpec((None,) * len(lead) + rest, lambda i: index,
                        pipeline_mode=pl.Buffered(1))


def _tile_specs(width):
    return [pl.BlockSpec((ROW_TILE, width), lambda i: (i, 0))]


def _real_specs(width, lp, seq):
    tiles = seq // ROW_TILE
    blocks = lp // SWA_BLOCK

    def spec(j):
        return pl.BlockSpec((SWA_BLOCK, width),
                            lambda i: ((i // tiles) * blocks + 1 + (i % tiles) * SUB + j, 0))
    return [spec(j) for j in range(SUB)]


def _embed_specs(width, lp, seq):
    blocks = lp // SWA_BLOCK
    x_blocks = seq // SWA_BLOCK

    def spec(j):
        def index(i):
            blk = i * SUB + j
            return ((blk // blocks) * x_blocks + jnp.maximum(blk % blocks - 1, 0), 0)
        return pl.BlockSpec((SWA_BLOCK, width), index)
    return [spec(j) for j in range(SUB)]


ROW_SPLIT = 4
GROUP_ROWS = ROW_TILE // ROW_SPLIT


def _group(part):
    return slice(part * GROUP_ROWS, (part + 1) * GROUP_ROWS)


def _load_rows(refs, part):
    if len(refs) == 1:
        return refs[0][_group(part), :]
    per = len(refs) // ROW_SPLIT
    return jnp.concatenate([r[...] for r in refs[part * per:(part + 1) * per]], axis=0)


def _drain(stages):
    for _ in stages:
        pass


def _interleave(main, side):
    for _ in main:
        next(side, None)
    _drain(side)


def _ffn_stages(xs, nwa, nwb, wg_ref, wu_ref, wd_ref, a_ref, outs):
    hns = [_rms(x, nwa).astype(BF16) for x in xs]
    d_ff = wg_ref.shape[1]
    for part, hn in enumerate(hns):
        for c in range(d_ff // FF_TILE):
            cols = slice(c * FF_TILE, (c + 1) * FF_TILE)
            g = _dot(hn, wg_ref[:, cols])
            u = _dot(hn, wu_ref[:, cols])
            a_ref[_group(part), cols] = (_silu(g) * u).astype(BF16)
            yield
    fs = []
    for part in range(len(xs)):
        fs.append(_dot(a_ref[_group(part), :], wd_ref[...]))
        yield
    outs.extend(x + 0.5 * _rms(f, nwb) for x, f in zip(xs, fs))


def _ffn_core(xs, nwa, nwb, wg_ref, wu_ref, wd_ref, a_ref):
    outs = []
    _drain(_ffn_stages(xs, nwa, nwb, wg_ref, wu_ref, wd_ref, a_ref, outs))
    return outs


CONV_HALO = 8


def _ffn_proj_body(*refs, n_rows, embed_blocks, n_out, conv_out, lp):
    row_refs = refs[:n_rows]
    refs = refs[n_rows:]
    if embed_blocks:
        meta_ref, refs = refs[0], refs[1:]
    if conv_out is not None:
        convw_ref, refs = refs[0], refs[1:]
        xpad_ref = refs[9 + n_out]

        @pl.when(pl.program_id(0) == 0)
        def _():
            xpad_ref[0:CONV_HALO, :] = jnp.zeros((CONV_HALO, xpad_ref.shape[1]), F32)

    nwa_ref, nwb_ref, wg_ref, wu_ref, wd_ref, nwm_ref, win_ref = refs[:7]
    h_ref = refs[7]
    out_refs = refs[8:8 + n_out]
    a_ref = refs[8 + n_out]
    offsets = np.cumsum([0] + [o.shape[1] for o in out_refs])
    xs = []
    for part in range(ROW_SPLIT):
        if embed_blocks:
            per = n_rows // ROW_SPLIT
            first = pl.program_id(0) * SUB + part * per
            xs.append(jnp.concatenate(
                [jnp.where((first + j) % embed_blocks == 0, meta_ref[...], r[...])
                 for j, r in enumerate(row_refs[part * per:(part + 1) * per])], axis=0))
        else:
            xs.append(_load_rows(row_refs, part))
    hs = _ffn_core(xs, nwa_ref[...], nwb_ref[...], wg_ref, wu_ref, wd_ref, a_ref)
    hns = [_rms(h, nwm_ref[...]).astype(BF16) for h in hs]
    for part, (h, hn) in enumerate(zip(hs, hns)):
        rows = _group(part)
        h_ref[rows, :] = h
        for i, o_ref in enumerate(out_refs):
            p = _dot(hn, win_ref[:, offsets[i]:offsets[i + 1]])
            if i == conv_out:
                xpad_ref[CONV_HALO + part * GROUP_ROWS:CONV_HALO + (part + 1) * GROUP_ROWS,
                         :] = p
            else:
                o_ref[rows, :] = p.astype(o_ref.dtype)

    if conv_out is not None:
        taps = convw_ref.shape[0]
        cw = convw_ref[...]
        y = cw[taps - 1:taps, :] * xpad_ref[CONV_HALO:, :]
        for j in range(taps - 1):
            lo = CONV_HALO - (taps - 1 - j)
            y = y + cw[j:j + 1, :] * xpad_ref[lo:lo + ROW_TILE, :]
        tail = xpad_ref[ROW_TILE:, :]
        row = (pl.program_id(0) * ROW_TILE) % lp + lax.broadcasted_iota(
            jnp.int32, (ROW_TILE, 1), 0)
        row = jnp.where(row >= lp, row - lp, row)
        out_refs[conv_out][...] = jnp.where(row >= PAD, _silu(y), 0.0).astype(
            out_refs[conv_out].dtype)
        xpad_ref[0:CONV_HALO, :] = tail


def _ffn_proj(src, meta_block, norm4, layer, wg, wu, wd, w_in, widths, dtypes, rows, lp, seq,
              conv_out=None, conv_w=None):
    d = norm4.shape[-1]
    d_ff = wg.shape[-1]
    assert sum(widths) == w_in.shape[1] and lp > ROW_TILE
    if meta_block is None:
        row_specs, extra, extra_specs, embed_blocks = _tile_specs(d), [], [], 0
    else:
        row_specs = _embed_specs(d, lp, seq)
        extra, extra_specs, embed_blocks = [meta_block], [_resident(meta_block.shape)], (
            lp // SWA_BLOCK)
    scratch = [pltpu.VMEM((ROW_TILE, d_ff), BF16)]
    if conv_out is not None:
        extra, extra_specs = extra + [conv_w], extra_specs + [_resident(conv_w.shape)]
        scratch.append(pltpu.VMEM((CONV_HALO + ROW_TILE, widths[conv_out]), F32))
    tile = lambda n: pl.BlockSpec((ROW_TILE, n), lambda i: (i, 0))
    return pl.pallas_call(
        functools.partial(_ffn_proj_body, n_rows=len(row_specs), embed_blocks=embed_blocks,
                          n_out=len(widths), conv_out=conv_out, lp=lp),
        grid=(rows // ROW_TILE,),
        in_specs=row_specs + extra_specs
        + [_pick(norm4, layer, 0), _pick(norm4, layer, 1), _pick(wg, layer, 0),
           _pick(wu, layer, 0), _pick(wd, layer, 0), _pick(norm4, layer, 2),
           _resident(w_in.shape)],
        out_specs=[tile(d)] + [tile(n) for n in widths],
        out_shape=[jax.ShapeDtypeStruct((rows, d), F32)]
        + [jax.ShapeDtypeStruct((rows, n), dt) for n, dt in zip(widths, dtypes)],
        scratch_shapes=scratch,
        compiler_params=_params("arbitrary"),
        name="ffn_proj",
    )(*([src] * len(row_specs)), *extra, norm4, norm4, wg, wu, wd, norm4, w_in)


def _proj_ffn_body(*refs, n_rows, n_x):
    x_groups = [refs[g * n_rows:(g + 1) * n_rows] for g in range(n_x)]
    refs = refs[n_x * n_rows:]
    h_refs = refs[:n_rows]
    wo_ref, nwo_ref, nwa_ref, nwb_ref, wg_ref, wu_ref, wd_ref, o_ref, a_ref = refs[n_rows:]
    accs = []
    for part in range(ROW_SPLIT):
        acc = None
        off = 0
        for group in x_groups:
            xs = _load_rows(group, part)
            k = xs.shape[1]
            p = _dot(xs, wo_ref[off:off + k, :])
            acc = p if acc is None else acc + p
            off += k
        accs.append(acc)
    hs = [_load_rows(h_refs, part) + _rms(acc, nwo_ref[...]) for part, acc in enumerate(accs)]
    outs = _ffn_core(hs, nwa_ref[...], nwb_ref[...], wg_ref, wu_ref, wd_ref, a_ref)
    for part, out in enumerate(outs):
        o_ref[_group(part), :] = out


def _proj_ffn(xs, h, w_out, norm4, layer, wg, wu, wd, final, batch, lp, seq):
    d = norm4.shape[-1]
    d_ff = wg.shape[-1]
    if final:
        specs = lambda width: _real_specs(width, lp, seq)
        out_rows = batch * seq
    else:
        specs = _tile_specs
        out_rows = batch * lp
    h_specs = specs(d)
    n_rows = len(h_specs)
    in_specs, args = [], []
    for x in xs:
        in_specs += specs(x.shape[1])
        args += [x] * n_rows
    in_specs += h_specs + [_resident(w_out.shape), _pick(norm4, layer, 3),
                           _pick(norm4, layer, 4), _pick(norm4, layer, 5),
                           _pick(wg, layer, 1), _pick(wu, layer, 1), _pick(wd, layer, 1)]
    args += [h] * n_rows + [w_out, norm4, norm4, norm4, wg, wu, wd]
    return pl.pallas_call(
        functools.partial(_proj_ffn_body, n_rows=n_rows, n_x=len(xs)),
        grid=(out_rows // ROW_TILE,),
        in_specs=in_specs,
        out_specs=pl.BlockSpec((ROW_TILE, d), lambda i: (i, 0)),
        out_shape=jax.ShapeDtypeStruct((out_rows, d), F32),
        scratch_shapes=[pltpu.VMEM((ROW_TILE, d_ff), BF16)],
        compiler_params=_params("parallel"),
        name="proj_ffn",
    )(*args)


def _t5_bucket_np(rel):
    n = np.maximum(rel, 0)
    max_exact = REL_BUCKETS // 2
    n_f = np.maximum(n, 1).astype(np.float32)
    large = max_exact + (np.log(n_f / np.float32(max_exact))
                         / np.float32(math.log(REL_MAX_DIST / max_exact))
                         * np.float32(REL_BUCKETS - max_exact)).astype(np.int32)
    large = np.minimum(large, REL_BUCKETS - 1)
    return np.where(n < max_exact, n, large).astype(np.int32)


def _swa_bucket_index():
    i = np.arange(SWA_BLOCK)[:, None]
    c = np.arange(SWA_BLOCK)[None, :]
    out = np.full((3, SWA_BLOCK, 2 * SWA_BLOCK), -1, np.int32)
    for variant in range(3):
        pos_q = variant * SWA_BLOCK + i - PAD
        pos_k = variant * SWA_BLOCK + c - PAD - np.where(c > i, SWA_BLOCK, 0)
        rel_b = pos_q - pos_k
        assert ((rel_b >= 0) & (rel_b < SWA_BLOCK)).all()
        out[variant, :, :SWA_BLOCK] = np.where(pos_k >= N_META, _t5_bucket_np(rel_b), -1)
        rel_m = pos_q - c
        meta = (c < N_META) & (rel_m >= 0)
        out[variant, :, SWA_BLOCK:] = np.where(meta, _t5_bucket_np(rel_m), -1)
    return out


_SWA_BUCKET_INDEX = _swa_bucket_index()


def _bias_body(tbl_ref, idx_ref, o_ref):
    head = pl.program_id(1)
    idx = idx_ref[0]
    acc = jnp.full(idx.shape, NEG_INF, F32)
    for b in range(REL_BUCKETS):
        acc = jnp.where(idx == b, tbl_ref[b, head], acc)
    o_ref[0, 0] = acc


def _swa_bias(rel_table):
    idx = jnp.asarray(_SWA_BUCKET_INDEX)
    group = SWA_Q_HEADS // SWA_KV_HEADS
    width = idx.shape[2]
    return pl.pallas_call(
        _bias_body,
        grid=(3, SWA_Q_HEADS),
        in_specs=[pl.BlockSpec(memory_space=pltpu.SMEM),
                  pl.BlockSpec((1, SWA_BLOCK, width), lambda v, h: (v, 0, 0))],
        out_specs=pl.BlockSpec((1, 1, SWA_BLOCK, width),
                               lambda v, h: (v, h // group, h % group, 0)),
        out_shape=jax.ShapeDtypeStruct((3, SWA_KV_HEADS, group * SWA_BLOCK, width), F32),
        compiler_params=_params("arbitrary", "arbitrary"),
        name="swa_bias",
    )(rel_table, idx)


def _swa_stages(sink_ref, q_ref, kc_ref, kp_ref, km_ref, vc_ref, vp_ref, vm_ref, bias_ref,
                step, blocks, store):
    dh = SWA_HEAD_DIM
    blk = SWA_BLOCK
    group = SWA_Q_HEADS // SWA_KV_HEADS
    scale = dh ** -0.5
    zpad = jnp.zeros((blk - N_META, dh), BF16)
    row = lax.broadcasted_iota(jnp.int32, (group * blk, blk), 0) % blk
    col = lax.broadcasted_iota(jnp.int32, (group * blk, blk), 1)
    from_prev = col > row

    ones_col = (lax.broadcasted_iota(jnp.int32, (3 * blk, dh), 1) == 0).astype(BF16)

    units = [(j, hk) for j in range(blocks) for hk in range(SWA_KV_HEADS)]
    scores, values = [], []
    for j, hk in units:
        r = slice(j * blk, (j + 1) * blk)
        kv = slice(hk * dh, (hk + 1) * dh)
        k_prev = kp_ref[:, kv] if j == 0 else kc_ref[(j - 1) * blk:j * blk, kv]
        v_prev = vp_ref[:, kv] if j == 0 else vc_ref[(j - 1) * blk:j * blk, kv]
        k_all = jnp.concatenate([k_prev, kc_ref[r, kv], km_ref[:, kv], zpad], axis=0)
        v_all = jnp.concatenate([v_prev, vc_ref[r, kv], vm_ref[:, kv], zpad], axis=0)
        values.append(jnp.concatenate([v_all, ones_col], axis=1))
        q = jnp.concatenate([q_ref[r, (hk * group + g) * dh:(hk * group + g + 1) * dh]
                             for g in range(group)], axis=0) * scale
        scores.append(_dot_nt(q, k_all))
        yield

    probs, sink_terms = [], []
    for (j, hk), s in zip(units, scores):
        bias = bias_ref[jnp.minimum(step * blocks + j, 2), hk]
        s_band = jnp.where(from_prev, s[:, :blk], s[:, blk:2 * blk]) + bias[:, :blk]
        s_meta = s[:, 2 * blk:] + bias[:, blk:]
        sink = jnp.concatenate([jnp.full((blk, 1), sink_ref[hk * group + g], F32)
                                for g in range(group)], axis=0)
        m = jnp.maximum(jnp.max(jnp.maximum(s_band, s_meta), axis=-1, keepdims=True), sink)
        e_band = jnp.exp(s_band - m)
        e_meta = jnp.exp(s_meta - m)
        sink_terms.append(jnp.exp(sink - m))
        probs.append(jnp.concatenate([jnp.where(from_prev, e_band, 0.0),
                                      jnp.where(from_prev, 0.0, e_band), e_meta],
                                     axis=1).astype(BF16))

    for (j, hk), p, v_ext, sink_term in zip(units, probs, values, sink_terms):
        pv = _dot(p, v_ext)
        o = pv[:, :dh] / (pv[:, dh:dh + 1] + sink_term)
        for g in range(group):
            hq = hk * group + g
            store(slice(j * blk, (j + 1) * blk), slice(hq * dh, (hq + 1) * dh),
                  o[g * blk:(g + 1) * blk])
        yield


def _swa_body(sink_ref, q_ref, kc_ref, kp_ref, km_ref, vc_ref, vp_ref, vm_ref, bias_ref, o_ref,
              *, blocks):
    def store(rows, cols, value):
        o_ref[rows, cols] = value.astype(o_ref.dtype)

    _drain(_swa_stages(sink_ref, q_ref, kc_ref, kp_ref, km_ref, vc_ref, vp_ref, vm_ref,
                       bias_ref, pl.program_id(1), blocks, store))


def _swa(q, k, v, bias, sinks, batch, lp, blocks):
    rows = q.shape[0]
    nb = lp // SWA_BLOCK
    steps = nb // blocks
    meta_blocks = lp // N_META
    cur = lambda b, s: (b * steps + s, 0)
    prev = lambda b, s: (b * nb + jnp.maximum(s * blocks - 1, 0), 0)
    meta = lambda b, s: (b * meta_blocks + PAD // N_META, 0)
    kv_w = k.shape[1]
    rs = blocks * SWA_BLOCK
    return pl.pallas_call(
        functools.partial(_swa_body, blocks=blocks),
        grid=(batch, steps),
        in_specs=[pl.BlockSpec(memory_space=pltpu.SMEM),
                  pl.BlockSpec((rs, q.shape[1]), cur),
                  pl.BlockSpec((rs, kv_w), cur), pl.BlockSpec((SWA_BLOCK, kv_w), prev),
                  pl.BlockSpec((N_META, kv_w), meta),
                  pl.BlockSpec((rs, kv_w), cur), pl.BlockSpec((SWA_BLOCK, kv_w), prev),
                  pl.BlockSpec((N_META, kv_w), meta),
                  _resident(bias.shape)],
        out_specs=pl.BlockSpec((rs, q.shape[1]), cur),
        out_shape=jax.ShapeDtypeStruct((rows, q.shape[1]), BF16),
        compiler_params=_params("parallel", "arbitrary"),
        name="swa",
    )(sinks, q, k, k, k, v, v, v, bias)


def _tri_masks():
    row = lax.broadcasted_iota(jnp.int32, (CHUNK, CHUNK), 0)
    col = lax.broadcasted_iota(jnp.int32, (CHUNK, CHUNK), 1)
    return row >= col, row > col, row == col


def _chunk_cumsum(x, group):
    rows = group * CHUNK
    ri = lax.broadcasted_iota(jnp.int32, (rows, rows), 0)
    ci = lax.broadcasted_iota(jnp.int32, (rows, rows), 1)
    tri = ((ri >= ci) & (ri // CHUNK == ci // CHUNK)).astype(BF16)
    hi = x.astype(BF16)
    rest = x - hi.astype(F32)
    mid = rest.astype(BF16)
    lo = (rest - mid.astype(F32)).astype(BF16)
    out = []
    for start in range(0, x.shape[0], rows):
        part = slice(start, start + rows)
        out.append((_dot(tri, lo[part]) + _dot(tri, mid[part])) + _dot(tri, hi[part]))
    return out[0] if len(out) == 1 else jnp.concatenate(out, axis=0)


def _deltanet_stages(act_ref, z_ref, gates_ref, alog_ref, dtb_ref, nw_ref, s_ref, chunks,
                     store):
    nh, dh = DN_HEADS, DN_HEAD_DIM
    gt = gates_ref[...]
    beta_full = jax.nn.sigmoid(gt)
    g_full = -jnp.exp(alog_ref[...]) * _softplus(gt + dtb_ref[...])
    incl, strict, diag = _tri_masks()
    eye = diag.astype(F32)
    gc_all = _chunk_cumsum(g_full, chunks)
    gc_chunks = [gc_all[c * CHUNK:(c + 1) * CHUNK, :] for c in range(chunks)]
    items = [(c, h) for c in range(chunks) for h in range(nh)]
    yield

    neg_a, akt16, qd16, rhs16, decay = [], [], [], [], []
    for c, h in items:
        if h == 0 and c % 2 == 0 and c > 0:
            yield
        r = slice(c * CHUNK, (c + 1) * CHUNK)
        q = act_ref[r, h * dh:(h + 1) * dh].astype(F32)
        k = act_ref[r, DN_DIM + h * dh:DN_DIM + (h + 1) * dh].astype(F32)
        v = act_ref[r, 2 * DN_DIM + h * dh:2 * DN_DIM + (h + 1) * dh].astype(F32)
        q = q * lax.rsqrt(jnp.sum(q * q, axis=-1, keepdims=True) + 1e-6) * dh ** -0.5
        k = k * lax.rsqrt(jnp.sum(k * k, axis=-1, keepdims=True) + 1e-6)
        beta = jnp.broadcast_to(beta_full[r, h:h + 1], (CHUNK, dh))
        gc = jnp.broadcast_to(gc_chunks[c][:, nh + h:nh + h + 1], (CHUNK, dh))
        gc_row = jnp.transpose(gc)[:CHUNK, :]
        gamma = jnp.where(incl, jnp.exp(jnp.where(incl, gc[:, :CHUNK] - gc_row, 0.0)), 0.0)
        k_beta = k * beta
        kq = _dot_nt(jnp.concatenate([k_beta, q], axis=0).astype(BF16), k.astype(BF16))
        neg_a.append(jnp.where(strict, -(kq[:CHUNK] * gamma), 0.0))
        attn16 = jnp.where(incl, kq[CHUNK:] * gamma, 0.0).astype(BF16)
        gc_last = gc[CHUNK - 1:CHUNK, :]
        qd16.append((q * jnp.exp(gc)).astype(BF16))
        kdt16 = jnp.transpose(k * jnp.exp(gc_last - gc)).astype(BF16)
        akt16.append(jnp.concatenate([attn16, kdt16], axis=0))
        rhs16.append(jnp.concatenate([v * beta, k_beta * jnp.exp(gc)], axis=1).astype(BF16))
        decay.append(jnp.exp(gc_last))

    yield
    levels = int(math.log2(CHUNK)) - 1
    t = [eye + x for x in neg_a]
    p16 = [x.astype(BF16) for x in neg_a]
    p16 = [_dot(x, x).astype(BF16) for x in p16]
    yield
    for level in range(levels):
        if level + 1 < levels:
            tp = [_dot(jnp.concatenate([y.astype(BF16), x], axis=0), x) for y, x in zip(t, p16)]
            t = [y + r[:CHUNK] for y, r in zip(t, tp)]
            p16 = [r[CHUNK:].astype(BF16) for r in tp]
        else:
            t = [y + _dot(y.astype(BF16), x) for y, x in zip(t, p16)]
        yield
    uw = [_dot(y.astype(BF16), b) for y, b in zip(t, rhs16)]
    wq16 = [jnp.concatenate([x[:, dh:].astype(BF16), y], axis=0)
            for x, y in zip(uw, qd16)]
    yield

    nw = nw_ref[...]
    state = [s_ref[h] for h in range(nh)]
    for c in range(chunks):
        r = slice(c * CHUNK, (c + 1) * CHUNK)
        pair = [c * nh + h for h in range(nh)]
        s16 = [s.astype(BF16) for s in state]
        wqs = [_dot(wq16[i], s16[h]) for h, i in enumerate(pair)]
        yield
        vn16 = [(uw[i][:, :dh] - wqs[h][:CHUNK]).astype(BF16) for h, i in enumerate(pair)]
        ak = [_dot(akt16[i], vn16[h]) for h, i in enumerate(pair)]
        state = [state[h] * decay[i] + ak[h][CHUNK:] for h, i in enumerate(pair)]
        for h in range(nh):
            hc = slice(h * dh, (h + 1) * dh)
            out = wqs[h][CHUNK:] + ak[h][:CHUNK]
            store(r, hc, _rms(out, nw) * _silu(z_ref[r, hc].astype(F32)))
        yield
    for h in range(nh):
        s_ref[h] = state[h]


def _deltanet_body(act_ref, z_ref, gates_ref, alog_ref, dtb_ref, nw_ref, o_ref, s_ref, *,
                   chunks):
    @pl.when(pl.program_id(1) == 0)
    def _():
        s_ref[...] = jnp.zeros_like(s_ref)

    def store(rows, cols, value):
        o_ref[rows, cols] = value.astype(o_ref.dtype)

    _drain(_deltanet_stages(act_ref, z_ref, gates_ref, alog_ref, dtb_ref, nw_ref, s_ref,
                            chunks, store))


def _deltanet(act, z, gates, alog_vec, dtb_vec, nw, batch, lp, chunks):
    rows = act.shape[0]
    rs = chunks * CHUNK
    steps = lp // rs
    idx = lambda b, s: (b * steps + s, 0)
    return pl.pallas_call(
        functools.partial(_deltanet_body, chunks=chunks),
        grid=(batch, steps),
        in_specs=[pl.BlockSpec((rs, act.shape[1]), idx), pl.BlockSpec((rs, z.shape[1]), idx),
                  pl.BlockSpec((rs, gates.shape[1]), idx),
                  _resident(alog_vec.shape), _resident(dtb_vec.shape), _resident(nw.shape)],
        out_specs=pl.BlockSpec((rs, DN_DIM), idx),
        out_shape=jax.ShapeDtypeStruct((rows, DN_DIM), BF16),
        scratch_shapes=[pltpu.VMEM((DN_HEADS, DN_HEAD_DIM, DN_HEAD_DIM), F32)],
        compiler_params=_params("parallel", "arbitrary"),
        name="deltanet",
    )(act, z, gates, alog_vec, dtb_vec, nw)


def _mix_ffn_body(sink_ref, q_ref, kc_ref, kp_ref, km_ref, vc_ref, vp_ref, vm_ref, bias_ref,
                  act_ref, z_ref, gates_ref, alog_ref, dtb_ref, dnw_ref,
                  h_ref, wo_ref, nwo_ref, nwa_ref, nwb_ref, wg_ref, wu_ref, wd_ref,
                  o_ref, s_ref, mix_ref, a_ref, *, blocks, chunks, steps, last):
    t = pl.program_id(0)
    seq_step = jnp.minimum(t, last) % steps
    write_slot = t % 2
    read_slot = 1 - write_slot
    swa_width = q_ref.shape[1]
    groups = q_ref.shape[0] // GROUP_ROWS

    @pl.when(t == 0)
    def _():
        mix_ref[...] = jnp.zeros_like(mix_ref)

    @pl.when(seq_step == 0)
    def _():
        s_ref[...] = jnp.zeros_like(s_ref)

    def store_swa(rows, cols, value):
        mix_ref[write_slot, rows, cols] = value.astype(mix_ref.dtype)

    def store_dn(rows, cols, value):
        cols = slice(swa_width + cols.start, swa_width + cols.stop)
        mix_ref[write_slot, rows, cols] = value.astype(mix_ref.dtype)

    def mixer_stages():
        dn = _deltanet_stages(act_ref, z_ref, gates_ref, alog_ref, dtb_ref, dnw_ref,
                              s_ref, chunks, store_dn)
        swa = _swa_stages(sink_ref, q_ref, kc_ref, kp_ref, km_ref, vc_ref, vp_ref,
                          vm_ref, bias_ref, seq_step, blocks, store_swa)
        for _ in dn:
            yield
            if next(swa, StopIteration) is not StopIteration:
                yield
        yield from swa

    def ffn_stages():
        accs = []
        for part in range(groups):
            accs.append(_dot(mix_ref[read_slot, _group(part), :], wo_ref[...]))
            yield
        hs = [h_ref[_group(part), :] + _rms(acc, nwo_ref[...]) for part, acc in enumerate(accs)]
        outs = []
        yield from _ffn_stages(hs, nwa_ref[...], nwb_ref[...], wg_ref, wu_ref, wd_ref, a_ref,
                               outs)
        for part, out in enumerate(outs):
            o_ref[_group(part), :] = out

    _interleave(ffn_stages(), mixer_stages())


def _mix_ffn(qa, ka, va, bias, sinks, act, z, gates, alog_vec, dtb_vec, dn_nw, h, w_out, norm4,
             layer, wg, wu, wd, batch, lp, blocks, chunks):
    d = norm4.shape[-1]
    d_ff = wg.shape[-1]
    rs = blocks * SWA_BLOCK
    assert rs == chunks * CHUNK and rs % GROUP_ROWS == 0
    nb = lp // SWA_BLOCK
    steps = lp // rs
    last = batch * steps - 1
    meta_blocks = lp // N_META

    def tile(t):
        return jnp.minimum(t, last)

    cur = lambda t: (tile(t), 0)
    prev = lambda t: ((tile(t) // steps) * nb + jnp.maximum((tile(t) % steps) * blocks - 1, 0), 0)
    meta = lambda t: ((tile(t) // steps) * meta_blocks + PAD // N_META, 0)
    behind = lambda t: (jnp.maximum(t - 1, 0), 0)
    kv_w = ka.shape[1]
    return pl.pallas_call(
        functools.partial(_mix_ffn_body, blocks=blocks, chunks=chunks, steps=steps, last=last),
        grid=(last + 2,),
        in_specs=[pl.BlockSpec(memory_space=pltpu.SMEM),
                  pl.BlockSpec((rs, qa.shape[1]), cur),
                  pl.BlockSpec((rs, kv_w), cur), pl.BlockSpec((SWA_BLOCK, kv_w), prev),
                  pl.BlockSpec((N_META, kv_w), meta),
                  pl.BlockSpec((rs, kv_w), cur), pl.BlockSpec((SWA_BLOCK, kv_w), prev),
                  pl.BlockSpec((N_META, kv_w), meta),
                  _resident(bias.shape),
                  pl.BlockSpec((rs, act.shape[1]), cur), pl.BlockSpec((rs, z.shape[1]), cur),
                  pl.BlockSpec((rs, gates.shape[1]), cur),
                  _resident(alog_vec.shape), _resident(dtb_vec.shape), _resident(dn_nw.shape),
                  pl.BlockSpec((rs, d), behind), _resident(w_out.shape),
                  _pick(norm4, layer, 3), _pick(norm4, layer, 4), _pick(norm4, layer, 5),
                  _pick(wg, layer, 1), _pick(wu, layer, 1), _pick(wd, layer, 1)],
        out_specs=pl.BlockSpec((rs, d), behind),
        out_shape=jax.ShapeDtypeStruct(h.shape, F32),
        scratch_shapes=[pltpu.VMEM((DN_HEADS, DN_HEAD_DIM, DN_HEAD_DIM), F32),
                        pltpu.VMEM((2, rs, qa.shape[1] + DN_DIM), BF16),
                        pltpu.VMEM((rs, d_ff), BF16)],
        compiler_params=_params("arbitrary"),
        name="mix_ffn",
    )(sinks, qa, ka, ka, ka, va, va, va, bias, act, z, gates, alog_vec, dtb_vec, dn_nw,
      h, w_out, norm4, norm4, norm4, wg, wu, wd)


def _gla_body(q_ref, k_ref, v_ref, g_ref, gk_ref, wgu_ref, bg_ref, nw_ref, o_ref, s_ref, *,
              chunks):
    nh = GLA_HEADS
    dk = q_ref.shape[1] // nh
    dv = v_ref.shape[1] // nh

    @pl.when(pl.program_id(1) == 0)
    def _():
        s_ref[...] = jnp.zeros_like(s_ref)

    logits = _dot(gk_ref[...].astype(BF16), wgu_ref[...]) + bg_ref[...]
    glog = (jnp.minimum(logits, 0.0) - jnp.log(1.0 + jnp.exp(-jnp.abs(logits)))) / GLA_GATE_NORM
    incl, _, _ = _tri_masks()
    bcum_all = _chunk_cumsum(glog, 1)
    bcum = [bcum_all[c * CHUNK:(c + 1) * CHUNK, :] for c in range(chunks)]

    items = [(c, h) for c in range(chunks) for h in range(nh)]
    qd16, kn16, kdt16, decay = [], [], [], []
    for c, h in items:
        r = slice(c * CHUNK, (c + 1) * CHUNK)
        kc = slice(h * dk, (h + 1) * dk)
        bc = bcum[c][:, kc]
        q = q_ref[r, kc].astype(F32) * dk ** -0.5
        k = k_ref[r, kc].astype(F32)
        b_last = bc[CHUNK - 1:CHUNK, :]
        qd16.append((q * jnp.exp(bc)).astype(BF16))
        kn16.append((k * jnp.exp(-bc)).astype(BF16))
        kdt16.append(jnp.transpose(k * jnp.exp(b_last - bc)).astype(BF16))
        decay_rows = jnp.broadcast_to(jnp.exp(b_last), (8, dk))
        decay.append(jnp.transpose(decay_rows)[:, :1])
    values = [v_ref[c * CHUNK:(c + 1) * CHUNK, h * dv:(h + 1) * dv] for c, h in items]
    attn16 = [jnp.where(incl, _dot_nt(a, b), 0.0).astype(BF16) for a, b in zip(qd16, kn16)]
    intra = [_dot(a, v) for a, v in zip(attn16, values)]
    update = [_dot(kt, v) for kt, v in zip(kdt16, values)]

    nw = nw_ref[...]
    state = [s_ref[h] for h in range(nh)]
    for c in range(chunks):
        r = slice(c * CHUNK, (c + 1) * CHUNK)
        for h in range(nh):
            i = c * nh + h
            vc = slice(h * dv, (h + 1) * dv)
            o = intra[i] + _dot(qd16[i], state[h].astype(BF16))
            state[h] = state[h] * decay[i] + update[i]
            o = _rms(o, nw) * _silu(g_ref[r, vc].astype(F32))
            o_ref[r, vc] = o.astype(o_ref.dtype)
    for h in range(nh):
        s_ref[h] = state[h]


def _gla(q, k, v, g, gk, wgu, bg, nw, batch, lp, chunks):
    rows = q.shape[0]
    rs = chunks * CHUNK
    steps = lp // rs
    idx = lambda b, s: (b * steps + s, 0)
    dk = q.shape[1] // GLA_HEADS
    dv = v.shape[1] // GLA_HEADS
    return pl.pallas_call(
        functools.partial(_gla_body, chunks=chunks),
        grid=(batch, steps),
        in_specs=[pl.BlockSpec((rs, q.shape[1]), idx), pl.BlockSpec((rs, k.shape[1]), idx),
                  pl.BlockSpec((rs, v.shape[1]), idx), pl.BlockSpec((rs, g.shape[1]), idx),
                  pl.BlockSpec((rs, gk.shape[1]), idx), _resident(wgu.shape),
                  _resident(bg.shape), _resident(nw.shape)],
        out_specs=pl.BlockSpec((rs, v.shape[1]), idx),
        out_shape=jax.ShapeDtypeStruct((rows, v.shape[1]), BF16),
        scratch_shapes=[pltpu.VMEM((GLA_HEADS, dk, dv), F32)],
        compiler_params=_params("parallel", "arbitrary"),
        name="gla",
    )(q, k, v, g, gk, wgu, bg, nw)


def _pad_cols(w, width):
    return jnp.pad(w, ((0, 0), (0, width - w.shape[1])))


def _per_step(n, preferred):
    for c in preferred:
        if n % c == 0:
            return c
    return 1


def kernel(x, meta_tokens, norm_w, ffn_w_gate, ffn_w_up, ffn_w_down, rel_bias_table,
           even_w_in, even_conv_w, swa_sinks, dn_a_log, dn_dt_bias, dn_norm_w, even_w_out,
           odd_w_in, gla_w_gate_up, gla_b_gate, gla_norm_w, odd_w_out):
    batch, seq, d = x.shape
    depth = norm_w.shape[0]
    lp = SWA_BLOCK + seq
    assert seq % ROW_TILE == 0 and (batch * lp) % ROW_TILE == 0
    rows = batch * lp
    chunks = _per_step(lp // CHUNK, (6, 3, 2))
    swa_blocks = _per_step(lp // SWA_BLOCK, (3, 2))

    meta_block = jnp.pad(meta_tokens.astype(x.dtype), ((PAD, 0), (0, 0)))
    norm4 = norm_w[:, :, None, :]
    wg = ffn_w_gate.astype(BF16)
    wu = ffn_w_up.astype(BF16)
    wd = ffn_w_down.astype(BF16)

    swa_q = SWA_Q_HEADS * SWA_HEAD_DIM
    swa_kv = SWA_KV_HEADS * SWA_HEAD_DIM
    bias = None
    h = x.reshape(batch * seq, d)
    for layer in range(depth):
        i = layer // 2
        embed = meta_block if layer == 0 else None
        final = layer == depth - 1
        if layer % 2 == 0:
            assert even_w_in.shape[2] == swa_q + 2 * swa_kv + 4 * DN_DIM + 2 * DN_HEADS
            widths = (swa_q, swa_kv, swa_kv, 3 * DN_DIM, DN_DIM, LANE)
            w_in = _pad_cols(even_w_in[i], sum(widths)).astype(BF16)
            h, qa, ka, va, act_b, z_b, gates = _ffn_proj(
                h, embed, norm4, layer, wg, wu, wd, w_in, widths,
                (BF16, BF16, BF16, BF16, BF16, F32), rows, lp, seq,
                conv_out=3, conv_w=even_conv_w[i])
            if bias is None:
                bias = _swa_bias(rel_bias_table)
            lane_pad = (DN_HEADS, LANE - 2 * DN_HEADS)
            alog_vec = jnp.pad(dn_a_log[i], lane_pad)[None, :]
            dtb_vec = jnp.pad(dn_dt_bias[i], lane_pad)[None, :]
            dn_nw = dn_norm_w[i][None, :]
            w_out = even_w_out[i]
            if not final and swa_blocks * SWA_BLOCK == chunks * CHUNK:
                h = _mix_ffn(qa, ka, va, bias, swa_sinks[i], act_b, z_b, gates, alog_vec,
                             dtb_vec, dn_nw, h, w_out.astype(BF16), norm4, layer, wg, wu, wd,
                             batch, lp, swa_blocks, chunks)
                continue
            o_a = _swa(qa, ka, va, bias, swa_sinks[i], batch, lp, swa_blocks)
            o_b = _deltanet(act_b, z_b, gates, alog_vec, dtb_vec, dn_nw, batch, lp, chunks)
            mixed = [o_a, o_b]
        else:
            key_dim = gla_w_gate_up.shape[2]
            val_dim = odd_w_out.shape[1]
            widths = (key_dim, key_dim, val_dim, val_dim, LANE)
            w_in = _pad_cols(odd_w_in[i], sum(widths)).astype(BF16)
            h, q, k, v, g, gk = _ffn_proj(
                h, embed, norm4, layer, wg, wu, wd, w_in, widths,
                (BF16, BF16, BF16, BF16, F32), rows, lp, seq)
            wgu = jnp.pad(gla_w_gate_up[i], ((0, LANE - GLA_GATE_RANK), (0, 0))).astype(BF16)
            o = _gla(q, k, v, g, gk, wgu, gla_b_gate[i][None, :], gla_norm_w[i][None, :],
                     batch, lp, chunks)
            mixed, w_out = [o], odd_w_out[i]
        h = _proj_ffn(mixed, h, w_out.astype(BF16), norm4, layer, wg, wu, wd, final,
                      batch, lp, seq)
    return h.reshape(batch, seq, d)
```

```python
import functools
import math

import numpy as np
import jax
import jax.numpy as jnp
from jax import lax
from jax.experimental import pallas as pl
from jax.experimental.pallas import tpu as pltpu

F32 = jnp.float32
BF16 = jnp.bfloat16
HIGHEST = lax.Precision.HIGHEST

N_META = 16
NORM_EPS = 1e-6
NEG_INF = -1e30
SWA_Q_HEADS = 8
SWA_KV_HEADS = 2
SWA_HEAD_DIM = 64
SWA_BLOCK = 128
REL_BUCKETS = 32
REL_MAX_DIST = 128
DN_HEADS = 4
DN_HEAD_DIM = 128
DN_DIM = DN_HEADS * DN_HEAD_DIM
DN_CONV = 4
GLA_HEADS = 4
GLA_GATE_RANK = 16
GLA_GATE_NORM = 16.0
CHUNK = 64

PAD = SWA_BLOCK - N_META
LANE = 128
ROW_TILE = 512
FF_TILE = 256
VMEM_LIMIT = 56 * 1024 * 1024


def _rms(x, w):
    return x * lax.rsqrt(jnp.mean(x * x, axis=-1, keepdims=True) + NORM_EPS) * w


def _silu(x):
    return x * jax.nn.sigmoid(x)


def _softplus(x):
    return jnp.maximum(x, 0.0) + jnp.log(1.0 + jnp.exp(-jnp.abs(x)))


def _dot(a, b, precision=None):
    return jnp.dot(a, b, preferred_element_type=F32, precision=precision)


def _dot_nt(a, b, precision=None):
    return lax.dot_general(a, b, (((1,), (1,)), ((), ())), preferred_element_type=F32,
                           precision=precision)


def _dot_tn(a, b, precision=None):
    return lax.dot_general(a, b, (((0,), (0,)), ((), ())), preferred_element_type=F32,
                           precision=precision)


def _resident(shape):
    nd = len(shape)
    return pl.BlockSpec(shape, lambda *_: (0,) * nd, pipeline_mode=pl.Buffered(1))


def _params(*semantics):
    return pltpu.CompilerParams(dimension_semantics=semantics, vmem_limit_bytes=VMEM_LIMIT)


SUB = ROW_TILE // SWA_BLOCK


def _pick(arr, *lead):
    rest = arr.shape[len(lead):]
    index = tuple(lead) + (0,) * len(rest)
    return pl.BlockSpec((None,) * len(lead) + rest, lambda i: index,
                        pipeline_mode=pl.Buffered(1))


def _tile_specs(width):
    return [pl.BlockSpec((ROW_TILE, width), lambda i: (i, 0))]


def _real_specs(width, lp, seq):
    tiles = seq // ROW_TILE
    blocks = lp // SWA_BLOCK

    def spec(j):
        return pl.BlockSpec((SWA_BLOCK, width),
                            lambda i: ((i // tiles) * blocks + 1 + (i % tiles) * SUB + j, 0))
    return [spec(j) for j in range(SUB)]


def _embed_specs(width, lp, seq):
    blocks = lp // SWA_BLOCK
    x_blocks = seq // SWA_BLOCK

    def spec(j):
        def index(i):
            blk = i * SUB + j
            return ((blk // blocks) * x_blocks + jnp.maximum(blk % blocks - 1, 0), 0)
        return pl.BlockSpec((SWA_BLOCK, width), index)
    return [spec(j) for j in range(SUB)]


ROW_SPLIT = 4
GROUP_ROWS = ROW_TILE // ROW_SPLIT


def _group(part):
    return slice(part * GROUP_ROWS, (part + 1) * GROUP_ROWS)


def _load_rows(refs, part):
    if len(refs) == 1:
        return refs[0][_group(part), :]
    per = len(refs) // ROW_SPLIT
    return jnp.concatenate([r[...] for r in refs[part * per:(part + 1) * per]], axis=0)


def _drain(stages):
    for _ in stages:
        pass


def _interleave(main, side):
    for _ in main:
        next(side, None)
    _drain(side)


def _ffn_stages(xs, nwa, nwb, wg_ref, wu_ref, wd_ref, a_ref, outs):
    hns = [_rms(x, nwa).astype(BF16) for x in xs]
    d_ff = wg_ref.shape[1]
    for part, hn in enumerate(hns):
        for c in range(d_ff // FF_TILE):
            cols = slice(c * FF_TILE, (c + 1) * FF_TILE)
            g = _dot(hn, wg_ref[:, cols])
            u = _dot(hn, wu_ref[:, cols])
            a_ref[_group(part), cols] = (_silu(g) * u).astype(BF16)
            yield
    fs = []
    for part in range(len(xs)):
        fs.append(_dot(a_ref[_group(part), :], wd_ref[...]))
        yield
    outs.extend(x + 0.5 * _rms(f, nwb) for x, f in zip(xs, fs))


def _ffn_core(xs, nwa, nwb, wg_ref, wu_ref, wd_ref, a_ref):
    outs = []
    _drain(_ffn_stages(xs, nwa, nwb, wg_ref, wu_ref, wd_ref, a_ref, outs))
    return outs


CONV_HALO = 8


def _ffn_proj_body(*refs, n_rows, embed_blocks, n_out, conv_out, lp):
    row_refs = refs[:n_rows]
    refs = refs[n_rows:]
    if embed_blocks:
        meta_ref, refs = refs[0], refs[1:]
    if conv_out is not None:
        convw_ref, refs = refs[0], refs[1:]
        xpad_ref = refs[10 + n_out]

        @pl.when(pl.program_id(0) == 0)
        def _():
            xpad_ref[0:CONV_HALO, :] = jnp.zeros((CONV_HALO, xpad_ref.shape[1]), F32)

    nwa_ref, nwb_ref, wg_ref, wu_ref, wd_ref, nwm_ref, win_ref, wtail_ref = refs[:8]
    h_ref = refs[8]
    out_refs = refs[9:9 + n_out]
    a_ref = refs[9 + n_out]
    offsets = np.cumsum([0] + [o.shape[1] for o in out_refs])
    xs = []
    for part in range(ROW_SPLIT):
        if embed_blocks:
            per = n_rows // ROW_SPLIT
            first = pl.program_id(0) * SUB + part * per
            xs.append(jnp.concatenate(
                [jnp.where((first + j) % embed_blocks == 0, meta_ref[...], r[...])
                 for j, r in enumerate(row_refs[part * per:(part + 1) * per])], axis=0))
        else:
            xs.append(_load_rows(row_refs, part))
    hs = _ffn_core(xs, nwa_ref[...], nwb_ref[...], wg_ref, wu_ref, wd_ref, a_ref)
    hns = [_rms(h, nwm_ref[...]).astype(BF16) for h in hs]
    for part, (h, hn) in enumerate(zip(hs, hns)):
        rows = _group(part)
        h_ref[rows, :] = h
        for i, o_ref in enumerate(out_refs):
            if i + 1 < n_out:
                p = _dot(hn, win_ref[:, offsets[i]:offsets[i + 1]])
            else:
                p = _dot(hn, wtail_ref[...])
            if i == conv_out:
                xpad_ref[CONV_HALO + part * GROUP_ROWS:CONV_HALO + (part + 1) * GROUP_ROWS,
                         :] = p
            else:
                o_ref[rows, :] = p.astype(o_ref.dtype)

    if conv_out is not None:
        taps = convw_ref.shape[0]
        cw = convw_ref[...]
        y = cw[taps - 1:taps, :] * xpad_ref[CONV_HALO:, :]
        for j in range(taps - 1):
            lo = CONV_HALO - (taps - 1 - j)
            y = y + cw[j:j + 1, :] * xpad_ref[lo:lo + ROW_TILE, :]
        tail = xpad_ref[ROW_TILE:, :]
        row = (pl.program_id(0) * ROW_TILE) % lp + lax.broadcasted_iota(
            jnp.int32, (ROW_TILE, 1), 0)
        row = jnp.where(row >= lp, row - lp, row)
        out_refs[conv_out][...] = jnp.where(row >= PAD, _silu(y), 0.0).astype(
            out_refs[conv_out].dtype)
        xpad_ref[0:CONV_HALO, :] = tail


def _ffn_proj(src, meta_block, norm4, layer, wg, wu, wd, w_in, w_tail, widths, dtypes, rows, lp,
              seq, conv_out=None, conv_w=None):
    d = norm4.shape[-1]
    d_ff = wg.shape[-1]
    assert sum(widths[:-1]) == w_in.shape[1] and widths[-1] == w_tail.shape[1]
    assert lp > ROW_TILE
    if meta_block is None:
        row_specs, extra, extra_specs, embed_blocks = _tile_specs(d), [], [], 0
    else:
        row_specs = _embed_specs(d, lp, seq)
        extra, extra_specs, embed_blocks = [meta_block], [_resident(meta_block.shape)], (
            lp // SWA_BLOCK)
    scratch = [pltpu.VMEM((ROW_TILE, d_ff), BF16)]
    if conv_out is not None:
        extra, extra_specs = extra + [conv_w], extra_specs + [_resident(conv_w.shape)]
        scratch.append(pltpu.VMEM((CONV_HALO + ROW_TILE, widths[conv_out]), F32))
    tile = lambda n: pl.BlockSpec((ROW_TILE, n), lambda i: (i, 0))
    return pl.pallas_call(
        functools.partial(_ffn_proj_body, n_rows=len(row_specs), embed_blocks=embed_blocks,
                          n_out=len(widths), conv_out=conv_out, lp=lp),
        grid=(rows // ROW_TILE,),
        in_specs=row_specs + extra_specs
        + [_pick(norm4, layer, 0), _pick(norm4, layer, 1), _pick(wg, layer, 0),
           _pick(wu, layer, 0), _pick(wd, layer, 0), _pick(norm4, layer, 2),
           _resident(w_in.shape), _resident(w_tail.shape)],
        out_specs=[tile(d)] + [tile(n) for n in widths],
        out_shape=[jax.ShapeDtypeStruct((rows, d), F32)]
        + [jax.ShapeDtypeStruct((rows, n), dt) for n, dt in zip(widths, dtypes)],
        scratch_shapes=scratch,
        compiler_params=_params("arbitrary"),
        name="ffn_proj",
    )(*([src] * len(row_specs)), *extra, norm4, norm4, wg, wu, wd, norm4, w_in, w_tail)


def _proj_ffn_body(*refs, n_rows, n_x):
    x_groups = [refs[g * n_rows:(g + 1) * n_rows] for g in range(n_x)]
    refs = refs[n_x * n_rows:]
    h_refs = refs[:n_rows]
    wo_ref, nwo_ref, nwa_ref, nwb_ref, wg_ref, wu_ref, wd_ref, o_ref, a_ref = refs[n_rows:]
    accs = []
    for part in range(ROW_SPLIT):
        acc = None
        off = 0
        for group in x_groups:
            xs = _load_rows(group, part)
            k = xs.shape[1]
            p = _dot(xs, wo_ref[off:off + k, :])
            acc = p if acc is None else acc + p
            off += k
        accs.append(acc)
    hs = [_load_rows(h_refs, part) + _rms(acc, nwo_ref[...]) for part, acc in enumerate(accs)]
    outs = _ffn_core(hs, nwa_ref[...], nwb_ref[...], wg_ref, wu_ref, wd_ref, a_ref)
    for part, out in enumerate(outs):
        o_ref[_group(part), :] = out


def _proj_ffn(xs, h, w_out, norm4, layer, wg, wu, wd, final, batch, lp, seq):
    d = norm4.shape[-1]
    d_ff = wg.shape[-1]
    if final:
        specs = lambda width: _real_specs(width, lp, seq)
        out_rows = batch * seq
    else:
        specs = _tile_specs
        out_rows = batch * lp
    h_specs = specs(d)
    n_rows = len(h_specs)
    in_specs, args = [], []
    for x in xs:
        in_specs += specs(x.shape[1])
        args += [x] * n_rows
    in_specs += h_specs + [_resident(w_out.shape), _pick(norm4, layer, 3),
                           _pick(norm4, layer, 4), _pick(norm4, layer, 5),
                           _pick(wg, layer, 1), _pick(wu, layer, 1), _pick(wd, layer, 1)]
    args += [h] * n_rows + [w_out, norm4, norm4, norm4, wg, wu, wd]
    return pl.pallas_call(
        functools.partial(_proj_ffn_body, n_rows=n_rows, n_x=len(xs)),
        grid=(out_rows // ROW_TILE,),
        in_specs=in_specs,
        out_specs=pl.BlockSpec((ROW_TILE, d), lambda i: (i, 0)),
        out_shape=jax.ShapeDtypeStruct((out_rows, d), F32),
        scratch_shapes=[pltpu.VMEM((ROW_TILE, d_ff), BF16)],
        compiler_params=_params("parallel"),
        name="proj_ffn",
    )(*args)


def _t5_bucket_np(rel):
    n = np.maximum(rel, 0)
    max_exact = REL_BUCKETS // 2
    n_f = np.maximum(n, 1).astype(np.float32)
    large = max_exact + (np.log(n_f / np.float32(max_exact))
                         / np.float32(math.log(REL_MAX_DIST / max_exact))
                         * np.float32(REL_BUCKETS - max_exact)).astype(np.int32)
    large = np.minimum(large, REL_BUCKETS - 1)
    return np.where(n < max_exact, n, large).astype(np.int32)


def _swa_bucket_index():
    i = np.arange(SWA_BLOCK)[:, None]
    c = np.arange(SWA_BLOCK)[None, :]
    out = np.full((3, SWA_BLOCK, 2 * SWA_BLOCK), -1, np.int32)
    for variant in range(3):
        pos_q = variant * SWA_BLOCK + i - PAD
        pos_k = variant * SWA_BLOCK + c - PAD - np.where(c > i, SWA_BLOCK, 0)
        rel_b = pos_q - pos_k
        assert ((rel_b >= 0) & (rel_b < SWA_BLOCK)).all()
        out[variant, :, :SWA_BLOCK] = np.where(pos_k >= N_META, _t5_bucket_np(rel_b), -1)
        rel_m = pos_q - c
        meta = (c < N_META) & (rel_m >= 0)
        out[variant, :, SWA_BLOCK:] = np.where(meta, _t5_bucket_np(rel_m), -1)
    return out


_SWA_BUCKET_INDEX = _swa_bucket_index()


def _bias_body(tbl_ref, idx_ref, o_ref):
    idx = idx_ref[0]
    group = SWA_Q_HEADS // SWA_KV_HEADS
    for head in range(SWA_Q_HEADS):
        acc = jnp.full(idx.shape, NEG_INF, F32)
        for b in range(REL_BUCKETS):
            acc = jnp.where(idx == b, tbl_ref[b, head], acc)
        rows = slice((head % group) * SWA_BLOCK, (head % group + 1) * SWA_BLOCK)
        o_ref[0, head // group, rows, :] = acc


def _swa_bias(rel_table):
    idx = jnp.asarray(_SWA_BUCKET_INDEX)
    group = SWA_Q_HEADS // SWA_KV_HEADS
    width = idx.shape[2]
    return pl.pallas_call(
        _bias_body,
        grid=(3,),
        in_specs=[pl.BlockSpec(memory_space=pltpu.SMEM),
                  pl.BlockSpec((1, SWA_BLOCK, width), lambda v: (v, 0, 0))],
        out_specs=pl.BlockSpec((1, SWA_KV_HEADS, group * SWA_BLOCK, width),
                               lambda v: (v, 0, 0, 0)),
        out_shape=jax.ShapeDtypeStruct((3, SWA_KV_HEADS, group * SWA_BLOCK, width), F32),
        compiler_params=_params("arbitrary"),
        name="swa_bias",
    )(rel_table, idx)


def _swa_stages(sink_ref, q_ref, kc_ref, kp_ref, km_ref, vc_ref, vp_ref, vm_ref, bias_ref,
                step, blocks, store):
    dh = SWA_HEAD_DIM
    blk = SWA_BLOCK
    group = SWA_Q_HEADS // SWA_KV_HEADS
    scale = dh ** -0.5
    zpad = jnp.zeros((blk - N_META, dh), BF16)
    row = lax.broadcasted_iota(jnp.int32, (group * blk, blk), 0) % blk
    col = lax.broadcasted_iota(jnp.int32, (group * blk, blk), 1)
    from_prev = col > row

    ones_col = (lax.broadcasted_iota(jnp.int32, (3 * blk, dh), 1) == 0).astype(BF16)

    units = [(j, hk) for j in range(blocks) for hk in range(SWA_KV_HEADS)]
    scores, values = [], []
    for j, hk in units:
        r = slice(j * blk, (j + 1) * blk)
        kv = slice(hk * dh, (hk + 1) * dh)
        k_prev = kp_ref[:, kv] if j == 0 else kc_ref[(j - 1) * blk:j * blk, kv]
        v_prev = vp_ref[:, kv] if j == 0 else vc_ref[(j - 1) * blk:j * blk, kv]
        k_all = jnp.concatenate([k_prev, kc_ref[r, kv], km_ref[:, kv], zpad], axis=0)
        v_all = jnp.concatenate([v_prev, vc_ref[r, kv], vm_ref[:, kv], zpad], axis=0)
        values.append(jnp.concatenate([v_all, ones_col], axis=1))
        q = jnp.concatenate([q_ref[r, (hk * group + g) * dh:(hk * group + g + 1) * dh]
                             for g in range(group)], axis=0) * scale
        scores.append(_dot_nt(q, k_all))
        yield

    probs, sink_terms = [], []
    for (j, hk), s in zip(units, scores):
        bias = bias_ref[jnp.minimum(step * blocks + j, 2), hk]
        s_band = jnp.where(from_prev, s[:, :blk], s[:, blk:2 * blk]) + bias[:, :blk]
        s_meta = s[:, 2 * blk:] + bias[:, blk:]
        sink = jnp.concatenate([jnp.full((blk, 1), sink_ref[hk * group + g], F32)
                                for g in range(group)], axis=0)
        m = jnp.maximum(jnp.max(jnp.maximum(s_band, s_meta), axis=-1, keepdims=True), sink)
        e_band = jnp.exp(s_band - m)
        e_meta = jnp.exp(s_meta - m)
        sink_terms.append(jnp.exp(sink - m))
        probs.append(jnp.concatenate([jnp.where(from_prev, e_band, 0.0),
                                      jnp.where(from_prev, 0.0, e_band), e_meta],
                                     axis=1).astype(BF16))

    for (j, hk), p, v_ext, sink_term in zip(units, probs, values, sink_terms):
        pv = _dot(p, v_ext)
        o = pv[:, :dh] / (pv[:, dh:dh + 1] + sink_term)
        for g in range(group):
            hq = hk * group + g
            store(slice(j * blk, (j + 1) * blk), slice(hq * dh, (hq + 1) * dh),
                  o[g * blk:(g + 1) * blk])
        yield


def _swa_body(sink_ref, q_ref, kc_ref, kp_ref, km_ref, vc_ref, vp_ref, vm_ref, bias_ref, o_ref,
              *, blocks):
    def store(rows, cols, value):
        o_ref[rows, cols] = value.astype(o_ref.dtype)

    _drain(_swa_stages(sink_ref, q_ref, kc_ref, kp_ref, km_ref, vc_ref, vp_ref, vm_ref,
                       bias_ref, pl.program_id(1), blocks, store))


def _swa(q, k, v, bias, sinks, batch, lp, blocks):
    rows = q.shape[0]
    nb = lp // SWA_BLOCK
    steps = nb // blocks
    meta_blocks = lp // N_META
    cur = lambda b, s: (b * steps + s, 0)
    prev = lambda b, s: (b * nb + jnp.maximum(s * blocks - 1, 0), 0)
    meta = lambda b, s: (b * meta_blocks + PAD // N_META, 0)
    kv_w = k.shape[1]
    rs = blocks * SWA_BLOCK
    return pl.pallas_call(
        functools.partial(_swa_body, blocks=blocks),
        grid=(batch, steps),
        in_specs=[pl.BlockSpec(memory_space=pltpu.SMEM),
                  pl.BlockSpec((rs, q.shape[1]), cur),
                  pl.BlockSpec((rs, kv_w), cur), pl.BlockSpec((SWA_BLOCK, kv_w), prev),
                  pl.BlockSpec((N_META, kv_w), meta),
                  pl.BlockSpec((rs, kv_w), cur), pl.BlockSpec((SWA_BLOCK, kv_w), prev),
                  pl.BlockSpec((N_META, kv_w), meta),
                  _resident(bias.shape)],
        out_specs=pl.BlockSpec((rs, q.shape[1]), cur),
        out_shape=jax.ShapeDtypeStruct((rows, q.shape[1]), BF16),
        compiler_params=_params("parallel", "arbitrary"),
        name="swa",
    )(sinks, q, k, k, k, v, v, v, bias)


def _tri_masks():
    row = lax.broadcasted_iota(jnp.int32, (CHUNK, CHUNK), 0)
    col = lax.broadcasted_iota(jnp.int32, (CHUNK, CHUNK), 1)
    return row >= col, row > col, row == col


def _deltanet_stages(act_ref, z_ref, gates_ref, alog_ref, dtb_ref, nw_ref, s_ref, chunks,
                     store):
    nh, dh = DN_HEADS, DN_HEAD_DIM
    gt = gates_ref[...]
    beta_full = jax.nn.sigmoid(gt)
    g_full = -jnp.exp(alog_ref[...]) * _softplus(gt + dtb_ref[...])
    incl, strict, diag = _tri_masks()
    eye = diag.astype(F32)
    tri = incl.astype(F32)
    gc_chunks = [_dot(tri, g_full[c * CHUNK:(c + 1) * CHUNK, :], HIGHEST)
                 for c in range(chunks)]
    items = [(c, h) for c in range(chunks) for h in range(nh)]
    yield

    neg_a, akt16, qd16, rhs16, decay = [], [], [], [], []
    for c, h in items:
        if h == 0 and c % 2 == 0 and c > 0:
            yield
        r = slice(c * CHUNK, (c + 1) * CHUNK)
        q = act_ref[r, h * dh:(h + 1) * dh].astype(F32)
        k = act_ref[r, DN_DIM + h * dh:DN_DIM + (h + 1) * dh].astype(F32)
        v = act_ref[r, 2 * DN_DIM + h * dh:2 * DN_DIM + (h + 1) * dh].astype(F32)
        q = q * lax.rsqrt(jnp.sum(q * q, axis=-1, keepdims=True) + 1e-6) * dh ** -0.5
        k = k * lax.rsqrt(jnp.sum(k * k, axis=-1, keepdims=True) + 1e-6)
        beta = jnp.broadcast_to(beta_full[r, h:h + 1], (CHUNK, dh))
        gc = jnp.broadcast_to(gc_chunks[c][:, nh + h:nh + h + 1], (CHUNK, dh))
        gc_row = jnp.transpose(gc)[:CHUNK, :]
        gamma = jnp.where(incl, jnp.exp(jnp.where(incl, gc[:, :CHUNK] - gc_row, 0.0)), 0.0)
        k_beta = k * beta
        kq = _dot_nt(jnp.concatenate([k_beta, q], axis=0).astype(BF16), k.astype(BF16))
        neg_a.append(jnp.where(strict, -(kq[:CHUNK] * gamma), 0.0))
        attn16 = jnp.where(incl, kq[CHUNK:] * gamma, 0.0).astype(BF16)
        gc_last = gc[CHUNK - 1:CHUNK, :]
        qd16.append((q * jnp.exp(gc)).astype(BF16))
        kdt16 = jnp.transpose(k * jnp.exp(gc_last - gc)).astype(BF16)
        akt16.append(jnp.concatenate([attn16, kdt16], axis=0))
        rhs16.append(jnp.concatenate([v * beta, k_beta * jnp.exp(gc)], axis=1).astype(BF16))
        decay.append(jnp.exp(gc_last))

    yield
    levels = int(math.log2(CHUNK)) - 1
    t = [eye + x for x in neg_a]
    p16 = [x.astype(BF16) for x in neg_a]
    p16 = [_dot(x, x).astype(BF16) for x in p16]
    yield
    for level in range(levels):
        if level + 1 < levels:
            tp = [_dot(jnp.concatenate([y.astype(BF16), x], axis=0), x) for y, x in zip(t, p16)]
            t = [y + r[:CHUNK] for y, r in zip(t, tp)]
            p16 = [r[CHUNK:].astype(BF16) for r in tp]
        else:
            t = [y + _dot(y.astype(BF16), x) for y, x in zip(t, p16)]
        yield
    uw = [_dot(y.astype(BF16), b) for y, b in zip(t, rhs16)]
    wq16 = [jnp.concatenate([x[:, dh:].astype(BF16), y], axis=0)
            for x, y in zip(uw, qd16)]
    yield

    nw = nw_ref[...]
    state = [s_ref[h] for h in range(nh)]
    for c in range(chunks):
        r = slice(c * CHUNK, (c + 1) * CHUNK)
        pair = [c * nh + h for h in range(nh)]
        s16 = [s.astype(BF16) for s in state]
        wqs = [_dot(wq16[i], s16[h]) for h, i in enumerate(pair)]
        yield
        vn16 = [(uw[i][:, :dh] - wqs[h][:CHUNK]).astype(BF16) for h, i in enumerate(pair)]
        ak = [_dot(akt16[i], vn16[h]) for h, i in enumerate(pair)]
        state = [state[h] * decay[i] + ak[h][CHUNK:] for h, i in enumerate(pair)]
        for h in range(nh):
            hc = slice(h * dh, (h + 1) * dh)
            out = wqs[h][CHUNK:] + ak[h][:CHUNK]
            store(r, hc, _rms(out, nw) * _silu(z_ref[r, hc].astype(F32)))
        yield
    for h in range(nh):
        s_ref[h] = state[h]


def _deltanet_body(act_ref, z_ref, gates_ref, alog_ref, dtb_ref, nw_ref, o_ref, s_ref, *,
                   chunks):
    @pl.when(pl.program_id(1) == 0)
    def _():
        s_ref[...] = jnp.zeros_like(s_ref)

    def store(rows, cols, value):
        o_ref[rows, cols] = value.astype(o_ref.dtype)

    _drain(_deltanet_stages(act_ref, z_ref, gates_ref, alog_ref, dtb_ref, nw_ref, s_ref,
                            chunks, store))


def _deltanet(act, z, gates, alog_vec, dtb_vec, nw, batch, lp, chunks):
    rows = act.shape[0]
    rs = chunks * CHUNK
    steps = lp // rs
    idx = lambda b, s: (b * steps + s, 0)
    return pl.pallas_call(
        functools.partial(_deltanet_body, chunks=chunks),
        grid=(batch, steps),
        in_specs=[pl.BlockSpec((rs, act.shape[1]), idx), pl.BlockSpec((rs, z.shape[1]), idx),
                  pl.BlockSpec((rs, gates.shape[1]), idx),
                  _resident(alog_vec.shape), _resident(dtb_vec.shape), _resident(nw.shape)],
        out_specs=pl.BlockSpec((rs, DN_DIM), idx),
        out_shape=jax.ShapeDtypeStruct((rows, DN_DIM), BF16),
        scratch_shapes=[pltpu.VMEM((DN_HEADS, DN_HEAD_DIM, DN_HEAD_DIM), F32)],
        compiler_params=_params("parallel", "arbitrary"),
        name="deltanet",
    )(act, z, gates, alog_vec, dtb_vec, nw)


def _mix_ffn_body(sink_ref, q_ref, kc_ref, kp_ref, km_ref, vc_ref, vp_ref, vm_ref, bias_ref,
                  act_ref, z_ref, gates_ref, alog_ref, dtb_ref, dnw_ref,
                  h_ref, wo_ref, nwo_ref, nwa_ref, nwb_ref, wg_ref, wu_ref, wd_ref,
                  o_ref, s_ref, mix_ref, a_ref, *, blocks, chunks, steps, last):
    t = pl.program_id(0)
    seq_step = jnp.minimum(t, last) % steps
    write_slot = t % 2
    read_slot = 1 - write_slot
    swa_width = q_ref.shape[1]
    groups = q_ref.shape[0] // GROUP_ROWS

    @pl.when(t == 0)
    def _():
        mix_ref[...] = jnp.zeros_like(mix_ref)

    @pl.when(seq_step == 0)
    def _():
        s_ref[...] = jnp.zeros_like(s_ref)

    def store_swa(rows, cols, value):
        mix_ref[write_slot, rows, cols] = value.astype(mix_ref.dtype)

    def store_dn(rows, cols, value):
        cols = slice(swa_width + cols.start, swa_width + cols.stop)
        mix_ref[write_slot, rows, cols] = value.astype(mix_ref.dtype)

    def mixer_stages():
        dn = _deltanet_stages(act_ref, z_ref, gates_ref, alog_ref, dtb_ref, dnw_ref,
                              s_ref, chunks, store_dn)
        swa = _swa_stages(sink_ref, q_ref, kc_ref, kp_ref, km_ref, vc_ref, vp_ref,
                          vm_ref, bias_ref, seq_step, blocks, store_swa)
        for _ in dn:
            yield
            if next(swa, StopIteration) is not StopIteration:
                yield
        yield from swa

    def ffn_stages():
        accs = []
        for part in range(groups):
            accs.append(_dot(mix_ref[read_slot, _group(part), :], wo_ref[...]))
            yield
        hs = [h_ref[_group(part), :] + _rms(acc, nwo_ref[...]) for part, acc in enumerate(accs)]
        outs = []
        yield from _ffn_stages(hs, nwa_ref[...], nwb_ref[...], wg_ref, wu_ref, wd_ref, a_ref,
                               outs)
        for part, out in enumerate(outs):
            o_ref[_group(part), :] = out

    _interleave(ffn_stages(), mixer_stages())


def _mix_ffn(qa, ka, va, bias, sinks, act, z, gates, alog_vec, dtb_vec, dn_nw, h, w_out, norm4,
             layer, wg, wu, wd, batch, lp, blocks, chunks):
    d = norm4.shape[-1]
    d_ff = wg.shape[-1]
    rs = blocks * SWA_BLOCK
    assert rs == chunks * CHUNK and rs % GROUP_ROWS == 0
    nb = lp // SWA_BLOCK
    steps = lp // rs
    last = batch * steps - 1
    meta_blocks = lp // N_META

    def tile(t):
        return jnp.minimum(t, last)

    cur = lambda t: (tile(t), 0)
    prev = lambda t: ((tile(t) // steps) * nb + jnp.maximum((tile(t) % steps) * blocks - 1, 0), 0)
    meta = lambda t: ((tile(t) // steps) * meta_blocks + PAD // N_META, 0)
    behind = lambda t: (jnp.maximum(t - 1, 0), 0)
    kv_w = ka.shape[1]
    return pl.pallas_call(
        functools.partial(_mix_ffn_body, blocks=blocks, chunks=chunks, steps=steps, last=last),
        grid=(last + 2,),
        in_specs=[pl.BlockSpec(memory_space=pltpu.SMEM),
                  pl.BlockSpec((rs, qa.shape[1]), cur),
                  pl.BlockSpec((rs, kv_w), cur), pl.BlockSpec((SWA_BLOCK, kv_w), prev),
                  pl.BlockSpec((N_META, kv_w), meta),
                  pl.BlockSpec((rs, kv_w), cur), pl.BlockSpec((SWA_BLOCK, kv_w), prev),
                  pl.BlockSpec((N_META, kv_w), meta),
                  _resident(bias.shape),
                  pl.BlockSpec((rs, act.shape[1]), cur), pl.BlockSpec((rs, z.shape[1]), cur),
                  pl.BlockSpec((rs, gates.shape[1]), cur),
                  _resident(alog_vec.shape), _resident(dtb_vec.shape), _resident(dn_nw.shape),
                  pl.BlockSpec((rs, d), behind), _resident(w_out.shape),
                  _pick(norm4, layer, 3), _pick(norm4, layer, 4), _pick(norm4, layer, 5),
                  _pick(wg, layer, 1), _pick(wu, layer, 1), _pick(wd, layer, 1)],
        out_specs=pl.BlockSpec((rs, d), behind),
        out_shape=jax.ShapeDtypeStruct(h.shape, F32),
        scratch_shapes=[pltpu.VMEM((DN_HEADS, DN_HEAD_DIM, DN_HEAD_DIM), F32),
                        pltpu.VMEM((2, rs, qa.shape[1] + DN_DIM), BF16),
                        pltpu.VMEM((rs, d_ff), BF16)],
        compiler_params=_params("arbitrary"),
        name="mix_ffn",
    )(sinks, qa, ka, ka, ka, va, va, va, bias, act, z, gates, alog_vec, dtb_vec, dn_nw,
      h, w_out, norm4, norm4, norm4, wg, wu, wd)


def _gla_body(q_ref, k_ref, v_ref, g_ref, gk_ref, wgu_ref, bg_ref, nw_ref, o_ref, s_ref, *,
              chunks):
    nh = GLA_HEADS
    dk = q_ref.shape[1] // nh
    dv = v_ref.shape[1] // nh

    @pl.when(pl.program_id(1) == 0)
    def _():
        s_ref[...] = jnp.zeros_like(s_ref)

    logits = _dot(gk_ref[...].astype(BF16), wgu_ref[...]) + bg_ref[...]
    glog = (jnp.minimum(logits, 0.0) - jnp.log(1.0 + jnp.exp(-jnp.abs(logits)))) / GLA_GATE_NORM
    incl, _, _ = _tri_masks()
    tri = incl.astype(F32)
    bcum = [_dot(tri, glog[c * CHUNK:(c + 1) * CHUNK, :], HIGHEST) for c in range(chunks)]

    items = [(c, h) for c in range(chunks) for h in range(nh)]
    qd16, kn16, kdt16, decay = [], [], [], []
    for c, h in items:
        r = slice(c * CHUNK, (c + 1) * CHUNK)
        kc = slice(h * dk, (h + 1) * dk)
        bc = bcum[c][:, kc]
        q = q_ref[r, kc].astype(F32) * dk ** -0.5
        k = k_ref[r, kc].astype(F32)
        b_last = bc[CHUNK - 1:CHUNK, :]
        qd16.append((q * jnp.exp(bc)).astype(BF16))
        kn16.append((k * jnp.exp(-bc)).astype(BF16))
        kdt16.append(jnp.transpose(k * jnp.exp(b_last - bc)).astype(BF16))
        decay_rows = jnp.broadcast_to(jnp.exp(b_last), (8, dk))
        decay.append(jnp.transpose(decay_rows)[:, :1])
    values = [v_ref[c * CHUNK:(c + 1) * CHUNK, h * dv:(h + 1) * dv] for c, h in items]
    attn16 = [jnp.where(incl, _dot_nt(a, b), 0.0).astype(BF16) for a, b in zip(qd16, kn16)]
    intra = [_dot(a, v) for a, v in zip(attn16, values)]
    update = [_dot(kt, v) for kt, v in zip(kdt16, values)]

    nw = nw_ref[...]
    state = [s_ref[h] for h in range(nh)]
    for c in range(chunks):
        r = slice(c * CHUNK, (c + 1) * CHUNK)
        for h in range(nh):
            i = c * nh + h
            vc = slice(h * dv, (h + 1) * dv)
            o = intra[i] + _dot(qd16[i], state[h].astype(BF16))
            state[h] = state[h] * decay[i] + update[i]
            o = _rms(o, nw) * _silu(g_ref[r, vc].astype(F32))
            o_ref[r, vc] = o.astype(o_ref.dtype)
    for h in range(nh):
        s_ref[h] = state[h]


def _gla(q, k, v, g, gk, wgu, bg, nw, batch, lp, chunks):
    rows = q.shape[0]
    rs = chunks * CHUNK
    steps = lp // rs
    idx = lambda b, s: (b * steps + s, 0)
    dk = q.shape[1] // GLA_HEADS
    dv = v.shape[1] // GLA_HEADS
    return pl.pallas_call(
        functools.partial(_gla_body, chunks=chunks),
        grid=(batch, steps),
        in_specs=[pl.BlockSpec((rs, q.shape[1]), idx), pl.BlockSpec((rs, k.shape[1]), idx),
                  pl.BlockSpec((rs, v.shape[1]), idx), pl.BlockSpec((rs, g.shape[1]), idx),
                  pl.BlockSpec((rs, gk.shape[1]), idx), _resident(wgu.shape),
                  _resident(bg.shape), _resident(nw.shape)],
        out_specs=pl.BlockSpec((rs, v.shape[1]), idx),
        out_shape=jax.ShapeDtypeStruct((rows, v.shape[1]), BF16),
        scratch_shapes=[pltpu.VMEM((GLA_HEADS, dk, dv), F32)],
        compiler_params=_params("parallel", "arbitrary"),
        name="gla",
    )(q, k, v, g, gk, wgu, bg, nw)


def _split_cols(w, main, tail):
    rest = w[:, main:]
    rest = jnp.pad(rest, ((0, 0), (0, tail - rest.shape[1])))
    return w[:, :main].astype(BF16), rest.astype(BF16)


def _per_step(n, preferred):
    for c in preferred:
        if n % c == 0:
            return c
    return 1


def kernel(x, meta_tokens, norm_w, ffn_w_gate, ffn_w_up, ffn_w_down, rel_bias_table,
           even_w_in, even_conv_w, swa_sinks, dn_a_log, dn_dt_bias, dn_norm_w, even_w_out,
           odd_w_in, gla_w_gate_up, gla_b_gate, gla_norm_w, odd_w_out):
    batch, seq, d = x.shape
    depth = norm_w.shape[0]
    lp = SWA_BLOCK + seq
    assert seq % ROW_TILE == 0 and (batch * lp) % ROW_TILE == 0
    rows = batch * lp
    chunks = _per_step(lp // CHUNK, (6, 3, 2))
    swa_blocks = _per_step(lp // SWA_BLOCK, (3, 2))

    meta_block = jnp.pad(meta_tokens.astype(x.dtype), ((PAD, 0), (0, 0)))
    norm4 = norm_w[:, :, None, :]
    wg = ffn_w_gate.astype(BF16)
    wu = ffn_w_up.astype(BF16)
    wd = ffn_w_down.astype(BF16)

    swa_q = SWA_Q_HEADS * SWA_HEAD_DIM
    swa_kv = SWA_KV_HEADS * SWA_HEAD_DIM
    bias = None
    h = x.reshape(batch * seq, d)
    for layer in range(depth):
        i = layer // 2
        embed = meta_block if layer == 0 else None
        final = layer == depth - 1
        if layer % 2 == 0:
            assert even_w_in.shape[2] == swa_q + 2 * swa_kv + 4 * DN_DIM + 2 * DN_HEADS
            widths = (swa_q, swa_kv, swa_kv, 3 * DN_DIM, DN_DIM, LANE)
            w_in, w_tail = _split_cols(even_w_in[i], sum(widths[:-1]), widths[-1])
            h, qa, ka, va, act_b, z_b, gates = _ffn_proj(
                h, embed, norm4, layer, wg, wu, wd, w_in, w_tail, widths,
                (BF16, BF16, BF16, BF16, BF16, F32), rows, lp, seq,
                conv_out=3, conv_w=even_conv_w[i])
            if bias is None:
                bias = _swa_bias(rel_bias_table)
            lane_pad = (DN_HEADS, LANE - 2 * DN_HEADS)
            alog_vec = jnp.pad(dn_a_log[i], lane_pad)[None, :]
            dtb_vec = jnp.pad(dn_dt_bias[i], lane_pad)[None, :]
            dn_nw = dn_norm_w[i][None, :]
            w_out = even_w_out[i]
            if not final and swa_blocks * SWA_BLOCK == chunks * CHUNK:
                h = _mix_ffn(qa, ka, va, bias, swa_sinks[i], act_b, z_b, gates, alog_vec,
                             dtb_vec, dn_nw, h, w_out.astype(BF16), norm4, layer, wg, wu, wd,
                             batch, lp, swa_blocks, chunks)
                continue
            o_a = _swa(qa, ka, va, bias, swa_sinks[i], batch, lp, swa_blocks)
            o_b = _deltanet(act_b, z_b, gates, alog_vec, dtb_vec, dn_nw, batch, lp, chunks)
            mixed = [o_a, o_b]
        else:
            key_dim = gla_w_gate_up.shape[2]
            val_dim = odd_w_out.shape[1]
            widths = (key_dim, key_dim, val_dim, val_dim, LANE)
            w_in, w_tail = _split_cols(odd_w_in[i], sum(widths[:-1]), widths[-1])
            h, q, k, v, g, gk = _ffn_proj(
                h, embed, norm4, layer, wg, wu, wd, w_in, w_tail, widths,
                (BF16, BF16, BF16, BF16, F32), rows, lp, seq)
            wgu = jnp.pad(gla_w_gate_up[i], ((0, LANE - GLA_GATE_RANK), (0, 0))).astype(BF16)
            o = _gla(q, k, v, g, gk, wgu, gla_b_gate[i][None, :], gla_norm_w[i][None, :],
                     batch, lp, chunks)
            mixed, w_out = [o], odd_w_out[i]
        h = _proj_ffn(mixed, h, w_out.astype(BF16), norm4, layer, wg, wu, wd, final,
                      batch, lp, seq)
    return h.reshape(batch, seq, d)
```

```python
import functools
import math

import numpy as np
import jax
import jax.numpy as jnp
from jax import lax
from jax.experimental import pallas as pl
from jax.experimental.pallas import tpu as pltpu

F32 = jnp.float32
BF16 = jnp.bfloat16
HIGHEST = lax.Precision.HIGHEST

N_META = 16
NORM_EPS = 1e-6
NEG_INF = -1e30
SWA_Q_HEADS = 8
SWA_KV_HEADS = 2
SWA_HEAD_DIM = 64
SWA_BLOCK = 128
REL_BUCKETS = 32
REL_MAX_DIST = 128
DN_HEADS = 4
DN_HEAD_DIM = 128
DN_DIM = DN_HEADS * DN_HEAD_DIM
DN_CONV = 4
GLA_HEADS = 4
GLA_GATE_RANK = 16
GLA_GATE_NORM = 16.0
CHUNK = 64

PAD = SWA_BLOCK - N_META
LANE = 128
ROW_TILE = 512
FF_TILE = 256
VMEM_LIMIT = 56 * 1024 * 1024


def _rms(x, w):
    return x * lax.rsqrt(jnp.mean(x * x, axis=-1, keepdims=True) + NORM_EPS) * w


def _silu(x):
    return x * jax.nn.sigmoid(x)


def _softplus(x):
    return jnp.maximum(x, 0.0) + jnp.log(1.0 + jnp.exp(-jnp.abs(x)))


def _dot(a, b, precision=None):
    return jnp.dot(a, b, preferred_element_type=F32, precision=precision)


def _dot_nt(a, b, precision=None):
    return lax.dot_general(a, b, (((1,), (1,)), ((), ())), preferred_element_type=F32,
                           precision=precision)


def _dot_tn(a, b, precision=None):
    return lax.dot_general(a, b, (((0,), (0,)), ((), ())), preferred_element_type=F32,
                           precision=precision)


def _resident(shape):
    nd = len(shape)
    return pl.BlockSpec(shape, lambda *_: (0,) * nd, pipeline_mode=pl.Buffered(1))


def _params(*semantics):
    return pltpu.CompilerParams(dimension_semantics=semantics, vmem_limit_bytes=VMEM_LIMIT)


SUB = ROW_TILE // SWA_BLOCK


def _pick(arr, *lead):
    rest = arr.shape[len(lead):]
    index = tuple(lead) + (0,) * len(rest)
    return pl.BlockSpec((None,) * len(lead) + rest, lambda i: index,
                        pipeline_mode=pl.Buffered(1))


def _after_prep(spec, prep):
    if spec.index_map is None:
        return spec
    index = spec.index_map
    return pl.BlockSpec(spec.block_shape, lambda i: index(jnp.maximum(i - prep, 0)),
                        pipeline_mode=spec.pipeline_mode)


def _weight_chunk_specs(wg, wd, layer, which):
    d, d_ff = wg.shape[-2:]
    last = d_ff // FF_TILE - 1
    cols = pl.BlockSpec((None, None, d, FF_TILE), lambda i: (layer, which, 0, jnp.minimum(i, last)))
    rows = pl.BlockSpec((None, None, FF_TILE, d), lambda i: (layer, which, jnp.minimum(i, last), 0))
    return [cols, cols, rows]


def _weight_scratch(wg):
    d, d_ff = wg.shape[-2:]
    chunks = d_ff // FF_TILE
    return [pltpu.VMEM((chunks, d, FF_TILE), BF16), pltpu.VMEM((chunks, d, FF_TILE), BF16),
            pltpu.VMEM((d_ff, d), BF16)]


def _prepare_weights(chunk_refs, scratch_refs):
    wg_blk, wu_blk, wd_blk = chunk_refs
    wg_s, wu_s, wd_s = scratch_refs
    prep = wg_s.shape[0]
    i = pl.program_id(0)

    @pl.when(i < prep)
    def _():
        wg_s[i] = wg_blk[...].astype(BF16)
        wu_s[i] = wu_blk[...].astype(BF16)
        wd_s[pl.ds(pl.multiple_of(i * FF_TILE, FF_TILE), FF_TILE), :] = wd_blk[...].astype(BF16)

    return prep


def _tile_specs(width):
    return [pl.BlockSpec((ROW_TILE, width), lambda i: (i, 0))]


def _real_specs(width, lp, seq):
    tiles = seq // ROW_TILE
    blocks = lp // SWA_BLOCK

    def spec(j):
        return pl.BlockSpec((SWA_BLOCK, width),
                            lambda i: ((i // tiles) * blocks + 1 + (i % tiles) * SUB + j, 0))
    return [spec(j) for j in range(SUB)]


def _embed_specs(width, lp, seq):
    blocks = lp // SWA_BLOCK
    x_blocks = seq // SWA_BLOCK

    def spec(j):
        def index(i):
            blk = i * SUB + j
            return ((blk // blocks) * x_blocks + jnp.maximum(blk % blocks - 1, 0), 0)
        return pl.BlockSpec((SWA_BLOCK, width), index)
    return [spec(j) for j in range(SUB)]


ROW_SPLIT = 4
GROUP_ROWS = ROW_TILE // ROW_SPLIT


def _group(part):
    return slice(part * GROUP_ROWS, (part + 1) * GROUP_ROWS)


def _load_rows(refs, part):
    if len(refs) == 1:
        return refs[0][_group(part), :]
    per = len(refs) // ROW_SPLIT
    return jnp.concatenate([r[...] for r in refs[part * per:(part + 1) * per]], axis=0)


def _drain(stages):
    for _ in stages:
        pass


def _interleave(main, side):
    for _ in main:
        next(side, None)
    _drain(side)


def _ffn_stages(xs, nwa, nwb, wg_ref, wu_ref, wd_ref, a_ref, outs):
    hns = [_rms(x, nwa).astype(BF16) for x in xs]
    for part, hn in enumerate(hns):
        for c in range(wg_ref.shape[0]):
            cols = slice(c * FF_TILE, (c + 1) * FF_TILE)
            g = _dot(hn, wg_ref[c])
            u = _dot(hn, wu_ref[c])
            a_ref[_group(part), cols] = (_silu(g) * u).astype(BF16)
            yield
    fs = []
    for part in range(len(xs)):
        fs.append(_dot(a_ref[_group(part), :], wd_ref[...]))
        yield
    outs.extend(x + 0.5 * _rms(f, nwb) for x, f in zip(xs, fs))


def _ffn_core(xs, nwa, nwb, wg_ref, wu_ref, wd_ref, a_ref):
    outs = []
    _drain(_ffn_stages(xs, nwa, nwb, wg_ref, wu_ref, wd_ref, a_ref, outs))
    return outs


CONV_HALO = 8


def _ffn_proj_body(*refs, n_rows, embed_blocks, n_out, conv_out, lp):
    lead = n_rows + bool(embed_blocks) + (conv_out is not None) + 2
    weights = refs[-3:]
    prep = _prepare_weights(refs[lead:lead + 3], weights)
    step = pl.program_id(0) - prep
    refs = refs[:lead] + weights + refs[lead + 3:-3]

    @pl.when(step >= 0)
    def _():
        _ffn_proj_step(step, *refs, n_rows=n_rows, embed_blocks=embed_blocks, n_out=n_out,
                       conv_out=conv_out, lp=lp)


def _ffn_proj_step(step, *refs, n_rows, embed_blocks, n_out, conv_out, lp):
    row_refs = refs[:n_rows]
    refs = refs[n_rows:]
    if embed_blocks:
        meta_ref, refs = refs[0], refs[1:]
    if conv_out is not None:
        convw_ref, refs = refs[0], refs[1:]
        xpad_ref = refs[10 + n_out]

        @pl.when(step == 0)
        def _():
            xpad_ref[0:CONV_HALO, :] = jnp.zeros((CONV_HALO, xpad_ref.shape[1]), F32)

    nwa_ref, nwb_ref, wg_ref, wu_ref, wd_ref, nwm_ref, win_ref, wtail_ref = refs[:8]
    h_ref = refs[8]
    out_refs = refs[9:9 + n_out]
    a_ref = refs[9 + n_out]
    offsets = np.cumsum([0] + [o.shape[1] for o in out_refs])
    xs = []
    for part in range(ROW_SPLIT):
        if embed_blocks:
            per = n_rows // ROW_SPLIT
            first = step * SUB + part * per
            xs.append(jnp.concatenate(
                [jnp.where((first + j) % embed_blocks == 0, meta_ref[...], r[...])
                 for j, r in enumerate(row_refs[part * per:(part + 1) * per])], axis=0))
        else:
            xs.append(_load_rows(row_refs, part))
    hs = _ffn_core(xs, nwa_ref[...], nwb_ref[...], wg_ref, wu_ref, wd_ref, a_ref)
    hns = [_rms(h, nwm_ref[...]).astype(BF16) for h in hs]
    for part, (h, hn) in enumerate(zip(hs, hns)):
        rows = _group(part)
        h_ref[rows, :] = h
        for i, o_ref in enumerate(out_refs):
            if i + 1 < n_out:
                p = _dot(hn, win_ref[:, offsets[i]:offsets[i + 1]])
            else:
                p = _dot(hn, wtail_ref[...])
            if i == conv_out:
                xpad_ref[CONV_HALO + part * GROUP_ROWS:CONV_HALO + (part + 1) * GROUP_ROWS,
                         :] = p
            else:
                o_ref[rows, :] = p.astype(o_ref.dtype)

    if conv_out is not None:
        taps = convw_ref.shape[0]
        cw = convw_ref[...]
        y = cw[taps - 1:taps, :] * xpad_ref[CONV_HALO:, :]
        for j in range(taps - 1):
            lo = CONV_HALO - (taps - 1 - j)
            y = y + cw[j:j + 1, :] * xpad_ref[lo:lo + ROW_TILE, :]
        tail = xpad_ref[ROW_TILE:, :]
        row = (step * ROW_TILE) % lp + lax.broadcasted_iota(jnp.int32, (ROW_TILE, 1), 0)
        row = jnp.where(row >= lp, row - lp, row)
        out_refs[conv_out][...] = jnp.where(row >= PAD, _silu(y), 0.0).astype(
            out_refs[conv_out].dtype)
        xpad_ref[0:CONV_HALO, :] = tail


def _ffn_proj(src, meta_block, norm4, layer, wg, wu, wd, w_in, w_tail, widths, dtypes, rows, lp,
              seq, conv_out=None, conv_w=None):
    d = norm4.shape[-1]
    d_ff = wg.shape[-1]
    assert sum(widths[:-1]) == w_in.shape[1] and widths[-1] == w_tail.shape[1]
    assert lp > ROW_TILE
    if meta_block is None:
        row_specs, extra, extra_specs, embed_blocks = _tile_specs(d), [], [], 0
    else:
        row_specs = _embed_specs(d, lp, seq)
        extra, extra_specs, embed_blocks = [meta_block], [_resident(meta_block.shape)], (
            lp // SWA_BLOCK)
    scratch = [pltpu.VMEM((ROW_TILE, d_ff), BF16)]
    if conv_out is not None:
        extra, extra_specs = extra + [conv_w], extra_specs + [_resident(conv_w.shape)]
        scratch.append(pltpu.VMEM((CONV_HALO + ROW_TILE, widths[conv_out]), F32))
    tile = lambda n: pl.BlockSpec((ROW_TILE, n), lambda i: (i, 0))
    prep = d_ff // FF_TILE
    shift = lambda specs: [_after_prep(s, prep) for s in specs]
    return pl.pallas_call(
        functools.partial(_ffn_proj_body, n_rows=len(row_specs), embed_blocks=embed_blocks,
                          n_out=len(widths), conv_out=conv_out, lp=lp),
        grid=(prep + rows // ROW_TILE,),
        in_specs=shift(row_specs + extra_specs
                       + [_pick(norm4, layer, 0), _pick(norm4, layer, 1)])
        + _weight_chunk_specs(wg, wd, layer, 0)
        + shift([_pick(norm4, layer, 2), _resident(w_in.shape), _resident(w_tail.shape)]),
        out_specs=shift([tile(d)] + [tile(n) for n in widths]),
        out_shape=[jax.ShapeDtypeStruct((rows, d), F32)]
        + [jax.ShapeDtypeStruct((rows, n), dt) for n, dt in zip(widths, dtypes)],
        scratch_shapes=scratch + _weight_scratch(wg),
        compiler_params=_params("arbitrary"),
        name="ffn_proj",
    )(*([src] * len(row_specs)), *extra, norm4, norm4, wg, wu, wd, norm4, w_in, w_tail)


def _proj_ffn_body(*refs, n_rows, n_x):
    x_groups = [refs[g * n_rows:(g + 1) * n_rows] for g in range(n_x)]
    refs = refs[n_x * n_rows:]
    h_refs = refs[:n_rows]
    (wo_ref, nwo_ref, nwa_ref, nwb_ref, wg_blk, wu_blk, wd_blk, o_ref, a_ref,
     wg_ref, wu_ref, wd_ref) = refs[n_rows:]
    prep = _prepare_weights((wg_blk, wu_blk, wd_blk), (wg_ref, wu_ref, wd_ref))

    @pl.when(pl.program_id(0) >= prep)
    def _():
        accs = []
        for part in range(ROW_SPLIT):
            acc = None
            off = 0
            for group in x_groups:
                xs = _load_rows(group, part)
                k = xs.shape[1]
                p = _dot(xs, wo_ref[off:off + k, :])
                acc = p if acc is None else acc + p
                off += k
            accs.append(acc)
        hs = [_load_rows(h_refs, part) + _rms(acc, nwo_ref[...])
              for part, acc in enumerate(accs)]
        outs = _ffn_core(hs, nwa_ref[...], nwb_ref[...], wg_ref, wu_ref, wd_ref, a_ref)
        for part, out in enumerate(outs):
            o_ref[_group(part), :] = out


def _proj_ffn(xs, h, w_out, norm4, layer, wg, wu, wd, final, batch, lp, seq):
    d = norm4.shape[-1]
    d_ff = wg.shape[-1]
    if final:
        specs = lambda width: _real_specs(width, lp, seq)
        out_rows = batch * seq
    else:
        specs = _tile_specs
        out_rows = batch * lp
    h_specs = specs(d)
    n_rows = len(h_specs)
    in_specs, args = [], []
    for x in xs:
        in_specs += specs(x.shape[1])
        args += [x] * n_rows
    in_specs += h_specs + [_resident(w_out.shape), _pick(norm4, layer, 3),
                           _pick(norm4, layer, 4), _pick(norm4, layer, 5)]
    prep = d_ff // FF_TILE
    in_specs = [_after_prep(s, prep) for s in in_specs] + _weight_chunk_specs(wg, wd, layer, 1)
    args += [h] * n_rows + [w_out, norm4, norm4, norm4, wg, wu, wd]
    return pl.pallas_call(
        functools.partial(_proj_ffn_body, n_rows=n_rows, n_x=len(xs)),
        grid=(prep + out_rows // ROW_TILE,),
        in_specs=in_specs,
        out_specs=_after_prep(pl.BlockSpec((ROW_TILE, d), lambda i: (i, 0)), prep),
        out_shape=jax.ShapeDtypeStruct((out_rows, d), F32),
        scratch_shapes=[pltpu.VMEM((ROW_TILE, d_ff), BF16)] + _weight_scratch(wg),
        compiler_params=_params("arbitrary"),
        name="proj_ffn",
    )(*args)


def _t5_bucket_np(rel):
    n = np.maximum(rel, 0)
    max_exact = REL_BUCKETS // 2
    n_f = np.maximum(n, 1).astype(np.float32)
    large = max_exact + (np.log(n_f / np.float32(max_exact))
                         / np.float32(math.log(REL_MAX_DIST / max_exact))
                         * np.float32(REL_BUCKETS - max_exact)).astype(np.int32)
    large = np.minimum(large, REL_BUCKETS - 1)
    return np.where(n < max_exact, n, large).astype(np.int32)


def _swa_bucket_index():
    i = np.arange(SWA_BLOCK)[:, None]
    c = np.arange(SWA_BLOCK)[None, :]
    out = np.full((3, SWA_BLOCK, 2 * SWA_BLOCK), -1, np.int32)
    for variant in range(3):
        pos_q = variant * SWA_BLOCK + i - PAD
        pos_k = variant * SWA_BLOCK + c - PAD - np.where(c > i, SWA_BLOCK, 0)
        rel_b = pos_q - pos_k
        assert ((rel_b >= 0) & (rel_b < SWA_BLOCK)).all()
        out[variant, :, :SWA_BLOCK] = np.where(pos_k >= N_META, _t5_bucket_np(rel_b), -1)
        rel_m = pos_q - c
        meta = (c < N_META) & (rel_m >= 0)
        out[variant, :, SWA_BLOCK:] = np.where(meta, _t5_bucket_np(rel_m), -1)
    return out


_SWA_BUCKET_INDEX = _swa_bucket_index()


def _bias_body(tbl_ref, idx_ref, o_ref):
    idx = idx_ref[0]
    group = SWA_Q_HEADS // SWA_KV_HEADS
    for head in range(SWA_Q_HEADS):
        acc = jnp.full(idx.shape, NEG_INF, F32)
        for b in range(REL_BUCKETS):
            acc = jnp.where(idx == b, tbl_ref[b, head], acc)
        rows = slice((head % group) * SWA_BLOCK, (head % group + 1) * SWA_BLOCK)
        o_ref[0, head // group, rows, :] = acc


def _swa_bias(rel_table):
    idx = jnp.asarray(_SWA_BUCKET_INDEX)
    group = SWA_Q_HEADS // SWA_KV_HEADS
    width = idx.shape[2]
    return pl.pallas_call(
        _bias_body,
        grid=(3,),
        in_specs=[pl.BlockSpec(memory_space=pltpu.SMEM),
                  pl.BlockSpec((1, SWA_BLOCK, width), lambda v: (v, 0, 0))],
        out_specs=pl.BlockSpec((1, SWA_KV_HEADS, group * SWA_BLOCK, width),
                               lambda v: (v, 0, 0, 0)),
        out_shape=jax.ShapeDtypeStruct((3, SWA_KV_HEADS, group * SWA_BLOCK, width), F32),
        compiler_params=_params("arbitrary"),
        name="swa_bias",
    )(rel_table, idx)


def _swa_stages(sink_ref, q_ref, kc_ref, kp_ref, km_ref, vc_ref, vp_ref, vm_ref, bias_ref,
                step, blocks, store):
    dh = SWA_HEAD_DIM
    blk = SWA_BLOCK
    group = SWA_Q_HEADS // SWA_KV_HEADS
    scale = dh ** -0.5
    zpad = jnp.zeros((blk - N_META, dh), BF16)
    row = lax.broadcasted_iota(jnp.int32, (group * blk, blk), 0) % blk
    col = lax.broadcasted_iota(jnp.int32, (group * blk, blk), 1)
    from_prev = col > row

    ones_col = (lax.broadcasted_iota(jnp.int32, (3 * blk, dh), 1) == 0).astype(BF16)

    units = [(j, hk) for j in range(blocks) for hk in range(SWA_KV_HEADS)]
    scores, values = [], []
    for j, hk in units:
        r = slice(j * blk, (j + 1) * blk)
        kv = slice(hk * dh, (hk + 1) * dh)
        k_prev = kp_ref[:, kv] if j == 0 else kc_ref[(j - 1) * blk:j * blk, kv]
        v_prev = vp_ref[:, kv] if j == 0 else vc_ref[(j - 1) * blk:j * blk, kv]
        k_all = jnp.concatenate([k_prev, kc_ref[r, kv], km_ref[:, kv], zpad], axis=0)
        v_all = jnp.concatenate([v_prev, vc_ref[r, kv], vm_ref[:, kv], zpad], axis=0)
        values.append(jnp.concatenate([v_all, ones_col], axis=1))
        q = jnp.concatenate([q_ref[r, (hk * group + g) * dh:(hk * group + g + 1) * dh]
                             for g in range(group)], axis=0) * scale
        scores.append(_dot_nt(q, k_all))
        yield

    probs, sink_terms = [], []
    for (j, hk), s in zip(units, scores):
        bias = bias_ref[jnp.minimum(step * blocks + j, 2), hk]
        s_band = jnp.where(from_prev, s[:, :blk], s[:, blk:2 * blk]) + bias[:, :blk]
        s_meta = s[:, 2 * blk:] + bias[:, blk:]
        sink = jnp.concatenate([jnp.full((blk, 1), sink_ref[hk * group + g], F32)
                                for g in range(group)], axis=0)
        m = jnp.maximum(jnp.max(jnp.maximum(s_band, s_meta), axis=-1, keepdims=True), sink)
        e_band = jnp.exp(s_band - m)
        e_meta = jnp.exp(s_meta - m)
        sink_terms.append(jnp.exp(sink - m))
        probs.append(jnp.concatenate([jnp.where(from_prev, e_band, 0.0),
                                      jnp.where(from_prev, 0.0, e_band), e_meta],
                                     axis=1).astype(BF16))

    for (j, hk), p, v_ext, sink_term in zip(units, probs, values, sink_terms):
        pv = _dot(p, v_ext)
        o = pv[:, :dh] / (pv[:, dh:dh + 1] + sink_term)
        for g in range(group):
            hq = hk * group + g
            store(slice(j * blk, (j + 1) * blk), slice(hq * dh, (hq + 1) * dh),
                  o[g * blk:(g + 1) * blk])
        yield


def _swa_body(sink_ref, q_ref, kc_ref, kp_ref, km_ref, vc_ref, vp_ref, vm_ref, bias_ref, o_ref,
              *, blocks):
    def store(rows, cols, value):
        o_ref[rows, cols] = value.astype(o_ref.dtype)

    _drain(_swa_stages(sink_ref, q_ref, kc_ref, kp_ref, km_ref, vc_ref, vp_ref, vm_ref,
                       bias_ref, pl.program_id(1), blocks, store))


def _swa(q, k, v, bias, sinks, batch, lp, blocks):
    rows = q.shape[0]
    nb = lp // SWA_BLOCK
    steps = nb // blocks
    meta_blocks = lp // N_META
    cur = lambda b, s: (b * steps + s, 0)
    prev = lambda b, s: (b * nb + jnp.maximum(s * blocks - 1, 0), 0)
    meta = lambda b, s: (b * meta_blocks + PAD // N_META, 0)
    kv_w = k.shape[1]
    rs = blocks * SWA_BLOCK
    return pl.pallas_call(
        functools.partial(_swa_body, blocks=blocks),
        grid=(batch, steps),
        in_specs=[pl.BlockSpec(memory_space=pltpu.SMEM),
                  pl.BlockSpec((rs, q.shape[1]), cur),
                  pl.BlockSpec((rs, kv_w), cur), pl.BlockSpec((SWA_BLOCK, kv_w), prev),
                  pl.BlockSpec((N_META, kv_w), meta),
                  pl.BlockSpec((rs, kv_w), cur), pl.BlockSpec((SWA_BLOCK, kv_w), prev),
                  pl.BlockSpec((N_META, kv_w), meta),
                  _resident(bias.shape)],
        out_specs=pl.BlockSpec((rs, q.shape[1]), cur),
        out_shape=jax.ShapeDtypeStruct((rows, q.shape[1]), BF16),
        compiler_params=_params("parallel", "arbitrary"),
        name="swa",
    )(sinks, q, k, k, k, v, v, v, bias)


def _tri_masks():
    row = lax.broadcasted_iota(jnp.int32, (CHUNK, CHUNK), 0)
    col = lax.broadcasted_iota(jnp.int32, (CHUNK, CHUNK), 1)
    return row >= col, row > col, row == col


def _deltanet_stages(act_ref, z_ref, gates_ref, alog_ref, dtb_ref, nw_ref, s_ref, chunks,
                     store):
    nh, dh = DN_HEADS, DN_HEAD_DIM
    gt = gates_ref[...]
    beta_full = jax.nn.sigmoid(gt)
    g_full = -jnp.exp(alog_ref[...]) * _softplus(gt + dtb_ref[...])
    incl, strict, diag = _tri_masks()
    eye = diag.astype(F32)
    tri = incl.astype(F32)
    gc_chunks = [_dot(tri, g_full[c * CHUNK:(c + 1) * CHUNK, :], HIGHEST)
                 for c in range(chunks)]
    items = [(c, h) for c in range(chunks) for h in range(nh)]
    yield

    neg_a, akt16, qd16, rhs16, decay = [], [], [], [], []
    for c, h in items:
        if h == 0 and c % 2 == 0 and c > 0:
            yield
        r = slice(c * CHUNK, (c + 1) * CHUNK)
        q = act_ref[r, h * dh:(h + 1) * dh].astype(F32)
        k = act_ref[r, DN_DIM + h * dh:DN_DIM + (h + 1) * dh].astype(F32)
        v = act_ref[r, 2 * DN_DIM + h * dh:2 * DN_DIM + (h + 1) * dh].astype(F32)
        q = q * lax.rsqrt(jnp.sum(q * q, axis=-1, keepdims=True) + 1e-6) * dh ** -0.5
        k = k * lax.rsqrt(jnp.sum(k * k, axis=-1, keepdims=True) + 1e-6)
        beta = jnp.broadcast_to(beta_full[r, h:h + 1], (CHUNK, dh))
        gc = jnp.broadcast_to(gc_chunks[c][:, nh + h:nh + h + 1], (CHUNK, dh))
        gc_row = jnp.transpose(gc)[:CHUNK, :]
        gamma = jnp.where(incl, jnp.exp(jnp.where(incl, gc[:, :CHUNK] - gc_row, 0.0)), 0.0)
        k_beta = k * beta
        kq = _dot_nt(jnp.concatenate([k_beta, q], axis=0).astype(BF16), k.astype(BF16))
        neg_a.append(jnp.where(strict, -(kq[:CHUNK] * gamma), 0.0))
        attn16 = jnp.where(incl, kq[CHUNK:] * gamma, 0.0).astype(BF16)
        gc_last = gc[CHUNK - 1:CHUNK, :]
        qd16.append((q * jnp.exp(gc)).astype(BF16))
        kdt16 = jnp.transpose(k * jnp.exp(gc_last - gc)).astype(BF16)
        akt16.append(jnp.concatenate([attn16, kdt16], axis=0))
        rhs16.append(jnp.concatenate([v * beta, k_beta * jnp.exp(gc)], axis=1).astype(BF16))
        decay.append(jnp.exp(gc_last))

    yield
    levels = int(math.log2(CHUNK)) - 1
    t = [eye + x for x in neg_a]
    p16 = [x.astype(BF16) for x in neg_a]
    p16 = [_dot(x, x).astype(BF16) for x in p16]
    yield
    for level in range(levels):
        if level + 1 < levels:
            tp = [_dot(jnp.concatenate([y.astype(BF16), x], axis=0), x) for y, x in zip(t, p16)]
            t = [y + r[:CHUNK] for y, r in zip(t, tp)]
            p16 = [r[CHUNK:].astype(BF16) for r in tp]
        else:
            t = [y + _dot(y.astype(BF16), x) for y, x in zip(t, p16)]
        yield
    uw = [_dot(y.astype(BF16), b) for y, b in zip(t, rhs16)]
    wq16 = [jnp.concatenate([x[:, dh:].astype(BF16), y], axis=0)
            for x, y in zip(uw, qd16)]
    yield

    nw = nw_ref[...]
    state = [s_ref[h] for h in range(nh)]
    for c in range(chunks):
        r = slice(c * CHUNK, (c + 1) * CHUNK)
        pair = [c * nh + h for h in range(nh)]
        s16 = [s.astype(BF16) for s in state]
        wqs = [_dot(wq16[i], s16[h]) for h, i in enumerate(pair)]
        yield
        vn16 = [(uw[i][:, :dh] - wqs[h][:CHUNK]).astype(BF16) for h, i in enumerate(pair)]
        ak = [_dot(akt16[i], vn16[h]) for h, i in enumerate(pair)]
        state = [state[h] * decay[i] + ak[h][CHUNK:] for h, i in enumerate(pair)]
        for h in range(nh):
            hc = slice(h * dh, (h + 1) * dh)
            out = wqs[h][CHUNK:] + ak[h][:CHUNK]
            store(r, hc, _rms(out, nw) * _silu(z_ref[r, hc].astype(F32)))
        yield
    for h in range(nh):
        s_ref[h] = state[h]


def _deltanet_body(act_ref, z_ref, gates_ref, alog_ref, dtb_ref, nw_ref, o_ref, s_ref, *,
                   chunks):
    @pl.when(pl.program_id(1) == 0)
    def _():
        s_ref[...] = jnp.zeros_like(s_ref)

    def store(rows, cols, value):
        o_ref[rows, cols] = value.astype(o_ref.dtype)

    _drain(_deltanet_stages(act_ref, z_ref, gates_ref, alog_ref, dtb_ref, nw_ref, s_ref,
                            chunks, store))


def _deltanet(act, z, gates, alog_vec, dtb_vec, nw, batch, lp, chunks):
    rows = act.shape[0]
    rs = chunks * CHUNK
    steps = lp // rs
    idx = lambda b, s: (b * steps + s, 0)
    return pl.pallas_call(
        functools.partial(_deltanet_body, chunks=chunks),
        grid=(batch, steps),
        in_specs=[pl.BlockSpec((rs, act.shape[1]), idx), pl.BlockSpec((rs, z.shape[1]), idx),
                  pl.BlockSpec((rs, gates.shape[1]), idx),
                  _resident(alog_vec.shape), _resident(dtb_vec.shape), _resident(nw.shape)],
        out_specs=pl.BlockSpec((rs, DN_DIM), idx),
        out_shape=jax.ShapeDtypeStruct((rows, DN_DIM), BF16),
        scratch_shapes=[pltpu.VMEM((DN_HEADS, DN_HEAD_DIM, DN_HEAD_DIM), F32)],
        compiler_params=_params("parallel", "arbitrary"),
        name="deltanet",
    )(act, z, gates, alog_vec, dtb_vec, nw)


def _mix_ffn_body(*refs, blocks, chunks, steps, last):
    *refs, wg_s, wu_s, wd_s = refs
    prep = _prepare_weights(refs[20:23], (wg_s, wu_s, wd_s))
    t = pl.program_id(0) - prep
    refs = refs[:20] + [wg_s, wu_s, wd_s] + refs[23:]

    @pl.when(t >= 0)
    def _():
        _mix_ffn_step(t, *refs, blocks=blocks, chunks=chunks, steps=steps, last=last)


def _mix_ffn_step(t, sink_ref, q_ref, kc_ref, kp_ref, km_ref, vc_ref, vp_ref, vm_ref, bias_ref,
                  act_ref, z_ref, gates_ref, alog_ref, dtb_ref, dnw_ref,
                  h_ref, wo_ref, nwo_ref, nwa_ref, nwb_ref, wg_ref, wu_ref, wd_ref,
                  o_ref, s_ref, mix_ref, a_ref, *, blocks, chunks, steps, last):
    seq_step = jnp.minimum(t, last) % steps
    write_slot = t % 2
    read_slot = 1 - write_slot
    swa_width = q_ref.shape[1]
    groups = q_ref.shape[0] // GROUP_ROWS

    @pl.when(t == 0)
    def _():
        mix_ref[...] = jnp.zeros_like(mix_ref)

    @pl.when(seq_step == 0)
    def _():
        s_ref[...] = jnp.zeros_like(s_ref)

    def store_swa(rows, cols, value):
        mix_ref[write_slot, rows, cols] = value.astype(mix_ref.dtype)

    def store_dn(rows, cols, value):
        cols = slice(swa_width + cols.start, swa_width + cols.stop)
        mix_ref[write_slot, rows, cols] = value.astype(mix_ref.dtype)

    def mixer_stages():
        dn = _deltanet_stages(act_ref, z_ref, gates_ref, alog_ref, dtb_ref, dnw_ref,
                              s_ref, chunks, store_dn)
        swa = _swa_stages(sink_ref, q_ref, kc_ref, kp_ref, km_ref, vc_ref, vp_ref,
                          vm_ref, bias_ref, seq_step, blocks, store_swa)
        for _ in dn:
            yield
            if next(swa, StopIteration) is not StopIteration:
                yield
        yield from swa

    def ffn_stages():
        accs = []
        for part in range(groups):
            accs.append(_dot(mix_ref[read_slot, _group(part), :], wo_ref[...]))
            yield
        hs = [h_ref[_group(part), :] + _rms(acc, nwo_ref[...]) for part, acc in enumerate(accs)]
        outs = []
        yield from _ffn_stages(hs, nwa_ref[...], nwb_ref[...], wg_ref, wu_ref, wd_ref, a_ref,
                               outs)
        for part, out in enumerate(outs):
            o_ref[_group(part), :] = out

    _interleave(ffn_stages(), mixer_stages())


def _mix_ffn(qa, ka, va, bias, sinks, act, z, gates, alog_vec, dtb_vec, dn_nw, h, w_out, norm4,
             layer, wg, wu, wd, batch, lp, blocks, chunks):
    d = norm4.shape[-1]
    d_ff = wg.shape[-1]
    rs = blocks * SWA_BLOCK
    assert rs == chunks * CHUNK and rs % GROUP_ROWS == 0
    nb = lp // SWA_BLOCK
    steps = lp // rs
    last = batch * steps - 1
    meta_blocks = lp // N_META

    def tile(t):
        return jnp.minimum(t, last)

    cur = lambda t: (tile(t), 0)
    prev = lambda t: ((tile(t) // steps) * nb + jnp.maximum((tile(t) % steps) * blocks - 1, 0), 0)
    meta = lambda t: ((tile(t) // steps) * meta_blocks + PAD // N_META, 0)
    behind = lambda t: (jnp.maximum(t - 1, 0), 0)
    kv_w = ka.shape[1]
    prep = d_ff // FF_TILE
    in_specs = [pl.BlockSpec(memory_space=pltpu.SMEM),
                pl.BlockSpec((rs, qa.shape[1]), cur),
                pl.BlockSpec((rs, kv_w), cur), pl.BlockSpec((SWA_BLOCK, kv_w), prev),
                pl.BlockSpec((N_META, kv_w), meta),
                pl.BlockSpec((rs, kv_w), cur), pl.BlockSpec((SWA_BLOCK, kv_w), prev),
                pl.BlockSpec((N_META, kv_w), meta),
                _resident(bias.shape),
                pl.BlockSpec((rs, act.shape[1]), cur), pl.BlockSpec((rs, z.shape[1]), cur),
                pl.BlockSpec((rs, gates.shape[1]), cur),
                _resident(alog_vec.shape), _resident(dtb_vec.shape), _resident(dn_nw.shape),
                pl.BlockSpec((rs, d), behind), _resident(w_out.shape),
                _pick(norm4, layer, 3), _pick(norm4, layer, 4), _pick(norm4, layer, 5)]
    return pl.pallas_call(
        functools.partial(_mix_ffn_body, blocks=blocks, chunks=chunks, steps=steps, last=last),
        grid=(prep + last + 2,),
        in_specs=[_after_prep(s, prep) for s in in_specs]
        + _weight_chunk_specs(wg, wd, layer, 1),
        out_specs=_after_prep(pl.BlockSpec((rs, d), behind), prep),
        out_shape=jax.ShapeDtypeStruct(h.shape, F32),
        scratch_shapes=[pltpu.VMEM((DN_HEADS, DN_HEAD_DIM, DN_HEAD_DIM), F32),
                        pltpu.VMEM((2, rs, qa.shape[1] + DN_DIM), BF16),
                        pltpu.VMEM((rs, d_ff), BF16)] + _weight_scratch(wg),
        compiler_params=_params("arbitrary"),
        name="mix_ffn",
    )(sinks, qa, ka, ka, ka, va, va, va, bias, act, z, gates, alog_vec, dtb_vec, dn_nw,
      h, w_out, norm4, norm4, norm4, wg, wu, wd)


def _gla_body(q_ref, k_ref, v_ref, g_ref, gk_ref, wgu_ref, bg_ref, nw_ref, o_ref, s_ref, *,
              chunks):
    nh = GLA_HEADS
    dk = q_ref.shape[1] // nh
    dv = v_ref.shape[1] // nh

    @pl.when(pl.program_id(1) == 0)
    def _():
        s_ref[...] = jnp.zeros_like(s_ref)

    logits = _dot(gk_ref[...].astype(BF16), wgu_ref[...]) + bg_ref[...]
    glog = (jnp.minimum(logits, 0.0) - jnp.log(1.0 + jnp.exp(-jnp.abs(logits)))) / GLA_GATE_NORM
    incl, _, _ = _tri_masks()
    tri = incl.astype(F32)
    bcum = [_dot(tri, glog[c * CHUNK:(c + 1) * CHUNK, :], HIGHEST) for c in range(chunks)]

    items = [(c, h) for c in range(chunks) for h in range(nh)]
    qd16, kn16, kdt16, decay = [], [], [], []
    for c, h in items:
        r = slice(c * CHUNK, (c + 1) * CHUNK)
        kc = slice(h * dk, (h + 1) * dk)
        bc = bcum[c][:, kc]
        q = q_ref[r, kc].astype(F32) * dk ** -0.5
        k = k_ref[r, kc].astype(F32)
        b_last = bc[CHUNK - 1:CHUNK, :]
        qd16.append((q * jnp.exp(bc)).astype(BF16))
        kn16.append((k * jnp.exp(-bc)).astype(BF16))
        kdt16.append(jnp.transpose(k * jnp.exp(b_last - bc)).astype(BF16))
        decay_rows = jnp.broadcast_to(jnp.exp(b_last), (8, dk))
        decay.append(jnp.transpose(decay_rows)[:, :1])
    values = [v_ref[c * CHUNK:(c + 1) * CHUNK, h * dv:(h + 1) * dv] for c, h in items]
    attn16 = [jnp.where(incl, _dot_nt(a, b), 0.0).astype(BF16) for a, b in zip(qd16, kn16)]
    intra = [_dot(a, v) for a, v in zip(attn16, values)]
    update = [_dot(kt, v) for kt, v in zip(kdt16, values)]

    nw = nw_ref[...]
    state = [s_ref[h] for h in range(nh)]
    for c in range(chunks):
        r = slice(c * CHUNK, (c + 1) * CHUNK)
        for h in range(nh):
            i = c * nh + h
            vc = slice(h * dv, (h + 1) * dv)
            o = intra[i] + _dot(qd16[i], state[h].astype(BF16))
            state[h] = state[h] * decay[i] + update[i]
            o = _rms(o, nw) * _silu(g_ref[r, vc].astype(F32))
            o_ref[r, vc] = o.astype(o_ref.dtype)
    for h in range(nh):
        s_ref[h] = state[h]


def _gla(q, k, v, g, gk, wgu, bg, nw, batch, lp, chunks):
    rows = q.shape[0]
    rs = chunks * CHUNK
    steps = lp // rs
    idx = lambda b, s: (b * steps + s, 0)
    dk = q.shape[1] // GLA_HEADS
    dv = v.shape[1] // GLA_HEADS
    return pl.pallas_call(
        functools.partial(_gla_body, chunks=chunks),
        grid=(batch, steps),
        in_specs=[pl.BlockSpec((rs, q.shape[1]), idx), pl.BlockSpec((rs, k.shape[1]), idx),
                  pl.BlockSpec((rs, v.shape[1]), idx), pl.BlockSpec((rs, g.shape[1]), idx),
                  pl.BlockSpec((rs, gk.shape[1]), idx), _resident(wgu.shape),
                  _resident(bg.shape), _resident(nw.shape)],
        out_specs=pl.BlockSpec((rs, v.shape[1]), idx),
        out_shape=jax.ShapeDtypeStruct((rows, v.shape[1]), BF16),
        scratch_shapes=[pltpu.VMEM((GLA_HEADS, dk, dv), F32)],
        compiler_params=_params("parallel", "arbitrary"),
        name="gla",
    )(q, k, v, g, gk, wgu, bg, nw)


def _split_cols(w, main, tail):
    rest = w[:, main:]
    rest = jnp.pad(rest, ((0, 0), (0, tail - rest.shape[1])))
    return w[:, :main].astype(BF16), rest.astype(BF16)


def _per_step(n, preferred):
    for c in preferred:
        if n % c == 0:
            return c
    return 1


def kernel(x, meta_tokens, norm_w, ffn_w_gate, ffn_w_up, ffn_w_down, rel_bias_table,
           even_w_in, even_conv_w, swa_sinks, dn_a_log, dn_dt_bias, dn_norm_w, even_w_out,
           odd_w_in, gla_w_gate_up, gla_b_gate, gla_norm_w, odd_w_out):
    batch, seq, d = x.shape
    depth = norm_w.shape[0]
    lp = SWA_BLOCK + seq
    assert seq % ROW_TILE == 0 and (batch * lp) % ROW_TILE == 0
    rows = batch * lp
    chunks = _per_step(lp // CHUNK, (6, 3, 2))
    swa_blocks = _per_step(lp // SWA_BLOCK, (3, 2))

    meta_block = jnp.pad(meta_tokens.astype(x.dtype), ((PAD, 0), (0, 0)))
    norm4 = norm_w[:, :, None, :]
    wg, wu, wd = ffn_w_gate, ffn_w_up, ffn_w_down

    swa_q = SWA_Q_HEADS * SWA_HEAD_DIM
    swa_kv = SWA_KV_HEADS * SWA_HEAD_DIM
    bias = None
    h = x.reshape(batch * seq, d)
    for layer in range(depth):
        i = layer // 2
        embed = meta_block if layer == 0 else None
        final = layer == depth - 1
        if layer % 2 == 0:
            assert even_w_in.shape[2] == swa_q + 2 * swa_kv + 4 * DN_DIM + 2 * DN_HEADS
            widths = (swa_q, swa_kv, swa_kv, 3 * DN_DIM, DN_DIM, LANE)
            w_in, w_tail = _split_cols(even_w_in[i], sum(widths[:-1]), widths[-1])
            h, qa, ka, va, act_b, z_b, gates = _ffn_proj(
                h, embed, norm4, layer, wg, wu, wd, w_in, w_tail, widths,
                (BF16, BF16, BF16, BF16, BF16, F32), rows, lp, seq,
                conv_out=3, conv_w=even_conv_w[i])
            if bias is None:
                bias = _swa_bias(rel_bias_table)
            lane_pad = (DN_HEADS, LANE - 2 * DN_HEADS)
            alog_vec = jnp.pad(dn_a_log[i], lane_pad)[None, :]
            dtb_vec = jnp.pad(dn_dt_bias[i], lane_pad)[None, :]
            dn_nw = dn_norm_w[i][None, :]
            w_out = even_w_out[i]
            if not final and swa_blocks * SWA_BLOCK == chunks * CHUNK:
                h = _mix_ffn(qa, ka, va, bias, swa_sinks[i], act_b, z_b, gates, alog_vec,
                             dtb_vec, dn_nw, h, w_out.astype(BF16), norm4, layer, wg, wu, wd,
                             batch, lp, swa_blocks, chunks)
                continue
            o_a = _swa(qa, ka, va, bias, swa_sinks[i], batch, lp, swa_blocks)
            o_b = _deltanet(act_b, z_b, gates, alog_vec, dtb_vec, dn_nw, batch, lp, chunks)
            mixed = [o_a, o_b]
        else:
            key_dim = gla_w_gate_up.shape[2]
            val_dim = odd_w_out.shape[1]
            widths = (key_dim, key_dim, val_dim, val_dim, LANE)
            w_in, w_tail = _split_cols(odd_w_in[i], sum(widths[:-1]), widths[-1])
            h, q, k, v, g, gk = _ffn_proj(
                h, embed, norm4, layer, wg, wu, wd, w_in, w_tail, widths,
                (BF16, BF16, BF16, BF16, F32), rows, lp, seq)
            wgu = jnp.pad(gla_w_gate_up[i], ((0, LANE - GLA_GATE_RANK), (0, 0))).astype(BF16)
            o = _gla(q, k, v, g, gk, wgu, gla_b_gate[i][None, :], gla_norm_w[i][None, :],
                     batch, lp, chunks)
            mixed, w_out = [o], odd_w_out[i]
        h = _proj_ffn(mixed, h, w_out.astype(BF16), norm4, layer, wg, wu, wd, final,
                      batch, lp, seq)
    return h.reshape(batch, seq, d)
```

```python
import functools
import math

import numpy as np
import jax
import jax.numpy as jnp
from jax import lax
from jax.experimental import pallas as pl
from jax.experimental.pallas import tpu as pltpu

F32 = jnp.float32
BF16 = jnp.bfloat16
HIGHEST = lax.Precision.HIGHEST

N_META = 16
NORM_EPS = 1e-6
NEG_INF = -1e30
SWA_Q_HEADS = 8
SWA_KV_HEADS = 2
SWA_HEAD_DIM = 64
SWA_BLOCK = 128
REL_BUCKETS = 32
REL_MAX_DIST = 128
DN_HEADS = 4
DN_HEAD_DIM = 128
DN_DIM = DN_HEADS * DN_HEAD_DIM
DN_CONV = 4
GLA_HEADS = 4
GLA_GATE_RANK = 16
GLA_GATE_NORM = 16.0
CHUNK = 64

PAD = SWA_BLOCK - N_META
LANE = 128
ROW_TILE = 512
FF_TILE = 256
VMEM_LIMIT = 60 * 1024 * 1024


def _rms(x, w):
    return x * lax.rsqrt(jnp.mean(x * x, axis=-1, keepdims=True) + NORM_EPS) * w


def _silu(x):
    return x * jax.nn.sigmoid(x)


def _softplus(x):
    return jnp.maximum(x, 0.0) + jnp.log(1.0 + jnp.exp(-jnp.abs(x)))


def _dot(a, b, precision=None):
    return jnp.dot(a, b, preferred_element_type=F32, precision=precision)


def _dot_nt(a, b, precision=None):
    return lax.dot_general(a, b, (((1,), (1,)), ((), ())), preferred_element_type=F32,
                           precision=precision)


def _dot_tn(a, b, precision=None):
    return lax.dot_general(a, b, (((0,), (0,)), ((), ())), preferred_element_type=F32,
                           precision=precision)


def _resident(shape):
    nd = len(shape)
    return pl.BlockSpec(shape, lambda *_: (0,) * nd, pipeline_mode=pl.Buffered(1))


def _params(*semantics):
    return pltpu.CompilerParams(dimension_semantics=semantics, vmem_limit_bytes=VMEM_LIMIT)


SUB = ROW_TILE // SWA_BLOCK


def _pick(arr, *lead):
    rest = arr.shape[len(lead):]
    index = tuple(lead) + (0,) * len(rest)
    return pl.BlockSpec((None,) * len(lead) + rest, lambda i: index,
                        pipeline_mode=pl.Buffered(1))


def _after_prep(spec, prep):
    if spec.index_map is None:
        return spec
    index = spec.index_map
    return pl.BlockSpec(spec.block_shape, lambda i: index(jnp.maximum(i - prep, 0)),
                        pipeline_mode=spec.pipeline_mode)


def _weight_chunk_specs(wg, wd, layer, which):
    d, d_ff = wg.shape[-2:]
    last = d_ff // FF_TILE - 1
    cols = pl.BlockSpec((None, None, d, FF_TILE), lambda i: (layer, which, 0, jnp.minimum(i, last)))
    rows = pl.BlockSpec((None, None, FF_TILE, d), lambda i: (layer, which, jnp.minimum(i, last), 0))
    return [cols, cols, rows]


def _weight_scratch(wg):
    d, d_ff = wg.shape[-2:]
    chunks = d_ff // FF_TILE
    return [pltpu.VMEM((chunks, d, FF_TILE), BF16), pltpu.VMEM((chunks, d, FF_TILE), BF16),
            pltpu.VMEM((d_ff, d), BF16)]


def _prepare_weights(chunk_refs, scratch_refs):
    wg_blk, wu_blk, wd_blk = chunk_refs
    wg_s, wu_s, wd_s = scratch_refs
    prep = wg_s.shape[0]
    i = pl.program_id(0)

    @pl.when(i < prep)
    def _():
        wg_s[i] = wg_blk[...].astype(BF16)
        wu_s[i] = wu_blk[...].astype(BF16)
        wd_s[pl.ds(pl.multiple_of(i * FF_TILE, FF_TILE), FF_TILE), :] = wd_blk[...].astype(BF16)

    return prep


def _tile_specs(width):
    return [pl.BlockSpec((ROW_TILE, width), lambda i: (i, 0))]


def _real_specs(width, lp, seq):
    tiles = seq // ROW_TILE
    blocks = lp // SWA_BLOCK

    def spec(j):
        return pl.BlockSpec((SWA_BLOCK, width),
                            lambda i: ((i // tiles) * blocks + 1 + (i % tiles) * SUB + j, 0))
    return [spec(j) for j in range(SUB)]


def _embed_specs(width, lp, seq):
    blocks = lp // SWA_BLOCK
    x_blocks = seq // SWA_BLOCK

    def spec(j):
        def index(i):
            blk = i * SUB + j
            return ((blk // blocks) * x_blocks + jnp.maximum(blk % blocks - 1, 0), 0)
        return pl.BlockSpec((SWA_BLOCK, width), index)
    return [spec(j) for j in range(SUB)]


ROW_SPLIT = 4
GROUP_ROWS = ROW_TILE // ROW_SPLIT


def _group(part):
    return slice(part * GROUP_ROWS, (part + 1) * GROUP_ROWS)


def _load_rows(refs, part):
    if len(refs) == 1:
        return refs[0][_group(part), :]
    per = len(refs) // ROW_SPLIT
    return jnp.concatenate([r[...] for r in refs[part * per:(part + 1) * per]], axis=0)


def _drain(stages):
    for _ in stages:
        pass


def _interleave(main, side):
    for _ in main:
        next(side, None)
    _drain(side)


def _ffn_stages(xs, nwa, nwb, wg_ref, wu_ref, wd_ref, a_ref, outs):
    hns = [_rms(x, nwa).astype(BF16) for x in xs]
    for part, hn in enumerate(hns):
        for c in range(wg_ref.shape[0]):
            cols = slice(c * FF_TILE, (c + 1) * FF_TILE)
            g = _dot(hn, wg_ref[c])
            u = _dot(hn, wu_ref[c])
            a_ref[_group(part), cols] = (_silu(g) * u).astype(BF16)
            yield
    fs = []
    for part in range(len(xs)):
        fs.append(_dot(a_ref[_group(part), :], wd_ref[...]))
        yield
    outs.extend(x + 0.5 * _rms(f, nwb) for x, f in zip(xs, fs))


def _ffn_core(xs, nwa, nwb, wg_ref, wu_ref, wd_ref, a_ref):
    outs = []
    _drain(_ffn_stages(xs, nwa, nwb, wg_ref, wu_ref, wd_ref, a_ref, outs))
    return outs


CONV_HALO = 8


def _ffn_proj_body(*refs, n_rows, embed_blocks, n_out, conv_out, lp, tail_valid, prologue):
    lead = n_rows + bool(embed_blocks) + (conv_out is not None) + 2
    wg_s, wu_s, wd_s, win_s, wtail_s = refs[-5:]
    nwm_ref, win_blk, wtail_blk = refs[lead + 3:lead + 6]
    _prepare_weights(refs[lead:lead + 3], (wg_s, wu_s, wd_s))
    step = pl.program_id(0) - prologue

    @pl.when(pl.program_id(0) < win_s.shape[0])
    def _():
        win_s[pl.program_id(0)] = win_blk[...].astype(BF16)
        lane = lax.broadcasted_iota(jnp.int32, wtail_blk.shape, 1)
        wtail_s[...] = jnp.where(lane < tail_valid, wtail_blk[...], 0.0).astype(BF16)

    refs = refs[:lead] + (wg_s, wu_s, wd_s, nwm_ref, win_s, wtail_s) + refs[lead + 6:-5]

    @pl.when(step >= 0)
    def _():
        _ffn_proj_step(step, *refs, n_rows=n_rows, embed_blocks=embed_blocks, n_out=n_out,
                       conv_out=conv_out, lp=lp)


def _ffn_proj_step(step, *refs, n_rows, embed_blocks, n_out, conv_out, lp):
    row_refs = refs[:n_rows]
    refs = refs[n_rows:]
    if embed_blocks:
        meta_ref, refs = refs[0], refs[1:]
    if conv_out is not None:
        convw_ref, refs = refs[0], refs[1:]
        xpad_ref = refs[10 + n_out]

        @pl.when(step == 0)
        def _():
            xpad_ref[0:CONV_HALO, :] = jnp.zeros((CONV_HALO, xpad_ref.shape[1]), F32)

    nwa_ref, nwb_ref, wg_ref, wu_ref, wd_ref, nwm_ref, win_ref, wtail_ref = refs[:8]
    h_ref = refs[8]
    out_refs = refs[9:9 + n_out]
    a_ref = refs[9 + n_out]
    offsets = np.cumsum([0] + [o.shape[1] for o in out_refs])
    xs = []
    for part in range(ROW_SPLIT):
        if embed_blocks:
            per = n_rows // ROW_SPLIT
            first = step * SUB + part * per
            xs.append(jnp.concatenate(
                [jnp.where((first + j) % embed_blocks == 0, meta_ref[...], r[...])
                 for j, r in enumerate(row_refs[part * per:(part + 1) * per])], axis=0))
        else:
            xs.append(_load_rows(row_refs, part))
    hs = _ffn_core(xs, nwa_ref[...], nwb_ref[...], wg_ref, wu_ref, wd_ref, a_ref)
    hns = [_rms(h, nwm_ref[...]).astype(BF16) for h in hs]
    width = win_ref.shape[2]
    for part, (h, hn) in enumerate(zip(hs, hns)):
        rows = _group(part)
        h_ref[rows, :] = h
        for c in range(win_ref.shape[0]):
            p = _dot(hn, win_ref[c])
            for i in range(n_out - 1):
                lo = max(c * width, offsets[i])
                hi = min((c + 1) * width, offsets[i + 1])
                if lo >= hi:
                    continue
                piece = p[:, lo - c * width:hi - c * width]
                cols = slice(lo - offsets[i], hi - offsets[i])
                if i == conv_out:
                    xpad_ref[CONV_HALO + part * GROUP_ROWS:CONV_HALO + (part + 1) * GROUP_ROWS,
                             cols] = piece
                else:
                    out_refs[i][rows, cols] = piece.astype(out_refs[i].dtype)
        out_refs[-1][rows, :] = _dot(hn, wtail_ref[...]).astype(out_refs[-1].dtype)

    if conv_out is not None:
        taps = convw_ref.shape[0]
        cw = convw_ref[...]
        y = cw[taps - 1:taps, :] * xpad_ref[CONV_HALO:, :]
        for j in range(taps - 1):
            lo = CONV_HALO - (taps - 1 - j)
            y = y + cw[j:j + 1, :] * xpad_ref[lo:lo + ROW_TILE, :]
        tail = xpad_ref[ROW_TILE:, :]
        row = (step * ROW_TILE) % lp + lax.broadcasted_iota(jnp.int32, (ROW_TILE, 1), 0)
        row = jnp.where(row >= lp, row - lp, row)
        out_refs[conv_out][...] = jnp.where(row >= PAD, _silu(y), 0.0).astype(
            out_refs[conv_out].dtype)
        xpad_ref[0:CONV_HALO, :] = tail


def _ffn_proj(src, meta_block, norm4, layer, wg, wu, wd, w_in, index, widths, dtypes, rows, lp,
              seq, conv_out=None, conv_w=None):
    d = norm4.shape[-1]
    d_ff = wg.shape[-1]
    prep = d_ff // FF_TILE
    main = sum(widths[:-1])
    tail_valid = w_in.shape[2] - main
    assert 0 < tail_valid <= widths[-1] == LANE and main % LANE == 0 and lp > ROW_TILE
    width = FF_TILE
    assert main % width == 0
    win_chunks = main // width
    prep = max(prep, win_chunks)
    win_spec = pl.BlockSpec((None, d, width),
                            lambda i: (index, 0, jnp.minimum(i, win_chunks - 1)))
    wtail_spec = pl.BlockSpec((None, d, LANE), lambda i: (index, 0, main // LANE))
    if meta_block is None:
        row_specs, extra, extra_specs, embed_blocks = _tile_specs(d), [], [], 0
    else:
        row_specs = _embed_specs(d, lp, seq)
        extra, extra_specs, embed_blocks = [meta_block], [_resident(meta_block.shape)], (
            lp // SWA_BLOCK)
    scratch = [pltpu.VMEM((ROW_TILE, d_ff), BF16)]
    if conv_out is not None:
        extra, extra_specs = extra + [conv_w], extra_specs + [_resident(conv_w.shape)]
        scratch.append(pltpu.VMEM((CONV_HALO + ROW_TILE, widths[conv_out]), F32))
    tile = lambda n: pl.BlockSpec((ROW_TILE, n), lambda i: (i, 0))
    shift = lambda specs: [_after_prep(s, prep) for s in specs]
    return pl.pallas_call(
        functools.partial(_ffn_proj_body, n_rows=len(row_specs), embed_blocks=embed_blocks,
                          n_out=len(widths), conv_out=conv_out, lp=lp, tail_valid=tail_valid,
                          prologue=prep),
        grid=(prep + rows // ROW_TILE,),
        in_specs=shift(row_specs + extra_specs
                       + [_pick(norm4, layer, 0), _pick(norm4, layer, 1)])
        + _weight_chunk_specs(wg, wd, layer, 0)
        + shift([_pick(norm4, layer, 2)]) + [win_spec, wtail_spec],
        out_specs=shift([tile(d)] + [tile(n) for n in widths]),
        out_shape=[jax.ShapeDtypeStruct((rows, d), F32)]
        + [jax.ShapeDtypeStruct((rows, n), dt) for n, dt in zip(widths, dtypes)],
        scratch_shapes=scratch + _weight_scratch(wg)
        + [pltpu.VMEM((win_chunks, d, width), BF16), pltpu.VMEM((d, LANE), BF16)],
        compiler_params=_params("arbitrary"),
        name="ffn_proj",
    )(*([src] * len(row_specs)), *extra, norm4, norm4, wg, wu, wd, norm4, w_in, w_in)


def _proj_ffn_body(*refs, n_rows, n_x):
    x_groups = [refs[g * n_rows:(g + 1) * n_rows] for g in range(n_x)]
    refs = refs[n_x * n_rows:]
    h_refs = refs[:n_rows]
    (wo_ref, nwo_ref, nwa_ref, nwb_ref, wg_blk, wu_blk, wd_blk, o_ref, a_ref,
     wg_ref, wu_ref, wd_ref) = refs[n_rows:]
    prep = _prepare_weights((wg_blk, wu_blk, wd_blk), (wg_ref, wu_ref, wd_ref))

    @pl.when(pl.program_id(0) >= prep)
    def _():
        accs = []
        for part in range(ROW_SPLIT):
            acc = None
            off = 0
            for group in x_groups:
                xs = _load_rows(group, part)
                k = xs.shape[1]
                p = _dot(xs, wo_ref[off:off + k, :])
                acc = p if acc is None else acc + p
                off += k
            accs.append(acc)
        hs = [_load_rows(h_refs, part) + _rms(acc, nwo_ref[...])
              for part, acc in enumerate(accs)]
        outs = _ffn_core(hs, nwa_ref[...], nwb_ref[...], wg_ref, wu_ref, wd_ref, a_ref)
        for part, out in enumerate(outs):
            o_ref[_group(part), :] = out


def _proj_ffn(xs, h, w_out, norm4, layer, wg, wu, wd, final, batch, lp, seq):
    d = norm4.shape[-1]
    d_ff = wg.shape[-1]
    if final:
        specs = lambda width: _real_specs(width, lp, seq)
        out_rows = batch * seq
    else:
        specs = _tile_specs
        out_rows = batch * lp
    h_specs = specs(d)
    n_rows = len(h_specs)
    in_specs, args = [], []
    for x in xs:
        in_specs += specs(x.shape[1])
        args += [x] * n_rows
    in_specs += h_specs + [_resident(w_out.shape), _pick(norm4, layer, 3),
                           _pick(norm4, layer, 4), _pick(norm4, layer, 5)]
    prep = d_ff // FF_TILE
    in_specs = [_after_prep(s, prep) for s in in_specs] + _weight_chunk_specs(wg, wd, layer, 1)
    args += [h] * n_rows + [w_out, norm4, norm4, norm4, wg, wu, wd]
    return pl.pallas_call(
        functools.partial(_proj_ffn_body, n_rows=n_rows, n_x=len(xs)),
        grid=(prep + out_rows // ROW_TILE,),
        in_specs=in_specs,
        out_specs=_after_prep(pl.BlockSpec((ROW_TILE, d), lambda i: (i, 0)), prep),
        out_shape=jax.ShapeDtypeStruct((out_rows, d), F32),
        scratch_shapes=[pltpu.VMEM((ROW_TILE, d_ff), BF16)] + _weight_scratch(wg),
        compiler_params=_params("arbitrary"),
        name="proj_ffn",
    )(*args)


def _t5_bucket_np(rel):
    n = np.maximum(rel, 0)
    max_exact = REL_BUCKETS // 2
    n_f = np.maximum(n, 1).astype(np.float32)
    large = max_exact + (np.log(n_f / np.float32(max_exact))
                         / np.float32(math.log(REL_MAX_DIST / max_exact))
                         * np.float32(REL_BUCKETS - max_exact)).astype(np.int32)
    large = np.minimum(large, REL_BUCKETS - 1)
    return np.where(n < max_exact, n, large).astype(np.int32)


def _swa_bucket_index():
    i = np.arange(SWA_BLOCK)[:, None]
    c = np.arange(SWA_BLOCK)[None, :]
    out = np.full((3, SWA_BLOCK, 2 * SWA_BLOCK), -1, np.int32)
    for variant in range(3):
        pos_q = variant * SWA_BLOCK + i - PAD
        pos_k = variant * SWA_BLOCK + c - PAD - np.where(c > i, SWA_BLOCK, 0)
        rel_b = pos_q - pos_k
        assert ((rel_b >= 0) & (rel_b < SWA_BLOCK)).all()
        out[variant, :, :SWA_BLOCK] = np.where(pos_k >= N_META, _t5_bucket_np(rel_b), -1)
        rel_m = pos_q - c
        meta = (c < N_META) & (rel_m >= 0)
        out[variant, :, SWA_BLOCK:] = np.where(meta, _t5_bucket_np(rel_m), -1)
    return out


_SWA_BUCKET_INDEX = _swa_bucket_index()


def _bias_body(tbl_ref, idx_ref, o_ref):
    idx = idx_ref[0]
    group = SWA_Q_HEADS // SWA_KV_HEADS
    for head in range(SWA_Q_HEADS):
        acc = jnp.full(idx.shape, NEG_INF, F32)
        for b in range(REL_BUCKETS):
            acc = jnp.where(idx == b, tbl_ref[b, head], acc)
        rows = slice((head % group) * SWA_BLOCK, (head % group + 1) * SWA_BLOCK)
        o_ref[0, head // group, rows, :] = acc


def _swa_bias(rel_table):
    idx = jnp.asarray(_SWA_BUCKET_INDEX)
    group = SWA_Q_HEADS // SWA_KV_HEADS
    width = idx.shape[2]
    return pl.pallas_call(
        _bias_body,
        grid=(3,),
        in_specs=[pl.BlockSpec(memory_space=pltpu.SMEM),
                  pl.BlockSpec((1, SWA_BLOCK, width), lambda v: (v, 0, 0))],
        out_specs=pl.BlockSpec((1, SWA_KV_HEADS, group * SWA_BLOCK, width),
                               lambda v: (v, 0, 0, 0)),
        out_shape=jax.ShapeDtypeStruct((3, SWA_KV_HEADS, group * SWA_BLOCK, width), F32),
        compiler_params=_params("arbitrary"),
        name="swa_bias",
    )(rel_table, idx)


def _swa_stages(sink_ref, q_ref, kc_ref, kp_ref, km_ref, vc_ref, vp_ref, vm_ref, bias_ref,
                step, blocks, store):
    dh = SWA_HEAD_DIM
    blk = SWA_BLOCK
    group = SWA_Q_HEADS // SWA_KV_HEADS
    scale = dh ** -0.5
    zpad = jnp.zeros((blk - N_META, dh), BF16)
    row = lax.broadcasted_iota(jnp.int32, (group * blk, blk), 0) % blk
    col = lax.broadcasted_iota(jnp.int32, (group * blk, blk), 1)
    from_prev = col > row

    ones_col = (lax.broadcasted_iota(jnp.int32, (3 * blk, dh), 1) == 0).astype(BF16)

    units = [(j, hk) for j in range(blocks) for hk in range(SWA_KV_HEADS)]
    scores, values = [], []
    for j, hk in units:
        r = slice(j * blk, (j + 1) * blk)
        kv = slice(hk * dh, (hk + 1) * dh)
        k_prev = kp_ref[:, kv] if j == 0 else kc_ref[(j - 1) * blk:j * blk, kv]
        v_prev = vp_ref[:, kv] if j == 0 else vc_ref[(j - 1) * blk:j * blk, kv]
        k_all = jnp.concatenate([k_prev, kc_ref[r, kv], km_ref[:, kv], zpad], axis=0)
        v_all = jnp.concatenate([v_prev, vc_ref[r, kv], vm_ref[:, kv], zpad], axis=0)
        values.append(jnp.concatenate([v_all, ones_col], axis=1))
        q = jnp.concatenate([q_ref[r, (hk * group + g) * dh:(hk * group + g + 1) * dh]
                             for g in range(group)], axis=0) * scale
        scores.append(_dot_nt(q, k_all))
        yield

    probs, sink_terms = [], []
    for (j, hk), s in zip(units, scores):
        bias = bias_ref[jnp.minimum(step * blocks + j, 2), hk]
        s_band = jnp.where(from_prev, s[:, :blk], s[:, blk:2 * blk]) + bias[:, :blk]
        s_meta = s[:, 2 * blk:] + bias[:, blk:]
        sink = jnp.concatenate([jnp.full((blk, 1), sink_ref[hk * group + g], F32)
                                for g in range(group)], axis=0)
        m = jnp.maximum(jnp.max(jnp.maximum(s_band, s_meta), axis=-1, keepdims=True), sink)
        e_band = jnp.exp(s_band - m)
        e_meta = jnp.exp(s_meta - m)
        sink_terms.append(jnp.exp(sink - m))
        probs.append(jnp.concatenate([jnp.where(from_prev, e_band, 0.0),
                                      jnp.where(from_prev, 0.0, e_band), e_meta],
                                     axis=1).astype(BF16))

    for (j, hk), p, v_ext, sink_term in zip(units, probs, values, sink_terms):
        pv = _dot(p, v_ext)
        o = pv[:, :dh] / (pv[:, dh:dh + 1] + sink_term)
        for g in range(group):
            hq = hk * group + g
            store(slice(j * blk, (j + 1) * blk), slice(hq * dh, (hq + 1) * dh),
                  o[g * blk:(g + 1) * blk])
        yield


def _swa_body(sink_ref, q_ref, kc_ref, kp_ref, km_ref, vc_ref, vp_ref, vm_ref, bias_ref, o_ref,
              *, blocks):
    def store(rows, cols, value):
        o_ref[rows, cols] = value.astype(o_ref.dtype)

    _drain(_swa_stages(sink_ref, q_ref, kc_ref, kp_ref, km_ref, vc_ref, vp_ref, vm_ref,
                       bias_ref, pl.program_id(1), blocks, store))


def _swa(q, k, v, bias, sinks, batch, lp, blocks):
    rows = q.shape[0]
    nb = lp // SWA_BLOCK
    steps = nb // blocks
    meta_blocks = lp // N_META
    cur = lambda b, s: (b * steps + s, 0)
    prev = lambda b, s: (b * nb + jnp.maximum(s * blocks - 1, 0), 0)
    meta = lambda b, s: (b * meta_blocks + PAD // N_META, 0)
    kv_w = k.shape[1]
    rs = blocks * SWA_BLOCK
    return pl.pallas_call(
        functools.partial(_swa_body, blocks=blocks),
        grid=(batch, steps),
        in_specs=[pl.BlockSpec(memory_space=pltpu.SMEM),
                  pl.BlockSpec((rs, q.shape[1]), cur),
                  pl.BlockSpec((rs, kv_w), cur), pl.BlockSpec((SWA_BLOCK, kv_w), prev),
                  pl.BlockSpec((N_META, kv_w), meta),
                  pl.BlockSpec((rs, kv_w), cur), pl.BlockSpec((SWA_BLOCK, kv_w), prev),
                  pl.BlockSpec((N_META, kv_w), meta),
                  _resident(bias.shape)],
        out_specs=pl.BlockSpec((rs, q.shape[1]), cur),
        out_shape=jax.ShapeDtypeStruct((rows, q.shape[1]), BF16),
        compiler_params=_params("parallel", "arbitrary"),
        name="swa",
    )(sinks, q, k, k, k, v, v, v, bias)


def _tri_masks():
    row = lax.broadcasted_iota(jnp.int32, (CHUNK, CHUNK), 0)
    col = lax.broadcasted_iota(jnp.int32, (CHUNK, CHUNK), 1)
    return row >= col, row > col, row == col


def _deltanet_stages(act_ref, z_ref, gates_ref, alog_ref, dtb_ref, nw_ref, s_ref, chunks,
                     store):
    nh, dh = DN_HEADS, DN_HEAD_DIM
    gt = gates_ref[...]
    beta_full = jax.nn.sigmoid(gt)
    g_full = -jnp.exp(alog_ref[...]) * _softplus(gt + dtb_ref[...])
    incl, strict, diag = _tri_masks()
    eye = diag.astype(F32)
    tri = incl.astype(F32)
    gc_chunks = [_dot(tri, g_full[c * CHUNK:(c + 1) * CHUNK, :], HIGHEST)
                 for c in range(chunks)]
    items = [(c, h) for c in range(chunks) for h in range(nh)]
    yield

    neg_a, akt16, qd16, rhs16, decay = [], [], [], [], []
    for c, h in items:
        if h == 0 and c % 2 == 0 and c > 0:
            yield
        r = slice(c * CHUNK, (c + 1) * CHUNK)
        q = act_ref[r, h * dh:(h + 1) * dh].astype(F32)
        k = act_ref[r, DN_DIM + h * dh:DN_DIM + (h + 1) * dh].astype(F32)
        v = act_ref[r, 2 * DN_DIM + h * dh:2 * DN_DIM + (h + 1) * dh].astype(F32)
        q = q * lax.rsqrt(jnp.sum(q * q, axis=-1, keepdims=True) + 1e-6) * dh ** -0.5
        k = k * lax.rsqrt(jnp.sum(k * k, axis=-1, keepdims=True) + 1e-6)
        beta = jnp.broadcast_to(beta_full[r, h:h + 1], (CHUNK, dh))
        gc = jnp.broadcast_to(gc_chunks[c][:, nh + h:nh + h + 1], (CHUNK, dh))
        gc_row = jnp.transpose(gc)[:CHUNK, :]
        gamma = jnp.where(incl, jnp.exp(jnp.where(incl, gc[:, :CHUNK] - gc_row, 0.0)), 0.0)
        k_beta = k * beta
        kq = _dot_nt(jnp.concatenate([k_beta, q], axis=0).astype(BF16), k.astype(BF16))
        neg_a.append(jnp.where(strict, -(kq[:CHUNK] * gamma), 0.0))
        attn16 = jnp.where(incl, kq[CHUNK:] * gamma, 0.0).astype(BF16)
        gc_last = gc[CHUNK - 1:CHUNK, :]
        qd16.append((q * jnp.exp(gc)).astype(BF16))
        kdt16 = jnp.transpose(k * jnp.exp(gc_last - gc)).astype(BF16)
        akt16.append(jnp.concatenate([attn16, kdt16], axis=0))
        rhs16.append(jnp.concatenate([v * beta, k_beta * jnp.exp(gc)], axis=1).astype(BF16))
        decay.append(jnp.exp(gc_last))

    yield
    levels = int(math.log2(CHUNK)) - 1
    t = [eye + x for x in neg_a]
    p16 = [x.astype(BF16) for x in neg_a]
    p16 = [_dot(x, x).astype(BF16) for x in p16]
    yield
    for level in range(levels):
        if level + 1 < levels:
            tp = [_dot(jnp.concatenate([y.astype(BF16), x], axis=0), x) for y, x in zip(t, p16)]
            t = [y + r[:CHUNK] for y, r in zip(t, tp)]
            p16 = [r[CHUNK:].astype(BF16) for r in tp]
        else:
            t = [y + _dot(y.astype(BF16), x) for y, x in zip(t, p16)]
        yield
    uw = [_dot(y.astype(BF16), b) for y, b in zip(t, rhs16)]
    wq16 = [jnp.concatenate([x[:, dh:].astype(BF16), y], axis=0)
            for x, y in zip(uw, qd16)]
    yield

    nw = nw_ref[...]
    state = [s_ref[h] for h in range(nh)]
    for c in range(chunks):
        r = slice(c * CHUNK, (c + 1) * CHUNK)
        pair = [c * nh + h for h in range(nh)]
        s16 = [s.astype(BF16) for s in state]
        wqs = [_dot(wq16[i], s16[h]) for h, i in enumerate(pair)]
        yield
        vn16 = [(uw[i][:, :dh] - wqs[h][:CHUNK]).astype(BF16) for h, i in enumerate(pair)]
        ak = [_dot(akt16[i], vn16[h]) for h, i in enumerate(pair)]
        state = [state[h] * decay[i] + ak[h][CHUNK:] for h, i in enumerate(pair)]
        for h in range(nh):
            hc = slice(h * dh, (h + 1) * dh)
            out = wqs[h][CHUNK:] + ak[h][:CHUNK]
            store(r, hc, _rms(out, nw) * _silu(z_ref[r, hc].astype(F32)))
        yield
    for h in range(nh):
        s_ref[h] = state[h]


def _deltanet_body(act_ref, z_ref, gates_ref, alog_ref, dtb_ref, nw_ref, o_ref, s_ref, *,
                   chunks):
    @pl.when(pl.program_id(1) == 0)
    def _():
        s_ref[...] = jnp.zeros_like(s_ref)

    def store(rows, cols, value):
        o_ref[rows, cols] = value.astype(o_ref.dtype)

    _drain(_deltanet_stages(act_ref, z_ref, gates_ref, alog_ref, dtb_ref, nw_ref, s_ref,
                            chunks, store))


def _deltanet(act, z, gates, alog_vec, dtb_vec, nw, batch, lp, chunks):
    rows = act.shape[0]
    rs = chunks * CHUNK
    steps = lp // rs
    idx = lambda b, s: (b * steps + s, 0)
    return pl.pallas_call(
        functools.partial(_deltanet_body, chunks=chunks),
        grid=(batch, steps),
        in_specs=[pl.BlockSpec((rs, act.shape[1]), idx), pl.BlockSpec((rs, z.shape[1]), idx),
                  pl.BlockSpec((rs, gates.shape[1]), idx),
                  _resident(alog_vec.shape), _resident(dtb_vec.shape), _resident(nw.shape)],
        out_specs=pl.BlockSpec((rs, DN_DIM), idx),
        out_shape=jax.ShapeDtypeStruct((rows, DN_DIM), BF16),
        scratch_shapes=[pltpu.VMEM((DN_HEADS, DN_HEAD_DIM, DN_HEAD_DIM), F32)],
        compiler_params=_params("parallel", "arbitrary"),
        name="deltanet",
    )(act, z, gates, alog_vec, dtb_vec, nw)


def _mix_ffn_body(*refs, blocks, chunks, steps, last):
    *refs, wg_s, wu_s, wd_s = refs
    prep = _prepare_weights(refs[20:23], (wg_s, wu_s, wd_s))
    t = pl.program_id(0) - prep
    refs = refs[:20] + [wg_s, wu_s, wd_s] + refs[23:]

    @pl.when(t >= 0)
    def _():
        _mix_ffn_step(t, *refs, blocks=blocks, chunks=chunks, steps=steps, last=last)


def _mix_ffn_step(t, sink_ref, q_ref, kc_ref, kp_ref, km_ref, vc_ref, vp_ref, vm_ref, bias_ref,
                  act_ref, z_ref, gates_ref, alog_ref, dtb_ref, dnw_ref,
                  h_ref, wo_ref, nwo_ref, nwa_ref, nwb_ref, wg_ref, wu_ref, wd_ref,
                  o_ref, s_ref, mix_ref, a_ref, *, blocks, chunks, steps, last):
    seq_step = jnp.minimum(t, last) % steps
    write_slot = t % 2
    read_slot = 1 - write_slot
    swa_width = q_ref.shape[1]
    groups = q_ref.shape[0] // GROUP_ROWS

    @pl.when(t == 0)
    def _():
        mix_ref[...] = jnp.zeros_like(mix_ref)

    @pl.when(seq_step == 0)
    def _():
        s_ref[...] = jnp.zeros_like(s_ref)

    def store_swa(rows, cols, value):
        mix_ref[write_slot, rows, cols] = value.astype(mix_ref.dtype)

    def store_dn(rows, cols, value):
        cols = slice(swa_width + cols.start, swa_width + cols.stop)
        mix_ref[write_slot, rows, cols] = value.astype(mix_ref.dtype)

    def mixer_stages():
        dn = _deltanet_stages(act_ref, z_ref, gates_ref, alog_ref, dtb_ref, dnw_ref,
                              s_ref, chunks, store_dn)
        swa = _swa_stages(sink_ref, q_ref, kc_ref, kp_ref, km_ref, vc_ref, vp_ref,
                          vm_ref, bias_ref, seq_step, blocks, store_swa)
        for _ in dn:
            yield
            if next(swa, StopIteration) is not StopIteration:
                yield
        yield from swa

    def ffn_stages():
        accs = []
        for part in range(groups):
            accs.append(_dot(mix_ref[read_slot, _group(part), :], wo_ref[...]))
            yield
        hs = [h_ref[_group(part), :] + _rms(acc, nwo_ref[...]) for part, acc in enumerate(accs)]
        outs = []
        yield from _ffn_stages(hs, nwa_ref[...], nwb_ref[...], wg_ref, wu_ref, wd_ref, a_ref,
                               outs)
        for part, out in enumerate(outs):
            o_ref[_group(part), :] = out

    _interleave(ffn_stages(), mixer_stages())


def _mix_ffn(qa, ka, va, bias, sinks, act, z, gates, alog_vec, dtb_vec, dn_nw, h, w_out, norm4,
             layer, wg, wu, wd, batch, lp, blocks, chunks):
    d = norm4.shape[-1]
    d_ff = wg.shape[-1]
    rs = blocks * SWA_BLOCK
    assert rs == chunks * CHUNK and rs % GROUP_ROWS == 0
    nb = lp // SWA_BLOCK
    steps = lp // rs
    last = batch * steps - 1
    meta_blocks = lp // N_META

    def tile(t):
        return jnp.minimum(t, last)

    cur = lambda t: (tile(t), 0)
    prev = lambda t: ((tile(t) // steps) * nb + jnp.maximum((tile(t) % steps) * blocks - 1, 0), 0)
    meta = lambda t: ((tile(t) // steps) * meta_blocks + PAD // N_META, 0)
    behind = lambda t: (jnp.maximum(t - 1, 0), 0)
    kv_w = ka.shape[1]
    prep = d_ff // FF_TILE
    in_specs = [pl.BlockSpec(memory_space=pltpu.SMEM),
                pl.BlockSpec((rs, qa.shape[1]), cur),
                pl.BlockSpec((rs, kv_w), cur), pl.BlockSpec((SWA_BLOCK, kv_w), prev),
                pl.BlockSpec((N_META, kv_w), meta),
                pl.BlockSpec((rs, kv_w), cur), pl.BlockSpec((SWA_BLOCK, kv_w), prev),
                pl.BlockSpec((N_META, kv_w), meta),
                _resident(bias.shape),
                pl.BlockSpec((rs, act.shape[1]), cur), pl.BlockSpec((rs, z.shape[1]), cur),
                pl.BlockSpec((rs, gates.shape[1]), cur),
                _resident(alog_vec.shape), _resident(dtb_vec.shape), _resident(dn_nw.shape),
                pl.BlockSpec((rs, d), behind), _resident(w_out.shape),
                _pick(norm4, layer, 3), _pick(norm4, layer, 4), _pick(norm4, layer, 5)]
    return pl.pallas_call(
        functools.partial(_mix_ffn_body, blocks=blocks, chunks=chunks, steps=steps, last=last),
        grid=(prep + last + 2,),
        in_specs=[_after_prep(s, prep) for s in in_specs]
        + _weight_chunk_specs(wg, wd, layer, 1),
        out_specs=_after_prep(pl.BlockSpec((rs, d), behind), prep),
        out_shape=jax.ShapeDtypeStruct(h.shape, F32),
        scratch_shapes=[pltpu.VMEM((DN_HEADS, DN_HEAD_DIM, DN_HEAD_DIM), F32),
                        pltpu.VMEM((2, rs, qa.shape[1] + DN_DIM), BF16),
                        pltpu.VMEM((rs, d_ff), BF16)] + _weight_scratch(wg),
        compiler_params=_params("arbitrary"),
        name="mix_ffn",
    )(sinks, qa, ka, ka, ka, va, va, va, bias, act, z, gates, alog_vec, dtb_vec, dn_nw,
      h, w_out, norm4, norm4, norm4, wg, wu, wd)


def _gla_body(q_ref, k_ref, v_ref, g_ref, gk_ref, wgu_ref, bg_ref, nw_ref, o_ref, s_ref, *,
              chunks):
    nh = GLA_HEADS
    dk = q_ref.shape[1] // nh
    dv = v_ref.shape[1] // nh

    @pl.when(pl.program_id(1) == 0)
    def _():
        s_ref[...] = jnp.zeros_like(s_ref)

    logits = _dot(gk_ref[...].astype(BF16), wgu_ref[...]) + bg_ref[...]
    glog = (jnp.minimum(logits, 0.0) - jnp.log(1.0 + jnp.exp(-jnp.abs(logits)))) / GLA_GATE_NORM
    incl, _, _ = _tri_masks()
    tri = incl.astype(F32)
    bcum = [_dot(tri, glog[c * CHUNK:(c + 1) * CHUNK, :], HIGHEST) for c in range(chunks)]

    items = [(c, h) for c in range(chunks) for h in range(nh)]
    qd16, kn16, kdt16, decay = [], [], [], []
    for c, h in items:
        r = slice(c * CHUNK, (c + 1) * CHUNK)
        kc = slice(h * dk, (h + 1) * dk)
        bc = bcum[c][:, kc]
        q = q_ref[r, kc].astype(F32) * dk ** -0.5
        k = k_ref[r, kc].astype(F32)
        b_last = bc[CHUNK - 1:CHUNK, :]
        qd16.append((q * jnp.exp(bc)).astype(BF16))
        kn16.append((k * jnp.exp(-bc)).astype(BF16))
        kdt16.append(jnp.transpose(k * jnp.exp(b_last - bc)).astype(BF16))
        decay_rows = jnp.broadcast_to(jnp.exp(b_last), (8, dk))
        decay.append(jnp.transpose(decay_rows)[:, :1])
    values = [v_ref[c * CHUNK:(c + 1) * CHUNK, h * dv:(h + 1) * dv] for c, h in items]
    attn16 = [jnp.where(incl, _dot_nt(a, b), 0.0).astype(BF16) for a, b in zip(qd16, kn16)]
    intra = [_dot(a, v) for a, v in zip(attn16, values)]
    update = [_dot(kt, v) for kt, v in zip(kdt16, values)]

    nw = nw_ref[...]
    state = [s_ref[h] for h in range(nh)]
    for c in range(chunks):
        r = slice(c * CHUNK, (c + 1) * CHUNK)
        for h in range(nh):
            i = c * nh + h
            vc = slice(h * dv, (h + 1) * dv)
            o = intra[i] + _dot(qd16[i], state[h].astype(BF16))
            state[h] = state[h] * decay[i] + update[i]
            o = _rms(o, nw) * _silu(g_ref[r, vc].astype(F32))
            o_ref[r, vc] = o.astype(o_ref.dtype)
    for h in range(nh):
        s_ref[h] = state[h]


def _gla(q, k, v, g, gk, wgu, bg, nw, batch, lp, chunks):
    rows = q.shape[0]
    rs = chunks * CHUNK
    steps = lp // rs
    idx = lambda b, s: (b * steps + s, 0)
    dk = q.shape[1] // GLA_HEADS
    dv = v.shape[1] // GLA_HEADS
    return pl.pallas_call(
        functools.partial(_gla_body, chunks=chunks),
        grid=(batch, steps),
        in_specs=[pl.BlockSpec((rs, q.shape[1]), idx), pl.BlockSpec((rs, k.shape[1]), idx),
                  pl.BlockSpec((rs, v.shape[1]), idx), pl.BlockSpec((rs, g.shape[1]), idx),
                  pl.BlockSpec((rs, gk.shape[1]), idx), _resident(wgu.shape),
                  _resident(bg.shape), _resident(nw.shape)],
        out_specs=pl.BlockSpec((rs, v.shape[1]), idx),
        out_shape=jax.ShapeDtypeStruct((rows, v.shape[1]), BF16),
        scratch_shapes=[pltpu.VMEM((GLA_HEADS, dk, dv), F32)],
        compiler_params=_params("parallel", "arbitrary"),
        name="gla",
    )(q, k, v, g, gk, wgu, bg, nw)


def _per_step(n, preferred):
    for c in preferred:
        if n % c == 0:
            return c
    return 1


def kernel(x, meta_tokens, norm_w, ffn_w_gate, ffn_w_up, ffn_w_down, rel_bias_table,
           even_w_in, even_conv_w, swa_sinks, dn_a_log, dn_dt_bias, dn_norm_w, even_w_out,
           odd_w_in, gla_w_gate_up, gla_b_gate, gla_norm_w, odd_w_out):
    batch, seq, d = x.shape
    depth = norm_w.shape[0]
    lp = SWA_BLOCK + seq
    assert seq % ROW_TILE == 0 and (batch * lp) % ROW_TILE == 0
    rows = batch * lp
    chunks = _per_step(lp // CHUNK, (6, 3, 2))
    swa_blocks = _per_step(lp // SWA_BLOCK, (3, 2))

    meta_block = jnp.pad(meta_tokens.astype(x.dtype), ((PAD, 0), (0, 0)))
    norm4 = norm_w[:, :, None, :]
    wg, wu, wd = ffn_w_gate, ffn_w_up, ffn_w_down

    swa_q = SWA_Q_HEADS * SWA_HEAD_DIM
    swa_kv = SWA_KV_HEADS * SWA_HEAD_DIM
    bias = None
    h = x.reshape(batch * seq, d)
    for layer in range(depth):
        i = layer // 2
        embed = meta_block if layer == 0 else None
        final = layer == depth - 1
        if layer % 2 == 0:
            assert even_w_in.shape[2] == swa_q + 2 * swa_kv + 4 * DN_DIM + 2 * DN_HEADS
            widths = (swa_q, swa_kv, swa_kv, 3 * DN_DIM, DN_DIM, LANE)
            h, qa, ka, va, act_b, z_b, gates = _ffn_proj(
                h, embed, norm4, layer, wg, wu, wd, even_w_in, i, widths,
                (BF16, BF16, BF16, BF16, BF16, F32), rows, lp, seq,
                conv_out=3, conv_w=even_conv_w[i])
            if bias is None:
                bias = _swa_bias(rel_bias_table)
            lane_pad = (DN_HEADS, LANE - 2 * DN_HEADS)
            alog_vec = jnp.pad(dn_a_log[i], lane_pad)[None, :]
            dtb_vec = jnp.pad(dn_dt_bias[i], lane_pad)[None, :]
            dn_nw = dn_norm_w[i][None, :]
            w_out = even_w_out[i]
            if not final and swa_blocks * SWA_BLOCK == chunks * CHUNK:
                h = _mix_ffn(qa, ka, va, bias, swa_sinks[i], act_b, z_b, gates, alog_vec,
                             dtb_vec, dn_nw, h, w_out.astype(BF16), norm4, layer, wg, wu, wd,
                             batch, lp, swa_blocks, chunks)
                continue
            o_a = _swa(qa, ka, va, bias, swa_sinks[i], batch, lp, swa_blocks)
            o_b = _deltanet(act_b, z_b, gates, alog_vec, dtb_vec, dn_nw, batch, lp, chunks)
            mixed = [o_a, o_b]
        else:
            key_dim = gla_w_gate_up.shape[2]
            val_dim = odd_w_out.shape[1]
            widths = (key_dim, key_dim, val_dim, val_dim, LANE)
            h, q, k, v, g, gk = _ffn_proj(
                h, embed, norm4, layer, wg, wu, wd, odd_w_in, i, widths,
                (BF16, BF16, BF16, BF16, F32), rows, lp, seq)
            wgu = jnp.pad(gla_w_gate_up[i], ((0, LANE - GLA_GATE_RANK), (0, 0))).astype(BF16)
            o = _gla(q, k, v, g, gk, wgu, gla_b_gate[i][None, :], gla_norm_w[i][None, :],
                     batch, lp, chunks)
            mixed, w_out = [o], odd_w_out[i]
        h = _proj_ffn(mixed, h, w_out.astype(BF16), norm4, layer, wg, wu, wd, final,
                      batch, lp, seq)
    return h.reshape(batch, seq, d)
```

```python
import functools
import math

import numpy as np
import jax
import jax.numpy as jnp
from jax import lax
from jax.experimental import pallas as pl
from jax.experimental.pallas import tpu as pltpu

F32 = jnp.float32
BF16 = jnp.bfloat16
HIGHEST = lax.Precision.HIGHEST

N_META = 16
NORM_EPS = 1e-6
NEG_INF = -1e30
SWA_Q_HEADS = 8
SWA_KV_HEADS = 2
SWA_HEAD_DIM = 64
SWA_BLOCK = 128
REL_BUCKETS = 32
REL_MAX_DIST = 128
DN_HEADS = 4
DN_HEAD_DIM = 128
DN_DIM = DN_HEADS * DN_HEAD_DIM
DN_CONV = 4
GLA_HEADS = 4
GLA_GATE_RANK = 16
GLA_GATE_NORM = 16.0
CHUNK = 64

PAD = SWA_BLOCK - N_META
LANE = 128
ROW_TILE = 512
FF_TILE = 256
VMEM_LIMIT = 60 * 1024 * 1024


def _rms(x, w):
    return x * lax.rsqrt(jnp.mean(x * x, axis=-1, keepdims=True) + NORM_EPS) * w


def _silu(x):
    return x * jax.nn.sigmoid(x)


def _softplus(x):
    return jnp.maximum(x, 0.0) + jnp.log(1.0 + jnp.exp(-jnp.abs(x)))


def _dot(a, b, precision=None):
    return jnp.dot(a, b, preferred_element_type=F32, precision=precision)


def _dot_nt(a, b, precision=None):
    return lax.dot_general(a, b, (((1,), (1,)), ((), ())), preferred_element_type=F32,
                           precision=precision)


def _dot_tn(a, b, precision=None):
    return lax.dot_general(a, b, (((0,), (0,)), ((), ())), preferred_element_type=F32,
                           precision=precision)


def _resident(shape):
    nd = len(shape)
    return pl.BlockSpec(shape, lambda *_: (0,) * nd, pipeline_mode=pl.Buffered(1))


def _params(*semantics):
    return pltpu.CompilerParams(dimension_semantics=semantics, vmem_limit_bytes=VMEM_LIMIT)


SUB = ROW_TILE // SWA_BLOCK


def _pick(arr, *lead):
    rest = arr.shape[len(lead):]
    index = tuple(lead) + (0,) * len(rest)
    return pl.BlockSpec((None,) * len(lead) + rest, lambda i: index,
                        pipeline_mode=pl.Buffered(1))


def _after_prep(spec, prep):
    if spec.index_map is None:
        return spec
    index = spec.index_map
    return pl.BlockSpec(spec.block_shape, lambda i: index(jnp.maximum(i - prep, 0)),
                        pipeline_mode=spec.pipeline_mode)


def _weight_chunk_specs(wg, wd, layer, which):
    d, d_ff = wg.shape[-2:]
    last = d_ff // FF_TILE - 1
    cols = pl.BlockSpec((None, None, d, FF_TILE), lambda i: (layer, which, 0, jnp.minimum(i, last)))
    rows = pl.BlockSpec((None, None, FF_TILE, d), lambda i: (layer, which, jnp.minimum(i, last), 0))
    return [cols, cols, rows]


def _weight_scratch(wg):
    d, d_ff = wg.shape[-2:]
    chunks = d_ff // FF_TILE
    return [pltpu.VMEM((chunks, d, FF_TILE), BF16), pltpu.VMEM((chunks, d, FF_TILE), BF16),
            pltpu.VMEM((d_ff, d), BF16)]


def _prepare_weights(chunk_refs, scratch_refs):
    wg_blk, wu_blk, wd_blk = chunk_refs
    wg_s, wu_s, wd_s = scratch_refs
    prep = wg_s.shape[0]
    i = pl.program_id(0)

    @pl.when(i < prep)
    def _():
        wg_s[i] = wg_blk[...].astype(BF16)
        wu_s[i] = wu_blk[...].astype(BF16)
        wd_s[pl.ds(pl.multiple_of(i * FF_TILE, FF_TILE), FF_TILE), :] = wd_blk[...].astype(BF16)

    return prep


def _tile_specs(width):
    return [pl.BlockSpec((ROW_TILE, width), lambda i: (i, 0))]


def _real_specs(width, lp, seq):
    tiles = seq // ROW_TILE
    blocks = lp // SWA_BLOCK

    def spec(j):
        return pl.BlockSpec((SWA_BLOCK, width),
                            lambda i: ((i // tiles) * blocks + 1 + (i % tiles) * SUB + j, 0))
    return [spec(j) for j in range(SUB)]


def _embed_specs(width, lp, seq):
    blocks = lp // SWA_BLOCK
    x_blocks = seq // SWA_BLOCK

    def spec(j):
        def index(i):
            blk = i * SUB + j
            return ((blk // blocks) * x_blocks + jnp.maximum(blk % blocks - 1, 0), 0)
        return pl.BlockSpec((SWA_BLOCK, width), index)
    return [spec(j) for j in range(SUB)]


ROW_SPLIT = 4
GROUP_ROWS = ROW_TILE // ROW_SPLIT


def _group(part):
    return slice(part * GROUP_ROWS, (part + 1) * GROUP_ROWS)


def _load_rows(refs, part):
    if len(refs) == 1:
        return refs[0][_group(part), :]
    per = len(refs) // ROW_SPLIT
    return jnp.concatenate([r[...] for r in refs[part * per:(part + 1) * per]], axis=0)


def _drain(stages):
    for _ in stages:
        pass


def _interleave(main, side):
    for _ in main:
        next(side, None)
    _drain(side)


def _ffn_stages(xs, nwa, nwb, wg_ref, wu_ref, wd_ref, a_ref, outs):
    hns = [_rms(x, nwa).astype(BF16) for x in xs]
    for part, hn in enumerate(hns):
        for c in range(wg_ref.shape[0]):
            cols = slice(c * FF_TILE, (c + 1) * FF_TILE)
            g = _dot(hn, wg_ref[c])
            u = _dot(hn, wu_ref[c])
            a_ref[_group(part), cols] = (_silu(g) * u).astype(BF16)
            yield
    fs = []
    for part in range(len(xs)):
        fs.append(_dot(a_ref[_group(part), :], wd_ref[...]))
        yield
    outs.extend(x + 0.5 * _rms(f, nwb) for x, f in zip(xs, fs))


def _ffn_core(xs, nwa, nwb, wg_ref, wu_ref, wd_ref, a_ref):
    outs = []
    _drain(_ffn_stages(xs, nwa, nwb, wg_ref, wu_ref, wd_ref, a_ref, outs))
    return outs


CONV_HALO = 8


def _ffn_proj_body(*refs, n_rows, embed_blocks, n_out, conv_out, lp, tail_valid, prologue):
    lead = n_rows + bool(embed_blocks) + (conv_out is not None) + 2
    wg_s, wu_s, wd_s, win_s, wtail_s = refs[-5:]
    nwm_ref, win_blk, wtail_blk = refs[lead + 3:lead + 6]
    _prepare_weights(refs[lead:lead + 3], (wg_s, wu_s, wd_s))
    step = pl.program_id(0) - prologue

    @pl.when(pl.program_id(0) < win_s.shape[0])
    def _():
        win_s[pl.program_id(0)] = win_blk[...].astype(BF16)
        lane = lax.broadcasted_iota(jnp.int32, wtail_blk.shape, 1)
        wtail_s[...] = jnp.where(lane < tail_valid, wtail_blk[...], 0.0).astype(BF16)

    refs = refs[:lead] + (wg_s, wu_s, wd_s, nwm_ref, win_s, wtail_s) + refs[lead + 6:-5]

    @pl.when(step >= 0)
    def _():
        _ffn_proj_step(step, *refs, n_rows=n_rows, embed_blocks=embed_blocks, n_out=n_out,
                       conv_out=conv_out, lp=lp)


def _ffn_proj_step(step, *refs, n_rows, embed_blocks, n_out, conv_out, lp):
    row_refs = refs[:n_rows]
    refs = refs[n_rows:]
    if embed_blocks:
        meta_ref, refs = refs[0], refs[1:]
    if conv_out is not None:
        convw_ref, refs = refs[0], refs[1:]
        xpad_ref = refs[10 + n_out]

        @pl.when(step == 0)
        def _():
            xpad_ref[0:CONV_HALO, :] = jnp.zeros((CONV_HALO, xpad_ref.shape[1]), F32)

    nwa_ref, nwb_ref, wg_ref, wu_ref, wd_ref, nwm_ref, win_ref, wtail_ref = refs[:8]
    h_ref = refs[8]
    out_refs = refs[9:9 + n_out]
    a_ref = refs[9 + n_out]
    offsets = np.cumsum([0] + [o.shape[1] for o in out_refs])
    xs = []
    for part in range(ROW_SPLIT):
        if embed_blocks:
            per = n_rows // ROW_SPLIT
            first = step * SUB + part * per
            xs.append(jnp.concatenate(
                [jnp.where((first + j) % embed_blocks == 0, meta_ref[...], r[...])
                 for j, r in enumerate(row_refs[part * per:(part + 1) * per])], axis=0))
        else:
            xs.append(_load_rows(row_refs, part))
    hs = _ffn_core(xs, nwa_ref[...], nwb_ref[...], wg_ref, wu_ref, wd_ref, a_ref)
    hns = [_rms(h, nwm_ref[...]).astype(BF16) for h in hs]
    width = win_ref.shape[2]
    for part, (h, hn) in enumerate(zip(hs, hns)):
        rows = _group(part)
        h_ref[rows, :] = h
        for c in range(win_ref.shape[0]):
            p = _dot(hn, win_ref[c])
            for i in range(n_out - 1):
                lo = max(c * width, offsets[i])
                hi = min((c + 1) * width, offsets[i + 1])
                if lo >= hi:
                    continue
                piece = p[:, lo - c * width:hi - c * width]
                cols = slice(lo - offsets[i], hi - offsets[i])
                if i == conv_out:
                    xpad_ref[CONV_HALO + part * GROUP_ROWS:CONV_HALO + (part + 1) * GROUP_ROWS,
                             cols] = piece
                else:
                    out_refs[i][rows, cols] = piece.astype(out_refs[i].dtype)
        out_refs[-1][rows, :] = _dot(hn, wtail_ref[...]).astype(out_refs[-1].dtype)

    if conv_out is not None:
        taps = convw_ref.shape[0]
        cw = convw_ref[...]
        y = cw[taps - 1:taps, :] * xpad_ref[CONV_HALO:, :]
        for j in range(taps - 1):
            lo = CONV_HALO - (taps - 1 - j)
            y = y + cw[j:j + 1, :] * xpad_ref[lo:lo + ROW_TILE, :]
        tail = xpad_ref[ROW_TILE:, :]
        row = (step * ROW_TILE) % lp + lax.broadcasted_iota(jnp.int32, (ROW_TILE, 1), 0)
        row = jnp.where(row >= lp, row - lp, row)
        out_refs[conv_out][...] = jnp.where(row >= PAD, _silu(y), 0.0).astype(
            out_refs[conv_out].dtype)
        xpad_ref[0:CONV_HALO, :] = tail


def _ffn_proj(src, meta_block, norm4, layer, wg, wu, wd, w_in, index, widths, dtypes, rows, lp,
              seq, conv_out=None, conv_w=None):
    d = norm4.shape[-1]
    d_ff = wg.shape[-1]
    prep = d_ff // FF_TILE
    main = sum(widths[:-1])
    tail_valid = w_in.shape[2] - main
    assert 0 < tail_valid <= widths[-1] == LANE and main % LANE == 0 and lp > ROW_TILE
    width = FF_TILE
    assert main % width == 0
    win_chunks = main // width
    prep = max(prep, win_chunks)
    w_in = w_in[index]
    win_spec = pl.BlockSpec((d, width), lambda i: (0, jnp.minimum(i, win_chunks - 1)))
    wtail_spec = pl.BlockSpec((d, LANE), lambda i: (0, main // LANE))
    if meta_block is None:
        row_specs, extra, extra_specs, embed_blocks = _tile_specs(d), [], [], 0
    else:
        row_specs = _embed_specs(d, lp, seq)
        extra, extra_specs, embed_blocks = [meta_block], [_resident(meta_block.shape)], (
            lp // SWA_BLOCK)
    scratch = [pltpu.VMEM((ROW_TILE, d_ff), BF16)]
    if conv_out is not None:
        extra, extra_specs = extra + [conv_w], extra_specs + [_resident(conv_w.shape)]
        scratch.append(pltpu.VMEM((CONV_HALO + ROW_TILE, widths[conv_out]), F32))
    tile = lambda n: pl.BlockSpec((ROW_TILE, n), lambda i: (i, 0))
    shift = lambda specs: [_after_prep(s, prep) for s in specs]
    return pl.pallas_call(
        functools.partial(_ffn_proj_body, n_rows=len(row_specs), embed_blocks=embed_blocks,
                          n_out=len(widths), conv_out=conv_out, lp=lp, tail_valid=tail_valid,
                          prologue=prep),
        grid=(prep + rows // ROW_TILE,),
        in_specs=shift(row_specs + extra_specs
                       + [_pick(norm4, layer, 0), _pick(norm4, layer, 1)])
        + _weight_chunk_specs(wg, wd, layer, 0)
        + shift([_pick(norm4, layer, 2)]) + [win_spec, wtail_spec],
        out_specs=shift([tile(d)] + [tile(n) for n in widths]),
        out_shape=[jax.ShapeDtypeStruct((rows, d), F32)]
        + [jax.ShapeDtypeStruct((rows, n), dt) for n, dt in zip(widths, dtypes)],
        scratch_shapes=scratch + _weight_scratch(wg)
        + [pltpu.VMEM((win_chunks, d, width), BF16), pltpu.VMEM((d, LANE), BF16)],
        compiler_params=_params("arbitrary"),
        name="ffn_proj",
    )(*([src] * len(row_specs)), *extra, norm4, norm4, wg, wu, wd, norm4, w_in, w_in)


def _proj_ffn_body(*refs, n_rows, n_x):
    x_groups = [refs[g * n_rows:(g + 1) * n_rows] for g in range(n_x)]
    refs = refs[n_x * n_rows:]
    h_refs = refs[:n_rows]
    (wo_ref, nwo_ref, nwa_ref, nwb_ref, wg_blk, wu_blk, wd_blk, o_ref, a_ref,
     wg_ref, wu_ref, wd_ref) = refs[n_rows:]
    prep = _prepare_weights((wg_blk, wu_blk, wd_blk), (wg_ref, wu_ref, wd_ref))

    @pl.when(pl.program_id(0) >= prep)
    def _():
        accs = []
        for part in range(ROW_SPLIT):
            acc = None
            off = 0
            for group in x_groups:
                xs = _load_rows(group, part)
                k = xs.shape[1]
                p = _dot(xs, wo_ref[off:off + k, :])
                acc = p if acc is None else acc + p
                off += k
            accs.append(acc)
        hs = [_load_rows(h_refs, part) + _rms(acc, nwo_ref[...])
              for part, acc in enumerate(accs)]
        outs = _ffn_core(hs, nwa_ref[...], nwb_ref[...], wg_ref, wu_ref, wd_ref, a_ref)
        for part, out in enumerate(outs):
            o_ref[_group(part), :] = out


def _proj_ffn(xs, h, w_out, norm4, layer, wg, wu, wd, final, batch, lp, seq):
    d = norm4.shape[-1]
    d_ff = wg.shape[-1]
    if final:
        specs = lambda width: _real_specs(width, lp, seq)
        out_rows = batch * seq
    else:
        specs = _tile_specs
        out_rows = batch * lp
    h_specs = specs(d)
    n_rows = len(h_specs)
    in_specs, args = [], []
    for x in xs:
        in_specs += specs(x.shape[1])
        args += [x] * n_rows
    in_specs += h_specs + [_resident(w_out.shape), _pick(norm4, layer, 3),
                           _pick(norm4, layer, 4), _pick(norm4, layer, 5)]
    prep = d_ff // FF_TILE
    in_specs = [_after_prep(s, prep) for s in in_specs] + _weight_chunk_specs(wg, wd, layer, 1)
    args += [h] * n_rows + [w_out, norm4, norm4, norm4, wg, wu, wd]
    return pl.pallas_call(
        functools.partial(_proj_ffn_body, n_rows=n_rows, n_x=len(xs)),
        grid=(prep + out_rows // ROW_TILE,),
        in_specs=in_specs,
        out_specs=_after_prep(pl.BlockSpec((ROW_TILE, d), lambda i: (i, 0)), prep),
        out_shape=jax.ShapeDtypeStruct((out_rows, d), F32),
        scratch_shapes=[pltpu.VMEM((ROW_TILE, d_ff), BF16)] + _weight_scratch(wg),
        compiler_params=_params("arbitrary"),
        name="proj_ffn",
    )(*args)


def _t5_bucket_np(rel):
    n = np.maximum(rel, 0)
    max_exact = REL_BUCKETS // 2
    n_f = np.maximum(n, 1).astype(np.float32)
    large = max_exact + (np.log(n_f / np.float32(max_exact))
                         / np.float32(math.log(REL_MAX_DIST / max_exact))
                         * np.float32(REL_BUCKETS - max_exact)).astype(np.int32)
    large = np.minimum(large, REL_BUCKETS - 1)
    return np.where(n < max_exact, n, large).astype(np.int32)


def _swa_bucket_index():
    i = np.arange(SWA_BLOCK)[:, None]
    c = np.arange(SWA_BLOCK)[None, :]
    out = np.full((3, SWA_BLOCK, 2 * SWA_BLOCK), -1, np.int32)
    for variant in range(3):
        pos_q = variant * SWA_BLOCK + i - PAD
        pos_k = variant * SWA_BLOCK + c - PAD - np.where(c > i, SWA_BLOCK, 0)
        rel_b = pos_q - pos_k
        assert ((rel_b >= 0) & (rel_b < SWA_BLOCK)).all()
        out[variant, :, :SWA_BLOCK] = np.where(pos_k >= N_META, _t5_bucket_np(rel_b), -1)
        rel_m = pos_q - c
        meta = (c < N_META) & (rel_m >= 0)
        out[variant, :, SWA_BLOCK:] = np.where(meta, _t5_bucket_np(rel_m), -1)
    return out


_SWA_BUCKET_INDEX = _swa_bucket_index()


def _bias_body(tbl_ref, idx_ref, o_ref):
    idx = idx_ref[0]
    group = SWA_Q_HEADS // SWA_KV_HEADS
    for head in range(SWA_Q_HEADS):
        acc = jnp.full(idx.shape, NEG_INF, F32)
        for b in range(REL_BUCKETS):
            acc = jnp.where(idx == b, tbl_ref[b, head], acc)
        rows = slice((head % group) * SWA_BLOCK, (head % group + 1) * SWA_BLOCK)
        o_ref[0, head // group, rows, :] = acc


def _swa_bias(rel_table):
    idx = jnp.asarray(_SWA_BUCKET_INDEX)
    group = SWA_Q_HEADS // SWA_KV_HEADS
    width = idx.shape[2]
    return pl.pallas_call(
        _bias_body,
        grid=(3,),
        in_specs=[pl.BlockSpec(memory_space=pltpu.SMEM),
                  pl.BlockSpec((1, SWA_BLOCK, width), lambda v: (v, 0, 0))],
        out_specs=pl.BlockSpec((1, SWA_KV_HEADS, group * SWA_BLOCK, width),
                               lambda v: (v, 0, 0, 0)),
        out_shape=jax.ShapeDtypeStruct((3, SWA_KV_HEADS, group * SWA_BLOCK, width), F32),
        compiler_params=_params("arbitrary"),
        name="swa_bias",
    )(rel_table, idx)


def _swa_stages(sink_ref, q_ref, kc_ref, kp_ref, km_ref, vc_ref, vp_ref, vm_ref, bias_ref,
                step, blocks, store):
    dh = SWA_HEAD_DIM
    blk = SWA_BLOCK
    group = SWA_Q_HEADS // SWA_KV_HEADS
    scale = dh ** -0.5
    zpad = jnp.zeros((blk - N_META, dh), BF16)
    row = lax.broadcasted_iota(jnp.int32, (group * blk, blk), 0) % blk
    col = lax.broadcasted_iota(jnp.int32, (group * blk, blk), 1)
    from_prev = col > row

    ones_col = (lax.broadcasted_iota(jnp.int32, (3 * blk, dh), 1) == 0).astype(BF16)

    units = [(j, hk) for j in range(blocks) for hk in range(SWA_KV_HEADS)]
    scores, values = [], []
    for j, hk in units:
        r = slice(j * blk, (j + 1) * blk)
        kv = slice(hk * dh, (hk + 1) * dh)
        k_prev = kp_ref[:, kv] if j == 0 else kc_ref[(j - 1) * blk:j * blk, kv]
        v_prev = vp_ref[:, kv] if j == 0 else vc_ref[(j - 1) * blk:j * blk, kv]
        k_all = jnp.concatenate([k_prev, kc_ref[r, kv], km_ref[:, kv], zpad], axis=0)
        v_all = jnp.concatenate([v_prev, vc_ref[r, kv], vm_ref[:, kv], zpad], axis=0)
        values.append(jnp.concatenate([v_all, ones_col], axis=1))
        q = jnp.concatenate([q_ref[r, (hk * group + g) * dh:(hk * group + g + 1) * dh]
                             for g in range(group)], axis=0) * scale
        scores.append(_dot_nt(q, k_all))
        yield

    probs, sink_terms = [], []
    for (j, hk), s in zip(units, scores):
        bias = bias_ref[jnp.minimum(step * blocks + j, 2), hk]
        s_band = jnp.where(from_prev, s[:, :blk], s[:, blk:2 * blk]) + bias[:, :blk]
        s_meta = s[:, 2 * blk:] + bias[:, blk:]
        sink = jnp.concatenate([jnp.full((blk, 1), sink_ref[hk * group + g], F32)
                                for g in range(group)], axis=0)
        m = jnp.maximum(jnp.max(jnp.maximum(s_band, s_meta), axis=-1, keepdims=True), sink)
        e_band = jnp.exp(s_band - m)
        e_meta = jnp.exp(s_meta - m)
        sink_terms.append(jnp.exp(sink - m))
        probs.append(jnp.concatenate([jnp.where(from_prev, e_band, 0.0),
                                      jnp.where(from_prev, 0.0, e_band), e_meta],
                                     axis=1).astype(BF16))

    for (j, hk), p, v_ext, sink_term in zip(units, probs, values, sink_terms):
        pv = _dot(p, v_ext)
        o = pv[:, :dh] / (pv[:, dh:dh + 1] + sink_term)
        for g in range(group):
            hq = hk * group + g
            store(slice(j * blk, (j + 1) * blk), slice(hq * dh, (hq + 1) * dh),
                  o[g * blk:(g + 1) * blk])
        yield


def _swa_body(sink_ref, q_ref, kc_ref, kp_ref, km_ref, vc_ref, vp_ref, vm_ref, bias_ref, o_ref,
              *, blocks):
    def store(rows, cols, value):
        o_ref[rows, cols] = value.astype(o_ref.dtype)

    _drain(_swa_stages(sink_ref, q_ref, kc_ref, kp_ref, km_ref, vc_ref, vp_ref, vm_ref,
                       bias_ref, pl.program_id(1), blocks, store))


def _swa(q, k, v, bias, sinks, batch, lp, blocks):
    rows = q.shape[0]
    nb = lp // SWA_BLOCK
    steps = nb // blocks
    meta_blocks = lp // N_META
    cur = lambda b, s: (b * steps + s, 0)
    prev = lambda b, s: (b * nb + jnp.maximum(s * blocks - 1, 0), 0)
    meta = lambda b, s: (b * meta_blocks + PAD // N_META, 0)
    kv_w = k.shape[1]
    rs = blocks * SWA_BLOCK
    return pl.pallas_call(
        functools.partial(_swa_body, blocks=blocks),
        grid=(batch, steps),
        in_specs=[pl.BlockSpec(memory_space=pltpu.SMEM),
                  pl.BlockSpec((rs, q.shape[1]), cur),
                  pl.BlockSpec((rs, kv_w), cur), pl.BlockSpec((SWA_BLOCK, kv_w), prev),
                  pl.BlockSpec((N_META, kv_w), meta),
                  pl.BlockSpec((rs, kv_w), cur), pl.BlockSpec((SWA_BLOCK, kv_w), prev),
                  pl.BlockSpec((N_META, kv_w), meta),
                  _resident(bias.shape)],
        out_specs=pl.BlockSpec((rs, q.shape[1]), cur),
        out_shape=jax.ShapeDtypeStruct((rows, q.shape[1]), BF16),
        compiler_params=_params("parallel", "arbitrary"),
        name="swa",
    )(sinks, q, k, k, k, v, v, v, bias)


def _tri_masks():
    row = lax.broadcasted_iota(jnp.int32, (CHUNK, CHUNK), 0)
    col = lax.broadcasted_iota(jnp.int32, (CHUNK, CHUNK), 1)
    return row >= col, row > col, row == col


def _deltanet_stages(act_ref, z_ref, gates_ref, alog_ref, dtb_ref, nw_ref, s_ref, chunks,
                     store):
    nh, dh = DN_HEADS, DN_HEAD_DIM
    gt = gates_ref[...]
    beta_full = jax.nn.sigmoid(gt)
    g_full = -jnp.exp(alog_ref[...]) * _softplus(gt + dtb_ref[...])
    incl, strict, diag = _tri_masks()
    eye = diag.astype(F32)
    tri = incl.astype(F32)
    gc_chunks = [_dot(tri, g_full[c * CHUNK:(c + 1) * CHUNK, :], HIGHEST)
                 for c in range(chunks)]
    items = [(c, h) for c in range(chunks) for h in range(nh)]
    yield

    neg_a, akt16, qd16, rhs16, decay = [], [], [], [], []
    for c, h in items:
        if h == 0 and c % 2 == 0 and c > 0:
            yield
        r = slice(c * CHUNK, (c + 1) * CHUNK)
        q = act_ref[r, h * dh:(h + 1) * dh].astype(F32)
        k = act_ref[r, DN_DIM + h * dh:DN_DIM + (h + 1) * dh].astype(F32)
        v = act_ref[r, 2 * DN_DIM + h * dh:2 * DN_DIM + (h + 1) * dh].astype(F32)
        q = q * lax.rsqrt(jnp.sum(q * q, axis=-1, keepdims=True) + 1e-6) * dh ** -0.5
        k = k * lax.rsqrt(jnp.sum(k * k, axis=-1, keepdims=True) + 1e-6)
        beta = jnp.broadcast_to(beta_full[r, h:h + 1], (CHUNK, dh))
        gc = jnp.broadcast_to(gc_chunks[c][:, nh + h:nh + h + 1], (CHUNK, dh))
        gc_row = jnp.transpose(gc)[:CHUNK, :]
        gamma = jnp.where(incl, jnp.exp(jnp.where(incl, gc[:, :CHUNK] - gc_row, 0.0)), 0.0)
        k_beta = k * beta
        kq = _dot_nt(jnp.concatenate([k_beta, q], axis=0).astype(BF16), k.astype(BF16))
        neg_a.append(jnp.where(strict, -(kq[:CHUNK] * gamma), 0.0))
        attn16 = jnp.where(incl, kq[CHUNK:] * gamma, 0.0).astype(BF16)
        gc_last = gc[CHUNK - 1:CHUNK, :]
        qd16.append((q * jnp.exp(gc)).astype(BF16))
        kdt16 = jnp.transpose(k * jnp.exp(gc_last - gc)).astype(BF16)
        akt16.append(jnp.concatenate([attn16, kdt16], axis=0))
        rhs16.append(jnp.concatenate([v * beta, k_beta * jnp.exp(gc)], axis=1).astype(BF16))
        decay.append(jnp.exp(gc_last))

    yield
    levels = int(math.log2(CHUNK)) - 1
    t = [eye + x for x in neg_a]
    p16 = [x.astype(BF16) for x in neg_a]
    p16 = [_dot(x, x).astype(BF16) for x in p16]
    yield
    for level in range(levels):
        if level + 1 < levels:
            tp = [_dot(jnp.concatenate([y.astype(BF16), x], axis=0), x) for y, x in zip(t, p16)]
            t = [y + r[:CHUNK] for y, r in zip(t, tp)]
            p16 = [r[CHUNK:].astype(BF16) for r in tp]
        else:
            t = [y + _dot(y.astype(BF16), x) for y, x in zip(t, p16)]
        yield
    uw = [_dot(y.astype(BF16), b) for y, b in zip(t, rhs16)]
    wq16 = [jnp.concatenate([x[:, dh:].astype(BF16), y], axis=0)
            for x, y in zip(uw, qd16)]
    yield

    nw = nw_ref[...]
    state = [s_ref[h] for h in range(nh)]
    for c in range(chunks):
        r = slice(c * CHUNK, (c + 1) * CHUNK)
        pair = [c * nh + h for h in range(nh)]
        s16 = [s.astype(BF16) for s in state]
        wqs = [_dot(wq16[i], s16[h]) for h, i in enumerate(pair)]
        yield
        vn16 = [(uw[i][:, :dh] - wqs[h][:CHUNK]).astype(BF16) for h, i in enumerate(pair)]
        ak = [_dot(akt16[i], vn16[h]) for h, i in enumerate(pair)]
        state = [state[h] * decay[i] + ak[h][CHUNK:] for h, i in enumerate(pair)]
        for h in range(nh):
            hc = slice(h * dh, (h + 1) * dh)
            out = wqs[h][CHUNK:] + ak[h][:CHUNK]
            store(r, hc, _rms(out, nw) * _silu(z_ref[r, hc].astype(F32)))
        yield
    for h in range(nh):
        s_ref[h] = state[h]


def _deltanet_body(act_ref, z_ref, gates_ref, alog_ref, dtb_ref, nw_ref, o_ref, s_ref, *,
                   chunks):
    @pl.when(pl.program_id(1) == 0)
    def _():
        s_ref[...] = jnp.zeros_like(s_ref)

    def store(rows, cols, value):
        o_ref[rows, cols] = value.astype(o_ref.dtype)

    _drain(_deltanet_stages(act_ref, z_ref, gates_ref, alog_ref, dtb_ref, nw_ref, s_ref,
                            chunks, store))


def _deltanet(act, z, gates, alog_vec, dtb_vec, nw, batch, lp, chunks):
    rows = act.shape[0]
    rs = chunks * CHUNK
    steps = lp // rs
    idx = lambda b, s: (b * steps + s, 0)
    return pl.pallas_call(
        functools.partial(_deltanet_body, chunks=chunks),
        grid=(batch, steps),
        in_specs=[pl.BlockSpec((rs, act.shape[1]), idx), pl.BlockSpec((rs, z.shape[1]), idx),
                  pl.BlockSpec((rs, gates.shape[1]), idx),
                  _resident(alog_vec.shape), _resident(dtb_vec.shape), _resident(nw.shape)],
        out_specs=pl.BlockSpec((rs, DN_DIM), idx),
        out_shape=jax.ShapeDtypeStruct((rows, DN_DIM), BF16),
        scratch_shapes=[pltpu.VMEM((DN_HEADS, DN_HEAD_DIM, DN_HEAD_DIM), F32)],
        compiler_params=_params("parallel", "arbitrary"),
        name="deltanet",
    )(act, z, gates, alog_vec, dtb_vec, nw)


def _mix_ffn_body(*refs, blocks, chunks, steps, last):
    *refs, wg_s, wu_s, wd_s = refs
    prep = _prepare_weights(refs[20:23], (wg_s, wu_s, wd_s))
    t = pl.program_id(0) - prep
    refs = refs[:20] + [wg_s, wu_s, wd_s] + refs[23:]

    @pl.when(t >= 0)
    def _():
        _mix_ffn_step(t, *refs, blocks=blocks, chunks=chunks, steps=steps, last=last)


def _mix_ffn_step(t, sink_ref, q_ref, kc_ref, kp_ref, km_ref, vc_ref, vp_ref, vm_ref, bias_ref,
                  act_ref, z_ref, gates_ref, alog_ref, dtb_ref, dnw_ref,
                  h_ref, wo_ref, nwo_ref, nwa_ref, nwb_ref, wg_ref, wu_ref, wd_ref,
                  o_ref, s_ref, mix_ref, a_ref, *, blocks, chunks, steps, last):
    seq_step = jnp.minimum(t, last) % steps
    write_slot = t % 2
    read_slot = 1 - write_slot
    swa_width = q_ref.shape[1]
    groups = q_ref.shape[0] // GROUP_ROWS

    @pl.when(t == 0)
    def _():
        mix_ref[...] = jnp.zeros_like(mix_ref)

    @pl.when(seq_step == 0)
    def _():
        s_ref[...] = jnp.zeros_like(s_ref)

    def store_swa(rows, cols, value):
        mix_ref[write_slot, rows, cols] = value.astype(mix_ref.dtype)

    def store_dn(rows, cols, value):
        cols = slice(swa_width + cols.start, swa_width + cols.stop)
        mix_ref[write_slot, rows, cols] = value.astype(mix_ref.dtype)

    def mixer_stages():
        dn = _deltanet_stages(act_ref, z_ref, gates_ref, alog_ref, dtb_ref, dnw_ref,
                              s_ref, chunks, store_dn)
        swa = _swa_stages(sink_ref, q_ref, kc_ref, kp_ref, km_ref, vc_ref, vp_ref,
                          vm_ref, bias_ref, seq_step, blocks, store_swa)
        for _ in dn:
            yield
            if next(swa, StopIteration) is not StopIteration:
                yield
        yield from swa

    def ffn_stages():
        accs = []
        for part in range(groups):
            accs.append(_dot(mix_ref[read_slot, _group(part), :], wo_ref[...]))
            yield
        hs = [h_ref[_group(part), :] + _rms(acc, nwo_ref[...]) for part, acc in enumerate(accs)]
        outs = []
        yield from _ffn_stages(hs, nwa_ref[...], nwb_ref[...], wg_ref, wu_ref, wd_ref, a_ref,
                               outs)
        for part, out in enumerate(outs):
            o_ref[_group(part), :] = out

    _interleave(ffn_stages(), mixer_stages())


def _mix_ffn(qa, ka, va, bias, sinks, act, z, gates, alog_vec, dtb_vec, dn_nw, h, w_out, norm4,
             layer, wg, wu, wd, batch, lp, blocks, chunks):
    d = norm4.shape[-1]
    d_ff = wg.shape[-1]
    rs = blocks * SWA_BLOCK
    assert rs == chunks * CHUNK and rs % GROUP_ROWS == 0
    nb = lp // SWA_BLOCK
    steps = lp // rs
    last = batch * steps - 1
    meta_blocks = lp // N_META

    def tile(t):
        return jnp.minimum(t, last)

    cur = lambda t: (tile(t), 0)
    prev = lambda t: ((tile(t) // steps) * nb + jnp.maximum((tile(t) % steps) * blocks - 1, 0), 0)
    meta = lambda t: ((tile(t) // steps) * meta_blocks + PAD // N_META, 0)
    behind = lambda t: (jnp.maximum(t - 1, 0), 0)
    kv_w = ka.shape[1]
    prep = d_ff // FF_TILE
    in_specs = [pl.BlockSpec(memory_space=pltpu.SMEM),
                pl.BlockSpec((rs, qa.shape[1]), cur),
                pl.BlockSpec((rs, kv_w), cur), pl.BlockSpec((SWA_BLOCK, kv_w), prev),
                pl.BlockSpec((N_META, kv_w), meta),
                pl.BlockSpec((rs, kv_w), cur), pl.BlockSpec((SWA_BLOCK, kv_w), prev),
                pl.BlockSpec((N_META, kv_w), meta),
                _resident(bias.shape),
                pl.BlockSpec((rs, act.shape[1]), cur), pl.BlockSpec((rs, z.shape[1]), cur),
                pl.BlockSpec((rs, gates.shape[1]), cur),
                _resident(alog_vec.shape), _resident(dtb_vec.shape), _resident(dn_nw.shape),
                pl.BlockSpec((rs, d), behind), _resident(w_out.shape),
                _pick(norm4, layer, 3), _pick(norm4, layer, 4), _pick(norm4, layer, 5)]
    return pl.pallas_call(
        functools.partial(_mix_ffn_body, blocks=blocks, chunks=chunks, steps=steps, last=last),
        grid=(prep + last + 2,),
        in_specs=[_after_prep(s, prep) for s in in_specs]
        + _weight_chunk_specs(wg, wd, layer, 1),
        out_specs=_after_prep(pl.BlockSpec((rs, d), behind), prep),
        out_shape=jax.ShapeDtypeStruct(h.shape, F32),
        scratch_shapes=[pltpu.VMEM((DN_HEADS, DN_HEAD_DIM, DN_HEAD_DIM), F32),
                        pltpu.VMEM((2, rs, qa.shape[1] + DN_DIM), BF16),
                        pltpu.VMEM((rs, d_ff), BF16)] + _weight_scratch(wg),
        compiler_params=_params("arbitrary"),
        name="mix_ffn",
    )(sinks, qa, ka, ka, ka, va, va, va, bias, act, z, gates, alog_vec, dtb_vec, dn_nw,
      h, w_out, norm4, norm4, norm4, wg, wu, wd)


def _gla_body(q_ref, k_ref, v_ref, g_ref, gk_ref, wgu_ref, bg_ref, nw_ref, o_ref, s_ref, *,
              chunks):
    nh = GLA_HEADS
    dk = q_ref.shape[1] // nh
    dv = v_ref.shape[1] // nh

    @pl.when(pl.program_id(1) == 0)
    def _():
        s_ref[...] = jnp.zeros_like(s_ref)

    logits = _dot(gk_ref[...].astype(BF16), wgu_ref[...]) + bg_ref[...]
    glog = (jnp.minimum(logits, 0.0) - jnp.log(1.0 + jnp.exp(-jnp.abs(logits)))) / GLA_GATE_NORM
    incl, _, _ = _tri_masks()
    tri = incl.astype(F32)
    bcum = [_dot(tri, glog[c * CHUNK:(c + 1) * CHUNK, :], HIGHEST) for c in range(chunks)]

    items = [(c, h) for c in range(chunks) for h in range(nh)]
    qd16, kn16, kdt16, decay = [], [], [], []
    for c, h in items:
        r = slice(c * CHUNK, (c + 1) * CHUNK)
        kc = slice(h * dk, (h + 1) * dk)
        bc = bcum[c][:, kc]
        q = q_ref[r, kc].astype(F32) * dk ** -0.5
        k = k_ref[r, kc].astype(F32)
        b_last = bc[CHUNK - 1:CHUNK, :]
        qd16.append((q * jnp.exp(bc)).astype(BF16))
        kn16.append((k * jnp.exp(-bc)).astype(BF16))
        kdt16.append(jnp.transpose(k * jnp.exp(b_last - bc)).astype(BF16))
        decay_rows = jnp.broadcast_to(jnp.exp(b_last), (8, dk))
        decay.append(jnp.transpose(decay_rows)[:, :1])
    values = [v_ref[c * CHUNK:(c + 1) * CHUNK, h * dv:(h + 1) * dv] for c, h in items]
    attn16 = [jnp.where(incl, _dot_nt(a, b), 0.0).astype(BF16) for a, b in zip(qd16, kn16)]
    intra = [_dot(a, v) for a, v in zip(attn16, values)]
    update = [_dot(kt, v) for kt, v in zip(kdt16, values)]

    nw = nw_ref[...]
    state = [s_ref[h] for h in range(nh)]
    for c in range(chunks):
        r = slice(c * CHUNK, (c + 1) * CHUNK)
        for h in range(nh):
            i = c * nh + h
            vc = slice(h * dv, (h + 1) * dv)
            o = intra[i] + _dot(qd16[i], state[h].astype(BF16))
            state[h] = state[h] * decay[i] + update[i]
            o = _rms(o, nw) * _silu(g_ref[r, vc].astype(F32))
            o_ref[r, vc] = o.astype(o_ref.dtype)
    for h in range(nh):
        s_ref[h] = state[h]


def _gla(q, k, v, g, gk, wgu, bg, nw, batch, lp, chunks):
    rows = q.shape[0]
    rs = chunks * CHUNK
    steps = lp // rs
    idx = lambda b, s: (b * steps + s, 0)
    dk = q.shape[1] // GLA_HEADS
    dv = v.shape[1] // GLA_HEADS
    return pl.pallas_call(
        functools.partial(_gla_body, chunks=chunks),
        grid=(batch, steps),
        in_specs=[pl.BlockSpec((rs, q.shape[1]), idx), pl.BlockSpec((rs, k.shape[1]), idx),
                  pl.BlockSpec((rs, v.shape[1]), idx), pl.BlockSpec((rs, g.shape[1]), idx),
                  pl.BlockSpec((rs, gk.shape[1]), idx), _resident(wgu.shape),
                  _resident(bg.shape), _resident(nw.shape)],
        out_specs=pl.BlockSpec((rs, v.shape[1]), idx),
        out_shape=jax.ShapeDtypeStruct((rows, v.shape[1]), BF16),
        scratch_shapes=[pltpu.VMEM((GLA_HEADS, dk, dv), F32)],
        compiler_params=_params("parallel", "arbitrary"),
        name="gla",
    )(q, k, v, g, gk, wgu, bg, nw)


def _per_step(n, preferred):
    for c in preferred:
        if n % c == 0:
            return c
    return 1


def kernel(x, meta_tokens, norm_w, ffn_w_gate, ffn_w_up, ffn_w_down, rel_bias_table,
           even_w_in, even_conv_w, swa_sinks, dn_a_log, dn_dt_bias, dn_norm_w, even_w_out,
           odd_w_in, gla_w_gate_up, gla_b_gate, gla_norm_w, odd_w_out):
    batch, seq, d = x.shape
    depth = norm_w.shape[0]
    lp = SWA_BLOCK + seq
    assert seq % ROW_TILE == 0 and (batch * lp) % ROW_TILE == 0
    rows = batch * lp
    chunks = _per_step(lp // CHUNK, (6, 3, 2))
    swa_blocks = _per_step(lp // SWA_BLOCK, (3, 2))

    meta_block = jnp.pad(meta_tokens.astype(x.dtype), ((PAD, 0), (0, 0)))
    norm4 = norm_w[:, :, None, :]
    wg, wu, wd = ffn_w_gate, ffn_w_up, ffn_w_down

    swa_q = SWA_Q_HEADS * SWA_HEAD_DIM
    swa_kv = SWA_KV_HEADS * SWA_HEAD_DIM
    bias = None
    h = x.reshape(batch * seq, d)
    for layer in range(depth):
        i = layer // 2
        embed = meta_block if layer == 0 else None
        final = layer == depth - 1
        if layer % 2 == 0:
            assert even_w_in.shape[2] == swa_q + 2 * swa_kv + 4 * DN_DIM + 2 * DN_HEADS
            widths = (swa_q, swa_kv, swa_kv, 3 * DN_DIM, DN_DIM, LANE)
            h, qa, ka, va, act_b, z_b, gates = _ffn_proj(
                h, embed, norm4, layer, wg, wu, wd, even_w_in, i, widths,
                (BF16, BF16, BF16, BF16, BF16, F32), rows, lp, seq,
                conv_out=3, conv_w=even_conv_w[i])
            if bias is None:
                bias = _swa_bias(rel_bias_table)
            lane_pad = (DN_HEADS, LANE - 2 * DN_HEADS)
            alog_vec = jnp.pad(dn_a_log[i], lane_pad)[None, :]
            dtb_vec = jnp.pad(dn_dt_bias[i], lane_pad)[None, :]
            dn_nw = dn_norm_w[i][None, :]
            w_out = even_w_out[i]
            if not final and swa_blocks * SWA_BLOCK == chunks * CHUNK:
                h = _mix_ffn(qa, ka, va, bias, swa_sinks[i], act_b, z_b, gates, alog_vec,
                             dtb_vec, dn_nw, h, w_out.astype(BF16), norm4, layer, wg, wu, wd,
                             batch, lp, swa_blocks, chunks)
                continue
            o_a = _swa(qa, ka, va, bias, swa_sinks[i], batch, lp, swa_blocks)
            o_b = _deltanet(act_b, z_b, gates, alog_vec, dtb_vec, dn_nw, batch, lp, chunks)
            mixed = [o_a, o_b]
        else:
            key_dim = gla_w_gate_up.shape[2]
            val_dim = odd_w_out.shape[1]
            widths = (key_dim, key_dim, val_dim, val_dim, LANE)
            h, q, k, v, g, gk = _ffn_proj(
                h, embed, norm4, layer, wg, wu, wd, odd_w_in, i, widths,
                (BF16, BF16, BF16, BF16, F32), rows, lp, seq)
            wgu = jnp.pad(gla_w_gate_up[i], ((0, LANE - GLA_GATE_RANK), (0, 0))).astype(BF16)
            o = _gla(q, k, v, g, gk, wgu, gla_b_gate[i][None, :], gla_norm_w[i][None, :],
                     batch, lp, chunks)
            mixed, w_out = [o], odd_w_out[i]
        h = _proj_ffn(mixed, h, w_out.astype(BF16), norm4, layer, wg, wu, wd, final,
                      batch, lp, seq)
    return h.reshape(batch, seq, d)
```

```python
import functools
import math

import numpy as np
import jax
import jax.numpy as jnp
from jax import lax
from jax.experimental import pallas as pl
from jax.experimental.pallas import tpu as pltpu

F32 = jnp.float32
BF16 = jnp.bfloat16
HIGHEST = lax.Precision.HIGHEST

N_META = 16
NORM_EPS = 1e-6
NEG_INF = -1e30
SWA_Q_HEADS = 8
SWA_KV_HEADS = 2
SWA_HEAD_DIM = 64
SWA_BLOCK = 128
REL_BUCKETS = 32
REL_MAX_DIST = 128
DN_HEADS = 4
DN_HEAD_DIM = 128
DN_DIM = DN_HEADS * DN_HEAD_DIM
DN_CONV = 4
GLA_HEADS = 4
GLA_GATE_RANK = 16
GLA_GATE_NORM = 16.0
CHUNK = 64

PAD = SWA_BLOCK - N_META
LANE = 128
ROW_TILE = 512
FF_TILE = 256
VMEM_LIMIT = 60 * 1024 * 1024


def _rms(x, w):
    return x * lax.rsqrt(jnp.mean(x * x, axis=-1, keepdims=True) + NORM_EPS) * w


def _silu(x):
    return x * jax.nn.sigmoid(x)


def _softplus(x):
    return jnp.maximum(x, 0.0) + jnp.log(1.0 + jnp.exp(-jnp.abs(x)))


def _dot(a, b, precision=None):
    return jnp.dot(a, b, preferred_element_type=F32, precision=precision)


def _dot_nt(a, b, precision=None):
    return lax.dot_general(a, b, (((1,), (1,)), ((), ())), preferred_element_type=F32,
                           precision=precision)


def _dot_tn(a, b, precision=None):
    return lax.dot_general(a, b, (((0,), (0,)), ((), ())), preferred_element_type=F32,
                           precision=precision)


def _resident(shape):
    nd = len(shape)
    return pl.BlockSpec(shape, lambda *_: (0,) * nd, pipeline_mode=pl.Buffered(1))


def _params(*semantics):
    return pltpu.CompilerParams(dimension_semantics=semantics, vmem_limit_bytes=VMEM_LIMIT)


SUB = ROW_TILE // SWA_BLOCK


def _pick(arr, *lead):
    rest = arr.shape[len(lead):]
    index = tuple(lead) + (0,) * len(rest)
    return pl.BlockSpec((None,) * len(lead) + rest, lambda i: index,
                        pipeline_mode=pl.Buffered(1))


def _after_prep(spec, prep):
    if spec.index_map is None:
        return spec
    index = spec.index_map
    return pl.BlockSpec(spec.block_shape, lambda i: index(jnp.maximum(i - prep, 0)),
                        pipeline_mode=spec.pipeline_mode)


def _weight_chunk_specs(wg, wd, layer, which):
    d, d_ff = wg.shape[-2:]
    last = d_ff // FF_TILE - 1
    cols = pl.BlockSpec((None, None, d, FF_TILE), lambda i: (layer, which, 0, jnp.minimum(i, last)))
    rows = pl.BlockSpec((None, None, FF_TILE, d), lambda i: (layer, which, jnp.minimum(i, last), 0))
    return [cols, cols, rows]


def _weight_scratch(wg):
    d, d_ff = wg.shape[-2:]
    chunks = d_ff // FF_TILE
    return [pltpu.VMEM((chunks, d, FF_TILE), BF16), pltpu.VMEM((chunks, d, FF_TILE), BF16),
            pltpu.VMEM((d_ff, d), BF16)]


def _prepare_weights(chunk_refs, scratch_refs):
    wg_blk, wu_blk, wd_blk = chunk_refs
    wg_s, wu_s, wd_s = scratch_refs
    prep = wg_s.shape[0]
    i = pl.program_id(0)

    @pl.when(i < prep)
    def _():
        wg_s[i] = wg_blk[...].astype(BF16)
        wu_s[i] = wu_blk[...].astype(BF16)
        wd_s[pl.ds(pl.multiple_of(i * FF_TILE, FF_TILE), FF_TILE), :] = wd_blk[...].astype(BF16)

    return prep


def _tile_specs(width):
    return [pl.BlockSpec((ROW_TILE, width), lambda i: (i, 0))]


def _real_specs(width, lp, seq):
    tiles = seq // ROW_TILE
    blocks = lp // SWA_BLOCK

    def spec(j):
        return pl.BlockSpec((SWA_BLOCK, width),
                            lambda i: ((i // tiles) * blocks + 1 + (i % tiles) * SUB + j, 0))
    return [spec(j) for j in range(SUB)]


def _embed_specs(width, lp, seq):
    blocks = lp // SWA_BLOCK
    x_blocks = seq // SWA_BLOCK

    def spec(j):
        def index(i):
            blk = i * SUB + j
            return ((blk // blocks) * x_blocks + jnp.maximum(blk % blocks - 1, 0), 0)
        return pl.BlockSpec((SWA_BLOCK, width), index)
    return [spec(j) for j in range(SUB)]


ROW_SPLIT = 4
GROUP_ROWS = ROW_TILE // ROW_SPLIT


def _group(part):
    return slice(part * GROUP_ROWS, (part + 1) * GROUP_ROWS)


def _load_rows(refs, part):
    if len(refs) == 1:
        return refs[0][_group(part), :]
    per = len(refs) // ROW_SPLIT
    return jnp.concatenate([r[...] for r in refs[part * per:(part + 1) * per]], axis=0)


def _drain(stages):
    for _ in stages:
        pass


def _interleave(main, side):
    for _ in main:
        next(side, None)
    _drain(side)


def _ffn_stages(xs, nwa, nwb, wg_ref, wu_ref, wd_ref, a_ref, outs):
    hns = [_rms(x, nwa).astype(BF16) for x in xs]
    for part, hn in enumerate(hns):
        for c in range(wg_ref.shape[0]):
            cols = slice(c * FF_TILE, (c + 1) * FF_TILE)
            g = _dot(hn, wg_ref[c])
            u = _dot(hn, wu_ref[c])
            a_ref[_group(part), cols] = (_silu(g) * u).astype(BF16)
            yield
    fs = []
    for part in range(len(xs)):
        fs.append(_dot(a_ref[_group(part), :], wd_ref[...]))
        yield
    outs.extend(x + 0.5 * _rms(f, nwb) for x, f in zip(xs, fs))


def _ffn_core(xs, nwa, nwb, wg_ref, wu_ref, wd_ref, a_ref):
    outs = []
    _drain(_ffn_stages(xs, nwa, nwb, wg_ref, wu_ref, wd_ref, a_ref, outs))
    return outs


CONV_HALO = 8


def _ffn_proj_body(*refs, n_rows, embed_blocks, n_out, conv_out, lp, tail_valid, prologue):
    lead = n_rows + bool(embed_blocks) + (conv_out is not None) + 2
    wg_s, wu_s, wd_s, win_s, wtail_s = refs[-5:]
    nwm_ref, win_blk = refs[lead + 3:lead + 5]
    _prepare_weights(refs[lead:lead + 3], (wg_s, wu_s, wd_s))
    step = pl.program_id(0) - prologue
    win_chunks = win_s.shape[0]

    @pl.when(pl.program_id(0) < win_chunks)
    def _():
        win_s[pl.program_id(0)] = win_blk[...].astype(BF16)

    @pl.when(pl.program_id(0) == win_chunks)
    def _():
        tail = win_blk[:, :LANE]
        lane = lax.broadcasted_iota(jnp.int32, tail.shape, 1)
        wtail_s[...] = jnp.where(lane < tail_valid, tail, 0.0).astype(BF16)

    refs = refs[:lead] + (wg_s, wu_s, wd_s, nwm_ref, win_s, wtail_s) + refs[lead + 5:-5]

    @pl.when(step >= 0)
    def _():
        _ffn_proj_step(step, *refs, n_rows=n_rows, embed_blocks=embed_blocks, n_out=n_out,
                       conv_out=conv_out, lp=lp)


def _ffn_proj_step(step, *refs, n_rows, embed_blocks, n_out, conv_out, lp):
    row_refs = refs[:n_rows]
    refs = refs[n_rows:]
    if embed_blocks:
        meta_ref, refs = refs[0], refs[1:]
    if conv_out is not None:
        convw_ref, refs = refs[0], refs[1:]
        xpad_ref = refs[10 + n_out]

        @pl.when(step == 0)
        def _():
            xpad_ref[0:CONV_HALO, :] = jnp.zeros((CONV_HALO, xpad_ref.shape[1]), F32)

    nwa_ref, nwb_ref, wg_ref, wu_ref, wd_ref, nwm_ref, win_ref, wtail_ref = refs[:8]
    h_ref = refs[8]
    out_refs = refs[9:9 + n_out]
    a_ref = refs[9 + n_out]
    offsets = np.cumsum([0] + [o.shape[1] for o in out_refs])
    xs = []
    for part in range(ROW_SPLIT):
        if embed_blocks:
            per = n_rows // ROW_SPLIT
            first = step * SUB + part * per
            xs.append(jnp.concatenate(
                [jnp.where((first + j) % embed_blocks == 0, meta_ref[...], r[...])
                 for j, r in enumerate(row_refs[part * per:(part + 1) * per])], axis=0))
        else:
            xs.append(_load_rows(row_refs, part))
    hs = _ffn_core(xs, nwa_ref[...], nwb_ref[...], wg_ref, wu_ref, wd_ref, a_ref)
    hns = [_rms(h, nwm_ref[...]).astype(BF16) for h in hs]
    width = win_ref.shape[2]
    for part, (h, hn) in enumerate(zip(hs, hns)):
        rows = _group(part)
        h_ref[rows, :] = h
        for c in range(win_ref.shape[0]):
            p = _dot(hn, win_ref[c])
            for i in range(n_out - 1):
                lo = max(c * width, offsets[i])
                hi = min((c + 1) * width, offsets[i + 1])
                if lo >= hi:
                    continue
                piece = p[:, lo - c * width:hi - c * width]
                cols = slice(lo - offsets[i], hi - offsets[i])
                if i == conv_out:
                    xpad_ref[CONV_HALO + part * GROUP_ROWS:CONV_HALO + (part + 1) * GROUP_ROWS,
                             cols] = piece
                else:
                    out_refs[i][rows, cols] = piece.astype(out_refs[i].dtype)
        out_refs[-1][rows, :] = _dot(hn, wtail_ref[...]).astype(out_refs[-1].dtype)

    if conv_out is not None:
        taps = convw_ref.shape[0]
        cw = convw_ref[...]
        y = cw[taps - 1:taps, :] * xpad_ref[CONV_HALO:, :]
        for j in range(taps - 1):
            lo = CONV_HALO - (taps - 1 - j)
            y = y + cw[j:j + 1, :] * xpad_ref[lo:lo + ROW_TILE, :]
        tail = xpad_ref[ROW_TILE:, :]
        row = (step * ROW_TILE) % lp + lax.broadcasted_iota(jnp.int32, (ROW_TILE, 1), 0)
        row = jnp.where(row >= lp, row - lp, row)
        out_refs[conv_out][...] = jnp.where(row >= PAD, _silu(y), 0.0).astype(
            out_refs[conv_out].dtype)
        xpad_ref[0:CONV_HALO, :] = tail


def _ffn_proj(src, meta_block, norm4, layer, wg, wu, wd, w_in, index, widths, dtypes, rows, lp,
              seq, conv_out=None, conv_w=None):
    d = norm4.shape[-1]
    d_ff = wg.shape[-1]
    prep = d_ff // FF_TILE
    main = sum(widths[:-1])
    tail_valid = w_in.shape[2] - main
    assert 0 < tail_valid <= widths[-1] == LANE and main % LANE == 0 and lp > ROW_TILE
    width = FF_TILE
    assert main % width == 0
    win_chunks = main // width
    prep = max(prep, win_chunks + 1)
    w_in = w_in[index]
    win_spec = pl.BlockSpec((d, width), lambda i: (0, jnp.minimum(i, win_chunks)))
    if meta_block is None:
        row_specs, extra, extra_specs, embed_blocks = _tile_specs(d), [], [], 0
    else:
        row_specs = _embed_specs(d, lp, seq)
        extra, extra_specs, embed_blocks = [meta_block], [_resident(meta_block.shape)], (
            lp // SWA_BLOCK)
    scratch = [pltpu.VMEM((ROW_TILE, d_ff), BF16)]
    if conv_out is not None:
        extra, extra_specs = extra + [conv_w], extra_specs + [_resident(conv_w.shape)]
        scratch.append(pltpu.VMEM((CONV_HALO + ROW_TILE, widths[conv_out]), F32))
    tile = lambda n: pl.BlockSpec((ROW_TILE, n), lambda i: (i, 0))
    shift = lambda specs: [_after_prep(s, prep) for s in specs]
    return pl.pallas_call(
        functools.partial(_ffn_proj_body, n_rows=len(row_specs), embed_blocks=embed_blocks,
                          n_out=len(widths), conv_out=conv_out, lp=lp, tail_valid=tail_valid,
                          prologue=prep),
        grid=(prep + rows // ROW_TILE,),
        in_specs=shift(row_specs + extra_specs
                       + [_pick(norm4, layer, 0), _pick(norm4, layer, 1)])
        + _weight_chunk_specs(wg, wd, layer, 0)
        + shift([_pick(norm4, layer, 2)]) + [win_spec],
        out_specs=shift([tile(d)] + [tile(n) for n in widths]),
        out_shape=[jax.ShapeDtypeStruct((rows, d), F32)]
        + [jax.ShapeDtypeStruct((rows, n), dt) for n, dt in zip(widths, dtypes)],
        scratch_shapes=scratch + _weight_scratch(wg)
        + [pltpu.VMEM((win_chunks, d, width), BF16), pltpu.VMEM((d, LANE), BF16)],
        compiler_params=_params("arbitrary"),
        name="ffn_proj",
    )(*([src] * len(row_specs)), *extra, norm4, norm4, wg, wu, wd, norm4, w_in)


def _proj_ffn_body(*refs, n_rows, n_x):
    x_groups = [refs[g * n_rows:(g + 1) * n_rows] for g in range(n_x)]
    refs = refs[n_x * n_rows:]
    h_refs = refs[:n_rows]
    (wo_ref, nwo_ref, nwa_ref, nwb_ref, wg_blk, wu_blk, wd_blk, o_ref, a_ref,
     wg_ref, wu_ref, wd_ref) = refs[n_rows:]
    prep = _prepare_weights((wg_blk, wu_blk, wd_blk), (wg_ref, wu_ref, wd_ref))

    @pl.when(pl.program_id(0) >= prep)
    def _():
        accs = []
        for part in range(ROW_SPLIT):
            acc = None
            off = 0
            for group in x_groups:
                xs = _load_rows(group, part)
                k = xs.shape[1]
                p = _dot(xs, wo_ref[off:off + k, :])
                acc = p if acc is None else acc + p
                off += k
            accs.append(acc)
        hs = [_load_rows(h_refs, part) + _rms(acc, nwo_ref[...])
              for part, acc in enumerate(accs)]
        outs = _ffn_core(hs, nwa_ref[...], nwb_ref[...], wg_ref, wu_ref, wd_ref, a_ref)
        for part, out in enumerate(outs):
            o_ref[_group(part), :] = out


def _proj_ffn(xs, h, w_out, norm4, layer, wg, wu, wd, final, batch, lp, seq):
    d = norm4.shape[-1]
    d_ff = wg.shape[-1]
    if final:
        specs = lambda width: _real_specs(width, lp, seq)
        out_rows = batch * seq
    else:
        specs = _tile_specs
        out_rows = batch * lp
    h_specs = specs(d)
    n_rows = len(h_specs)
    in_specs, args = [], []
    for x in xs:
        in_specs += specs(x.shape[1])
        args += [x] * n_rows
    in_specs += h_specs + [_resident(w_out.shape), _pick(norm4, layer, 3),
                           _pick(norm4, layer, 4), _pick(norm4, layer, 5)]
    prep = d_ff // FF_TILE
    in_specs = [_after_prep(s, prep) for s in in_specs] + _weight_chunk_specs(wg, wd, layer, 1)
    args += [h] * n_rows + [w_out, norm4, norm4, norm4, wg, wu, wd]
    return pl.pallas_call(
        functools.partial(_proj_ffn_body, n_rows=n_rows, n_x=len(xs)),
        grid=(prep + out_rows // ROW_TILE,),
        in_specs=in_specs,
        out_specs=_after_prep(pl.BlockSpec((ROW_TILE, d), lambda i: (i, 0)), prep),
        out_shape=jax.ShapeDtypeStruct((out_rows, d), F32),
        scratch_shapes=[pltpu.VMEM((ROW_TILE, d_ff), BF16)] + _weight_scratch(wg),
        compiler_params=_params("arbitrary"),
        name="proj_ffn",
    )(*args)


def _t5_bucket_np(rel):
    n = np.maximum(rel, 0)
    max_exact = REL_BUCKETS // 2
    n_f = np.maximum(n, 1).astype(np.float32)
    large = max_exact + (np.log(n_f / np.float32(max_exact))
                         / np.float32(math.log(REL_MAX_DIST / max_exact))
                         * np.float32(REL_BUCKETS - max_exact)).astype(np.int32)
    large = np.minimum(large, REL_BUCKETS - 1)
    return np.where(n < max_exact, n, large).astype(np.int32)


def _swa_bucket_index():
    i = np.arange(SWA_BLOCK)[:, None]
    c = np.arange(SWA_BLOCK)[None, :]
    out = np.full((3, SWA_BLOCK, 2 * SWA_BLOCK), -1, np.int32)
    for variant in range(3):
        pos_q = variant * SWA_BLOCK + i - PAD
        pos_k = variant * SWA_BLOCK + c - PAD - np.where(c > i, SWA_BLOCK, 0)
        rel_b = pos_q - pos_k
        assert ((rel_b >= 0) & (rel_b < SWA_BLOCK)).all()
        out[variant, :, :SWA_BLOCK] = np.where(pos_k >= N_META, _t5_bucket_np(rel_b), -1)
        rel_m = pos_q - c
        meta = (c < N_META) & (rel_m >= 0)
        out[variant, :, SWA_BLOCK:] = np.where(meta, _t5_bucket_np(rel_m), -1)
    return out


_SWA_BUCKET_INDEX = _swa_bucket_index()


def _bias_body(tbl_ref, idx_ref, o_ref):
    idx = idx_ref[0]
    group = SWA_Q_HEADS // SWA_KV_HEADS
    for head in range(SWA_Q_HEADS):
        acc = jnp.full(idx.shape, NEG_INF, F32)
        for b in range(REL_BUCKETS):
            acc = jnp.where(idx == b, tbl_ref[b, head], acc)
        rows = slice((head % group) * SWA_BLOCK, (head % group + 1) * SWA_BLOCK)
        o_ref[0, head // group, rows, :] = acc


def _swa_bias(rel_table):
    idx = jnp.asarray(_SWA_BUCKET_INDEX)
    group = SWA_Q_HEADS // SWA_KV_HEADS
    width = idx.shape[2]
    return pl.pallas_call(
        _bias_body,
        grid=(3,),
        in_specs=[pl.BlockSpec(memory_space=pltpu.SMEM),
                  pl.BlockSpec((1, SWA_BLOCK, width), lambda v: (v, 0, 0))],
        out_specs=pl.BlockSpec((1, SWA_KV_HEADS, group * SWA_BLOCK, width),
                               lambda v: (v, 0, 0, 0)),
        out_shape=jax.ShapeDtypeStruct((3, SWA_KV_HEADS, group * SWA_BLOCK, width), F32),
        compiler_params=_params("arbitrary"),
        name="swa_bias",
    )(rel_table, idx)


def _swa_stages(sink_ref, q_ref, kc_ref, kp_ref, km_ref, vc_ref, vp_ref, vm_ref, bias_ref,
                step, blocks, store):
    dh = SWA_HEAD_DIM
    blk = SWA_BLOCK
    group = SWA_Q_HEADS // SWA_KV_HEADS
    scale = dh ** -0.5
    zpad = jnp.zeros((blk - N_META, dh), BF16)
    row = lax.broadcasted_iota(jnp.int32, (group * blk, blk), 0) % blk
    col = lax.broadcasted_iota(jnp.int32, (group * blk, blk), 1)
    from_prev = col > row

    ones_col = (lax.broadcasted_iota(jnp.int32, (3 * blk, dh), 1) == 0).astype(BF16)

    units = [(j, hk) for j in range(blocks) for hk in range(SWA_KV_HEADS)]
    scores, values = [], []
    for j, hk in units:
        r = slice(j * blk, (j + 1) * blk)
        kv = slice(hk * dh, (hk + 1) * dh)
        k_prev = kp_ref[:, kv] if j == 0 else kc_ref[(j - 1) * blk:j * blk, kv]
        v_prev = vp_ref[:, kv] if j == 0 else vc_ref[(j - 1) * blk:j * blk, kv]
        k_all = jnp.concatenate([k_prev, kc_ref[r, kv], km_ref[:, kv], zpad], axis=0)
        v_all = jnp.concatenate([v_prev, vc_ref[r, kv], vm_ref[:, kv], zpad], axis=0)
        values.append(jnp.concatenate([v_all, ones_col], axis=1))
        q = jnp.concatenate([q_ref[r, (hk * group + g) * dh:(hk * group + g + 1) * dh]
                             for g in range(group)], axis=0) * scale
        scores.append(_dot_nt(q, k_all))
        yield

    probs, sink_terms = [], []
    for (j, hk), s in zip(units, scores):
        bias = bias_ref[jnp.minimum(step * blocks + j, 2), hk]
        s_band = jnp.where(from_prev, s[:, :blk], s[:, blk:2 * blk]) + bias[:, :blk]
        s_meta = s[:, 2 * blk:] + bias[:, blk:]
        sink = jnp.concatenate([jnp.full((blk, 1), sink_ref[hk * group + g], F32)
                                for g in range(group)], axis=0)
        m = jnp.maximum(jnp.max(jnp.maximum(s_band, s_meta), axis=-1, keepdims=True), sink)
        e_band = jnp.exp(s_band - m)
        e_meta = jnp.exp(s_meta - m)
        sink_terms.append(jnp.exp(sink - m))
        probs.append(jnp.concatenate([jnp.where(from_prev, e_band, 0.0),
                                      jnp.where(from_prev, 0.0, e_band), e_meta],
                                     axis=1).astype(BF16))

    for (j, hk), p, v_ext, sink_term in zip(units, probs, values, sink_terms):
        pv = _dot(p, v_ext)
        o = pv[:, :dh] / (pv[:, dh:dh + 1] + sink_term)
        for g in range(group):
            hq = hk * group + g
            store(slice(j * blk, (j + 1) * blk), slice(hq * dh, (hq + 1) * dh),
                  o[g * blk:(g + 1) * blk])
        yield


def _swa_body(sink_ref, q_ref, kc_ref, kp_ref, km_ref, vc_ref, vp_ref, vm_ref, bias_ref, o_ref,
              *, blocks):
    def store(rows, cols, value):
        o_ref[rows, cols] = value.astype(o_ref.dtype)

    _drain(_swa_stages(sink_ref, q_ref, kc_ref, kp_ref, km_ref, vc_ref, vp_ref, vm_ref,
                       bias_ref, pl.program_id(1), blocks, store))


def _swa(q, k, v, bias, sinks, batch, lp, blocks):
    rows = q.shape[0]
    nb = lp // SWA_BLOCK
    steps = nb // blocks
    meta_blocks = lp // N_META
    cur = lambda b, s: (b * steps + s, 0)
    prev = lambda b, s: (b * nb + jnp.maximum(s * blocks - 1, 0), 0)
    meta = lambda b, s: (b * meta_blocks + PAD // N_META, 0)
    kv_w = k.shape[1]
    rs = blocks * SWA_BLOCK
    return pl.pallas_call(
        functools.partial(_swa_body, blocks=blocks),
        grid=(batch, steps),
        in_specs=[pl.BlockSpec(memory_space=pltpu.SMEM),
                  pl.BlockSpec((rs, q.shape[1]), cur),
                  pl.BlockSpec((rs, kv_w), cur), pl.BlockSpec((SWA_BLOCK, kv_w), prev),
                  pl.BlockSpec((N_META, kv_w), meta),
                  pl.BlockSpec((rs, kv_w), cur), pl.BlockSpec((SWA_BLOCK, kv_w), prev),
                  pl.BlockSpec((N_META, kv_w), meta),
                  _resident(bias.shape)],
        out_specs=pl.BlockSpec((rs, q.shape[1]), cur),
        out_shape=jax.ShapeDtypeStruct((rows, q.shape[1]), BF16),
        compiler_params=_params("parallel", "arbitrary"),
        name="swa",
    )(sinks, q, k, k, k, v, v, v, bias)


def _tri_masks():
    row = lax.broadcasted_iota(jnp.int32, (CHUNK, CHUNK), 0)
    col = lax.broadcasted_iota(jnp.int32, (CHUNK, CHUNK), 1)
    return row >= col, row > col, row == col


def _deltanet_stages(act_ref, z_ref, gates_ref, alog_ref, dtb_ref, nw_ref, s_ref, chunks,
                     store):
    nh, dh = DN_HEADS, DN_HEAD_DIM
    gt = gates_ref[...]
    beta_full = jax.nn.sigmoid(gt)
    g_full = -jnp.exp(alog_ref[...]) * _softplus(gt + dtb_ref[...])
    incl, strict, diag = _tri_masks()
    eye = diag.astype(F32)
    tri = incl.astype(F32)
    gc_chunks = [_dot(tri, g_full[c * CHUNK:(c + 1) * CHUNK, :], HIGHEST)
                 for c in range(chunks)]
    items = [(c, h) for c in range(chunks) for h in range(nh)]
    yield

    neg_a, akt16, qd16, rhs16, decay = [], [], [], [], []
    for c, h in items:
        if h == 0 and c % 2 == 0 and c > 0:
            yield
        r = slice(c * CHUNK, (c + 1) * CHUNK)
        q = act_ref[r, h * dh:(h + 1) * dh].astype(F32)
        k = act_ref[r, DN_DIM + h * dh:DN_DIM + (h + 1) * dh].astype(F32)
        v = act_ref[r, 2 * DN_DIM + h * dh:2 * DN_DIM + (h + 1) * dh].astype(F32)
        q = q * lax.rsqrt(jnp.sum(q * q, axis=-1, keepdims=True) + 1e-6) * dh ** -0.5
        k = k * lax.rsqrt(jnp.sum(k * k, axis=-1, keepdims=True) + 1e-6)
        beta = jnp.broadcast_to(beta_full[r, h:h + 1], (CHUNK, dh))
        gc = jnp.broadcast_to(gc_chunks[c][:, nh + h:nh + h + 1], (CHUNK, dh))
        gc_row = jnp.transpose(gc)[:CHUNK, :]
        gamma = jnp.where(incl, jnp.exp(jnp.where(incl, gc[:, :CHUNK] - gc_row, 0.0)), 0.0)
        k_beta = k * beta
        kq = _dot_nt(jnp.concatenate([k_beta, q], axis=0).astype(BF16), k.astype(BF16))
        neg_a.append(jnp.where(strict, -(kq[:CHUNK] * gamma), 0.0))
        attn16 = jnp.where(incl, kq[CHUNK:] * gamma, 0.0).astype(BF16)
        gc_last = gc[CHUNK - 1:CHUNK, :]
        qd16.append((q * jnp.exp(gc)).astype(BF16))
        kdt16 = jnp.transpose(k * jnp.exp(gc_last - gc)).astype(BF16)
        akt16.append(jnp.concatenate([attn16, kdt16], axis=0))
        rhs16.append(jnp.concatenate([v * beta, k_beta * jnp.exp(gc)], axis=1).astype(BF16))
        decay.append(jnp.exp(gc_last))

    yield
    levels = int(math.log2(CHUNK)) - 1
    t = [eye + x for x in neg_a]
    p16 = [x.astype(BF16) for x in neg_a]
    p16 = [_dot(x, x).astype(BF16) for x in p16]
    yield
    for level in range(levels):
        if level + 1 < levels:
            tp = [_dot(jnp.concatenate([y.astype(BF16), x], axis=0), x) for y, x in zip(t, p16)]
            t = [y + r[:CHUNK] for y, r in zip(t, tp)]
            p16 = [r[CHUNK:].astype(BF16) for r in tp]
        else:
            t = [y + _dot(y.astype(BF16), x) for y, x in zip(t, p16)]
        yield
    uw = [_dot(y.astype(BF16), b) for y, b in zip(t, rhs16)]
    wq16 = [jnp.concatenate([x[:, dh:].astype(BF16), y], axis=0)
            for x, y in zip(uw, qd16)]
    yield

    nw = nw_ref[...]
    state = [s_ref[h] for h in range(nh)]
    for c in range(chunks):
        r = slice(c * CHUNK, (c + 1) * CHUNK)
        pair = [c * nh + h for h in range(nh)]
        s16 = [s.astype(BF16) for s in state]
        wqs = [_dot(wq16[i], s16[h]) for h, i in enumerate(pair)]
        yield
        vn16 = [(uw[i][:, :dh] - wqs[h][:CHUNK]).astype(BF16) for h, i in enumerate(pair)]
        ak = [_dot(akt16[i], vn16[h]) for h, i in enumerate(pair)]
        state = [state[h] * decay[i] + ak[h][CHUNK:] for h, i in enumerate(pair)]
        for h in range(nh):
            hc = slice(h * dh, (h + 1) * dh)
            out = wqs[h][CHUNK:] + ak[h][:CHUNK]
            store(r, hc, _rms(out, nw) * _silu(z_ref[r, hc].astype(F32)))
        yield
    for h in range(nh):
        s_ref[h] = state[h]


def _deltanet_body(act_ref, z_ref, gates_ref, alog_ref, dtb_ref, nw_ref, o_ref, s_ref, *,
                   chunks):
    @pl.when(pl.program_id(1) == 0)
    def _():
        s_ref[...] = jnp.zeros_like(s_ref)

    def store(rows, cols, value):
        o_ref[rows, cols] = value.astype(o_ref.dtype)

    _drain(_deltanet_stages(act_ref, z_ref, gates_ref, alog_ref, dtb_ref, nw_ref, s_ref,
                            chunks, store))


def _deltanet(act, z, gates, alog_vec, dtb_vec, nw, batch, lp, chunks):
    rows = act.shape[0]
    rs = chunks * CHUNK
    steps = lp // rs
    idx = lambda b, s: (b * steps + s, 0)
    return pl.pallas_call(
        functools.partial(_deltanet_body, chunks=chunks),
        grid=(batch, steps),
        in_specs=[pl.BlockSpec((rs, act.shape[1]), idx), pl.BlockSpec((rs, z.shape[1]), idx),
                  pl.BlockSpec((rs, gates.shape[1]), idx),
                  _resident(alog_vec.shape), _resident(dtb_vec.shape), _resident(nw.shape)],
        out_specs=pl.BlockSpec((rs, DN_DIM), idx),
        out_shape=jax.ShapeDtypeStruct((rows, DN_DIM), BF16),
        scratch_shapes=[pltpu.VMEM((DN_HEADS, DN_HEAD_DIM, DN_HEAD_DIM), F32)],
        compiler_params=_params("parallel", "arbitrary"),
        name="deltanet",
    )(act, z, gates, alog_vec, dtb_vec, nw)


def _mix_ffn_body(*refs, blocks, chunks, steps, last):
    *refs, wg_s, wu_s, wd_s = refs
    prep = _prepare_weights(refs[20:23], (wg_s, wu_s, wd_s))
    t = pl.program_id(0) - prep
    refs = refs[:20] + [wg_s, wu_s, wd_s] + refs[23:]

    @pl.when(t >= 0)
    def _():
        _mix_ffn_step(t, *refs, blocks=blocks, chunks=chunks, steps=steps, last=last)


def _mix_ffn_step(t, sink_ref, q_ref, kc_ref, kp_ref, km_ref, vc_ref, vp_ref, vm_ref, bias_ref,
                  act_ref, z_ref, gates_ref, alog_ref, dtb_ref, dnw_ref,
                  h_ref, wo_ref, nwo_ref, nwa_ref, nwb_ref, wg_ref, wu_ref, wd_ref,
                  o_ref, s_ref, mix_ref, a_ref, *, blocks, chunks, steps, last):
    seq_step = jnp.minimum(t, last) % steps
    write_slot = t % 2
    read_slot = 1 - write_slot
    swa_width = q_ref.shape[1]
    groups = q_ref.shape[0] // GROUP_ROWS

    @pl.when(t == 0)
    def _():
        mix_ref[...] = jnp.zeros_like(mix_ref)

    @pl.when(seq_step == 0)
    def _():
        s_ref[...] = jnp.zeros_like(s_ref)

    def store_swa(rows, cols, value):
        mix_ref[write_slot, rows, cols] = value.astype(mix_ref.dtype)

    def store_dn(rows, cols, value):
        cols = slice(swa_width + cols.start, swa_width + cols.stop)
        mix_ref[write_slot, rows, cols] = value.astype(mix_ref.dtype)

    def mixer_stages():
        dn = _deltanet_stages(act_ref, z_ref, gates_ref, alog_ref, dtb_ref, dnw_ref,
                              s_ref, chunks, store_dn)
        swa = _swa_stages(sink_ref, q_ref, kc_ref, kp_ref, km_ref, vc_ref, vp_ref,
                          vm_ref, bias_ref, seq_step, blocks, store_swa)
        for _ in dn:
            yield
            if next(swa, StopIteration) is not StopIteration:
                yield
        yield from swa

    def ffn_stages():
        accs = []
        for part in range(groups):
            accs.append(_dot(mix_ref[read_slot, _group(part), :], wo_ref[...]))
            yield
        hs = [h_ref[_group(part), :] + _rms(acc, nwo_ref[...]) for part, acc in enumerate(accs)]
        outs = []
        yield from _ffn_stages(hs, nwa_ref[...], nwb_ref[...], wg_ref, wu_ref, wd_ref, a_ref,
                               outs)
        for part, out in enumerate(outs):
            o_ref[_group(part), :] = out

    _interleave(ffn_stages(), mixer_stages())


def _mix_ffn(qa, ka, va, bias, sinks, act, z, gates, alog_vec, dtb_vec, dn_nw, h, w_out, norm4,
             layer, wg, wu, wd, batch, lp, blocks, chunks):
    d = norm4.shape[-1]
    d_ff = wg.shape[-1]
    rs = blocks * SWA_BLOCK
    assert rs == chunks * CHUNK and rs % GROUP_ROWS == 0
    nb = lp // SWA_BLOCK
    steps = lp // rs
    last = batch * steps - 1
    meta_blocks = lp // N_META

    def tile(t):
        return jnp.minimum(t, last)

    cur = lambda t: (tile(t), 0)
    prev = lambda t: ((tile(t) // steps) * nb + jnp.maximum((tile(t) % steps) * blocks - 1, 0), 0)
    meta = lambda t: ((tile(t) // steps) * meta_blocks + PAD // N_META, 0)
    behind = lambda t: (jnp.maximum(t - 1, 0), 0)
    kv_w = ka.shape[1]
    prep = d_ff // FF_TILE
    in_specs = [pl.BlockSpec(memory_space=pltpu.SMEM),
                pl.BlockSpec((rs, qa.shape[1]), cur),
                pl.BlockSpec((rs, kv_w), cur), pl.BlockSpec((SWA_BLOCK, kv_w), prev),
                pl.BlockSpec((N_META, kv_w), meta),
                pl.BlockSpec((rs, kv_w), cur), pl.BlockSpec((SWA_BLOCK, kv_w), prev),
                pl.BlockSpec((N_META, kv_w), meta),
                _resident(bias.shape),
                pl.BlockSpec((rs, act.shape[1]), cur), pl.BlockSpec((rs, z.shape[1]), cur),
                pl.BlockSpec((rs, gates.shape[1]), cur),
                _resident(alog_vec.shape), _resident(dtb_vec.shape), _resident(dn_nw.shape),
                pl.BlockSpec((rs, d), behind), _resident(w_out.shape),
                _pick(norm4, layer, 3), _pick(norm4, layer, 4), _pick(norm4, layer, 5)]
    return pl.pallas_call(
        functools.partial(_mix_ffn_body, blocks=blocks, chunks=chunks, steps=steps, last=last),
        grid=(prep + last + 2,),
        in_specs=[_after_prep(s, prep) for s in in_specs]
        + _weight_chunk_specs(wg, wd, layer, 1),
        out_specs=_after_prep(pl.BlockSpec((rs, d), behind), prep),
        out_shape=jax.ShapeDtypeStruct(h.shape, F32),
        scratch_shapes=[pltpu.VMEM((DN_HEADS, DN_HEAD_DIM, DN_HEAD_DIM), F32),
                        pltpu.VMEM((2, rs, qa.shape[1] + DN_DIM), BF16),
                        pltpu.VMEM((rs, d_ff), BF16)] + _weight_scratch(wg),
        compiler_params=_params("arbitrary"),
        name="mix_ffn",
    )(sinks, qa, ka, ka, ka, va, va, va, bias, act, z, gates, alog_vec, dtb_vec, dn_nw,
      h, w_out, norm4, norm4, norm4, wg, wu, wd)


def _gla_body(q_ref, k_ref, v_ref, g_ref, gk_ref, wgu_ref, bg_ref, nw_ref, o_ref, s_ref, *,
              chunks):
    nh = GLA_HEADS
    dk = q_ref.shape[1] // nh
    dv = v_ref.shape[1] // nh

    @pl.when(pl.program_id(1) == 0)
    def _():
        s_ref[...] = jnp.zeros_like(s_ref)

    logits = _dot(gk_ref[...].astype(BF16), wgu_ref[...]) + bg_ref[...]
    glog = (jnp.minimum(logits, 0.0) - jnp.log(1.0 + jnp.exp(-jnp.abs(logits)))) / GLA_GATE_NORM
    incl, _, _ = _tri_masks()
    tri = incl.astype(F32)
    bcum = [_dot(tri, glog[c * CHUNK:(c + 1) * CHUNK, :], HIGHEST) for c in range(chunks)]

    items = [(c, h) for c in range(chunks) for h in range(nh)]
    qd16, kn16, kdt16, decay = [], [], [], []
    for c, h in items:
        r = slice(c * CHUNK, (c + 1) * CHUNK)
        kc = slice(h * dk, (h + 1) * dk)
        bc = bcum[c][:, kc]
        q = q_ref[r, kc].astype(F32) * dk ** -0.5
        k = k_ref[r, kc].astype(F32)
        b_last = bc[CHUNK - 1:CHUNK, :]
        qd16.append((q * jnp.exp(bc)).astype(BF16))
        kn16.append((k * jnp.exp(-bc)).astype(BF16))
        kdt16.append(jnp.transpose(k * jnp.exp(b_last - bc)).astype(BF16))
        decay_rows = jnp.broadcast_to(jnp.exp(b_last), (8, dk))
        decay.append(jnp.transpose(decay_rows)[:, :1])
    values = [v_ref[c * CHUNK:(c + 1) * CHUNK, h * dv:(h + 1) * dv] for c, h in items]
    attn16 = [jnp.where(incl, _dot_nt(a, b), 0.0).astype(BF16) for a, b in zip(qd16, kn16)]
    intra = [_dot(a, v) for a, v in zip(attn16, values)]
    update = [_dot(kt, v) for kt, v in zip(kdt16, values)]

    nw = nw_ref[...]
    state = [s_ref[h] for h in range(nh)]
    for c in range(chunks):
        r = slice(c * CHUNK, (c + 1) * CHUNK)
        for h in range(nh):
            i = c * nh + h
            vc = slice(h * dv, (h + 1) * dv)
            o = intra[i] + _dot(qd16[i], state[h].astype(BF16))
            state[h] = state[h] * decay[i] + update[i]
            o = _rms(o, nw) * _silu(g_ref[r, vc].astype(F32))
            o_ref[r, vc] = o.astype(o_ref.dtype)
    for h in range(nh):
        s_ref[h] = state[h]


def _gla(q, k, v, g, gk, wgu, bg, nw, batch, lp, chunks):
    rows = q.shape[0]
    rs = chunks * CHUNK
    steps = lp // rs
    idx = lambda b, s: (b * steps + s, 0)
    dk = q.shape[1] // GLA_HEADS
    dv = v.shape[1] // GLA_HEADS
    return pl.pallas_call(
        functools.partial(_gla_body, chunks=chunks),
        grid=(batch, steps),
        in_specs=[pl.BlockSpec((rs, q.shape[1]), idx), pl.BlockSpec((rs, k.shape[1]), idx),
                  pl.BlockSpec((rs, v.shape[1]), idx), pl.BlockSpec((rs, g.shape[1]), idx),
                  pl.BlockSpec((rs, gk.shape[1]), idx), _resident(wgu.shape),
                  _resident(bg.shape), _resident(nw.shape)],
        out_specs=pl.BlockSpec((rs, v.shape[1]), idx),
        out_shape=jax.ShapeDtypeStruct((rows, v.shape[1]), BF16),
        scratch_shapes=[pltpu.VMEM((GLA_HEADS, dk, dv), F32)],
        compiler_params=_params("parallel", "arbitrary"),
        name="gla",
    )(q, k, v, g, gk, wgu, bg, nw)


def _per_step(n, preferred):
    for c in preferred:
        if n % c == 0:
            return c
    return 1


def kernel(x, meta_tokens, norm_w, ffn_w_gate, ffn_w_up, ffn_w_down, rel_bias_table,
           even_w_in, even_conv_w, swa_sinks, dn_a_log, dn_dt_bias, dn_norm_w, even_w_out,
           odd_w_in, gla_w_gate_up, gla_b_gate, gla_norm_w, odd_w_out):
    batch, seq, d = x.shape
    depth = norm_w.shape[0]
    lp = SWA_BLOCK + seq
    assert seq % ROW_TILE == 0 and (batch * lp) % ROW_TILE == 0
    rows = batch * lp
    chunks = _per_step(lp // CHUNK, (6, 3, 2))
    swa_blocks = _per_step(lp // SWA_BLOCK, (3, 2))

    meta_block = jnp.pad(meta_tokens.astype(x.dtype), ((PAD, 0), (0, 0)))
    norm4 = norm_w[:, :, None, :]
    wg, wu, wd = ffn_w_gate, ffn_w_up, ffn_w_down

    swa_q = SWA_Q_HEADS * SWA_HEAD_DIM
    swa_kv = SWA_KV_HEADS * SWA_HEAD_DIM
    bias = None
    h = x.reshape(batch * seq, d)
    for layer in range(depth):
        i = layer // 2
        embed = meta_block if layer == 0 else None
        final = layer == depth - 1
        if layer % 2 == 0:
            assert even_w_in.shape[2] == swa_q + 2 * swa_kv + 4 * DN_DIM + 2 * DN_HEADS
            widths = (swa_q, swa_kv, swa_kv, 3 * DN_DIM, DN_DIM, LANE)
            h, qa, ka, va, act_b, z_b, gates = _ffn_proj(
                h, embed, norm4, layer, wg, wu, wd, even_w_in, i, widths,
                (BF16, BF16, BF16, BF16, BF16, F32), rows, lp, seq,
                conv_out=3, conv_w=even_conv_w[i])
            if bias is None:
                bias = _swa_bias(rel_bias_table)
            lane_pad = (DN_HEADS, LANE - 2 * DN_HEADS)
            alog_vec = jnp.pad(dn_a_log[i], lane_pad)[None, :]
            dtb_vec = jnp.pad(dn_dt_bias[i], lane_pad)[None, :]
            dn_nw = dn_norm_w[i][None, :]
            w_out = even_w_out[i]
            if not final and swa_blocks * SWA_BLOCK == chunks * CHUNK:
                h = _mix_ffn(qa, ka, va, bias, swa_sinks[i], act_b, z_b, gates, alog_vec,
                             dtb_vec, dn_nw, h, w_out.astype(BF16), norm4, layer, wg, wu, wd,
                             batch, lp, swa_blocks, chunks)
                continue
            o_a = _swa(qa, ka, va, bias, swa_sinks[i], batch, lp, swa_blocks)
            o_b = _deltanet(act_b, z_b, gates, alog_vec, dtb_vec, dn_nw, batch, lp, chunks)
            mixed = [o_a, o_b]
        else:
            key_dim = gla_w_gate_up.shape[2]
            val_dim = odd_w_out.shape[1]
            widths = (key_dim, key_dim, val_dim, val_dim, LANE)
            h, q, k, v, g, gk = _ffn_proj(
                h, embed, norm4, layer, wg, wu, wd, odd_w_in, i, widths,
                (BF16, BF16, BF16, BF16, F32), rows, lp, seq)
            wgu = jnp.pad(gla_w_gate_up[i], ((0, LANE - GLA_GATE_RANK), (0, 0))).astype(BF16)
            o = _gla(q, k, v, g, gk, wgu, gla_b_gate[i][None, :], gla_norm_w[i][None, :],
                     batch, lp, chunks)
            mixed, w_out = [o], odd_w_out[i]
        h = _proj_ffn(mixed, h, w_out.astype(BF16), norm4, layer, wg, wu, wd, final,
                      batch, lp, seq)
    return h.reshape(batch, seq, d)
```

```python
import functools
import math

import numpy as np
import jax
import jax.numpy as jnp
from jax import lax
from jax.experimental import pallas as pl
from jax.experimental.pallas import tpu as pltpu

F32 = jnp.float32
BF16 = jnp.bfloat16
HIGHEST = lax.Precision.HIGHEST

N_META = 16
NORM_EPS = 1e-6
NEG_INF = -1e30
SWA_Q_HEADS = 8
SWA_KV_HEADS = 2
SWA_HEAD_DIM = 64
SWA_BLOCK = 128
REL_BUCKETS = 32
REL_MAX_DIST = 128
DN_HEADS = 4
DN_HEAD_DIM = 128
DN_DIM = DN_HEADS * DN_HEAD_DIM
DN_CONV = 4
GLA_HEADS = 4
GLA_GATE_RANK = 16
GLA_GATE_NORM = 16.0
CHUNK = 64

PAD = SWA_BLOCK - N_META
LANE = 128
ROW_TILE = 512
FF_TILE = 256
VMEM_LIMIT = 60 * 1024 * 1024


def _rms(x, w):
    return x * lax.rsqrt(jnp.mean(x * x, axis=-1, keepdims=True) + NORM_EPS) * w


def _silu(x):
    return x * jax.nn.sigmoid(x)


def _softplus(x):
    return jnp.maximum(x, 0.0) + jnp.log(1.0 + jnp.exp(-jnp.abs(x)))


def _dot(a, b, precision=None):
    return jnp.dot(a, b, preferred_element_type=F32, precision=precision)


def _dot_nt(a, b, precision=None):
    return lax.dot_general(a, b, (((1,), (1,)), ((), ())), preferred_element_type=F32,
                           precision=precision)


def _dot_tn(a, b, precision=None):
    return lax.dot_general(a, b, (((0,), (0,)), ((), ())), preferred_element_type=F32,
                           precision=precision)


def _resident(shape):
    nd = len(shape)
    return pl.BlockSpec(shape, lambda *_: (0,) * nd, pipeline_mode=pl.Buffered(1))


def _params(*semantics):
    return pltpu.CompilerParams(dimension_semantics=semantics, vmem_limit_bytes=VMEM_LIMIT)


SUB = ROW_TILE // SWA_BLOCK


def _pick(arr, *lead):
    rest = arr.shape[len(lead):]
    index = tuple(lead) + (0,) * len(rest)
    return pl.BlockSpec((None,) * len(lead) + rest, lambda i: index,
                        pipeline_mode=pl.Buffered(1))


def _after_prep(spec, prep):
    if spec.index_map is None:
        return spec
    index = spec.index_map
    return pl.BlockSpec(spec.block_shape, lambda i: index(jnp.maximum(i - prep, 0)),
                        pipeline_mode=spec.pipeline_mode)


def _weight_chunk_specs(wg, wd, layer, which):
    d, d_ff = wg.shape[-2:]
    last = d_ff // FF_TILE - 1
    cols = pl.BlockSpec((None, None, d, FF_TILE), lambda i: (layer, which, 0, jnp.minimum(i, last)))
    rows = pl.BlockSpec((None, None, FF_TILE, d), lambda i: (layer, which, jnp.minimum(i, last), 0))
    return [cols, cols, rows]


def _weight_scratch(wg):
    d, d_ff = wg.shape[-2:]
    chunks = d_ff // FF_TILE
    return [pltpu.VMEM((chunks, d, FF_TILE), BF16), pltpu.VMEM((chunks, d, FF_TILE), BF16),
            pltpu.VMEM((d_ff, d), BF16)]


def _prepare_weights(chunk_refs, scratch_refs):
    wg_blk, wu_blk, wd_blk = chunk_refs
    wg_s, wu_s, wd_s = scratch_refs
    prep = wg_s.shape[0]
    i = pl.program_id(0)

    @pl.when(i < prep)
    def _():
        wg_s[i] = wg_blk[...].astype(BF16)
        wu_s[i] = wu_blk[...].astype(BF16)
        wd_s[pl.ds(pl.multiple_of(i * FF_TILE, FF_TILE), FF_TILE), :] = wd_blk[...].astype(BF16)

    return prep


def _tile_specs(width):
    return [pl.BlockSpec((ROW_TILE, width), lambda i: (i, 0))]


def _real_specs(width, lp, seq):
    tiles = seq // ROW_TILE
    blocks = lp // SWA_BLOCK

    def spec(j):
        return pl.BlockSpec((SWA_BLOCK, width),
                            lambda i: ((i // tiles) * blocks + 1 + (i % tiles) * SUB + j, 0))
    return [spec(j) for j in range(SUB)]


def _embed_specs(width, lp, seq):
    blocks = lp // SWA_BLOCK
    x_blocks = seq // SWA_BLOCK

    def spec(j):
        def index(i):
            blk = i * SUB + j
            return ((blk // blocks) * x_blocks + jnp.maximum(blk % blocks - 1, 0), 0)
        return pl.BlockSpec((SWA_BLOCK, width), index)
    return [spec(j) for j in range(SUB)]


ROW_SPLIT = 4
GROUP_ROWS = ROW_TILE // ROW_SPLIT


def _group(part):
    return slice(part * GROUP_ROWS, (part + 1) * GROUP_ROWS)


def _load_rows(refs, part):
    if len(refs) == 1:
        return refs[0][_group(part), :]
    per = len(refs) // ROW_SPLIT
    return jnp.concatenate([r[...] for r in refs[part * per:(part + 1) * per]], axis=0)


def _drain(stages):
    for _ in stages:
        pass


def _interleave(main, side):
    for _ in main:
        next(side, None)
    _drain(side)


def _ffn_stages(xs, nwa, nwb, wg_ref, wu_ref, wd_ref, a_ref, outs):
    hns = [_rms(x, nwa).astype(BF16) for x in xs]
    for part, hn in enumerate(hns):
        for c in range(wg_ref.shape[0]):
            cols = slice(c * FF_TILE, (c + 1) * FF_TILE)
            g = _dot(hn, wg_ref[c])
            u = _dot(hn, wu_ref[c])
            a_ref[_group(part), cols] = (_silu(g) * u).astype(BF16)
            yield
    fs = []
    for part in range(len(xs)):
        fs.append(_dot(a_ref[_group(part), :], wd_ref[...]))
        yield
    outs.extend(x + 0.5 * _rms(f, nwb) for x, f in zip(xs, fs))


def _ffn_core(xs, nwa, nwb, wg_ref, wu_ref, wd_ref, a_ref):
    outs = []
    _drain(_ffn_stages(xs, nwa, nwb, wg_ref, wu_ref, wd_ref, a_ref, outs))
    return outs


CONV_HALO = 8


def _ffn_proj_body(*refs, n_rows, embed_blocks, n_out, conv_out, lp, tail_valid, prologue):
    lead = n_rows + bool(embed_blocks) + (conv_out is not None) + 2
    wg_s, wu_s, wd_s, win_s, wtail_s = refs[-5:]
    nwm_ref, win_blk = refs[lead + 3:lead + 5]
    _prepare_weights(refs[lead:lead + 3], (wg_s, wu_s, wd_s))
    step = pl.program_id(0) - prologue
    win_chunks = win_s.shape[0]

    @pl.when(pl.program_id(0) < win_chunks)
    def _():
        win_s[pl.program_id(0)] = win_blk[...].astype(BF16)

    @pl.when(pl.program_id(0) == win_chunks)
    def _():
        tail = win_blk[:, :LANE]
        lane = lax.broadcasted_iota(jnp.int32, tail.shape, 1)
        wtail_s[...] = jnp.where(lane < tail_valid, tail, 0.0).astype(BF16)

    refs = refs[:lead] + (wg_s, wu_s, wd_s, nwm_ref, win_s, wtail_s) + refs[lead + 5:-5]

    @pl.when(step >= 0)
    def _():
        _ffn_proj_step(step, *refs, n_rows=n_rows, embed_blocks=embed_blocks, n_out=n_out,
                       conv_out=conv_out, lp=lp)


def _ffn_proj_step(step, *refs, n_rows, embed_blocks, n_out, conv_out, lp):
    row_refs = refs[:n_rows]
    refs = refs[n_rows:]
    if embed_blocks:
        meta_ref, refs = refs[0], refs[1:]
    if conv_out is not None:
        convw_ref, refs = refs[0], refs[1:]
        xpad_ref = refs[10 + n_out]

        @pl.when(step == 0)
        def _():
            xpad_ref[0:CONV_HALO, :] = jnp.zeros((CONV_HALO, xpad_ref.shape[1]), F32)

    nwa_ref, nwb_ref, wg_ref, wu_ref, wd_ref, nwm_ref, win_ref, wtail_ref = refs[:8]
    h_ref = refs[8]
    out_refs = refs[9:9 + n_out]
    a_ref = refs[9 + n_out]
    offsets = np.cumsum([0] + [o.shape[1] for o in out_refs])
    xs = []
    for part in range(ROW_SPLIT):
        if embed_blocks:
            per = n_rows // ROW_SPLIT
            first = step * SUB + part * per
            xs.append(jnp.concatenate(
                [jnp.where((first + j) % embed_blocks == 0, meta_ref[...], r[...])
                 for j, r in enumerate(row_refs[part * per:(part + 1) * per])], axis=0))
        else:
            xs.append(_load_rows(row_refs, part))
    hs = _ffn_core(xs, nwa_ref[...], nwb_ref[...], wg_ref, wu_ref, wd_ref, a_ref)
    hns = [_rms(h, nwm_ref[...]).astype(BF16) for h in hs]
    width = win_ref.shape[2]
    for part, (h, hn) in enumerate(zip(hs, hns)):
        rows = _group(part)
        h_ref[rows, :] = h
        for c in range(win_ref.shape[0]):
            p = _dot(hn, win_ref[c])
            for i in range(n_out - 1):
                lo = max(c * width, offsets[i])
                hi = min((c + 1) * width, offsets[i + 1])
                if lo >= hi:
                    continue
                piece = p[:, lo - c * width:hi - c * width]
                cols = slice(lo - offsets[i], hi - offsets[i])
                if i == conv_out:
                    xpad_ref[CONV_HALO + part * GROUP_ROWS:CONV_HALO + (part + 1) * GROUP_ROWS,
                             cols] = piece
                else:
                    out_refs[i][rows, cols] = piece.astype(out_refs[i].dtype)
        out_refs[-1][rows, :] = _dot(hn, wtail_ref[...]).astype(out_refs[-1].dtype)

    if conv_out is not None:
        taps = convw_ref.shape[0]
        cw = convw_ref[...]
        y = cw[taps - 1:taps, :] * xpad_ref[CONV_HALO:, :]
        for j in range(taps - 1):
            lo = CONV_HALO - (taps - 1 - j)
            y = y + cw[j:j + 1, :] * xpad_ref[lo:lo + ROW_TILE, :]
        tail = xpad_ref[ROW_TILE:, :]
        row = (step * ROW_TILE) % lp + lax.broadcasted_iota(jnp.int32, (ROW_TILE, 1), 0)
        row = jnp.where(row >= lp, row - lp, row)
        out_refs[conv_out][...] = jnp.where(row >= PAD, _silu(y), 0.0).astype(
            out_refs[conv_out].dtype)
        xpad_ref[0:CONV_HALO, :] = tail


def _ffn_proj(src, meta_block, norm4, layer, wg, wu, wd, w_in, index, widths, dtypes, rows, lp,
              seq, conv_out=None, conv_w=None):
    d = norm4.shape[-1]
    d_ff = wg.shape[-1]
    prep = d_ff // FF_TILE
    main = sum(widths[:-1])
    tail_valid = w_in.shape[2] - main
    assert 0 < tail_valid <= widths[-1] == LANE and main % LANE == 0 and lp > ROW_TILE
    width = FF_TILE
    assert main % width == 0
    win_chunks = main // width
    prep = max(prep, win_chunks + 1)
    w_in = w_in[index]
    win_spec = pl.BlockSpec((d, width), lambda i: (0, jnp.minimum(i, win_chunks)))
    if meta_block is None:
        row_specs, extra, extra_specs, embed_blocks = _tile_specs(d), [], [], 0
    else:
        row_specs = _embed_specs(d, lp, seq)
        extra, extra_specs, embed_blocks = [meta_block], [_resident(meta_block.shape)], (
            lp // SWA_BLOCK)
    scratch = [pltpu.VMEM((ROW_TILE, d_ff), BF16)]
    if conv_out is not None:
        extra, extra_specs = extra + [conv_w], extra_specs + [_resident(conv_w.shape)]
        scratch.append(pltpu.VMEM((CONV_HALO + ROW_TILE, widths[conv_out]), F32))
    tile = lambda n: pl.BlockSpec((ROW_TILE, n), lambda i: (i, 0))
    shift = lambda specs: [_after_prep(s, prep) for s in specs]
    return pl.pallas_call(
        functools.partial(_ffn_proj_body, n_rows=len(row_specs), embed_blocks=embed_blocks,
                          n_out=len(widths), conv_out=conv_out, lp=lp, tail_valid=tail_valid,
                          prologue=prep),
        grid=(prep + rows // ROW_TILE,),
        in_specs=shift(row_specs + extra_specs
                       + [_pick(norm4, layer, 0), _pick(norm4, layer, 1)])
        + _weight_chunk_specs(wg, wd, layer, 0)
        + shift([_pick(norm4, layer, 2)]) + [win_spec],
        out_specs=shift([tile(d)] + [tile(n) for n in widths]),
        out_shape=[jax.ShapeDtypeStruct((rows, d), F32)]
        + [jax.ShapeDtypeStruct((rows, n), dt) for n, dt in zip(widths, dtypes)],
        scratch_shapes=scratch + _weight_scratch(wg)
        + [pltpu.VMEM((win_chunks, d, width), BF16), pltpu.VMEM((d, LANE), BF16)],
        compiler_params=_params("arbitrary"),
        name="ffn_proj",
    )(*([src] * len(row_specs)), *extra, norm4, norm4, wg, wu, wd, norm4, w_in)


def _proj_ffn_body(*refs, n_rows, n_x):
    x_groups = [refs[g * n_rows:(g + 1) * n_rows] for g in range(n_x)]
    refs = refs[n_x * n_rows:]
    h_refs = refs[:n_rows]
    (wo_ref, nwo_ref, nwa_ref, nwb_ref, wg_blk, wu_blk, wd_blk, o_ref, a_ref,
     wg_ref, wu_ref, wd_ref) = refs[n_rows:]
    prep = _prepare_weights((wg_blk, wu_blk, wd_blk), (wg_ref, wu_ref, wd_ref))

    @pl.when(pl.program_id(0) >= prep)
    def _():
        accs = []
        for part in range(ROW_SPLIT):
            acc = None
            off = 0
            for group in x_groups:
                xs = _load_rows(group, part)
                k = xs.shape[1]
                p = _dot(xs, wo_ref[off:off + k, :])
                acc = p if acc is None else acc + p
                off += k
            accs.append(acc)
        hs = [_load_rows(h_refs, part) + _rms(acc, nwo_ref[...])
              for part, acc in enumerate(accs)]
        outs = _ffn_core(hs, nwa_ref[...], nwb_ref[...], wg_ref, wu_ref, wd_ref, a_ref)
        for part, out in enumerate(outs):
            o_ref[_group(part), :] = out


def _proj_ffn(xs, h, w_out, norm4, layer, wg, wu, wd, final, batch, lp, seq):
    d = norm4.shape[-1]
    d_ff = wg.shape[-1]
    if final:
        specs = lambda width: _real_specs(width, lp, seq)
        out_rows = batch * seq
    else:
        specs = _tile_specs
        out_rows = batch * lp
    h_specs = specs(d)
    n_rows = len(h_specs)
    in_specs, args = [], []
    for x in xs:
        in_specs += specs(x.shape[1])
        args += [x] * n_rows
    in_specs += h_specs + [_resident(w_out.shape), _pick(norm4, layer, 3),
                           _pick(norm4, layer, 4), _pick(norm4, layer, 5)]
    prep = d_ff // FF_TILE
    in_specs = [_after_prep(s, prep) for s in in_specs] + _weight_chunk_specs(wg, wd, layer, 1)
    args += [h] * n_rows + [w_out, norm4, norm4, norm4, wg, wu, wd]
    return pl.pallas_call(
        functools.partial(_proj_ffn_body, n_rows=n_rows, n_x=len(xs)),
        grid=(prep + out_rows // ROW_TILE,),
        in_specs=in_specs,
        out_specs=_after_prep(pl.BlockSpec((ROW_TILE, d), lambda i: (i, 0)), prep),
        out_shape=jax.ShapeDtypeStruct((out_rows, d), F32),
        scratch_shapes=[pltpu.VMEM((ROW_TILE, d_ff), BF16)] + _weight_scratch(wg),
        compiler_params=_params("arbitrary"),
        name="proj_ffn",
    )(*args)


def _t5_bucket_np(rel):
    n = np.maximum(rel, 0)
    max_exact = REL_BUCKETS // 2
    n_f = np.maximum(n, 1).astype(np.float32)
    large = max_exact + (np.log(n_f / np.float32(max_exact))
                         / np.float32(math.log(REL_MAX_DIST / max_exact))
                         * np.float32(REL_BUCKETS - max_exact)).astype(np.int32)
    large = np.minimum(large, REL_BUCKETS - 1)
    return np.where(n < max_exact, n, large).astype(np.int32)


def _swa_bucket_index():
    i = np.arange(SWA_BLOCK)[:, None]
    c = np.arange(SWA_BLOCK)[None, :]
    out = np.full((3, SWA_BLOCK, 2 * SWA_BLOCK), -1, np.int32)
    for variant in range(3):
        pos_q = variant * SWA_BLOCK + i - PAD
        pos_k = variant * SWA_BLOCK + c - PAD - np.where(c > i, SWA_BLOCK, 0)
        rel_b = pos_q - pos_k
        assert ((rel_b >= 0) & (rel_b < SWA_BLOCK)).all()
        out[variant, :, :SWA_BLOCK] = np.where(pos_k >= N_META, _t5_bucket_np(rel_b), -1)
        rel_m = pos_q - c
        meta = (c < N_META) & (rel_m >= 0)
        out[variant, :, SWA_BLOCK:] = np.where(meta, _t5_bucket_np(rel_m), -1)
    return out


_SWA_BUCKET_INDEX = _swa_bucket_index()


def _bias_body(tbl_ref, idx_ref, o_ref):
    idx = idx_ref[0]
    group = SWA_Q_HEADS // SWA_KV_HEADS
    for head in range(SWA_Q_HEADS):
        acc = jnp.full(idx.shape, NEG_INF, F32)
        for b in range(REL_BUCKETS):
            acc = jnp.where(idx == b, tbl_ref[b, head], acc)
        rows = slice((head % group) * SWA_BLOCK, (head % group + 1) * SWA_BLOCK)
        o_ref[0, head // group, rows, :] = acc


def _swa_bias(rel_table):
    idx = jnp.asarray(_SWA_BUCKET_INDEX)
    group = SWA_Q_HEADS // SWA_KV_HEADS
    width = idx.shape[2]
    return pl.pallas_call(
        _bias_body,
        grid=(3,),
        in_specs=[pl.BlockSpec(memory_space=pltpu.SMEM),
                  pl.BlockSpec((1, SWA_BLOCK, width), lambda v: (v, 0, 0))],
        out_specs=pl.BlockSpec((1, SWA_KV_HEADS, group * SWA_BLOCK, width),
                               lambda v: (v, 0, 0, 0)),
        out_shape=jax.ShapeDtypeStruct((3, SWA_KV_HEADS, group * SWA_BLOCK, width), F32),
        compiler_params=_params("arbitrary"),
        name="swa_bias",
    )(rel_table, idx)


def _swa_stages(sink_ref, q_ref, kc_ref, kp_ref, km_ref, vc_ref, vp_ref, vm_ref, bias_ref,
                step, blocks, store):
    dh = SWA_HEAD_DIM
    blk = SWA_BLOCK
    group = SWA_Q_HEADS // SWA_KV_HEADS
    scale = dh ** -0.5
    zpad = jnp.zeros((blk - N_META, dh), BF16)
    row = lax.broadcasted_iota(jnp.int32, (group * blk, blk), 0) % blk
    col = lax.broadcasted_iota(jnp.int32, (group * blk, blk), 1)
    from_prev = col > row

    ones_col = (lax.broadcasted_iota(jnp.int32, (3 * blk, dh), 1) == 0).astype(BF16)

    units = [(j, hk) for j in range(blocks) for hk in range(SWA_KV_HEADS)]
    scores, values = [], []
    for j, hk in units:
        r = slice(j * blk, (j + 1) * blk)
        kv = slice(hk * dh, (hk + 1) * dh)
        k_prev = kp_ref[:, kv] if j == 0 else kc_ref[(j - 1) * blk:j * blk, kv]
        v_prev = vp_ref[:, kv] if j == 0 else vc_ref[(j - 1) * blk:j * blk, kv]
        k_all = jnp.concatenate([k_prev, kc_ref[r, kv], km_ref[:, kv], zpad], axis=0)
        v_all = jnp.concatenate([v_prev, vc_ref[r, kv], vm_ref[:, kv], zpad], axis=0)
        values.append(jnp.concatenate([v_all, ones_col], axis=1))
        q = jnp.concatenate([q_ref[r, (hk * group + g) * dh:(hk * group + g + 1) * dh]
                             for g in range(group)], axis=0) * scale
        scores.append(_dot_nt(q, k_all))
        yield

    probs, sink_terms = [], []
    for (j, hk), s in zip(units, scores):
        bias = bias_ref[jnp.minimum(step * blocks + j, 2), hk]
        s_band = jnp.where(from_prev, s[:, :blk], s[:, blk:2 * blk]) + bias[:, :blk]
        s_meta = s[:, 2 * blk:] + bias[:, blk:]
        sink = jnp.concatenate([jnp.full((blk, 1), sink_ref[hk * group + g], F32)
                                for g in range(group)], axis=0)
        m = jnp.maximum(jnp.max(jnp.maximum(s_band, s_meta), axis=-1, keepdims=True), sink)
        e_band = jnp.exp(s_band - m)
        e_meta = jnp.exp(s_meta - m)
        sink_terms.append(jnp.exp(sink - m))
        probs.append(jnp.concatenate([jnp.where(from_prev, e_band, 0.0),
                                      jnp.where(from_prev, 0.0, e_band), e_meta],
                                     axis=1).astype(BF16))

    for (j, hk), p, v_ext, sink_term in zip(units, probs, values, sink_terms):
        pv = _dot(p, v_ext)
        o = pv[:, :dh] / (pv[:, dh:dh + 1] + sink_term)
        for g in range(group):
            hq = hk * group + g
            store(slice(j * blk, (j + 1) * blk), slice(hq * dh, (hq + 1) * dh),
                  o[g * blk:(g + 1) * blk])
        yield


def _swa_body(sink_ref, q_ref, kc_ref, kp_ref, km_ref, vc_ref, vp_ref, vm_ref, bias_ref, o_ref,
              *, blocks):
    def store(rows, cols, value):
        o_ref[rows, cols] = value.astype(o_ref.dtype)

    _drain(_swa_stages(sink_ref, q_ref, kc_ref, kp_ref, km_ref, vc_ref, vp_ref, vm_ref,
                       bias_ref, pl.program_id(1), blocks, store))


def _swa(q, k, v, bias, sinks, batch, lp, blocks):
    rows = q.shape[0]
    nb = lp // SWA_BLOCK
    steps = nb // blocks
    meta_blocks = lp // N_META
    cur = lambda b, s: (b * steps + s, 0)
    prev = lambda b, s: (b * nb + jnp.maximum(s * blocks - 1, 0), 0)
    meta = lambda b, s: (b * meta_blocks + PAD // N_META, 0)
    kv_w = k.shape[1]
    rs = blocks * SWA_BLOCK
    return pl.pallas_call(
        functools.partial(_swa_body, blocks=blocks),
        grid=(batch, steps),
        in_specs=[pl.BlockSpec(memory_space=pltpu.SMEM),
                  pl.BlockSpec((rs, q.shape[1]), cur),
                  pl.BlockSpec((rs, kv_w), cur), pl.BlockSpec((SWA_BLOCK, kv_w), prev),
                  pl.BlockSpec((N_META, kv_w), meta),
                  pl.BlockSpec((rs, kv_w), cur), pl.BlockSpec((SWA_BLOCK, kv_w), prev),
                  pl.BlockSpec((N_META, kv_w), meta),
                  _resident(bias.shape)],
        out_specs=pl.BlockSpec((rs, q.shape[1]), cur),
        out_shape=jax.ShapeDtypeStruct((rows, q.shape[1]), BF16),
        compiler_params=_params("parallel", "arbitrary"),
        name="swa",
    )(sinks, q, k, k, k, v, v, v, bias)


def _tri_masks():
    row = lax.broadcasted_iota(jnp.int32, (CHUNK, CHUNK), 0)
    col = lax.broadcasted_iota(jnp.int32, (CHUNK, CHUNK), 1)
    return row >= col, row > col, row == col


def _deltanet_stages(act_ref, z_ref, gates_ref, alog_ref, dtb_ref, nw_ref, s_ref, chunks,
                     store):
    nh, dh = DN_HEADS, DN_HEAD_DIM
    gt = gates_ref[...]
    beta_full = jax.nn.sigmoid(gt)
    g_full = -jnp.exp(alog_ref[...]) * _softplus(gt + dtb_ref[...])
    incl, strict, diag = _tri_masks()
    eye = diag.astype(F32)
    tri = incl.astype(F32)
    gc_chunks = [_dot(tri, g_full[c * CHUNK:(c + 1) * CHUNK, :], HIGHEST)
                 for c in range(chunks)]
    yield

    lane = lax.broadcasted_iota(jnp.int32, (CHUNK, 2 * CHUNK), 1)
    row = lax.broadcasted_iota(jnp.int32, (CHUNK, 2 * CHUNK), 0)
    first = lane < CHUNK
    col = jnp.where(first, lane, lane - CHUNK)
    incl2, strict2 = row >= col, row > col
    eye2 = (row == col).astype(F32)
    first_rows2 = lax.broadcasted_iota(jnp.int32, (2 * CHUNK, 2 * CHUNK), 1) < CHUNK

    def block_diag(a, b):
        zero = jnp.zeros_like(a)
        return jnp.concatenate([jnp.concatenate([a, zero], axis=1),
                                jnp.concatenate([zero, b], axis=1)], axis=0)

    def block_diag_halves(x):
        zero = jnp.zeros_like(x)
        return jnp.concatenate([jnp.where(first, x, zero), jnp.where(first, zero, x)], axis=0)

    pairs = [(c, p) for c in range(chunks) for p in range(nh // 2)]
    neg_a, akt16, rhs16, qd16, decay = [], [], [], {}, {}
    for c, p in pairs:
        if p == 0 and c % 2 == 0 and c > 0:
            yield
        r = slice(c * CHUNK, (c + 1) * CHUNK)
        ks, kqs, kdecs, gcs, rhs = [], [], [], [], []
        for h in (2 * p, 2 * p + 1):
            q = act_ref[r, h * dh:(h + 1) * dh].astype(F32)
            k = act_ref[r, DN_DIM + h * dh:DN_DIM + (h + 1) * dh].astype(F32)
            v = act_ref[r, 2 * DN_DIM + h * dh:2 * DN_DIM + (h + 1) * dh].astype(F32)
            q = q * lax.rsqrt(jnp.sum(q * q, axis=-1, keepdims=True) + 1e-6) * dh ** -0.5
            k = k * lax.rsqrt(jnp.sum(k * k, axis=-1, keepdims=True) + 1e-6)
            beta = jnp.broadcast_to(beta_full[r, h:h + 1], (CHUNK, dh))
            gc = jnp.broadcast_to(gc_chunks[c][:, nh + h:nh + h + 1], (CHUNK, dh))
            gc_last = gc[CHUNK - 1:CHUNK, :]
            k_beta = k * beta
            ks.append(k)
            kqs.append(jnp.concatenate([k_beta, q], axis=0).astype(BF16))
            kdecs.append(k * jnp.exp(gc_last - gc))
            gcs.append(gc)
            rhs.append(jnp.concatenate([v * beta, k_beta * jnp.exp(gc)], axis=1).astype(BF16))
            qd16[c, h] = (q * jnp.exp(gc)).astype(BF16)
            decay[c, h] = jnp.exp(gc_last)
        gc2 = jnp.where(first, gcs[0], gcs[1])
        gc_row2 = jnp.transpose(jnp.concatenate(gcs, axis=0))[:CHUNK, :]
        gamma2 = jnp.where(incl2, jnp.exp(jnp.where(incl2, gc2 - gc_row2, 0.0)), 0.0)
        k2 = jnp.concatenate(ks, axis=0).astype(BF16)
        kq2 = jnp.where(first_rows2, _dot_nt(kqs[0], k2), _dot_nt(kqs[1], k2))
        neg_a.append(jnp.where(strict2, -(kq2[:CHUNK] * gamma2), 0.0))
        attn16 = jnp.where(incl2, kq2[CHUNK:] * gamma2, 0.0).astype(BF16)
        kdt16 = jnp.transpose(jnp.concatenate(kdecs, axis=0)).astype(BF16)
        akt16.append(jnp.concatenate([attn16, kdt16], axis=0))
        rhs16.append(block_diag(rhs[0], rhs[1]))

    yield
    levels = int(math.log2(CHUNK)) - 1
    t = [eye2 + x for x in neg_a]
    p16 = [x.astype(BF16) for x in neg_a]
    p16 = [_dot(x, block_diag_halves(x)).astype(BF16) for x in p16]
    yield
    for level in range(levels):
        if level + 1 < levels:
            tp = [_dot(jnp.concatenate([y.astype(BF16), x], axis=0), block_diag_halves(x))
                  for y, x in zip(t, p16)]
            t = [y + r[:CHUNK] for y, r in zip(t, tp)]
            p16 = [r[CHUNK:].astype(BF16) for r in tp]
        else:
            t = [y + _dot(y.astype(BF16), block_diag_halves(x)) for y, x in zip(t, p16)]
        yield
    uw2 = [_dot(y.astype(BF16), b) for y, b in zip(t, rhs16)]
    u, wq16 = {}, {}
    for (c, p), x in zip(pairs, uw2):
        for half, h in enumerate((2 * p, 2 * p + 1)):
            u[c, h] = x[:, half * 2 * dh:half * 2 * dh + dh]
            w16 = x[:, half * 2 * dh + dh:(half + 1) * 2 * dh].astype(BF16)
            wq16[c, h] = jnp.concatenate([w16, qd16[c, h]], axis=0)
    yield

    nw = nw_ref[...]
    state = [s_ref[h] for h in range(nh)]
    for c in range(chunks):
        r = slice(c * CHUNK, (c + 1) * CHUNK)
        s16 = [s.astype(BF16) for s in state]
        wqs = [_dot(wq16[c, h], s16[h]) for h in range(nh)]
        yield
        for p in range(nh // 2):
            ha, hb = 2 * p, 2 * p + 1
            vn_a = (u[c, ha] - wqs[ha][:CHUNK]).astype(BF16)
            vn_b = (u[c, hb] - wqs[hb][:CHUNK]).astype(BF16)
            ak2 = _dot(akt16[c * (nh // 2) + p], block_diag(vn_a, vn_b))
            for half, h in enumerate((ha, hb)):
                hc = slice(h * dh, (h + 1) * dh)
                ak = ak2[:, half * dh:(half + 1) * dh]
                state[h] = state[h] * decay[c, h] + ak[CHUNK:]
                out = wqs[h][CHUNK:] + ak[:CHUNK]
                store(r, hc, _rms(out, nw) * _silu(z_ref[r, hc].astype(F32)))
        yield
    for h in range(nh):
        s_ref[h] = state[h]


def _deltanet_body(act_ref, z_ref, gates_ref, alog_ref, dtb_ref, nw_ref, o_ref, s_ref, *,
                   chunks):
    @pl.when(pl.program_id(1) == 0)
    def _():
        s_ref[...] = jnp.zeros_like(s_ref)

    def store(rows, cols, value):
        o_ref[rows, cols] = value.astype(o_ref.dtype)

    _drain(_deltanet_stages(act_ref, z_ref, gates_ref, alog_ref, dtb_ref, nw_ref, s_ref,
                            chunks, store))


def _deltanet(act, z, gates, alog_vec, dtb_vec, nw, batch, lp, chunks):
    rows = act.shape[0]
    rs = chunks * CHUNK
    steps = lp // rs
    idx = lambda b, s: (b * steps + s, 0)
    return pl.pallas_call(
        functools.partial(_deltanet_body, chunks=chunks),
        grid=(batch, steps),
        in_specs=[pl.BlockSpec((rs, act.shape[1]), idx), pl.BlockSpec((rs, z.shape[1]), idx),
                  pl.BlockSpec((rs, gates.shape[1]), idx),
                  _resident(alog_vec.shape), _resident(dtb_vec.shape), _resident(nw.shape)],
        out_specs=pl.BlockSpec((rs, DN_DIM), idx),
        out_shape=jax.ShapeDtypeStruct((rows, DN_DIM), BF16),
        scratch_shapes=[pltpu.VMEM((DN_HEADS, DN_HEAD_DIM, DN_HEAD_DIM), F32)],
        compiler_params=_params("parallel", "arbitrary"),
        name="deltanet",
    )(act, z, gates, alog_vec, dtb_vec, nw)


def _mix_ffn_body(*refs, blocks, chunks, steps, last):
    *refs, wg_s, wu_s, wd_s = refs
    prep = _prepare_weights(refs[20:23], (wg_s, wu_s, wd_s))
    t = pl.program_id(0) - prep
    refs = refs[:20] + [wg_s, wu_s, wd_s] + refs[23:]

    @pl.when(t >= 0)
    def _():
        _mix_ffn_step(t, *refs, blocks=blocks, chunks=chunks, steps=steps, last=last)


def _mix_ffn_step(t, sink_ref, q_ref, kc_ref, kp_ref, km_ref, vc_ref, vp_ref, vm_ref, bias_ref,
                  act_ref, z_ref, gates_ref, alog_ref, dtb_ref, dnw_ref,
                  h_ref, wo_ref, nwo_ref, nwa_ref, nwb_ref, wg_ref, wu_ref, wd_ref,
                  o_ref, s_ref, mix_ref, a_ref, *, blocks, chunks, steps, last):
    seq_step = jnp.minimum(t, last) % steps
    write_slot = t % 2
    read_slot = 1 - write_slot
    swa_width = q_ref.shape[1]
    groups = q_ref.shape[0] // GROUP_ROWS

    @pl.when(t == 0)
    def _():
        mix_ref[...] = jnp.zeros_like(mix_ref)

    @pl.when(seq_step == 0)
    def _():
        s_ref[...] = jnp.zeros_like(s_ref)

    def store_swa(rows, cols, value):
        mix_ref[write_slot, rows, cols] = value.astype(mix_ref.dtype)

    def store_dn(rows, cols, value):
        cols = slice(swa_width + cols.start, swa_width + cols.stop)
        mix_ref[write_slot, rows, cols] = value.astype(mix_ref.dtype)

    def mixer_stages():
        dn = _deltanet_stages(act_ref, z_ref, gates_ref, alog_ref, dtb_ref, dnw_ref,
                              s_ref, chunks, store_dn)
        swa = _swa_stages(sink_ref, q_ref, kc_ref, kp_ref, km_ref, vc_ref, vp_ref,
                          vm_ref, bias_ref, seq_step, blocks, store_swa)
        for _ in dn:
            yield
            if next(swa, StopIteration) is not StopIteration:
                yield
        yield from swa

    def ffn_stages():
        accs = []
        for part in range(groups):
            accs.append(_dot(mix_ref[read_slot, _group(part), :], wo_ref[...]))
            yield
        hs = [h_ref[_group(part), :] + _rms(acc, nwo_ref[...]) for part, acc in enumerate(accs)]
        outs = []
        yield from _ffn_stages(hs, nwa_ref[...], nwb_ref[...], wg_ref, wu_ref, wd_ref, a_ref,
                               outs)
        for part, out in enumerate(outs):
            o_ref[_group(part), :] = out

    _interleave(ffn_stages(), mixer_stages())


def _mix_ffn(qa, ka, va, bias, sinks, act, z, gates, alog_vec, dtb_vec, dn_nw, h, w_out, norm4,
             layer, wg, wu, wd, batch, lp, blocks, chunks):
    d = norm4.shape[-1]
    d_ff = wg.shape[-1]
    rs = blocks * SWA_BLOCK
    assert rs == chunks * CHUNK and rs % GROUP_ROWS == 0
    nb = lp // SWA_BLOCK
    steps = lp // rs
    last = batch * steps - 1
    meta_blocks = lp // N_META

    def tile(t):
        return jnp.minimum(t, last)

    cur = lambda t: (tile(t), 0)
    prev = lambda t: ((tile(t) // steps) * nb + jnp.maximum((tile(t) % steps) * blocks - 1, 0), 0)
    meta = lambda t: ((tile(t) // steps) * meta_blocks + PAD // N_META, 0)
    behind = lambda t: (jnp.maximum(t - 1, 0), 0)
    kv_w = ka.shape[1]
    prep = d_ff // FF_TILE
    in_specs = [pl.BlockSpec(memory_space=pltpu.SMEM),
                pl.BlockSpec((rs, qa.shape[1]), cur),
                pl.BlockSpec((rs, kv_w), cur), pl.BlockSpec((SWA_BLOCK, kv_w), prev),
                pl.BlockSpec((N_META, kv_w), meta),
                pl.BlockSpec((rs, kv_w), cur), pl.BlockSpec((SWA_BLOCK, kv_w), prev),
                pl.BlockSpec((N_META, kv_w), meta),
                _resident(bias.shape),
                pl.BlockSpec((rs, act.shape[1]), cur), pl.BlockSpec((rs, z.shape[1]), cur),
                pl.BlockSpec((rs, gates.shape[1]), cur),
                _resident(alog_vec.shape), _resident(dtb_vec.shape), _resident(dn_nw.shape),
                pl.BlockSpec((rs, d), behind), _resident(w_out.shape),
                _pick(norm4, layer, 3), _pick(norm4, layer, 4), _pick(norm4, layer, 5)]
    return pl.pallas_call(
        functools.partial(_mix_ffn_body, blocks=blocks, chunks=chunks, steps=steps, last=last),
        grid=(prep + last + 2,),
        in_specs=[_after_prep(s, prep) for s in in_specs]
        + _weight_chunk_specs(wg, wd, layer, 1),
        out_specs=_after_prep(pl.BlockSpec((rs, d), behind), prep),
        out_shape=jax.ShapeDtypeStruct(h.shape, F32),
        scratch_shapes=[pltpu.VMEM((DN_HEADS, DN_HEAD_DIM, DN_HEAD_DIM), F32),
                        pltpu.VMEM((2, rs, qa.shape[1] + DN_DIM), BF16),
                        pltpu.VMEM((rs, d_ff), BF16)] + _weight_scratch(wg),
        compiler_params=_params("arbitrary"),
        name="mix_ffn",
    )(sinks, qa, ka, ka, ka, va, va, va, bias, act, z, gates, alog_vec, dtb_vec, dn_nw,
      h, w_out, norm4, norm4, norm4, wg, wu, wd)


def _gla_body(q_ref, k_ref, v_ref, g_ref, gk_ref, wgu_ref, bg_ref, nw_ref, o_ref, s_ref, *,
              chunks):
    nh = GLA_HEADS
    dk = q_ref.shape[1] // nh
    dv = v_ref.shape[1] // nh

    @pl.when(pl.program_id(1) == 0)
    def _():
        s_ref[...] = jnp.zeros_like(s_ref)

    logits = _dot(gk_ref[...].astype(BF16), wgu_ref[...]) + bg_ref[...]
    glog = (jnp.minimum(logits, 0.0) - jnp.log(1.0 + jnp.exp(-jnp.abs(logits)))) / GLA_GATE_NORM
    incl, _, _ = _tri_masks()
    tri = incl.astype(F32)
    bcum = [_dot(tri, glog[c * CHUNK:(c + 1) * CHUNK, :], HIGHEST) for c in range(chunks)]

    items = [(c, h) for c in range(chunks) for h in range(nh)]
    qd16, kn16, kdt16, decay = [], [], [], []
    for c, h in items:
        r = slice(c * CHUNK, (c + 1) * CHUNK)
        kc = slice(h * dk, (h + 1) * dk)
        bc = bcum[c][:, kc]
        q = q_ref[r, kc].astype(F32) * dk ** -0.5
        k = k_ref[r, kc].astype(F32)
        b_last = bc[CHUNK - 1:CHUNK, :]
        qd16.append((q * jnp.exp(bc)).astype(BF16))
        kn16.append((k * jnp.exp(-bc)).astype(BF16))
        kdt16.append(jnp.transpose(k * jnp.exp(b_last - bc)).astype(BF16))
        decay_rows = jnp.broadcast_to(jnp.exp(b_last), (8, dk))
        decay.append(jnp.transpose(decay_rows)[:, :1])
    values = [v_ref[c * CHUNK:(c + 1) * CHUNK, h * dv:(h + 1) * dv] for c, h in items]
    attn16 = [jnp.where(incl, _dot_nt(a, b), 0.0).astype(BF16) for a, b in zip(qd16, kn16)]
    intra = [_dot(a, v) for a, v in zip(attn16, values)]
    update = [_dot(kt, v) for kt, v in zip(kdt16, values)]

    nw = nw_ref[...]
    state = [s_ref[h] for h in range(nh)]
    for c in range(chunks):
        r = slice(c * CHUNK, (c + 1) * CHUNK)
        for h in range(nh):
            i = c * nh + h
            vc = slice(h * dv, (h + 1) * dv)
            o = intra[i] + _dot(qd16[i], state[h].astype(BF16))
            state[h] = state[h] * decay[i] + update[i]
            o = _rms(o, nw) * _silu(g_ref[r, vc].astype(F32))
            o_ref[r, vc] = o.astype(o_ref.dtype)
    for h in range(nh):
        s_ref[h] = state[h]


def _gla(q, k, v, g, gk, wgu, bg, nw, batch, lp, chunks):
    rows = q.shape[0]
    rs = chunks * CHUNK
    steps = lp // rs
    idx = lambda b, s: (b * steps + s, 0)
    dk = q.shape[1] // GLA_HEADS
    dv = v.shape[1] // GLA_HEADS
    return pl.pallas_call(
        functools.partial(_gla_body, chunks=chunks),
        grid=(batch, steps),
        in_specs=[pl.BlockSpec((rs, q.shape[1]), idx), pl.BlockSpec((rs, k.shape[1]), idx),
                  pl.BlockSpec((rs, v.shape[1]), idx), pl.BlockSpec((rs, g.shape[1]), idx),
                  pl.BlockSpec((rs, gk.shape[1]), idx), _resident(wgu.shape),
                  _resident(bg.shape), _resident(nw.shape)],
        out_specs=pl.BlockSpec((rs, v.shape[1]), idx),
        out_shape=jax.ShapeDtypeStruct((rows, v.shape[1]), BF16),
        scratch_shapes=[pltpu.VMEM((GLA_HEADS, dk, dv), F32)],
        compiler_params=_params("parallel", "arbitrary"),
        name="gla",
    )(q, k, v, g, gk, wgu, bg, nw)


def _per_step(n, preferred):
    for c in preferred:
        if n % c == 0:
            return c
    return 1


def kernel(x, meta_tokens, norm_w, ffn_w_gate, ffn_w_up, ffn_w_down, rel_bias_table,
           even_w_in, even_conv_w, swa_sinks, dn_a_log, dn_dt_bias, dn_norm_w, even_w_out,
           odd_w_in, gla_w_gate_up, gla_b_gate, gla_norm_w, odd_w_out):
    batch, seq, d = x.shape
    depth = norm_w.shape[0]
    lp = SWA_BLOCK + seq
    assert seq % ROW_TILE == 0 and (batch * lp) % ROW_TILE == 0
    rows = batch * lp
    chunks = _per_step(lp // CHUNK, (6, 3, 2))
    swa_blocks = _per_step(lp // SWA_BLOCK, (3, 2))

    meta_block = jnp.pad(meta_tokens.astype(x.dtype), ((PAD, 0), (0, 0)))
    norm4 = norm_w[:, :, None, :]
    wg, wu, wd = ffn_w_gate, ffn_w_up, ffn_w_down

    swa_q = SWA_Q_HEADS * SWA_HEAD_DIM
    swa_kv = SWA_KV_HEADS * SWA_HEAD_DIM
    bias = None
    h = x.reshape(batch * seq, d)
    for layer in range(depth):
        i = layer // 2
        embed = meta_block if layer == 0 else None
        final = layer == depth - 1
        if layer % 2 == 0:
            assert even_w_in.shape[2] == swa_q + 2 * swa_kv + 4 * DN_DIM + 2 * DN_HEADS
            widths = (swa_q, swa_kv, swa_kv, 3 * DN_DIM, DN_DIM, LANE)
            h, qa, ka, va, act_b, z_b, gates = _ffn_proj(
                h, embed, norm4, layer, wg, wu, wd, even_w_in, i, widths,
                (BF16, BF16, BF16, BF16, BF16, F32), rows, lp, seq,
                conv_out=3, conv_w=even_conv_w[i])
            if bias is None:
                bias = _swa_bias(rel_bias_table)
            lane_pad = (DN_HEADS, LANE - 2 * DN_HEADS)
            alog_vec = jnp.pad(dn_a_log[i], lane_pad)[None, :]
            dtb_vec = jnp.pad(dn_dt_bias[i], lane_pad)[None, :]
            dn_nw = dn_norm_w[i][None, :]
            w_out = even_w_out[i]
            if not final and swa_blocks * SWA_BLOCK == chunks * CHUNK:
                h = _mix_ffn(qa, ka, va, bias, swa_sinks[i], act_b, z_b, gates, alog_vec,
                             dtb_vec, dn_nw, h, w_out.astype(BF16), norm4, layer, wg, wu, wd,
                             batch, lp, swa_blocks, chunks)
                continue
            o_a = _swa(qa, ka, va, bias, swa_sinks[i], batch, lp, swa_blocks)
            o_b = _deltanet(act_b, z_b, gates, alog_vec, dtb_vec, dn_nw, batch, lp, chunks)
            mixed = [o_a, o_b]
        else:
            key_dim = gla_w_gate_up.shape[2]
            val_dim = odd_w_out.shape[1]
            widths = (key_dim, key_dim, val_dim, val_dim, LANE)
            h, q, k, v, g, gk = _ffn_proj(
                h, embed, norm4, layer, wg, wu, wd, odd_w_in, i, widths,
                (BF16, BF16, BF16, BF16, F32), rows, lp, seq)
            wgu = jnp.pad(gla_w_gate_up[i], ((0, LANE - GLA_GATE_RANK), (0, 0))).astype(BF16)
            o = _gla(q, k, v, g, gk, wgu, gla_b_gate[i][None, :], gla_norm_w[i][None, :],
                     batch, lp, chunks)
            mixed, w_out = [o], odd_w_out[i]
        h = _proj_ffn(mixed, h, w_out.astype(BF16), norm4, layer, wg, wu, wd, final,
                      batch, lp, seq)
    return h.reshape(batch, seq, d)
```

```python
import functools
import math

import numpy as np
import jax
import jax.numpy as jnp
from jax import lax
from jax.experimental import pallas as pl
from jax.experimental.pallas import tpu as pltpu

F32 = jnp.float32
BF16 = jnp.bfloat16
HIGHEST = lax.Precision.HIGHEST

N_META = 16
NORM_EPS = 1e-6
NEG_INF = -1e30
SWA_Q_HEADS = 8
SWA_KV_HEADS = 2
SWA_HEAD_DIM = 64
SWA_BLOCK = 128
REL_BUCKETS = 32
REL_MAX_DIST = 128
DN_HEADS = 4
DN_HEAD_DIM = 128
DN_DIM = DN_HEADS * DN_HEAD_DIM
GLA_HEADS = 4
GLA_GATE_RANK = 16
GLA_GATE_NORM = 16.0
CHUNK = 64

PAD = SWA_BLOCK - N_META
LANE = 128
ROW_TILE = 512
FF_TILE = 256
VMEM_LIMIT = 60 * 1024 * 1024


def _rms(x, w):
    return x * lax.rsqrt(jnp.mean(x * x, axis=-1, keepdims=True) + NORM_EPS) * w


def _silu(x):
    return x * jax.nn.sigmoid(x)


def _softplus(x):
    return jnp.maximum(x, 0.0) + jnp.log(1.0 + jnp.exp(-jnp.abs(x)))


def _dot(a, b, precision=None):
    return jnp.dot(a, b, preferred_element_type=F32, precision=precision)


def _dot_nt(a, b):
    return lax.dot_general(a, b, (((1,), (1,)), ((), ())), preferred_element_type=F32)


def _resident(shape):
    nd = len(shape)
    return pl.BlockSpec(shape, lambda *_: (0,) * nd, pipeline_mode=pl.Buffered(1))


def _params(*semantics):
    return pltpu.CompilerParams(dimension_semantics=semantics, vmem_limit_bytes=VMEM_LIMIT)


SUB = ROW_TILE // SWA_BLOCK


def _pick(arr, *lead):
    rest = arr.shape[len(lead):]
    index = tuple(lead) + (0,) * len(rest)
    return pl.BlockSpec((None,) * len(lead) + rest, lambda i: index,
                        pipeline_mode=pl.Buffered(1))


def _after_prep(spec, prep):
    if spec.index_map is None:
        return spec
    index = spec.index_map
    return pl.BlockSpec(spec.block_shape, lambda i: index(jnp.maximum(i - prep, 0)),
                        pipeline_mode=spec.pipeline_mode)


def _weight_chunk_specs(wg, wd, layer, which):
    d, d_ff = wg.shape[-2:]
    last = d_ff // FF_TILE - 1
    cols = pl.BlockSpec((None, None, d, FF_TILE), lambda i: (layer, which, 0, jnp.minimum(i, last)))
    rows = pl.BlockSpec((None, None, FF_TILE, d), lambda i: (layer, which, jnp.minimum(i, last), 0))
    return [cols, cols, rows]


def _weight_scratch(wg):
    d, d_ff = wg.shape[-2:]
    chunks = d_ff // FF_TILE
    return [pltpu.VMEM((chunks, d, FF_TILE), BF16), pltpu.VMEM((chunks, d, FF_TILE), BF16),
            pltpu.VMEM((d_ff, d), BF16)]


def _prepare_weights(chunk_refs, scratch_refs):
    wg_blk, wu_blk, wd_blk = chunk_refs
    wg_s, wu_s, wd_s = scratch_refs
    prep = wg_s.shape[0]
    i = pl.program_id(0)

    @pl.when(i < prep)
    def _():
        wg_s[i] = wg_blk[...].astype(BF16)
        wu_s[i] = wu_blk[...].astype(BF16)
        wd_s[pl.ds(pl.multiple_of(i * FF_TILE, FF_TILE), FF_TILE), :] = wd_blk[...].astype(BF16)

    return prep


def _tile_specs(width):
    return [pl.BlockSpec((ROW_TILE, width), lambda i: (i, 0))]


def _real_specs(width, lp, seq):
    tiles = seq // ROW_TILE
    blocks = lp // SWA_BLOCK

    def spec(j):
        return pl.BlockSpec((SWA_BLOCK, width),
                            lambda i: ((i // tiles) * blocks + 1 + (i % tiles) * SUB + j, 0))
    return [spec(j) for j in range(SUB)]


def _embed_specs(width, lp, seq):
    blocks = lp // SWA_BLOCK
    x_blocks = seq // SWA_BLOCK

    def spec(j):
        def index(i):
            blk = i * SUB + j
            return ((blk // blocks) * x_blocks + jnp.maximum(blk % blocks - 1, 0), 0)
        return pl.BlockSpec((SWA_BLOCK, width), index)
    return [spec(j) for j in range(SUB)]


ROW_SPLIT = 4
GROUP_ROWS = ROW_TILE // ROW_SPLIT


def _group(part):
    return slice(part * GROUP_ROWS, (part + 1) * GROUP_ROWS)


def _load_rows(refs, part):
    if len(refs) == 1:
        return refs[0][_group(part), :]
    per = len(refs) // ROW_SPLIT
    return jnp.concatenate([r[...] for r in refs[part * per:(part + 1) * per]], axis=0)


def _drain(stages):
    for _ in stages:
        pass


def _interleave(main, side):
    for _ in main:
        next(side, None)
    _drain(side)


def _ffn_stages(xs, nwa, nwb, wg_ref, wu_ref, wd_ref, a_ref, outs):
    hns = [_rms(x, nwa).astype(BF16) for x in xs]
    for part, hn in enumerate(hns):
        for c in range(wg_ref.shape[0]):
            cols = slice(c * FF_TILE, (c + 1) * FF_TILE)
            g = _dot(hn, wg_ref[c])
            u = _dot(hn, wu_ref[c])
            a_ref[_group(part), cols] = (_silu(g) * u).astype(BF16)
            yield
    fs = []
    for part in range(len(xs)):
        fs.append(_dot(a_ref[_group(part), :], wd_ref[...]))
        yield
    outs.extend(x + 0.5 * _rms(f, nwb) for x, f in zip(xs, fs))


def _ffn_core(xs, nwa, nwb, wg_ref, wu_ref, wd_ref, a_ref):
    outs = []
    _drain(_ffn_stages(xs, nwa, nwb, wg_ref, wu_ref, wd_ref, a_ref, outs))
    return outs


CONV_HALO = 8


def _ffn_proj_body(*refs, n_rows, embed_blocks, n_out, conv_out, lp, tail_valid, prologue):
    lead = n_rows + bool(embed_blocks) + (conv_out is not None) + 2
    wg_s, wu_s, wd_s, win_s, wtail_s = refs[-5:]
    nwm_ref, win_blk = refs[lead + 3:lead + 5]
    _prepare_weights(refs[lead:lead + 3], (wg_s, wu_s, wd_s))
    step = pl.program_id(0) - prologue
    win_chunks = win_s.shape[0]

    @pl.when(pl.program_id(0) < win_chunks)
    def _():
        win_s[pl.program_id(0)] = win_blk[...].astype(BF16)

    @pl.when(pl.program_id(0) == win_chunks)
    def _():
        tail = win_blk[:, :LANE]
        lane = lax.broadcasted_iota(jnp.int32, tail.shape, 1)
        wtail_s[...] = jnp.where(lane < tail_valid, tail, 0.0).astype(BF16)

    refs = refs[:lead] + (wg_s, wu_s, wd_s, nwm_ref, win_s, wtail_s) + refs[lead + 5:-5]

    @pl.when(step >= 0)
    def _():
        _ffn_proj_step(step, *refs, n_rows=n_rows, embed_blocks=embed_blocks, n_out=n_out,
                       conv_out=conv_out, lp=lp)


def _ffn_proj_step(step, *refs, n_rows, embed_blocks, n_out, conv_out, lp):
    row_refs = refs[:n_rows]
    refs = refs[n_rows:]
    if embed_blocks:
        meta_ref, refs = refs[0], refs[1:]
    if conv_out is not None:
        convw_ref, refs = refs[0], refs[1:]
        xpad_ref = refs[10 + n_out]

        @pl.when(step == 0)
        def _():
            xpad_ref[0:CONV_HALO, :] = jnp.zeros((CONV_HALO, xpad_ref.shape[1]), F32)

    nwa_ref, nwb_ref, wg_ref, wu_ref, wd_ref, nwm_ref, win_ref, wtail_ref = refs[:8]
    h_ref = refs[8]
    out_refs = refs[9:9 + n_out]
    a_ref = refs[9 + n_out]
    offsets = np.cumsum([0] + [o.shape[1] for o in out_refs])
    xs = []
    for part in range(ROW_SPLIT):
        if embed_blocks:
            per = n_rows // ROW_SPLIT
            first = step * SUB + part * per
            xs.append(jnp.concatenate(
                [jnp.where((first + j) % embed_blocks == 0, meta_ref[...], r[...])
                 for j, r in enumerate(row_refs[part * per:(part + 1) * per])], axis=0))
        else:
            xs.append(_load_rows(row_refs, part))
    hs = _ffn_core(xs, nwa_ref[...], nwb_ref[...], wg_ref, wu_ref, wd_ref, a_ref)
    hns = [_rms(h, nwm_ref[...]).astype(BF16) for h in hs]
    width = win_ref.shape[2]
    for part, (h, hn) in enumerate(zip(hs, hns)):
        rows = _group(part)
        h_ref[rows, :] = h
        for c in range(win_ref.shape[0]):
            p = _dot(hn, win_ref[c])
            for i in range(n_out - 1):
                lo = max(c * width, offsets[i])
                hi = min((c + 1) * width, offsets[i + 1])
                if lo >= hi:
                    continue
                piece = p[:, lo - c * width:hi - c * width]
                cols = slice(lo - offsets[i], hi - offsets[i])
                if i == conv_out:
                    xpad_ref[CONV_HALO + part * GROUP_ROWS:CONV_HALO + (part + 1) * GROUP_ROWS,
                             cols] = piece
                else:
                    out_refs[i][rows, cols] = piece.astype(out_refs[i].dtype)
        out_refs[-1][rows, :] = _dot(hn, wtail_ref[...]).astype(out_refs[-1].dtype)

    if conv_out is not None:
        taps = convw_ref.shape[0]
        cw = convw_ref[...]
        y = cw[taps - 1:taps, :] * xpad_ref[CONV_HALO:, :]
        for j in range(taps - 1):
            lo = CONV_HALO - (taps - 1 - j)
            y = y + cw[j:j + 1, :] * xpad_ref[lo:lo + ROW_TILE, :]
        tail = xpad_ref[ROW_TILE:, :]
        row = (step * ROW_TILE) % lp + lax.broadcasted_iota(jnp.int32, (ROW_TILE, 1), 0)
        row = jnp.where(row >= lp, row - lp, row)
        out_refs[conv_out][...] = jnp.where(row >= PAD, _silu(y), 0.0).astype(
            out_refs[conv_out].dtype)
        xpad_ref[0:CONV_HALO, :] = tail


def _ffn_proj(src, meta_block, norm4, layer, wg, wu, wd, w_in, index, widths, dtypes, rows, lp,
              seq, conv_out=None, conv_w=None):
    d = norm4.shape[-1]
    d_ff = wg.shape[-1]
    prep = d_ff // FF_TILE
    main = sum(widths[:-1])
    tail_valid = w_in.shape[2] - main
    assert 0 < tail_valid <= widths[-1] == LANE and main % LANE == 0 and lp > ROW_TILE
    width = FF_TILE
    assert main % width == 0
    win_chunks = main // width
    prep = max(prep, win_chunks + 1)
    w_in = w_in[index]
    win_spec = pl.BlockSpec((d, width), lambda i: (0, jnp.minimum(i, win_chunks)))
    if meta_block is None:
        row_specs, extra, extra_specs, embed_blocks = _tile_specs(d), [], [], 0
    else:
        row_specs = _embed_specs(d, lp, seq)
        extra, extra_specs, embed_blocks = [meta_block], [_resident(meta_block.shape)], (
            lp // SWA_BLOCK)
    scratch = [pltpu.VMEM((ROW_TILE, d_ff), BF16)]
    if conv_out is not None:
        extra, extra_specs = extra + [conv_w], extra_specs + [_resident(conv_w.shape)]
        scratch.append(pltpu.VMEM((CONV_HALO + ROW_TILE, widths[conv_out]), F32))
    tile = lambda n: pl.BlockSpec((ROW_TILE, n), lambda i: (i, 0))
    shift = lambda specs: [_after_prep(s, prep) for s in specs]
    return pl.pallas_call(
        functools.partial(_ffn_proj_body, n_rows=len(row_specs), embed_blocks=embed_blocks,
                          n_out=len(widths), conv_out=conv_out, lp=lp, tail_valid=tail_valid,
                          prologue=prep),
        grid=(prep + rows // ROW_TILE,),
        in_specs=shift(row_specs + extra_specs
                       + [_pick(norm4, layer, 0), _pick(norm4, layer, 1)])
        + _weight_chunk_specs(wg, wd, layer, 0)
        + shift([_pick(norm4, layer, 2)]) + [win_spec],
        out_specs=shift([tile(d)] + [tile(n) for n in widths]),
        out_shape=[jax.ShapeDtypeStruct((rows, d), F32)]
        + [jax.ShapeDtypeStruct((rows, n), dt) for n, dt in zip(widths, dtypes)],
        scratch_shapes=scratch + _weight_scratch(wg)
        + [pltpu.VMEM((win_chunks, d, width), BF16), pltpu.VMEM((d, LANE), BF16)],
        compiler_params=_params("arbitrary"),
        name="ffn_proj",
    )(*([src] * len(row_specs)), *extra, norm4, norm4, wg, wu, wd, norm4, w_in)


def _proj_ffn_body(*refs, n_rows, n_x):
    x_groups = [refs[g * n_rows:(g + 1) * n_rows] for g in range(n_x)]
    refs = refs[n_x * n_rows:]
    h_refs = refs[:n_rows]
    (wo_ref, nwo_ref, nwa_ref, nwb_ref, wg_blk, wu_blk, wd_blk, o_ref, a_ref,
     wg_ref, wu_ref, wd_ref) = refs[n_rows:]
    prep = _prepare_weights((wg_blk, wu_blk, wd_blk), (wg_ref, wu_ref, wd_ref))

    @pl.when(pl.program_id(0) >= prep)
    def _():
        accs = []
        for part in range(ROW_SPLIT):
            acc = None
            off = 0
            for group in x_groups:
                xs = _load_rows(group, part)
                k = xs.shape[1]
                p = _dot(xs, wo_ref[off:off + k, :])
                acc = p if acc is None else acc + p
                off += k
            accs.append(acc)
        hs = [_load_rows(h_refs, part) + _rms(acc, nwo_ref[...])
              for part, acc in enumerate(accs)]
        outs = _ffn_core(hs, nwa_ref[...], nwb_ref[...], wg_ref, wu_ref, wd_ref, a_ref)
        for part, out in enumerate(outs):
            o_ref[_group(part), :] = out


def _proj_ffn(xs, h, w_out, norm4, layer, wg, wu, wd, final, batch, lp, seq):
    d = norm4.shape[-1]
    d_ff = wg.shape[-1]
    if final:
        specs = lambda width: _real_specs(width, lp, seq)
        out_rows = batch * seq
    else:
        specs = _tile_specs
        out_rows = batch * lp
    h_specs = specs(d)
    n_rows = len(h_specs)
    in_specs, args = [], []
    for x in xs:
        in_specs += specs(x.shape[1])
        args += [x] * n_rows
    in_specs += h_specs + [_resident(w_out.shape), _pick(norm4, layer, 3),
                           _pick(norm4, layer, 4), _pick(norm4, layer, 5)]
    prep = d_ff // FF_TILE
    in_specs = [_after_prep(s, prep) for s in in_specs] + _weight_chunk_specs(wg, wd, layer, 1)
    args += [h] * n_rows + [w_out, norm4, norm4, norm4, wg, wu, wd]
    return pl.pallas_call(
        functools.partial(_proj_ffn_body, n_rows=n_rows, n_x=len(xs)),
        grid=(prep + out_rows // ROW_TILE,),
        in_specs=in_specs,
        out_specs=_after_prep(pl.BlockSpec((ROW_TILE, d), lambda i: (i, 0)), prep),
        out_shape=jax.ShapeDtypeStruct((out_rows, d), F32),
        scratch_shapes=[pltpu.VMEM((ROW_TILE, d_ff), BF16)] + _weight_scratch(wg),
        compiler_params=_params("arbitrary"),
        name="proj_ffn",
    )(*args)


def _t5_bucket_np(rel):
    n = np.maximum(rel, 0)
    max_exact = REL_BUCKETS // 2
    n_f = np.maximum(n, 1).astype(np.float32)
    large = max_exact + (np.log(n_f / np.float32(max_exact))
                         / np.float32(math.log(REL_MAX_DIST / max_exact))
                         * np.float32(REL_BUCKETS - max_exact)).astype(np.int32)
    large = np.minimum(large, REL_BUCKETS - 1)
    return np.where(n < max_exact, n, large).astype(np.int32)


def _swa_bucket_index():
    i = np.arange(SWA_BLOCK)[:, None]
    c = np.arange(SWA_BLOCK)[None, :]
    out = np.full((3, SWA_BLOCK, 2 * SWA_BLOCK), -1, np.int32)
    for variant in range(3):
        pos_q = variant * SWA_BLOCK + i - PAD
        pos_k = variant * SWA_BLOCK + c - PAD - np.where(c > i, SWA_BLOCK, 0)
        rel_b = pos_q - pos_k
        assert ((rel_b >= 0) & (rel_b < SWA_BLOCK)).all()
        out[variant, :, :SWA_BLOCK] = np.where(pos_k >= N_META, _t5_bucket_np(rel_b), -1)
        rel_m = pos_q - c
        meta = (c < N_META) & (rel_m >= 0)
        out[variant, :, SWA_BLOCK:] = np.where(meta, _t5_bucket_np(rel_m), -1)
    return out


_SWA_BUCKET_INDEX = _swa_bucket_index()


def _bias_body(tbl_ref, idx_ref, o_ref):
    idx = idx_ref[0]
    group = SWA_Q_HEADS // SWA_KV_HEADS
    for head in range(SWA_Q_HEADS):
        acc = jnp.full(idx.shape, NEG_INF, F32)
        for b in range(REL_BUCKETS):
            acc = jnp.where(idx == b, tbl_ref[b, head], acc)
        rows = slice((head % group) * SWA_BLOCK, (head % group + 1) * SWA_BLOCK)
        o_ref[0, head // group, rows, :] = acc


def _swa_bias(rel_table):
    idx = jnp.asarray(_SWA_BUCKET_INDEX)
    group = SWA_Q_HEADS // SWA_KV_HEADS
    width = idx.shape[2]
    return pl.pallas_call(
        _bias_body,
        grid=(3,),
        in_specs=[pl.BlockSpec(memory_space=pltpu.SMEM),
                  pl.BlockSpec((1, SWA_BLOCK, width), lambda v: (v, 0, 0))],
        out_specs=pl.BlockSpec((1, SWA_KV_HEADS, group * SWA_BLOCK, width),
                               lambda v: (v, 0, 0, 0)),
        out_shape=jax.ShapeDtypeStruct((3, SWA_KV_HEADS, group * SWA_BLOCK, width), F32),
        compiler_params=_params("arbitrary"),
        name="swa_bias",
    )(rel_table, idx)


def _swa_stages(sink_ref, q_ref, kc_ref, kp_ref, km_ref, vc_ref, vp_ref, vm_ref, bias_ref,
                step, blocks, store):
    dh = SWA_HEAD_DIM
    blk = SWA_BLOCK
    group = SWA_Q_HEADS // SWA_KV_HEADS
    scale = dh ** -0.5
    zpad = jnp.zeros((blk - N_META, dh), BF16)
    row = lax.broadcasted_iota(jnp.int32, (group * blk, blk), 0) % blk
    col = lax.broadcasted_iota(jnp.int32, (group * blk, blk), 1)
    from_prev = col > row

    ones_col = (lax.broadcasted_iota(jnp.int32, (3 * blk, dh), 1) == 0).astype(BF16)

    units = [(j, hk) for j in range(blocks) for hk in range(SWA_KV_HEADS)]
    scores, values = [], []
    for j, hk in units:
        r = slice(j * blk, (j + 1) * blk)
        kv = slice(hk * dh, (hk + 1) * dh)
        k_prev = kp_ref[:, kv] if j == 0 else kc_ref[(j - 1) * blk:j * blk, kv]
        v_prev = vp_ref[:, kv] if j == 0 else vc_ref[(j - 1) * blk:j * blk, kv]
        k_all = jnp.concatenate([k_prev, kc_ref[r, kv], km_ref[:, kv], zpad], axis=0)
        v_all = jnp.concatenate([v_prev, vc_ref[r, kv], vm_ref[:, kv], zpad], axis=0)
        values.append(jnp.concatenate([v_all, ones_col], axis=1))
        q = jnp.concatenate([q_ref[r, (hk * group + g) * dh:(hk * group + g + 1) * dh]
                             for g in range(group)], axis=0) * scale
        scores.append(_dot_nt(q, k_all))
        yield

    probs, sink_terms = [], []
    for (j, hk), s in zip(units, scores):
        bias = bias_ref[jnp.minimum(step * blocks + j, 2), hk]
        s_band = jnp.where(from_prev, s[:, :blk], s[:, blk:2 * blk]) + bias[:, :blk]
        s_meta = s[:, 2 * blk:] + bias[:, blk:]
        sink = jnp.concatenate([jnp.full((blk, 1), sink_ref[hk * group + g], F32)
                                for g in range(group)], axis=0)
        m = jnp.maximum(jnp.max(jnp.maximum(s_band, s_meta), axis=-1, keepdims=True), sink)
        e_band = jnp.exp(s_band - m)
        e_meta = jnp.exp(s_meta - m)
        sink_terms.append(jnp.exp(sink - m))
        probs.append(jnp.concatenate([jnp.where(from_prev, e_band, 0.0),
                                      jnp.where(from_prev, 0.0, e_band), e_meta],
                                     axis=1).astype(BF16))

    for (j, hk), p, v_ext, sink_term in zip(units, probs, values, sink_terms):
        pv = _dot(p, v_ext)
        o = pv[:, :dh] / (pv[:, dh:dh + 1] + sink_term)
        for g in range(group):
            hq = hk * group + g
            store(slice(j * blk, (j + 1) * blk), slice(hq * dh, (hq + 1) * dh),
                  o[g * blk:(g + 1) * blk])
        yield


def _swa_body(sink_ref, q_ref, kc_ref, kp_ref, km_ref, vc_ref, vp_ref, vm_ref, bias_ref, o_ref,
              *, blocks):
    def store(rows, cols, value):
        o_ref[rows, cols] = value.astype(o_ref.dtype)

    _drain(_swa_stages(sink_ref, q_ref, kc_ref, kp_ref, km_ref, vc_ref, vp_ref, vm_ref,
                       bias_ref, pl.program_id(1), blocks, store))


def _swa(q, k, v, bias, sinks, batch, lp, blocks):
    rows = q.shape[0]
    nb = lp // SWA_BLOCK
    steps = nb // blocks
    meta_blocks = lp // N_META
    cur = lambda b, s: (b * steps + s, 0)
    prev = lambda b, s: (b * nb + jnp.maximum(s * blocks - 1, 0), 0)
    meta = lambda b, s: (b * meta_blocks + PAD // N_META, 0)
    kv_w = k.shape[1]
    rs = blocks * SWA_BLOCK
    return pl.pallas_call(
        functools.partial(_swa_body, blocks=blocks),
        grid=(batch, steps),
        in_specs=[pl.BlockSpec(memory_space=pltpu.SMEM),
                  pl.BlockSpec((rs, q.shape[1]), cur),
                  pl.BlockSpec((rs, kv_w), cur), pl.BlockSpec((SWA_BLOCK, kv_w), prev),
                  pl.BlockSpec((N_META, kv_w), meta),
                  pl.BlockSpec((rs, kv_w), cur), pl.BlockSpec((SWA_BLOCK, kv_w), prev),
                  pl.BlockSpec((N_META, kv_w), meta),
                  _resident(bias.shape)],
        out_specs=pl.BlockSpec((rs, q.shape[1]), cur),
        out_shape=jax.ShapeDtypeStruct((rows, q.shape[1]), BF16),
        compiler_params=_params("parallel", "arbitrary"),
        name="swa",
    )(sinks, q, k, k, k, v, v, v, bias)


def _tri_mask():
    row = lax.broadcasted_iota(jnp.int32, (CHUNK, CHUNK), 0)
    col = lax.broadcasted_iota(jnp.int32, (CHUNK, CHUNK), 1)
    return row >= col


def _deltanet_stages(act_ref, z_ref, gates_ref, alog_ref, dtb_ref, nw_ref, s_ref, chunks,
                     store):
    nh, dh = DN_HEADS, DN_HEAD_DIM
    gt = gates_ref[...]
    beta_full = jax.nn.sigmoid(gt)
    g_full = -jnp.exp(alog_ref[...]) * _softplus(gt + dtb_ref[...])
    tri = _tri_mask().astype(F32)
    gc_chunks = [_dot(tri, g_full[c * CHUNK:(c + 1) * CHUNK, :], HIGHEST)
                 for c in range(chunks)]
    yield

    lane = lax.broadcasted_iota(jnp.int32, (CHUNK, 2 * CHUNK), 1)
    row = lax.broadcasted_iota(jnp.int32, (CHUNK, 2 * CHUNK), 0)
    first = lane < CHUNK
    col = jnp.where(first, lane, lane - CHUNK)
    incl2, strict2 = row >= col, row > col
    eye2 = (row == col).astype(F32)
    first_rows2 = lax.broadcasted_iota(jnp.int32, (2 * CHUNK, 2 * CHUNK), 1) < CHUNK

    def block_diag(a, b):
        zero = jnp.zeros_like(a)
        return jnp.concatenate([jnp.concatenate([a, zero], axis=1),
                                jnp.concatenate([zero, b], axis=1)], axis=0)

    def block_diag_halves(x):
        zero = jnp.zeros_like(x)
        return jnp.concatenate([jnp.where(first, x, zero), jnp.where(first, zero, x)], axis=0)

    pairs = [(c, p) for c in range(chunks) for p in range(nh // 2)]
    neg_a, akt16, rhs16, qd16, decay = [], [], [], {}, {}
    for c, p in pairs:
        if p == 0 and c % 2 == 0 and c > 0:
            yield
        r = slice(c * CHUNK, (c + 1) * CHUNK)
        ks, kqs, kdecs, gcs, rhs = [], [], [], [], []
        for h in (2 * p, 2 * p + 1):
            q = act_ref[r, h * dh:(h + 1) * dh].astype(F32)
            k = act_ref[r, DN_DIM + h * dh:DN_DIM + (h + 1) * dh].astype(F32)
            v = act_ref[r, 2 * DN_DIM + h * dh:2 * DN_DIM + (h + 1) * dh].astype(F32)
            q = q * lax.rsqrt(jnp.sum(q * q, axis=-1, keepdims=True) + 1e-6) * dh ** -0.5
            k = k * lax.rsqrt(jnp.sum(k * k, axis=-1, keepdims=True) + 1e-6)
            beta = jnp.broadcast_to(beta_full[r, h:h + 1], (CHUNK, dh))
            gc = jnp.broadcast_to(gc_chunks[c][:, nh + h:nh + h + 1], (CHUNK, dh))
            gc_last = gc[CHUNK - 1:CHUNK, :]
            k_beta = k * beta
            ks.append(k)
            kqs.append(jnp.concatenate([k_beta, q], axis=0).astype(BF16))
            kdecs.append(k * jnp.exp(gc_last - gc))
            gcs.append(gc)
            rhs.append(jnp.concatenate([v * beta, k_beta * jnp.exp(gc)], axis=1).astype(BF16))
            qd16[c, h] = (q * jnp.exp(gc)).astype(BF16)
            decay[c, h] = jnp.exp(gc_last)
        gc2 = jnp.where(first, gcs[0], gcs[1])
        gc_row2 = jnp.transpose(jnp.concatenate(gcs, axis=0))[:CHUNK, :]
        gamma2 = jnp.where(incl2, jnp.exp(jnp.where(incl2, gc2 - gc_row2, 0.0)), 0.0)
        k2 = jnp.concatenate(ks, axis=0).astype(BF16)
        kq2 = jnp.where(first_rows2, _dot_nt(kqs[0], k2), _dot_nt(kqs[1], k2))
        neg_a.append(jnp.where(strict2, -(kq2[:CHUNK] * gamma2), 0.0))
        attn16 = jnp.where(incl2, kq2[CHUNK:] * gamma2, 0.0).astype(BF16)
        kdt16 = jnp.transpose(jnp.concatenate(kdecs, axis=0)).astype(BF16)
        akt16.append(jnp.concatenate([attn16, kdt16], axis=0))
        rhs16.append(block_diag(rhs[0], rhs[1]))

    yield
    levels = int(math.log2(CHUNK)) - 1
    t = [eye2 + x for x in neg_a]
    p16 = [x.astype(BF16) for x in neg_a]
    p16 = [_dot(x, block_diag_halves(x)).astype(BF16) for x in p16]
    yield
    for level in range(levels):
        if level + 1 < levels:
            tp = [_dot(jnp.concatenate([y.astype(BF16), x], axis=0), block_diag_halves(x))
                  for y, x in zip(t, p16)]
            t = [y + r[:CHUNK] for y, r in zip(t, tp)]
            p16 = [r[CHUNK:].astype(BF16) for r in tp]
        else:
            t = [y + _dot(y.astype(BF16), block_diag_halves(x)) for y, x in zip(t, p16)]
        yield
    uw2 = [_dot(y.astype(BF16), b) for y, b in zip(t, rhs16)]
    u, wq16 = {}, {}
    for (c, p), x in zip(pairs, uw2):
        for half, h in enumerate((2 * p, 2 * p + 1)):
            u[c, h] = x[:, half * 2 * dh:half * 2 * dh + dh]
            w16 = x[:, half * 2 * dh + dh:(half + 1) * 2 * dh].astype(BF16)
            wq16[c, h] = jnp.concatenate([w16, qd16[c, h]], axis=0)
    yield

    nw = nw_ref[...]
    state = [s_ref[h] for h in range(nh)]
    for c in range(chunks):
        r = slice(c * CHUNK, (c + 1) * CHUNK)
        s16 = [s.astype(BF16) for s in state]
        wqs = [_dot(wq16[c, h], s16[h]) for h in range(nh)]
        yield
        for p in range(nh // 2):
            ha, hb = 2 * p, 2 * p + 1
            vn_a = (u[c, ha] - wqs[ha][:CHUNK]).astype(BF16)
            vn_b = (u[c, hb] - wqs[hb][:CHUNK]).astype(BF16)
            ak2 = _dot(akt16[c * (nh // 2) + p], block_diag(vn_a, vn_b))
            for half, h in enumerate((ha, hb)):
                hc = slice(h * dh, (h + 1) * dh)
                ak = ak2[:, half * dh:(half + 1) * dh]
                state[h] = state[h] * decay[c, h] + ak[CHUNK:]
                out = wqs[h][CHUNK:] + ak[:CHUNK]
                store(r, hc, _rms(out, nw) * _silu(z_ref[r, hc].astype(F32)))
        yield
    for h in range(nh):
        s_ref[h] = state[h]


def _deltanet_body(act_ref, z_ref, gates_ref, alog_ref, dtb_ref, nw_ref, o_ref, s_ref, *,
                   chunks):
    @pl.when(pl.program_id(1) == 0)
    def _():
        s_ref[...] = jnp.zeros_like(s_ref)

    def store(rows, cols, value):
        o_ref[rows, cols] = value.astype(o_ref.dtype)

    _drain(_deltanet_stages(act_ref, z_ref, gates_ref, alog_ref, dtb_ref, nw_ref, s_ref,
                            chunks, store))


def _deltanet(act, z, gates, alog_vec, dtb_vec, nw, batch, lp, chunks):
    rows = act.shape[0]
    rs = chunks * CHUNK
    steps = lp // rs
    idx = lambda b, s: (b * steps + s, 0)
    return pl.pallas_call(
        functools.partial(_deltanet_body, chunks=chunks),
        grid=(batch, steps),
        in_specs=[pl.BlockSpec((rs, act.shape[1]), idx), pl.BlockSpec((rs, z.shape[1]), idx),
                  pl.BlockSpec((rs, gates.shape[1]), idx),
                  _resident(alog_vec.shape), _resident(dtb_vec.shape), _resident(nw.shape)],
        out_specs=pl.BlockSpec((rs, DN_DIM), idx),
        out_shape=jax.ShapeDtypeStruct((rows, DN_DIM), BF16),
        scratch_shapes=[pltpu.VMEM((DN_HEADS, DN_HEAD_DIM, DN_HEAD_DIM), F32)],
        compiler_params=_params("parallel", "arbitrary"),
        name="deltanet",
    )(act, z, gates, alog_vec, dtb_vec, nw)


def _mix_ffn_body(*refs, blocks, chunks, steps, last):
    *refs, wg_s, wu_s, wd_s = refs
    prep = _prepare_weights(refs[20:23], (wg_s, wu_s, wd_s))
    t = pl.program_id(0) - prep
    refs = refs[:20] + [wg_s, wu_s, wd_s] + refs[23:]

    @pl.when(t >= 0)
    def _():
        _mix_ffn_step(t, *refs, blocks=blocks, chunks=chunks, steps=steps, last=last)


def _mix_ffn_step(t, sink_ref, q_ref, kc_ref, kp_ref, km_ref, vc_ref, vp_ref, vm_ref, bias_ref,
                  act_ref, z_ref, gates_ref, alog_ref, dtb_ref, dnw_ref,
                  h_ref, wo_ref, nwo_ref, nwa_ref, nwb_ref, wg_ref, wu_ref, wd_ref,
                  o_ref, s_ref, mix_ref, a_ref, *, blocks, chunks, steps, last):
    seq_step = jnp.minimum(t, last) % steps
    write_slot = t % 2
    read_slot = 1 - write_slot
    swa_width = q_ref.shape[1]
    groups = q_ref.shape[0] // GROUP_ROWS

    @pl.when(t == 0)
    def _():
        mix_ref[...] = jnp.zeros_like(mix_ref)

    @pl.when(seq_step == 0)
    def _():
        s_ref[...] = jnp.zeros_like(s_ref)

    def store_swa(rows, cols, value):
        mix_ref[write_slot, rows, cols] = value.astype(mix_ref.dtype)

    def store_dn(rows, cols, value):
        cols = slice(swa_width + cols.start, swa_width + cols.stop)
        mix_ref[write_slot, rows, cols] = value.astype(mix_ref.dtype)

    def mixer_stages():
        dn = _deltanet_stages(act_ref, z_ref, gates_ref, alog_ref, dtb_ref, dnw_ref,
                              s_ref, chunks, store_dn)
        swa = _swa_stages(sink_ref, q_ref, kc_ref, kp_ref, km_ref, vc_ref, vp_ref,
                          vm_ref, bias_ref, seq_step, blocks, store_swa)
        for _ in dn:
            yield
            if next(swa, StopIteration) is not StopIteration:
                yield
        yield from swa

    def ffn_stages():
        accs = []
        for part in range(groups):
            accs.append(_dot(mix_ref[read_slot, _group(part), :], wo_ref[...]))
            yield
        hs = [h_ref[_group(part), :] + _rms(acc, nwo_ref[...]) for part, acc in enumerate(accs)]
        outs = []
        yield from _ffn_stages(hs, nwa_ref[...], nwb_ref[...], wg_ref, wu_ref, wd_ref, a_ref,
                               outs)
        for part, out in enumerate(outs):
            o_ref[_group(part), :] = out

    _interleave(ffn_stages(), mixer_stages())


def _mix_ffn(qa, ka, va, bias, sinks, act, z, gates, alog_vec, dtb_vec, dn_nw, h, w_out, norm4,
             layer, wg, wu, wd, batch, lp, blocks, chunks):
    d = norm4.shape[-1]
    d_ff = wg.shape[-1]
    rs = blocks * SWA_BLOCK
    assert rs == chunks * CHUNK and rs % GROUP_ROWS == 0
    nb = lp // SWA_BLOCK
    steps = lp // rs
    last = batch * steps - 1
    meta_blocks = lp // N_META

    def tile(t):
        return jnp.minimum(t, last)

    cur = lambda t: (tile(t), 0)
    prev = lambda t: ((tile(t) // steps) * nb + jnp.maximum((tile(t) % steps) * blocks - 1, 0), 0)
    meta = lambda t: ((tile(t) // steps) * meta_blocks + PAD // N_META, 0)
    behind = lambda t: (jnp.maximum(t - 1, 0), 0)
    kv_w = ka.shape[1]
    prep = d_ff // FF_TILE
    in_specs = [pl.BlockSpec(memory_space=pltpu.SMEM),
                pl.BlockSpec((rs, qa.shape[1]), cur),
                pl.BlockSpec((rs, kv_w), cur), pl.BlockSpec((SWA_BLOCK, kv_w), prev),
                pl.BlockSpec((N_META, kv_w), meta),
                pl.BlockSpec((rs, kv_w), cur), pl.BlockSpec((SWA_BLOCK, kv_w), prev),
                pl.BlockSpec((N_META, kv_w), meta),
                _resident(bias.shape),
                pl.BlockSpec((rs, act.shape[1]), cur), pl.BlockSpec((rs, z.shape[1]), cur),
                pl.BlockSpec((rs, gates.shape[1]), cur),
                _resident(alog_vec.shape), _resident(dtb_vec.shape), _resident(dn_nw.shape),
                pl.BlockSpec((rs, d), behind), _resident(w_out.shape),
                _pick(norm4, layer, 3), _pick(norm4, layer, 4), _pick(norm4, layer, 5)]
    return pl.pallas_call(
        functools.partial(_mix_ffn_body, blocks=blocks, chunks=chunks, steps=steps, last=last),
        grid=(prep + last + 2,),
        in_specs=[_after_prep(s, prep) for s in in_specs]
        + _weight_chunk_specs(wg, wd, layer, 1),
        out_specs=_after_prep(pl.BlockSpec((rs, d), behind), prep),
        out_shape=jax.ShapeDtypeStruct(h.shape, F32),
        scratch_shapes=[pltpu.VMEM((DN_HEADS, DN_HEAD_DIM, DN_HEAD_DIM), F32),
                        pltpu.VMEM((2, rs, qa.shape[1] + DN_DIM), BF16),
                        pltpu.VMEM((rs, d_ff), BF16)] + _weight_scratch(wg),
        compiler_params=_params("arbitrary"),
        name="mix_ffn",
    )(sinks, qa, ka, ka, ka, va, va, va, bias, act, z, gates, alog_vec, dtb_vec, dn_nw,
      h, w_out, norm4, norm4, norm4, wg, wu, wd)


def _gla_body(q_ref, k_ref, v_ref, g_ref, gk_ref, wgu_ref, bg_ref, nw_ref, o_ref, s_ref, *,
              chunks):
    nh = GLA_HEADS
    dk = q_ref.shape[1] // nh
    dv = v_ref.shape[1] // nh

    @pl.when(pl.program_id(1) == 0)
    def _():
        s_ref[...] = jnp.zeros_like(s_ref)

    logits = _dot(gk_ref[...].astype(BF16), wgu_ref[...]) + bg_ref[...]
    glog = (jnp.minimum(logits, 0.0) - jnp.log(1.0 + jnp.exp(-jnp.abs(logits)))) / GLA_GATE_NORM
    incl = _tri_mask()
    tri = incl.astype(F32)
    bcum = [_dot(tri, glog[c * CHUNK:(c + 1) * CHUNK, :], HIGHEST) for c in range(chunks)]

    items = [(c, h) for c in range(chunks) for h in range(nh)]
    qd16, kn16, kdt16, decay = [], [], [], []
    for c, h in items:
        r = slice(c * CHUNK, (c + 1) * CHUNK)
        kc = slice(h * dk, (h + 1) * dk)
        bc = bcum[c][:, kc]
        q = q_ref[r, kc].astype(F32) * dk ** -0.5
        k = k_ref[r, kc].astype(F32)
        b_last = bc[CHUNK - 1:CHUNK, :]
        qd16.append((q * jnp.exp(bc)).astype(BF16))
        kn16.append((k * jnp.exp(-bc)).astype(BF16))
        kdt16.append(jnp.transpose(k * jnp.exp(b_last - bc)).astype(BF16))
        decay_rows = jnp.broadcast_to(jnp.exp(b_last), (8, dk))
        decay.append(jnp.transpose(decay_rows)[:, :1])
    values = [v_ref[c * CHUNK:(c + 1) * CHUNK, h * dv:(h + 1) * dv] for c, h in items]
    attn16 = [jnp.where(incl, _dot_nt(a, b), 0.0).astype(BF16) for a, b in zip(qd16, kn16)]
    intra = [_dot(a, v) for a, v in zip(attn16, values)]
    update = [_dot(kt, v) for kt, v in zip(kdt16, values)]

    nw = nw_ref[...]
    state = [s_ref[h] for h in range(nh)]
    for c in range(chunks):
        r = slice(c * CHUNK, (c + 1) * CHUNK)
        for h in range(nh):
            i = c * nh + h
            vc = slice(h * dv, (h + 1) * dv)
            o = intra[i] + _dot(qd16[i], state[h].astype(BF16))
            state[h] = state[h] * decay[i] + update[i]
            o = _rms(o, nw) * _silu(g_ref[r, vc].astype(F32))
            o_ref[r, vc] = o.astype(o_ref.dtype)
    for h in range(nh):
        s_ref[h] = state[h]


def _gla(q, k, v, g, gk, wgu, bg, nw, batch, lp, chunks):
    rows = q.shape[0]
    rs = chunks * CHUNK
    steps = lp // rs
    idx = lambda b, s: (b * steps + s, 0)
    dk = q.shape[1] // GLA_HEADS
    dv = v.shape[1] // GLA_HEADS
    return pl.pallas_call(
        functools.partial(_gla_body, chunks=chunks),
        grid=(batch, steps),
        in_specs=[pl.BlockSpec((rs, q.shape[1]), idx), pl.BlockSpec((rs, k.shape[1]), idx),
                  pl.BlockSpec((rs, v.shape[1]), idx), pl.BlockSpec((rs, g.shape[1]), idx),
                  pl.BlockSpec((rs, gk.shape[1]), idx), _resident(wgu.shape),
                  _resident(bg.shape), _resident(nw.shape)],
        out_specs=pl.BlockSpec((rs, v.shape[1]), idx),
        out_shape=jax.ShapeDtypeStruct((rows, v.shape[1]), BF16),
        scratch_shapes=[pltpu.VMEM((GLA_HEADS, dk, dv), F32)],
        compiler_params=_params("parallel", "arbitrary"),
        name="gla",
    )(q, k, v, g, gk, wgu, bg, nw)


def _per_step(n, preferred):
    for c in preferred:
        if n % c == 0:
            return c
    return 1


def kernel(x, meta_tokens, norm_w, ffn_w_gate, ffn_w_up, ffn_w_down, rel_bias_table,
           even_w_in, even_conv_w, swa_sinks, dn_a_log, dn_dt_bias, dn_norm_w, even_w_out,
           odd_w_in, gla_w_gate_up, gla_b_gate, gla_norm_w, odd_w_out):
    batch, seq, d = x.shape
    depth = norm_w.shape[0]
    lp = SWA_BLOCK + seq
    assert seq % ROW_TILE == 0 and (batch * lp) % ROW_TILE == 0
    rows = batch * lp
    chunks = _per_step(lp // CHUNK, (6, 3, 2))
    swa_blocks = _per_step(lp // SWA_BLOCK, (3, 2))

    meta_block = jnp.pad(meta_tokens.astype(x.dtype), ((PAD, 0), (0, 0)))
    norm4 = norm_w[:, :, None, :]
    wg, wu, wd = ffn_w_gate, ffn_w_up, ffn_w_down

    swa_q = SWA_Q_HEADS * SWA_HEAD_DIM
    swa_kv = SWA_KV_HEADS * SWA_HEAD_DIM
    bias = None
    h = x.reshape(batch * seq, d)
    for layer in range(depth):
        i = layer // 2
        embed = meta_block if layer == 0 else None
        final = layer == depth - 1
        if layer % 2 == 0:
            assert even_w_in.shape[2] == swa_q + 2 * swa_kv + 4 * DN_DIM + 2 * DN_HEADS
            widths = (swa_q, swa_kv, swa_kv, 3 * DN_DIM, DN_DIM, LANE)
            h, qa, ka, va, act_b, z_b, gates = _ffn_proj(
                h, embed, norm4, layer, wg, wu, wd, even_w_in, i, widths,
                (BF16, BF16, BF16, BF16, BF16, F32), rows, lp, seq,
                conv_out=3, conv_w=even_conv_w[i])
            if bias is None:
                bias = _swa_bias(rel_bias_table)
            lane_pad = (DN_HEADS, LANE - 2 * DN_HEADS)
            alog_vec = jnp.pad(dn_a_log[i], lane_pad)[None, :]
            dtb_vec = jnp.pad(dn_dt_bias[i], lane_pad)[None, :]
            dn_nw = dn_norm_w[i][None, :]
            w_out = even_w_out[i]
            if not final and swa_blocks * SWA_BLOCK == chunks * CHUNK:
                h = _mix_ffn(qa, ka, va, bias, swa_sinks[i], act_b, z_b, gates, alog_vec,
                             dtb_vec, dn_nw, h, w_out.astype(BF16), norm4, layer, wg, wu, wd,
                             batch, lp, swa_blocks, chunks)
                continue
            o_a = _swa(qa, ka, va, bias, swa_sinks[i], batch, lp, swa_blocks)
            o_b = _deltanet(act_b, z_b, gates, alog_vec, dtb_vec, dn_nw, batch, lp, chunks)
            mixed = [o_a, o_b]
        else:
            key_dim = gla_w_gate_up.shape[2]
            val_dim = odd_w_out.shape[1]
            widths = (key_dim, key_dim, val_dim, val_dim, LANE)
            h, q, k, v, g, gk = _ffn_proj(
                h, embed, norm4, layer, wg, wu, wd, odd_w_in, i, widths,
                (BF16, BF16, BF16, BF16, F32), rows, lp, seq)
            wgu = jnp.pad(gla_w_gate_up[i], ((0, LANE - GLA_GATE_RANK), (0, 0))).astype(BF16)
            o = _gla(q, k, v, g, gk, wgu, gla_b_gate[i][None, :], gla_norm_w[i][None, :],
                     batch, lp, _per_step(lp // CHUNK, (11, 6, 3, 2)))
            mixed, w_out = [o], odd_w_out[i]
        h = _proj_ffn(mixed, h, w_out.astype(BF16), norm4, layer, wg, wu, wd, final,
                      batch, lp, seq)
    return h.reshape(batch, seq, d)
```

```python
import functools
import math

import numpy as np
import jax
import jax.numpy as jnp
from jax import lax
from jax.experimental import pallas as pl
from jax.experimental.pallas import tpu as pltpu

F32 = jnp.float32
BF16 = jnp.bfloat16
HIGHEST = lax.Precision.HIGHEST

N_META = 16
NORM_EPS = 1e-6
NEG_INF = -1e30
SWA_Q_HEADS = 8
SWA_KV_HEADS = 2
SWA_HEAD_DIM = 64
SWA_BLOCK = 128
REL_BUCKETS = 32
REL_MAX_DIST = 128
DN_HEADS = 4
DN_HEAD_DIM = 128
DN_DIM = DN_HEADS * DN_HEAD_DIM
GLA_HEADS = 4
GLA_GATE_RANK = 16
GLA_GATE_NORM = 16.0
CHUNK = 64

PAD = SWA_BLOCK - N_META
LANE = 128
ROW_TILE = 512
FF_TILE = 256
VMEM_LIMIT = 60 * 1024 * 1024


def _rms(x, w):
    return x * lax.rsqrt(jnp.mean(x * x, axis=-1, keepdims=True) + NORM_EPS) * w


def _silu(x):
    return x * jax.nn.sigmoid(x)


def _softplus(x):
    return jnp.maximum(x, 0.0) + jnp.log(1.0 + jnp.exp(-jnp.abs(x)))


def _dot(a, b, precision=None):
    return jnp.dot(a, b, preferred_element_type=F32, precision=precision)


def _dot_nt(a, b):
    return lax.dot_general(a, b, (((1,), (1,)), ((), ())), preferred_element_type=F32)


def _resident(shape):
    nd = len(shape)
    return pl.BlockSpec(shape, lambda *_: (0,) * nd, pipeline_mode=pl.Buffered(1))


def _params(*semantics):
    return pltpu.CompilerParams(dimension_semantics=semantics, vmem_limit_bytes=VMEM_LIMIT)


SUB = ROW_TILE // SWA_BLOCK


def _pick(arr, *lead):
    rest = arr.shape[len(lead):]
    index = tuple(lead) + (0,) * len(rest)
    return pl.BlockSpec((None,) * len(lead) + rest, lambda i: index,
                        pipeline_mode=pl.Buffered(1))


def _after_prep(spec, prep):
    if spec.index_map is None:
        return spec
    index = spec.index_map
    return pl.BlockSpec(spec.block_shape, lambda i: index(jnp.maximum(i - prep, 0)),
                        pipeline_mode=spec.pipeline_mode)


def _weight_chunk_specs(wg, wd, layer, which):
    d, d_ff = wg.shape[-2:]
    last = d_ff // FF_TILE - 1
    cols = pl.BlockSpec((None, None, d, FF_TILE), lambda i: (layer, which, 0, jnp.minimum(i, last)))
    rows = pl.BlockSpec((None, None, FF_TILE, d), lambda i: (layer, which, jnp.minimum(i, last), 0))
    return [cols, cols, rows]


def _weight_scratch(wg):
    d, d_ff = wg.shape[-2:]
    chunks = d_ff // FF_TILE
    return [pltpu.VMEM((chunks, d, FF_TILE), BF16), pltpu.VMEM((chunks, d, FF_TILE), BF16),
            pltpu.VMEM((d_ff, d), BF16)]


def _prepare_weights(chunk_refs, scratch_refs):
    wg_blk, wu_blk, wd_blk = chunk_refs
    wg_s, wu_s, wd_s = scratch_refs
    prep = wg_s.shape[0]
    i = pl.program_id(0)

    @pl.when(i < prep)
    def _():
        wg_s[i] = wg_blk[...].astype(BF16)
        wu_s[i] = wu_blk[...].astype(BF16)
        wd_s[pl.ds(pl.multiple_of(i * FF_TILE, FF_TILE), FF_TILE), :] = wd_blk[...].astype(BF16)

    return prep


def _tile_specs(width):
    return [pl.BlockSpec((ROW_TILE, width), lambda i: (i, 0))]


def _real_specs(width, lp, seq):
    tiles = seq // ROW_TILE
    blocks = lp // SWA_BLOCK

    def spec(j):
        return pl.BlockSpec((SWA_BLOCK, width),
                            lambda i: ((i // tiles) * blocks + 1 + (i % tiles) * SUB + j, 0))
    return [spec(j) for j in range(SUB)]


def _embed_specs(width, lp, seq):
    blocks = lp // SWA_BLOCK
    x_blocks = seq // SWA_BLOCK

    def spec(j):
        def index(i):
            blk = i * SUB + j
            return ((blk // blocks) * x_blocks + jnp.maximum(blk % blocks - 1, 0), 0)
        return pl.BlockSpec((SWA_BLOCK, width), index)
    return [spec(j) for j in range(SUB)]


ROW_SPLIT = 4
GROUP_ROWS = ROW_TILE // ROW_SPLIT
MIX_SPLIT = 2


def _group(part, rows=GROUP_ROWS):
    return slice(part * rows, (part + 1) * rows)


def _load_rows(refs, part):
    if len(refs) == 1:
        return refs[0][_group(part), :]
    per = len(refs) // ROW_SPLIT
    return jnp.concatenate([r[...] for r in refs[part * per:(part + 1) * per]], axis=0)


def _drain(stages):
    for _ in stages:
        pass


def _interleave(main, side):
    for _ in main:
        next(side, None)
    _drain(side)


def _ffn_stages(xs, nwa, nwb, wg_ref, wu_ref, wd_ref, a_ref, outs):
    hns = [_rms(x, nwa).astype(BF16) for x in xs]
    rows = xs[0].shape[0]
    for part, hn in enumerate(hns):
        for c in range(wg_ref.shape[0]):
            cols = slice(c * FF_TILE, (c + 1) * FF_TILE)
            g = _dot(hn, wg_ref[c])
            yield
            u = _dot(hn, wu_ref[c])
            a_ref[_group(part, rows), cols] = (_silu(g) * u).astype(BF16)
            yield
    fs = []
    for part in range(len(xs)):
        fs.append(_dot(a_ref[_group(part, rows), :], wd_ref[...]))
        yield
    outs.extend(x + 0.5 * _rms(f, nwb) for x, f in zip(xs, fs))


def _ffn_core(xs, nwa, nwb, wg_ref, wu_ref, wd_ref, a_ref):
    outs = []
    _drain(_ffn_stages(xs, nwa, nwb, wg_ref, wu_ref, wd_ref, a_ref, outs))
    return outs


CONV_HALO = 8


def _ffn_proj_body(*refs, n_rows, embed_blocks, n_out, conv_out, lp, tail_valid, prologue):
    lead = n_rows + bool(embed_blocks) + (conv_out is not None) + 2
    wg_s, wu_s, wd_s, win_s, wtail_s = refs[-5:]
    nwm_ref, win_blk = refs[lead + 3:lead + 5]
    _prepare_weights(refs[lead:lead + 3], (wg_s, wu_s, wd_s))
    step = pl.program_id(0) - prologue
    win_chunks = win_s.shape[0]

    @pl.when(pl.program_id(0) < win_chunks)
    def _():
        win_s[pl.program_id(0)] = win_blk[...].astype(BF16)

    @pl.when(pl.program_id(0) == win_chunks)
    def _():
        tail = win_blk[:, :LANE]
        lane = lax.broadcasted_iota(jnp.int32, tail.shape, 1)
        wtail_s[...] = jnp.where(lane < tail_valid, tail, 0.0).astype(BF16)

    refs = refs[:lead] + (wg_s, wu_s, wd_s, nwm_ref, win_s, wtail_s) + refs[lead + 5:-5]

    @pl.when(step >= 0)
    def _():
        _ffn_proj_step(step, *refs, n_rows=n_rows, embed_blocks=embed_blocks, n_out=n_out,
                       conv_out=conv_out, lp=lp)


def _ffn_proj_step(step, *refs, n_rows, embed_blocks, n_out, conv_out, lp):
    row_refs = refs[:n_rows]
    refs = refs[n_rows:]
    if embed_blocks:
        meta_ref, refs = refs[0], refs[1:]
    if conv_out is not None:
        convw_ref, refs = refs[0], refs[1:]
        xpad_ref = refs[10 + n_out]

        @pl.when(step == 0)
        def _():
            xpad_ref[0:CONV_HALO, :] = jnp.zeros((CONV_HALO, xpad_ref.shape[1]), F32)

    nwa_ref, nwb_ref, wg_ref, wu_ref, wd_ref, nwm_ref, win_ref, wtail_ref = refs[:8]
    h_ref = refs[8]
    out_refs = refs[9:9 + n_out]
    a_ref = refs[9 + n_out]
    offsets = np.cumsum([0] + [o.shape[1] for o in out_refs])
    xs = []
    for part in range(ROW_SPLIT):
        if embed_blocks:
            per = n_rows // ROW_SPLIT
            first = step * SUB + part * per
            xs.append(jnp.concatenate(
                [jnp.where((first + j) % embed_blocks == 0, meta_ref[...], r[...])
                 for j, r in enumerate(row_refs[part * per:(part + 1) * per])], axis=0))
        else:
            xs.append(_load_rows(row_refs, part))
    hs = _ffn_core(xs, nwa_ref[...], nwb_ref[...], wg_ref, wu_ref, wd_ref, a_ref)
    hns = [_rms(h, nwm_ref[...]).astype(BF16) for h in hs]
    width = win_ref.shape[2]
    for part, (h, hn) in enumerate(zip(hs, hns)):
        rows = _group(part)
        h_ref[rows, :] = h
        for c in range(win_ref.shape[0]):
            p = _dot(hn, win_ref[c])
            for i in range(n_out - 1):
                lo = max(c * width, offsets[i])
                hi = min((c + 1) * width, offsets[i + 1])
                if lo >= hi:
                    continue
                piece = p[:, lo - c * width:hi - c * width]
                cols = slice(lo - offsets[i], hi - offsets[i])
                if i == conv_out:
                    xpad_ref[CONV_HALO + part * GROUP_ROWS:CONV_HALO + (part + 1) * GROUP_ROWS,
                             cols] = piece
                else:
                    out_refs[i][rows, cols] = piece.astype(out_refs[i].dtype)
        out_refs[-1][rows, :] = _dot(hn, wtail_ref[...]).astype(out_refs[-1].dtype)

    if conv_out is not None:
        taps = convw_ref.shape[0]
        cw = convw_ref[...]
        y = cw[taps - 1:taps, :] * xpad_ref[CONV_HALO:, :]
        for j in range(taps - 1):
            lo = CONV_HALO - (taps - 1 - j)
            y = y + cw[j:j + 1, :] * xpad_ref[lo:lo + ROW_TILE, :]
        tail = xpad_ref[ROW_TILE:, :]
        row = (step * ROW_TILE) % lp + lax.broadcasted_iota(jnp.int32, (ROW_TILE, 1), 0)
        row = jnp.where(row >= lp, row - lp, row)
        out_refs[conv_out][...] = jnp.where(row >= PAD, _silu(y), 0.0).astype(
            out_refs[conv_out].dtype)
        xpad_ref[0:CONV_HALO, :] = tail


def _ffn_proj(src, meta_block, norm4, layer, wg, wu, wd, w_in, index, widths, dtypes, rows, lp,
              seq, conv_out=None, conv_w=None):
    d = norm4.shape[-1]
    d_ff = wg.shape[-1]
    prep = d_ff // FF_TILE
    main = sum(widths[:-1])
    tail_valid = w_in.shape[2] - main
    assert 0 < tail_valid <= widths[-1] == LANE and main % LANE == 0 and lp > ROW_TILE
    width = FF_TILE
    assert main % width == 0
    win_chunks = main // width
    prep = max(prep, win_chunks + 1)
    w_in = w_in[index]
    win_spec = pl.BlockSpec((d, width), lambda i: (0, jnp.minimum(i, win_chunks)))
    if meta_block is None:
        row_specs, extra, extra_specs, embed_blocks = _tile_specs(d), [], [], 0
    else:
        row_specs = _embed_specs(d, lp, seq)
        extra, extra_specs, embed_blocks = [meta_block], [_resident(meta_block.shape)], (
            lp // SWA_BLOCK)
    scratch = [pltpu.VMEM((ROW_TILE, d_ff), BF16)]
    if conv_out is not None:
        extra, extra_specs = extra + [conv_w], extra_specs + [_resident(conv_w.shape)]
        scratch.append(pltpu.VMEM((CONV_HALO + ROW_TILE, widths[conv_out]), F32))
    tile = lambda n: pl.BlockSpec((ROW_TILE, n), lambda i: (i, 0))
    shift = lambda specs: [_after_prep(s, prep) for s in specs]
    return pl.pallas_call(
        functools.partial(_ffn_proj_body, n_rows=len(row_specs), embed_blocks=embed_blocks,
                          n_out=len(widths), conv_out=conv_out, lp=lp, tail_valid=tail_valid,
                          prologue=prep),
        grid=(prep + rows // ROW_TILE,),
        in_specs=shift(row_specs + extra_specs
                       + [_pick(norm4, layer, 0), _pick(norm4, layer, 1)])
        + _weight_chunk_specs(wg, wd, layer, 0)
        + shift([_pick(norm4, layer, 2)]) + [win_spec],
        out_specs=shift([tile(d)] + [tile(n) for n in widths]),
        out_shape=[jax.ShapeDtypeStruct((rows, d), F32)]
        + [jax.ShapeDtypeStruct((rows, n), dt) for n, dt in zip(widths, dtypes)],
        scratch_shapes=scratch + _weight_scratch(wg)
        + [pltpu.VMEM((win_chunks, d, width), BF16), pltpu.VMEM((d, LANE), BF16)],
        compiler_params=_params("arbitrary"),
        name="ffn_proj",
    )(*([src] * len(row_specs)), *extra, norm4, norm4, wg, wu, wd, norm4, w_in)


def _proj_ffn_body(*refs, n_rows, n_x):
    x_groups = [refs[g * n_rows:(g + 1) * n_rows] for g in range(n_x)]
    refs = refs[n_x * n_rows:]
    h_refs = refs[:n_rows]
    (wo_ref, nwo_ref, nwa_ref, nwb_ref, wg_blk, wu_blk, wd_blk, o_ref, a_ref,
     wg_ref, wu_ref, wd_ref) = refs[n_rows:]
    prep = _prepare_weights((wg_blk, wu_blk, wd_blk), (wg_ref, wu_ref, wd_ref))

    @pl.when(pl.program_id(0) >= prep)
    def _():
        accs = []
        for part in range(ROW_SPLIT):
            acc = None
            off = 0
            for group in x_groups:
                xs = _load_rows(group, part)
                k = xs.shape[1]
                p = _dot(xs, wo_ref[off:off + k, :])
                acc = p if acc is None else acc + p
                off += k
            accs.append(acc)
        hs = [_load_rows(h_refs, part) + _rms(acc, nwo_ref[...])
              for part, acc in enumerate(accs)]
        outs = _ffn_core(hs, nwa_ref[...], nwb_ref[...], wg_ref, wu_ref, wd_ref, a_ref)
        for part, out in enumerate(outs):
            o_ref[_group(part), :] = out


def _proj_ffn(xs, h, w_out, norm4, layer, wg, wu, wd, final, batch, lp, seq):
    d = norm4.shape[-1]
    d_ff = wg.shape[-1]
    if final:
        specs = lambda width: _real_specs(width, lp, seq)
        out_rows = batch * seq
    else:
        specs = _tile_specs
        out_rows = batch * lp
    h_specs = specs(d)
    n_rows = len(h_specs)
    in_specs, args = [], []
    for x in xs:
        in_specs += specs(x.shape[1])
        args += [x] * n_rows
    in_specs += h_specs + [_resident(w_out.shape), _pick(norm4, layer, 3),
                           _pick(norm4, layer, 4), _pick(norm4, layer, 5)]
    prep = d_ff // FF_TILE
    in_specs = [_after_prep(s, prep) for s in in_specs] + _weight_chunk_specs(wg, wd, layer, 1)
    args += [h] * n_rows + [w_out, norm4, norm4, norm4, wg, wu, wd]
    return pl.pallas_call(
        functools.partial(_proj_ffn_body, n_rows=n_rows, n_x=len(xs)),
        grid=(prep + out_rows // ROW_TILE,),
        in_specs=in_specs,
        out_specs=_after_prep(pl.BlockSpec((ROW_TILE, d), lambda i: (i, 0)), prep),
        out_shape=jax.ShapeDtypeStruct((out_rows, d), F32),
        scratch_shapes=[pltpu.VMEM((ROW_TILE, d_ff), BF16)] + _weight_scratch(wg),
        compiler_params=_params("arbitrary"),
        name="proj_ffn",
    )(*args)


def _t5_bucket_np(rel):
    n = np.maximum(rel, 0)
    max_exact = REL_BUCKETS // 2
    n_f = np.maximum(n, 1).astype(np.float32)
    large = max_exact + (np.log(n_f / np.float32(max_exact))
                         / np.float32(math.log(REL_MAX_DIST / max_exact))
                         * np.float32(REL_BUCKETS - max_exact)).astype(np.int32)
    large = np.minimum(large, REL_BUCKETS - 1)
    return np.where(n < max_exact, n, large).astype(np.int32)


def _swa_bucket_index():
    i = np.arange(SWA_BLOCK)[:, None]
    c = np.arange(SWA_BLOCK)[None, :]
    out = np.full((3, SWA_BLOCK, 2 * SWA_BLOCK), -1, np.int32)
    for variant in range(3):
        pos_q = variant * SWA_BLOCK + i - PAD
        pos_k = variant * SWA_BLOCK + c - PAD - np.where(c > i, SWA_BLOCK, 0)
        rel_b = pos_q - pos_k
        assert ((rel_b >= 0) & (rel_b < SWA_BLOCK)).all()
        out[variant, :, :SWA_BLOCK] = np.where(pos_k >= N_META, _t5_bucket_np(rel_b), -1)
        rel_m = pos_q - c
        meta = (c < N_META) & (rel_m >= 0)
        out[variant, :, SWA_BLOCK:] = np.where(meta, _t5_bucket_np(rel_m), -1)
    return out


_SWA_BUCKET_INDEX = _swa_bucket_index()


def _bias_body(tbl_ref, idx_ref, o_ref):
    idx = idx_ref[0]
    group = SWA_Q_HEADS // SWA_KV_HEADS
    for head in range(SWA_Q_HEADS):
        acc = jnp.full(idx.shape, NEG_INF, F32)
        for b in range(REL_BUCKETS):
            acc = jnp.where(idx == b, tbl_ref[b, head], acc)
        rows = slice((head % group) * SWA_BLOCK, (head % group + 1) * SWA_BLOCK)
        o_ref[0, head // group, rows, :] = acc


def _swa_bias(rel_table):
    idx = jnp.asarray(_SWA_BUCKET_INDEX)
    group = SWA_Q_HEADS // SWA_KV_HEADS
    width = idx.shape[2]
    return pl.pallas_call(
        _bias_body,
        grid=(3,),
        in_specs=[pl.BlockSpec(memory_space=pltpu.SMEM),
                  pl.BlockSpec((1, SWA_BLOCK, width), lambda v: (v, 0, 0))],
        out_specs=pl.BlockSpec((1, SWA_KV_HEADS, group * SWA_BLOCK, width),
                               lambda v: (v, 0, 0, 0)),
        out_shape=jax.ShapeDtypeStruct((3, SWA_KV_HEADS, group * SWA_BLOCK, width), F32),
        compiler_params=_params("arbitrary"),
        name="swa_bias",
    )(rel_table, idx)


def _swa_stages(sink_ref, q_ref, kc_ref, kp_ref, km_ref, vc_ref, vp_ref, vm_ref, bias_ref,
                step, blocks, store):
    dh = SWA_HEAD_DIM
    blk = SWA_BLOCK
    group = SWA_Q_HEADS // SWA_KV_HEADS
    scale = dh ** -0.5
    zpad = jnp.zeros((blk - N_META, dh), BF16)
    row = lax.broadcasted_iota(jnp.int32, (group * blk, blk), 0) % blk
    col = lax.broadcasted_iota(jnp.int32, (group * blk, blk), 1)
    from_prev = col > row

    ones_col = (lax.broadcasted_iota(jnp.int32, (3 * blk, dh), 1) == 0).astype(BF16)

    units = [(j, hk) for j in range(blocks) for hk in range(SWA_KV_HEADS)]
    scores, values = [], []
    for j, hk in units:
        r = slice(j * blk, (j + 1) * blk)
        kv = slice(hk * dh, (hk + 1) * dh)
        k_prev = kp_ref[:, kv] if j == 0 else kc_ref[(j - 1) * blk:j * blk, kv]
        v_prev = vp_ref[:, kv] if j == 0 else vc_ref[(j - 1) * blk:j * blk, kv]
        k_all = jnp.concatenate([k_prev, kc_ref[r, kv], km_ref[:, kv], zpad], axis=0)
        v_all = jnp.concatenate([v_prev, vc_ref[r, kv], vm_ref[:, kv], zpad], axis=0)
        values.append(jnp.concatenate([v_all, ones_col], axis=1))
        q = jnp.concatenate([q_ref[r, (hk * group + g) * dh:(hk * group + g + 1) * dh]
                             for g in range(group)], axis=0) * scale
        scores.append(_dot_nt(q, k_all))
        yield

    probs, sink_terms = [], []
    for (j, hk), s in zip(units, scores):
        bias = bias_ref[jnp.minimum(step * blocks + j, 2), hk]
        s_band = jnp.where(from_prev, s[:, :blk], s[:, blk:2 * blk]) + bias[:, :blk]
        s_meta = s[:, 2 * blk:] + bias[:, blk:]
        sink = jnp.concatenate([jnp.full((blk, 1), sink_ref[hk * group + g], F32)
                                for g in range(group)], axis=0)
        m = jnp.maximum(jnp.max(jnp.maximum(s_band, s_meta), axis=-1, keepdims=True), sink)
        e_band = jnp.exp(s_band - m)
        e_meta = jnp.exp(s_meta - m)
        sink_terms.append(jnp.exp(sink - m))
        probs.append(jnp.concatenate([jnp.where(from_prev, e_band, 0.0),
                                      jnp.where(from_prev, 0.0, e_band), e_meta],
                                     axis=1).astype(BF16))

    for (j, hk), p, v_ext, sink_term in zip(units, probs, values, sink_terms):
        pv = _dot(p, v_ext)
        o = pv[:, :dh] / (pv[:, dh:dh + 1] + sink_term)
        for g in range(group):
            hq = hk * group + g
            store(slice(j * blk, (j + 1) * blk), slice(hq * dh, (hq + 1) * dh),
                  o[g * blk:(g + 1) * blk])
        yield


def _swa_body(sink_ref, q_ref, kc_ref, kp_ref, km_ref, vc_ref, vp_ref, vm_ref, bias_ref, o_ref,
              *, blocks):
    def store(rows, cols, value):
        o_ref[rows, cols] = value.astype(o_ref.dtype)

    _drain(_swa_stages(sink_ref, q_ref, kc_ref, kp_ref, km_ref, vc_ref, vp_ref, vm_ref,
                       bias_ref, pl.program_id(1), blocks, store))


def _swa(q, k, v, bias, sinks, batch, lp, blocks):
    rows = q.shape[0]
    nb = lp // SWA_BLOCK
    steps = nb // blocks
    meta_blocks = lp // N_META
    cur = lambda b, s: (b * steps + s, 0)
    prev = lambda b, s: (b * nb + jnp.maximum(s * blocks - 1, 0), 0)
    meta = lambda b, s: (b * meta_blocks + PAD // N_META, 0)
    kv_w = k.shape[1]
    rs = blocks * SWA_BLOCK
    return pl.pallas_call(
        functools.partial(_swa_body, blocks=blocks),
        grid=(batch, steps),
        in_specs=[pl.BlockSpec(memory_space=pltpu.SMEM),
                  pl.BlockSpec((rs, q.shape[1]), cur),
                  pl.BlockSpec((rs, kv_w), cur), pl.BlockSpec((SWA_BLOCK, kv_w), prev),
                  pl.BlockSpec((N_META, kv_w), meta),
                  pl.BlockSpec((rs, kv_w), cur), pl.BlockSpec((SWA_BLOCK, kv_w), prev),
                  pl.BlockSpec((N_META, kv_w), meta),
                  _resident(bias.shape)],
        out_specs=pl.BlockSpec((rs, q.shape[1]), cur),
        out_shape=jax.ShapeDtypeStruct((rows, q.shape[1]), BF16),
        compiler_params=_params("parallel", "arbitrary"),
        name="swa",
    )(sinks, q, k, k, k, v, v, v, bias)


def _tri_mask():
    row = lax.broadcasted_iota(jnp.int32, (CHUNK, CHUNK), 0)
    col = lax.broadcasted_iota(jnp.int32, (CHUNK, CHUNK), 1)
    return row >= col


def _deltanet_stages(act_ref, z_ref, gates_ref, alog_ref, dtb_ref, nw_ref, s_ref, chunks,
                     store):
    nh, dh = DN_HEADS, DN_HEAD_DIM
    gt = gates_ref[...]
    beta_full = jax.nn.sigmoid(gt)
    g_full = -jnp.exp(alog_ref[...]) * _softplus(gt + dtb_ref[...])
    tri = _tri_mask().astype(F32)
    gc_chunks = [_dot(tri, g_full[c * CHUNK:(c + 1) * CHUNK, :], HIGHEST)
                 for c in range(chunks)]
    yield

    lane = lax.broadcasted_iota(jnp.int32, (CHUNK, 2 * CHUNK), 1)
    row = lax.broadcasted_iota(jnp.int32, (CHUNK, 2 * CHUNK), 0)
    first = lane < CHUNK
    col = jnp.where(first, lane, lane - CHUNK)
    incl2, strict2 = row >= col, row > col
    eye2 = (row == col).astype(F32)
    first_rows2 = lax.broadcasted_iota(jnp.int32, (2 * CHUNK, 2 * CHUNK), 1) < CHUNK

    def block_diag(a, b):
        zero = jnp.zeros_like(a)
        return jnp.concatenate([jnp.concatenate([a, zero], axis=1),
                                jnp.concatenate([zero, b], axis=1)], axis=0)

    def block_diag_halves(x):
        zero = jnp.zeros_like(x)
        return jnp.concatenate([jnp.where(first, x, zero), jnp.where(first, zero, x)], axis=0)

    pairs = [(c, p) for c in range(chunks) for p in range(nh // 2)]
    neg_a, akt16, rhs16, qd16, decay = [], [], [], {}, {}
    for c, p in pairs:
        if p == 0 and c % 2 == 0 and c > 0:
            yield
        r = slice(c * CHUNK, (c + 1) * CHUNK)
        ks, kqs, kdecs, gcs, rhs = [], [], [], [], []
        for h in (2 * p, 2 * p + 1):
            q = act_ref[r, h * dh:(h + 1) * dh].astype(F32)
            k = act_ref[r, DN_DIM + h * dh:DN_DIM + (h + 1) * dh].astype(F32)
            v = act_ref[r, 2 * DN_DIM + h * dh:2 * DN_DIM + (h + 1) * dh].astype(F32)
            q = q * lax.rsqrt(jnp.sum(q * q, axis=-1, keepdims=True) + 1e-6) * dh ** -0.5
            k = k * lax.rsqrt(jnp.sum(k * k, axis=-1, keepdims=True) + 1e-6)
            beta = jnp.broadcast_to(beta_full[r, h:h + 1], (CHUNK, dh))
            gc = jnp.broadcast_to(gc_chunks[c][:, nh + h:nh + h + 1], (CHUNK, dh))
            gc_last = gc[CHUNK - 1:CHUNK, :]
            k_beta = k * beta
            ks.append(k)
            kqs.append(jnp.concatenate([k_beta, q], axis=0).astype(BF16))
            kdecs.append(k * jnp.exp(gc_last - gc))
            gcs.append(gc)
            rhs.append(jnp.concatenate([v * beta, k_beta * jnp.exp(gc)], axis=1).astype(BF16))
            qd16[c, h] = (q * jnp.exp(gc)).astype(BF16)
            decay[c, h] = jnp.exp(gc_last)
        gc2 = jnp.where(first, gcs[0], gcs[1])
        gc_row2 = jnp.transpose(jnp.concatenate(gcs, axis=0))[:CHUNK, :]
        gamma2 = jnp.where(incl2, jnp.exp(jnp.where(incl2, gc2 - gc_row2, 0.0)), 0.0)
        k2 = jnp.concatenate(ks, axis=0).astype(BF16)
        kq2 = jnp.where(first_rows2, _dot_nt(kqs[0], k2), _dot_nt(kqs[1], k2))
        neg_a.append(jnp.where(strict2, -(kq2[:CHUNK] * gamma2), 0.0))
        attn16 = jnp.where(incl2, kq2[CHUNK:] * gamma2, 0.0).astype(BF16)
        kdt16 = jnp.transpose(jnp.concatenate(kdecs, axis=0)).astype(BF16)
        akt16.append(jnp.concatenate([attn16, kdt16], axis=0))
        rhs16.append(block_diag(rhs[0], rhs[1]))

    yield
    levels = int(math.log2(CHUNK)) - 1
    t = [eye2 + x for x in neg_a]
    p16 = [x.astype(BF16) for x in neg_a]
    p16 = [_dot(x, block_diag_halves(x)).astype(BF16) for x in p16]
    yield
    for level in range(levels):
        if level + 1 < levels:
            tp = [_dot(jnp.concatenate([y.astype(BF16), x], axis=0), block_diag_halves(x))
                  for y, x in zip(t, p16)]
            t = [y + r[:CHUNK] for y, r in zip(t, tp)]
            p16 = [r[CHUNK:].astype(BF16) for r in tp]
        else:
            t = [y + _dot(y.astype(BF16), block_diag_halves(x)) for y, x in zip(t, p16)]
        yield
    uw2 = [_dot(y.astype(BF16), b) for y, b in zip(t, rhs16)]
    u, wq16 = {}, {}
    for (c, p), x in zip(pairs, uw2):
        for half, h in enumerate((2 * p, 2 * p + 1)):
            u[c, h] = x[:, half * 2 * dh:half * 2 * dh + dh]
            w16 = x[:, half * 2 * dh + dh:(half + 1) * 2 * dh].astype(BF16)
            wq16[c, h] = jnp.concatenate([w16, qd16[c, h]], axis=0)
    yield

    nw = nw_ref[...]
    state = [s_ref[h] for h in range(nh)]
    for c in range(chunks):
        r = slice(c * CHUNK, (c + 1) * CHUNK)
        s16 = [s.astype(BF16) for s in state]
        wqs = [_dot(wq16[c, h], s16[h]) for h in range(nh)]
        yield
        for p in range(nh // 2):
            ha, hb = 2 * p, 2 * p + 1
            vn_a = (u[c, ha] - wqs[ha][:CHUNK]).astype(BF16)
            vn_b = (u[c, hb] - wqs[hb][:CHUNK]).astype(BF16)
            ak2 = _dot(akt16[c * (nh // 2) + p], block_diag(vn_a, vn_b))
            for half, h in enumerate((ha, hb)):
                hc = slice(h * dh, (h + 1) * dh)
                ak = ak2[:, half * dh:(half + 1) * dh]
                state[h] = state[h] * decay[c, h] + ak[CHUNK:]
                out = wqs[h][CHUNK:] + ak[:CHUNK]
                store(r, hc, _rms(out, nw) * _silu(z_ref[r, hc].astype(F32)))
        yield
    for h in range(nh):
        s_ref[h] = state[h]


def _deltanet_body(act_ref, z_ref, gates_ref, alog_ref, dtb_ref, nw_ref, o_ref, s_ref, *,
                   chunks):
    @pl.when(pl.program_id(1) == 0)
    def _():
        s_ref[...] = jnp.zeros_like(s_ref)

    def store(rows, cols, value):
        o_ref[rows, cols] = value.astype(o_ref.dtype)

    _drain(_deltanet_stages(act_ref, z_ref, gates_ref, alog_ref, dtb_ref, nw_ref, s_ref,
                            chunks, store))


def _deltanet(act, z, gates, alog_vec, dtb_vec, nw, batch, lp, chunks):
    rows = act.shape[0]
    rs = chunks * CHUNK
    steps = lp // rs
    idx = lambda b, s: (b * steps + s, 0)
    return pl.pallas_call(
        functools.partial(_deltanet_body, chunks=chunks),
        grid=(batch, steps),
        in_specs=[pl.BlockSpec((rs, act.shape[1]), idx), pl.BlockSpec((rs, z.shape[1]), idx),
                  pl.BlockSpec((rs, gates.shape[1]), idx),
                  _resident(alog_vec.shape), _resident(dtb_vec.shape), _resident(nw.shape)],
        out_specs=pl.BlockSpec((rs, DN_DIM), idx),
        out_shape=jax.ShapeDtypeStruct((rows, DN_DIM), BF16),
        scratch_shapes=[pltpu.VMEM((DN_HEADS, DN_HEAD_DIM, DN_HEAD_DIM), F32)],
        compiler_params=_params("parallel", "arbitrary"),
        name="deltanet",
    )(act, z, gates, alog_vec, dtb_vec, nw)


def _mix_ffn_body(*refs, blocks, chunks, steps, last):
    *refs, wg_s, wu_s, wd_s = refs
    prep = _prepare_weights(refs[20:23], (wg_s, wu_s, wd_s))
    t = pl.program_id(0) - prep
    refs = refs[:20] + [wg_s, wu_s, wd_s] + refs[23:]

    @pl.when(t >= 0)
    def _():
        _mix_ffn_step(t, *refs, blocks=blocks, chunks=chunks, steps=steps, last=last)


def _mix_ffn_step(t, sink_ref, q_ref, kc_ref, kp_ref, km_ref, vc_ref, vp_ref, vm_ref, bias_ref,
                  act_ref, z_ref, gates_ref, alog_ref, dtb_ref, dnw_ref,
                  h_ref, wo_ref, nwo_ref, nwa_ref, nwb_ref, wg_ref, wu_ref, wd_ref,
                  o_ref, s_ref, mix_ref, a_ref, *, blocks, chunks, steps, last):
    seq_step = jnp.minimum(t, last) % steps
    write_slot = t % 2
    read_slot = 1 - write_slot
    swa_width = q_ref.shape[1]
    groups = MIX_SPLIT
    rows = q_ref.shape[0] // groups

    @pl.when(t == 0)
    def _():
        mix_ref[...] = jnp.zeros_like(mix_ref)

    @pl.when(seq_step == 0)
    def _():
        s_ref[...] = jnp.zeros_like(s_ref)

    def store_swa(rows, cols, value):
        mix_ref[write_slot, rows, cols] = value.astype(mix_ref.dtype)

    def store_dn(rows, cols, value):
        cols = slice(swa_width + cols.start, swa_width + cols.stop)
        mix_ref[write_slot, rows, cols] = value.astype(mix_ref.dtype)

    def mixer_stages():
        dn = _deltanet_stages(act_ref, z_ref, gates_ref, alog_ref, dtb_ref, dnw_ref,
                              s_ref, chunks, store_dn)
        swa = _swa_stages(sink_ref, q_ref, kc_ref, kp_ref, km_ref, vc_ref, vp_ref,
                          vm_ref, bias_ref, seq_step, blocks, store_swa)
        for _ in dn:
            yield
            if next(swa, StopIteration) is not StopIteration:
                yield
        yield from swa

    def ffn_stages():
        accs = []
        for part in range(groups):
            accs.append(_dot(mix_ref[read_slot, _group(part, rows), :], wo_ref[...]))
            yield
        hs = [h_ref[_group(part, rows), :] + _rms(acc, nwo_ref[...])
              for part, acc in enumerate(accs)]
        outs = []
        yield from _ffn_stages(hs, nwa_ref[...], nwb_ref[...], wg_ref, wu_ref, wd_ref, a_ref,
                               outs)
        for part, out in enumerate(outs):
            o_ref[_group(part, rows), :] = out

    _interleave(ffn_stages(), mixer_stages())


def _mix_ffn(qa, ka, va, bias, sinks, act, z, gates, alog_vec, dtb_vec, dn_nw, h, w_out, norm4,
             layer, wg, wu, wd, batch, lp, blocks, chunks):
    d = norm4.shape[-1]
    d_ff = wg.shape[-1]
    rs = blocks * SWA_BLOCK
    assert rs == chunks * CHUNK and rs % (16 * MIX_SPLIT) == 0
    nb = lp // SWA_BLOCK
    steps = lp // rs
    last = batch * steps - 1
    meta_blocks = lp // N_META

    def tile(t):
        return jnp.minimum(t, last)

    cur = lambda t: (tile(t), 0)
    prev = lambda t: ((tile(t) // steps) * nb + jnp.maximum((tile(t) % steps) * blocks - 1, 0), 0)
    meta = lambda t: ((tile(t) // steps) * meta_blocks + PAD // N_META, 0)
    behind = lambda t: (jnp.maximum(t - 1, 0), 0)
    kv_w = ka.shape[1]
    prep = d_ff // FF_TILE
    in_specs = [pl.BlockSpec(memory_space=pltpu.SMEM),
                pl.BlockSpec((rs, qa.shape[1]), cur),
                pl.BlockSpec((rs, kv_w), cur), pl.BlockSpec((SWA_BLOCK, kv_w), prev),
                pl.BlockSpec((N_META, kv_w), meta),
                pl.BlockSpec((rs, kv_w), cur), pl.BlockSpec((SWA_BLOCK, kv_w), prev),
                pl.BlockSpec((N_META, kv_w), meta),
                _resident(bias.shape),
                pl.BlockSpec((rs, act.shape[1]), cur), pl.BlockSpec((rs, z.shape[1]), cur),
                pl.BlockSpec((rs, gates.shape[1]), cur),
                _resident(alog_vec.shape), _resident(dtb_vec.shape), _resident(dn_nw.shape),
                pl.BlockSpec((rs, d), behind), _resident(w_out.shape),
                _pick(norm4, layer, 3), _pick(norm4, layer, 4), _pick(norm4, layer, 5)]
    return pl.pallas_call(
        functools.partial(_mix_ffn_body, blocks=blocks, chunks=chunks, steps=steps, last=last),
        grid=(prep + last + 2,),
        in_specs=[_after_prep(s, prep) for s in in_specs]
        + _weight_chunk_specs(wg, wd, layer, 1),
        out_specs=_after_prep(pl.BlockSpec((rs, d), behind), prep),
        out_shape=jax.ShapeDtypeStruct(h.shape, F32),
        scratch_shapes=[pltpu.VMEM((DN_HEADS, DN_HEAD_DIM, DN_HEAD_DIM), F32),
                        pltpu.VMEM((2, rs, qa.shape[1] + DN_DIM), BF16),
                        pltpu.VMEM((rs, d_ff), BF16)] + _weight_scratch(wg),
        compiler_params=_params("arbitrary"),
        name="mix_ffn",
    )(sinks, qa, ka, ka, ka, va, va, va, bias, act, z, gates, alog_vec, dtb_vec, dn_nw,
      h, w_out, norm4, norm4, norm4, wg, wu, wd)


def _gla_body(q_ref, k_ref, v_ref, g_ref, gk_ref, wgu_ref, bg_ref, nw_ref, o_ref, s_ref, *,
              chunks):
    nh = GLA_HEADS
    dk = q_ref.shape[1] // nh
    dv = v_ref.shape[1] // nh

    @pl.when(pl.program_id(1) == 0)
    def _():
        s_ref[...] = jnp.zeros_like(s_ref)

    logits = _dot(gk_ref[...].astype(BF16), wgu_ref[...]) + bg_ref[...]
    glog = (jnp.minimum(logits, 0.0) - jnp.log(1.0 + jnp.exp(-jnp.abs(logits)))) / GLA_GATE_NORM
    incl = _tri_mask()
    tri = incl.astype(F32)
    bcum = [_dot(tri, glog[c * CHUNK:(c + 1) * CHUNK, :], HIGHEST) for c in range(chunks)]

    items = [(c, h) for c in range(chunks) for h in range(nh)]
    qd16, kn16, kdt16, decay = [], [], [], []
    for c, h in items:
        r = slice(c * CHUNK, (c + 1) * CHUNK)
        kc = slice(h * dk, (h + 1) * dk)
        bc = bcum[c][:, kc]
        q = q_ref[r, kc].astype(F32) * dk ** -0.5
        k = k_ref[r, kc].astype(F32)
        b_last = bc[CHUNK - 1:CHUNK, :]
        qd16.append((q * jnp.exp(bc)).astype(BF16))
        kn16.append((k * jnp.exp(-bc)).astype(BF16))
        kdt16.append(jnp.transpose(k * jnp.exp(b_last - bc)).astype(BF16))
        decay_rows = jnp.broadcast_to(jnp.exp(b_last), (8, dk))
        decay.append(jnp.transpose(decay_rows)[:, :1])
    values = [v_ref[c * CHUNK:(c + 1) * CHUNK, h * dv:(h + 1) * dv] for c, h in items]
    attn16 = [jnp.where(incl, _dot_nt(a, b), 0.0).astype(BF16) for a, b in zip(qd16, kn16)]
    intra = [_dot(a, v) for a, v in zip(attn16, values)]
    update = [_dot(kt, v) for kt, v in zip(kdt16, values)]

    nw = nw_ref[...]
    state = [s_ref[h] for h in range(nh)]
    for c in range(chunks):
        r = slice(c * CHUNK, (c + 1) * CHUNK)
        for h in range(nh):
            i = c * nh + h
            vc = slice(h * dv, (h + 1) * dv)
            o = intra[i] + _dot(qd16[i], state[h].astype(BF16))
            state[h] = state[h] * decay[i] + update[i]
            o = _rms(o, nw) * _silu(g_ref[r, vc].astype(F32))
            o_ref[r, vc] = o.astype(o_ref.dtype)
    for h in range(nh):
        s_ref[h] = state[h]


def _gla(q, k, v, g, gk, wgu, bg, nw, batch, lp, chunks):
    rows = q.shape[0]
    rs = chunks * CHUNK
    steps = lp // rs
    idx = lambda b, s: (b * steps + s, 0)
    dk = q.shape[1] // GLA_HEADS
    dv = v.shape[1] // GLA_HEADS
    return pl.pallas_call(
        functools.partial(_gla_body, chunks=chunks),
        grid=(batch, steps),
        in_specs=[pl.BlockSpec((rs, q.shape[1]), idx), pl.BlockSpec((rs, k.shape[1]), idx),
                  pl.BlockSpec((rs, v.shape[1]), idx), pl.BlockSpec((rs, g.shape[1]), idx),
                  pl.BlockSpec((rs, gk.shape[1]), idx), _resident(wgu.shape),
                  _resident(bg.shape), _resident(nw.shape)],
        out_specs=pl.BlockSpec((rs, v.shape[1]), idx),
        out_shape=jax.ShapeDtypeStruct((rows, v.shape[1]), BF16),
        scratch_shapes=[pltpu.VMEM((GLA_HEADS, dk, dv), F32)],
        compiler_params=_params("parallel", "arbitrary"),
        name="gla",
    )(q, k, v, g, gk, wgu, bg, nw)


def _per_step(n, preferred):
    for c in preferred:
        if n % c == 0:
            return c
    return 1


def kernel(x, meta_tokens, norm_w, ffn_w_gate, ffn_w_up, ffn_w_down, rel_bias_table,
           even_w_in, even_conv_w, swa_sinks, dn_a_log, dn_dt_bias, dn_norm_w, even_w_out,
           odd_w_in, gla_w_gate_up, gla_b_gate, gla_norm_w, odd_w_out):
    batch, seq, d = x.shape
    depth = norm_w.shape[0]
    lp = SWA_BLOCK + seq
    assert seq % ROW_TILE == 0 and (batch * lp) % ROW_TILE == 0
    rows = batch * lp
    chunks = _per_step(lp // CHUNK, (6, 3, 2))
    swa_blocks = _per_step(lp // SWA_BLOCK, (3, 2))

    meta_block = jnp.pad(meta_tokens.astype(x.dtype), ((PAD, 0), (0, 0)))
    norm4 = norm_w[:, :, None, :]
    wg, wu, wd = ffn_w_gate, ffn_w_up, ffn_w_down

    swa_q = SWA_Q_HEADS * SWA_HEAD_DIM
    swa_kv = SWA_KV_HEADS * SWA_HEAD_DIM
    bias = None
    h = x.reshape(batch * seq, d)
    for layer in range(depth):
        i = layer // 2
        embed = meta_block if layer == 0 else None
        final = layer == depth - 1
        if layer % 2 == 0:
            assert even_w_in.shape[2] == swa_q + 2 * swa_kv + 4 * DN_DIM + 2 * DN_HEADS
            widths = (swa_q, swa_kv, swa_kv, 3 * DN_DIM, DN_DIM, LANE)
            h, qa, ka, va, act_b, z_b, gates = _ffn_proj(
                h, embed, norm4, layer, wg, wu, wd, even_w_in, i, widths,
                (BF16, BF16, BF16, BF16, BF16, F32), rows, lp, seq,
                conv_out=3, conv_w=even_conv_w[i])
            if bias is None:
                bias = _swa_bias(rel_bias_table)
            lane_pad = (DN_HEADS, LANE - 2 * DN_HEADS)
            alog_vec = jnp.pad(dn_a_log[i], lane_pad)[None, :]
            dtb_vec = jnp.pad(dn_dt_bias[i], lane_pad)[None, :]
            dn_nw = dn_norm_w[i][None, :]
            w_out = even_w_out[i]
            if not final and swa_blocks * SWA_BLOCK == chunks * CHUNK:
                h = _mix_ffn(qa, ka, va, bias, swa_sinks[i], act_b, z_b, gates, alog_vec,
                             dtb_vec, dn_nw, h, w_out.astype(BF16), norm4, layer, wg, wu, wd,
                             batch, lp, swa_blocks, chunks)
                continue
            o_a = _swa(qa, ka, va, bias, swa_sinks[i], batch, lp, swa_blocks)
            o_b = _deltanet(act_b, z_b, gates, alog_vec, dtb_vec, dn_nw, batch, lp, chunks)
            mixed = [o_a, o_b]
        else:
            key_dim = gla_w_gate_up.shape[2]
            val_dim = odd_w_out.shape[1]
            widths = (key_dim, key_dim, val_dim, val_dim, LANE)
            h, q, k, v, g, gk = _ffn_proj(
                h, embed, norm4, layer, wg, wu, wd, odd_w_in, i, widths,
                (BF16, BF16, BF16, BF16, F32), rows, lp, seq)
            wgu = jnp.pad(gla_w_gate_up[i], ((0, LANE - GLA_GATE_RANK), (0, 0))).astype(BF16)
            o = _gla(q, k, v, g, gk, wgu, gla_b_gate[i][None, :], gla_norm_w[i][None, :],
                     batch, lp, _per_step(lp // CHUNK, (11, 6, 3, 2)))
            mixed, w_out = [o], odd_w_out[i]
        h = _proj_ffn(mixed, h, w_out.astype(BF16), norm4, layer, wg, wu, wd, final,
                      batch, lp, seq)
    return h.reshape(batch, seq, d)
```

```python
import functools
import math

import numpy as np
import jax
import jax.numpy as jnp
from jax import lax
from jax.experimental import pallas as pl
from jax.experimental.pallas import tpu as pltpu

F32 = jnp.float32
BF16 = jnp.bfloat16
HIGHEST = lax.Precision.HIGHEST

N_META = 16
NORM_EPS = 1e-6
NEG_INF = -1e30
SWA_Q_HEADS = 8
SWA_KV_HEADS = 2
SWA_HEAD_DIM = 64
SWA_BLOCK = 128
REL_BUCKETS = 32
REL_MAX_DIST = 128
DN_HEADS = 4
DN_HEAD_DIM = 128
DN_DIM = DN_HEADS * DN_HEAD_DIM
GLA_HEADS = 4
GLA_GATE_RANK = 16
GLA_GATE_NORM = 16.0
CHUNK = 64

PAD = SWA_BLOCK - N_META
LANE = 128
ROW_TILE = 512
FF_TILE = 256
VMEM_LIMIT = 60 * 1024 * 1024


def _rms(x, w):
    return x * lax.rsqrt(jnp.mean(x * x, axis=-1, keepdims=True) + NORM_EPS) * w


def _silu(x):
    return x * jax.nn.sigmoid(x)


def _softplus(x):
    return jnp.maximum(x, 0.0) + jnp.log(1.0 + jnp.exp(-jnp.abs(x)))


def _dot(a, b, precision=None):
    return jnp.dot(a, b, preferred_element_type=F32, precision=precision)


def _dot_nt(a, b):
    return lax.dot_general(a, b, (((1,), (1,)), ((), ())), preferred_element_type=F32)


def _resident(shape):
    nd = len(shape)
    return pl.BlockSpec(shape, lambda *_: (0,) * nd, pipeline_mode=pl.Buffered(1))


def _params(*semantics):
    return pltpu.CompilerParams(dimension_semantics=semantics, vmem_limit_bytes=VMEM_LIMIT)


SUB = ROW_TILE // SWA_BLOCK


def _pick(arr, *lead):
    rest = arr.shape[len(lead):]
    index = tuple(lead) + (0,) * len(rest)
    return pl.BlockSpec((None,) * len(lead) + rest, lambda i: index,
                        pipeline_mode=pl.Buffered(1))


def _after_prep(spec, prep):
    if spec.index_map is None:
        return spec
    index = spec.index_map
    return pl.BlockSpec(spec.block_shape, lambda i: index(jnp.maximum(i - prep, 0)),
                        pipeline_mode=spec.pipeline_mode)


def _weight_chunk_specs(wg, wd, layer, which):
    d, d_ff = wg.shape[-2:]
    last = d_ff // FF_TILE - 1
    cols = pl.BlockSpec((None, None, d, FF_TILE), lambda i: (layer, which, 0, jnp.minimum(i, last)))
    rows = pl.BlockSpec((None, None, FF_TILE, d), lambda i: (layer, which, jnp.minimum(i, last), 0))
    return [cols, cols, rows]


def _weight_scratch(wg):
    d, d_ff = wg.shape[-2:]
    chunks = d_ff // FF_TILE
    return [pltpu.VMEM((chunks, d, FF_TILE), BF16), pltpu.VMEM((chunks, d, FF_TILE), BF16),
            pltpu.VMEM((d_ff, d), BF16)]


def _prepare_weights(chunk_refs, scratch_refs):
    wg_blk, wu_blk, wd_blk = chunk_refs
    wg_s, wu_s, wd_s = scratch_refs
    prep = wg_s.shape[0]
    i = pl.program_id(0)

    @pl.when(i < prep)
    def _():
        wg_s[i] = wg_blk[...].astype(BF16)
        wu_s[i] = wu_blk[...].astype(BF16)
        wd_s[pl.ds(pl.multiple_of(i * FF_TILE, FF_TILE), FF_TILE), :] = wd_blk[...].astype(BF16)

    return prep


def _tile_specs(width):
    return [pl.BlockSpec((ROW_TILE, width), lambda i: (i, 0))]


def _real_specs(width, lp, seq):
    tiles = seq // ROW_TILE
    blocks = lp // SWA_BLOCK

    def spec(j):
        return pl.BlockSpec((SWA_BLOCK, width),
                            lambda i: ((i // tiles) * blocks + 1 + (i % tiles) * SUB + j, 0))
    return [spec(j) for j in range(SUB)]


def _embed_specs(width, lp, seq):
    blocks = lp // SWA_BLOCK
    x_blocks = seq // SWA_BLOCK

    def spec(j):
        def index(i):
            blk = i * SUB + j
            return ((blk // blocks) * x_blocks + jnp.maximum(blk % blocks - 1, 0), 0)
        return pl.BlockSpec((SWA_BLOCK, width), index)
    return [spec(j) for j in range(SUB)]


ROW_SPLIT = 4
GROUP_ROWS = ROW_TILE // ROW_SPLIT
MIX_SPLIT = 2


def _group(part, rows=GROUP_ROWS):
    return slice(part * rows, (part + 1) * rows)


def _load_rows(refs, part):
    if len(refs) == 1:
        return refs[0][_group(part), :]
    per = len(refs) // ROW_SPLIT
    return jnp.concatenate([r[...] for r in refs[part * per:(part + 1) * per]], axis=0)


def _drain(stages):
    for _ in stages:
        pass


def _interleave(main, side):
    for _ in main:
        next(side, None)
    _drain(side)


def _ffn_stages(xs, nwa, nwb, wg_ref, wu_ref, wd_ref, a_ref, outs):
    hns = [_rms(x, nwa).astype(BF16) for x in xs]
    rows = xs[0].shape[0]
    for part, hn in enumerate(hns):
        for c in range(wg_ref.shape[0]):
            cols = slice(c * FF_TILE, (c + 1) * FF_TILE)
            g = _dot(hn, wg_ref[c])
            yield
            u = _dot(hn, wu_ref[c])
            a_ref[_group(part, rows), cols] = (_silu(g) * u).astype(BF16)
            yield
    fs = []
    for part in range(len(xs)):
        fs.append(_dot(a_ref[_group(part, rows), :], wd_ref[...]))
        yield
    outs.extend(x + 0.5 * _rms(f, nwb) for x, f in zip(xs, fs))


def _ffn_core(xs, nwa, nwb, wg_ref, wu_ref, wd_ref, a_ref):
    outs = []
    _drain(_ffn_stages(xs, nwa, nwb, wg_ref, wu_ref, wd_ref, a_ref, outs))
    return outs


CONV_HALO = 8


def _ffn_proj_body(*refs, n_rows, embed_blocks, n_out, conv_out, lp, tail_valid, prologue):
    lead = n_rows + bool(embed_blocks) + (conv_out is not None) + 2
    wg_s, wu_s, wd_s, win_s, wtail_s = refs[-5:]
    nwm_ref, win_blk = refs[lead + 3:lead + 5]
    _prepare_weights(refs[lead:lead + 3], (wg_s, wu_s, wd_s))
    step = pl.program_id(0) - prologue
    win_chunks = win_s.shape[0]

    @pl.when(pl.program_id(0) < win_chunks)
    def _():
        win_s[pl.program_id(0)] = win_blk[...].astype(BF16)

    @pl.when(pl.program_id(0) == win_chunks)
    def _():
        tail = win_blk[:, :LANE]
        lane = lax.broadcasted_iota(jnp.int32, tail.shape, 1)
        wtail_s[...] = jnp.where(lane < tail_valid, tail, 0.0).astype(BF16)

    refs = refs[:lead] + (wg_s, wu_s, wd_s, nwm_ref, win_s, wtail_s) + refs[lead + 5:-5]

    @pl.when(step >= 0)
    def _():
        _ffn_proj_step(step, *refs, n_rows=n_rows, embed_blocks=embed_blocks, n_out=n_out,
                       conv_out=conv_out, lp=lp)


def _ffn_proj_step(step, *refs, n_rows, embed_blocks, n_out, conv_out, lp):
    row_refs = refs[:n_rows]
    refs = refs[n_rows:]
    if embed_blocks:
        meta_ref, refs = refs[0], refs[1:]
    if conv_out is not None:
        convw_ref, refs = refs[0], refs[1:]
        xpad_ref = refs[10 + n_out]

        @pl.when(step == 0)
        def _():
            xpad_ref[0:CONV_HALO, :] = jnp.zeros((CONV_HALO, xpad_ref.shape[1]), F32)

    nwa_ref, nwb_ref, wg_ref, wu_ref, wd_ref, nwm_ref, win_ref, wtail_ref = refs[:8]
    h_ref = refs[8]
    out_refs = refs[9:9 + n_out]
    a_ref = refs[9 + n_out]
    offsets = np.cumsum([0] + [o.shape[1] for o in out_refs])
    xs = []
    for part in range(ROW_SPLIT):
        if embed_blocks:
            per = n_rows // ROW_SPLIT
            first = step * SUB + part * per
            xs.append(jnp.concatenate(
                [jnp.where((first + j) % embed_blocks == 0, meta_ref[...], r[...])
                 for j, r in enumerate(row_refs[part * per:(part + 1) * per])], axis=0))
        else:
            xs.append(_load_rows(row_refs, part))
    hs = _ffn_core(xs, nwa_ref[...], nwb_ref[...], wg_ref, wu_ref, wd_ref, a_ref)
    hns = [_rms(h, nwm_ref[...]).astype(BF16) for h in hs]
    width = win_ref.shape[2]
    for part, (h, hn) in enumerate(zip(hs, hns)):
        rows = _group(part)
        h_ref[rows, :] = h
        for c in range(win_ref.shape[0]):
            p = _dot(hn, win_ref[c])
            for i in range(n_out - 1):
                lo = max(c * width, offsets[i])
                hi = min((c + 1) * width, offsets[i + 1])
                if lo >= hi:
                    continue
                piece = p[:, lo - c * width:hi - c * width]
                cols = slice(lo - offsets[i], hi - offsets[i])
                if i == conv_out:
                    xpad_ref[CONV_HALO + part * GROUP_ROWS:CONV_HALO + (part + 1) * GROUP_ROWS,
                             cols] = piece
                else:
                    out_refs[i][rows, cols] = piece.astype(out_refs[i].dtype)
        out_refs[-1][rows, :] = _dot(hn, wtail_ref[...]).astype(out_refs[-1].dtype)

    if conv_out is not None:
        taps = convw_ref.shape[0]
        cw = convw_ref[...]
        y = cw[taps - 1:taps, :] * xpad_ref[CONV_HALO:, :]
        for j in range(taps - 1):
            lo = CONV_HALO - (taps - 1 - j)
            y = y + cw[j:j + 1, :] * xpad_ref[lo:lo + ROW_TILE, :]
        tail = xpad_ref[ROW_TILE:, :]
        row = (step * ROW_TILE) % lp + lax.broadcasted_iota(jnp.int32, (ROW_TILE, 1), 0)
        row = jnp.where(row >= lp, row - lp, row)
        out_refs[conv_out][...] = jnp.where(row >= PAD, _silu(y), 0.0).astype(
            out_refs[conv_out].dtype)
        xpad_ref[0:CONV_HALO, :] = tail


def _ffn_proj(src, meta_block, norm4, layer, wg, wu, wd, w_in, index, widths, dtypes, rows, lp,
              seq, conv_out=None, conv_w=None):
    d = norm4.shape[-1]
    d_ff = wg.shape[-1]
    prep = d_ff // FF_TILE
    main = sum(widths[:-1])
    tail_valid = w_in.shape[2] - main
    assert 0 < tail_valid <= widths[-1] == LANE and main % LANE == 0 and lp > ROW_TILE
    width = FF_TILE
    assert main % width == 0
    win_chunks = main // width
    prep = max(prep, win_chunks + 1)
    w_in = w_in[index]
    win_spec = pl.BlockSpec((d, width), lambda i: (0, jnp.minimum(i, win_chunks)))
    if meta_block is None:
        row_specs, extra, extra_specs, embed_blocks = _tile_specs(d), [], [], 0
    else:
        row_specs = _embed_specs(d, lp, seq)
        extra, extra_specs, embed_blocks = [meta_block], [_resident(meta_block.shape)], (
            lp // SWA_BLOCK)
    scratch = [pltpu.VMEM((ROW_TILE, d_ff), BF16)]
    if conv_out is not None:
        extra, extra_specs = extra + [conv_w], extra_specs + [_resident(conv_w.shape)]
        scratch.append(pltpu.VMEM((CONV_HALO + ROW_TILE, widths[conv_out]), F32))
    tile = lambda n: pl.BlockSpec((ROW_TILE, n), lambda i: (i, 0))
    shift = lambda specs: [_after_prep(s, prep) for s in specs]
    return pl.pallas_call(
        functools.partial(_ffn_proj_body, n_rows=len(row_specs), embed_blocks=embed_blocks,
                          n_out=len(widths), conv_out=conv_out, lp=lp, tail_valid=tail_valid,
                          prologue=prep),
        grid=(prep + rows // ROW_TILE,),
        in_specs=shift(row_specs + extra_specs
                       + [_pick(norm4, layer, 0), _pick(norm4, layer, 1)])
        + _weight_chunk_specs(wg, wd, layer, 0)
        + shift([_pick(norm4, layer, 2)]) + [win_spec],
        out_specs=shift([tile(d)] + [tile(n) for n in widths]),
        out_shape=[jax.ShapeDtypeStruct((rows, d), F32)]
        + [jax.ShapeDtypeStruct((rows, n), dt) for n, dt in zip(widths, dtypes)],
        scratch_shapes=scratch + _weight_scratch(wg)
        + [pltpu.VMEM((win_chunks, d, width), BF16), pltpu.VMEM((d, LANE), BF16)],
        compiler_params=_params("arbitrary"),
        name="ffn_proj",
    )(*([src] * len(row_specs)), *extra, norm4, norm4, wg, wu, wd, norm4, w_in)


def _proj_ffn_body(*refs, n_rows, n_x):
    x_groups = [refs[g * n_rows:(g + 1) * n_rows] for g in range(n_x)]
    refs = refs[n_x * n_rows:]
    h_refs = refs[:n_rows]
    (wo_ref, nwo_ref, nwa_ref, nwb_ref, wg_blk, wu_blk, wd_blk, o_ref, a_ref,
     wg_ref, wu_ref, wd_ref) = refs[n_rows:]
    prep = _prepare_weights((wg_blk, wu_blk, wd_blk), (wg_ref, wu_ref, wd_ref))

    @pl.when(pl.program_id(0) >= prep)
    def _():
        accs = []
        for part in range(ROW_SPLIT):
            acc = None
            off = 0
            for group in x_groups:
                xs = _load_rows(group, part)
                k = xs.shape[1]
                p = _dot(xs, wo_ref[off:off + k, :])
                acc = p if acc is None else acc + p
                off += k
            accs.append(acc)
        hs = [_load_rows(h_refs, part) + _rms(acc, nwo_ref[...])
              for part, acc in enumerate(accs)]
        outs = _ffn_core(hs, nwa_ref[...], nwb_ref[...], wg_ref, wu_ref, wd_ref, a_ref)
        for part, out in enumerate(outs):
            o_ref[_group(part), :] = out


def _proj_ffn(xs, h, w_out, norm4, layer, wg, wu, wd, final, batch, lp, seq):
    d = norm4.shape[-1]
    d_ff = wg.shape[-1]
    if final:
        specs = lambda width: _real_specs(width, lp, seq)
        out_rows = batch * seq
    else:
        specs = _tile_specs
        out_rows = batch * lp
    h_specs = specs(d)
    n_rows = len(h_specs)
    in_specs, args = [], []
    for x in xs:
        in_specs += specs(x.shape[1])
        args += [x] * n_rows
    in_specs += h_specs + [_resident(w_out.shape), _pick(norm4, layer, 3),
                           _pick(norm4, layer, 4), _pick(norm4, layer, 5)]
    prep = d_ff // FF_TILE
    in_specs = [_after_prep(s, prep) for s in in_specs] + _weight_chunk_specs(wg, wd, layer, 1)
    args += [h] * n_rows + [w_out, norm4, norm4, norm4, wg, wu, wd]
    return pl.pallas_call(
        functools.partial(_proj_ffn_body, n_rows=n_rows, n_x=len(xs)),
        grid=(prep + out_rows // ROW_TILE,),
        in_specs=in_specs,
        out_specs=_after_prep(pl.BlockSpec((ROW_TILE, d), lambda i: (i, 0)), prep),
        out_shape=jax.ShapeDtypeStruct((out_rows, d), F32),
        scratch_shapes=[pltpu.VMEM((ROW_TILE, d_ff), BF16)] + _weight_scratch(wg),
        compiler_params=_params("arbitrary"),
        name="proj_ffn",
    )(*args)


def _t5_bucket_np(rel):
    n = np.maximum(rel, 0)
    max_exact = REL_BUCKETS // 2
    n_f = np.maximum(n, 1).astype(np.float32)
    large = max_exact + (np.log(n_f / np.float32(max_exact))
                         / np.float32(math.log(REL_MAX_DIST / max_exact))
                         * np.float32(REL_BUCKETS - max_exact)).astype(np.int32)
    large = np.minimum(large, REL_BUCKETS - 1)
    return np.where(n < max_exact, n, large).astype(np.int32)


def _swa_bucket_index():
    i = np.arange(SWA_BLOCK)[:, None]
    c = np.arange(SWA_BLOCK)[None, :]
    out = np.full((3, SWA_BLOCK, 2 * SWA_BLOCK), -1, np.int32)
    for variant in range(3):
        pos_q = variant * SWA_BLOCK + i - PAD
        pos_k = variant * SWA_BLOCK + c - PAD - np.where(c > i, SWA_BLOCK, 0)
        rel_b = pos_q - pos_k
        assert ((rel_b >= 0) & (rel_b < SWA_BLOCK)).all()
        out[variant, :, :SWA_BLOCK] = np.where(pos_k >= N_META, _t5_bucket_np(rel_b), -1)
        rel_m = pos_q - c
        meta = (c < N_META) & (rel_m >= 0)
        out[variant, :, SWA_BLOCK:] = np.where(meta, _t5_bucket_np(rel_m), -1)
    return out


_SWA_BUCKET_INDEX = _swa_bucket_index()


def _bias_body(tbl_ref, idx_ref, o_ref):
    idx = idx_ref[0]
    group = SWA_Q_HEADS // SWA_KV_HEADS
    for head in range(SWA_Q_HEADS):
        acc = jnp.full(idx.shape, NEG_INF, F32)
        for b in range(REL_BUCKETS):
            acc = jnp.where(idx == b, tbl_ref[b, head], acc)
        rows = slice((head % group) * SWA_BLOCK, (head % group + 1) * SWA_BLOCK)
        o_ref[0, head // group, rows, :] = acc


def _swa_bias(rel_table):
    idx = jnp.asarray(_SWA_BUCKET_INDEX)
    group = SWA_Q_HEADS // SWA_KV_HEADS
    width = idx.shape[2]
    return pl.pallas_call(
        _bias_body,
        grid=(3,),
        in_specs=[pl.BlockSpec(memory_space=pltpu.SMEM),
                  pl.BlockSpec((1, SWA_BLOCK, width), lambda v: (v, 0, 0))],
        out_specs=pl.BlockSpec((1, SWA_KV_HEADS, group * SWA_BLOCK, width),
                               lambda v: (v, 0, 0, 0)),
        out_shape=jax.ShapeDtypeStruct((3, SWA_KV_HEADS, group * SWA_BLOCK, width), F32),
        compiler_params=_params("arbitrary"),
        name="swa_bias",
    )(rel_table, idx)


def _swa_stages(sink_ref, q_ref, kc_ref, kp_ref, km_ref, vc_ref, vp_ref, vm_ref, bias_ref,
                step, blocks, store):
    dh = SWA_HEAD_DIM
    blk = SWA_BLOCK
    group = SWA_Q_HEADS // SWA_KV_HEADS
    scale = dh ** -0.5
    zpad = jnp.zeros((blk - N_META, dh), BF16)
    row = lax.broadcasted_iota(jnp.int32, (group * blk, blk), 0) % blk
    col = lax.broadcasted_iota(jnp.int32, (group * blk, blk), 1)
    from_prev = col > row

    ones_col = (lax.broadcasted_iota(jnp.int32, (3 * blk, dh), 1) == 0).astype(BF16)

    units = [(j, hk) for j in range(blocks) for hk in range(SWA_KV_HEADS)]
    scores, values = [], []
    for j, hk in units:
        r = slice(j * blk, (j + 1) * blk)
        kv = slice(hk * dh, (hk + 1) * dh)
        k_prev = kp_ref[:, kv] if j == 0 else kc_ref[(j - 1) * blk:j * blk, kv]
        v_prev = vp_ref[:, kv] if j == 0 else vc_ref[(j - 1) * blk:j * blk, kv]
        k_all = jnp.concatenate([k_prev, kc_ref[r, kv], km_ref[:, kv], zpad], axis=0)
        v_all = jnp.concatenate([v_prev, vc_ref[r, kv], vm_ref[:, kv], zpad], axis=0)
        values.append(jnp.concatenate([v_all, ones_col], axis=1))
        q = jnp.concatenate([q_ref[r, (hk * group + g) * dh:(hk * group + g + 1) * dh]
                             for g in range(group)], axis=0) * scale
        scores.append(_dot_nt(q, k_all))
        yield

    probs, sink_terms = [], []
    for (j, hk), s in zip(units, scores):
        bias = bias_ref[jnp.minimum(step * blocks + j, 2), hk]
        s_band = jnp.where(from_prev, s[:, :blk], s[:, blk:2 * blk]) + bias[:, :blk]
        s_meta = s[:, 2 * blk:] + bias[:, blk:]
        sink = jnp.concatenate([jnp.full((blk, 1), sink_ref[hk * group + g], F32)
                                for g in range(group)], axis=0)
        m = jnp.maximum(jnp.max(jnp.maximum(s_band, s_meta), axis=-1, keepdims=True), sink)
        e_band = jnp.exp(s_band - m)
        e_meta = jnp.exp(s_meta - m)
        sink_terms.append(jnp.exp(sink - m))
        probs.append(jnp.concatenate([jnp.where(from_prev, e_band, 0.0),
                                      jnp.where(from_prev, 0.0, e_band), e_meta],
                                     axis=1).astype(BF16))

    for (j, hk), p, v_ext, sink_term in zip(units, probs, values, sink_terms):
        pv = _dot(p, v_ext)
        o = pv[:, :dh] / (pv[:, dh:dh + 1] + sink_term)
        for g in range(group):
            hq = hk * group + g
            store(slice(j * blk, (j + 1) * blk), slice(hq * dh, (hq + 1) * dh),
                  o[g * blk:(g + 1) * blk])
        yield


def _swa_body(sink_ref, q_ref, kc_ref, kp_ref, km_ref, vc_ref, vp_ref, vm_ref, bias_ref, o_ref,
              *, blocks):
    def store(rows, cols, value):
        o_ref[rows, cols] = value.astype(o_ref.dtype)

    _drain(_swa_stages(sink_ref, q_ref, kc_ref, kp_ref, km_ref, vc_ref, vp_ref, vm_ref,
                       bias_ref, pl.program_id(1), blocks, store))


def _swa(q, k, v, bias, sinks, batch, lp, blocks):
    rows = q.shape[0]
    nb = lp // SWA_BLOCK
    steps = nb // blocks
    meta_blocks = lp // N_META
    cur = lambda b, s: (b * steps + s, 0)
    prev = lambda b, s: (b * nb + jnp.maximum(s * blocks - 1, 0), 0)
    meta = lambda b, s: (b * meta_blocks + PAD // N_META, 0)
    kv_w = k.shape[1]
    rs = blocks * SWA_BLOCK
    return pl.pallas_call(
        functools.partial(_swa_body, blocks=blocks),
        grid=(batch, steps),
        in_specs=[pl.BlockSpec(memory_space=pltpu.SMEM),
                  pl.BlockSpec((rs, q.shape[1]), cur),
                  pl.BlockSpec((rs, kv_w), cur), pl.BlockSpec((SWA_BLOCK, kv_w), prev),
                  pl.BlockSpec((N_META, kv_w), meta),
                  pl.BlockSpec((rs, kv_w), cur), pl.BlockSpec((SWA_BLOCK, kv_w), prev),
                  pl.BlockSpec((N_META, kv_w), meta),
                  _resident(bias.shape)],
        out_specs=pl.BlockSpec((rs, q.shape[1]), cur),
        out_shape=jax.ShapeDtypeStruct((rows, q.shape[1]), BF16),
        compiler_params=_params("parallel", "arbitrary"),
        name="swa",
    )(sinks, q, k, k, k, v, v, v, bias)


def _tri_mask():
    row = lax.broadcasted_iota(jnp.int32, (CHUNK, CHUNK), 0)
    col = lax.broadcasted_iota(jnp.int32, (CHUNK, CHUNK), 1)
    return row >= col


def _deltanet_stages(act_ref, z_ref, gates_ref, alog_ref, dtb_ref, nw_ref, s_ref, chunks,
                     store):
    nh, dh = DN_HEADS, DN_HEAD_DIM
    gt = gates_ref[...]
    beta_full = jax.nn.sigmoid(gt)
    g_full = -jnp.exp(alog_ref[...]) * _softplus(gt + dtb_ref[...])
    tri = _tri_mask().astype(F32)
    gc_chunks = [_dot(tri, g_full[c * CHUNK:(c + 1) * CHUNK, :], HIGHEST)
                 for c in range(chunks)]
    yield

    lane = lax.broadcasted_iota(jnp.int32, (CHUNK, 2 * CHUNK), 1)
    row = lax.broadcasted_iota(jnp.int32, (CHUNK, 2 * CHUNK), 0)
    first = lane < CHUNK
    col = jnp.where(first, lane, lane - CHUNK)
    incl2, strict2 = row >= col, row > col
    eye2 = (row == col).astype(F32)
    first_rows2 = lax.broadcasted_iota(jnp.int32, (2 * CHUNK, 2 * CHUNK), 1) < CHUNK

    def block_diag(a, b):
        zero = jnp.zeros_like(a)
        return jnp.concatenate([jnp.concatenate([a, zero], axis=1),
                                jnp.concatenate([zero, b], axis=1)], axis=0)

    def block_diag_halves(x):
        zero = jnp.zeros_like(x)
        return jnp.concatenate([jnp.where(first, x, zero), jnp.where(first, zero, x)], axis=0)

    pairs = [(c, p) for c in range(chunks) for p in range(nh // 2)]
    neg_a, akt16, rhs16, qd16, decay = [], [], [], {}, {}
    for c, p in pairs:
        if (c, p) != (0, 0):
            yield
        r = slice(c * CHUNK, (c + 1) * CHUNK)
        ks, kqs, kdecs, gcs, rhs = [], [], [], [], []
        for h in (2 * p, 2 * p + 1):
            q = act_ref[r, h * dh:(h + 1) * dh].astype(F32)
            k = act_ref[r, DN_DIM + h * dh:DN_DIM + (h + 1) * dh].astype(F32)
            v = act_ref[r, 2 * DN_DIM + h * dh:2 * DN_DIM + (h + 1) * dh].astype(F32)
            q = q * lax.rsqrt(jnp.sum(q * q, axis=-1, keepdims=True) + 1e-6) * dh ** -0.5
            k = k * lax.rsqrt(jnp.sum(k * k, axis=-1, keepdims=True) + 1e-6)
            beta = jnp.broadcast_to(beta_full[r, h:h + 1], (CHUNK, dh))
            gc = jnp.broadcast_to(gc_chunks[c][:, nh + h:nh + h + 1], (CHUNK, dh))
            gc_last = gc[CHUNK - 1:CHUNK, :]
            k_beta = k * beta
            ks.append(k)
            kqs.append(jnp.concatenate([k_beta, q], axis=0).astype(BF16))
            kdecs.append(k * jnp.exp(gc_last - gc))
            gcs.append(gc)
            rhs.append(jnp.concatenate([v * beta, k_beta * jnp.exp(gc)], axis=1).astype(BF16))
            qd16[c, h] = (q * jnp.exp(gc)).astype(BF16)
            decay[c, h] = jnp.exp(gc_last)
        gc2 = jnp.where(first, gcs[0], gcs[1])
        gc_row2 = jnp.transpose(jnp.concatenate(gcs, axis=0))[:CHUNK, :]
        gamma2 = jnp.where(incl2, jnp.exp(jnp.where(incl2, gc2 - gc_row2, 0.0)), 0.0)
        k2 = jnp.concatenate(ks, axis=0).astype(BF16)
        kq2 = jnp.where(first_rows2, _dot_nt(kqs[0], k2), _dot_nt(kqs[1], k2))
        neg_a.append(jnp.where(strict2, -(kq2[:CHUNK] * gamma2), 0.0))
        attn16 = jnp.where(incl2, kq2[CHUNK:] * gamma2, 0.0).astype(BF16)
        kdt16 = jnp.transpose(jnp.concatenate(kdecs, axis=0)).astype(BF16)
        akt16.append(jnp.concatenate([attn16, kdt16], axis=0))
        rhs16.append(block_diag(rhs[0], rhs[1]))

    yield
    levels = int(math.log2(CHUNK)) - 1
    t = [eye2 + x for x in neg_a]
    p16 = [x.astype(BF16) for x in neg_a]
    p16 = [_dot(x, block_diag_halves(x)).astype(BF16) for x in p16]
    yield
    for level in range(levels):
        if level + 1 < levels:
            tp = [_dot(jnp.concatenate([y.astype(BF16), x], axis=0), block_diag_halves(x))
                  for y, x in zip(t, p16)]
            t = [y + r[:CHUNK] for y, r in zip(t, tp)]
            p16 = [r[CHUNK:].astype(BF16) for r in tp]
        else:
            t = [y + _dot(y.astype(BF16), block_diag_halves(x)) for y, x in zip(t, p16)]
        yield
    uw2 = [_dot(y.astype(BF16), b) for y, b in zip(t, rhs16)]
    u, wq16 = {}, {}
    for (c, p), x in zip(pairs, uw2):
        for half, h in enumerate((2 * p, 2 * p + 1)):
            u[c, h] = x[:, half * 2 * dh:half * 2 * dh + dh]
            w16 = x[:, half * 2 * dh + dh:(half + 1) * 2 * dh].astype(BF16)
            wq16[c, h] = jnp.concatenate([w16, qd16[c, h]], axis=0)
    yield

    nw = nw_ref[...]
    state = [s_ref[h] for h in range(nh)]
    for c in range(chunks):
        r = slice(c * CHUNK, (c + 1) * CHUNK)
        s16 = [s.astype(BF16) for s in state]
        wqs = [_dot(wq16[c, h], s16[h]) for h in range(nh)]
        yield True
        for p in range(nh // 2):
            ha, hb = 2 * p, 2 * p + 1
            vn_a = (u[c, ha] - wqs[ha][:CHUNK]).astype(BF16)
            vn_b = (u[c, hb] - wqs[hb][:CHUNK]).astype(BF16)
            ak2 = _dot(akt16[c * (nh // 2) + p], block_diag(vn_a, vn_b))
            for half, h in enumerate((ha, hb)):
                hc = slice(h * dh, (h + 1) * dh)
                ak = ak2[:, half * dh:(half + 1) * dh]
                state[h] = state[h] * decay[c, h] + ak[CHUNK:]
                out = wqs[h][CHUNK:] + ak[:CHUNK]
                store(r, hc, _rms(out, nw) * _silu(z_ref[r, hc].astype(F32)))
        yield True
    for h in range(nh):
        s_ref[h] = state[h]


def _deltanet_body(act_ref, z_ref, gates_ref, alog_ref, dtb_ref, nw_ref, o_ref, s_ref, *,
                   chunks):
    @pl.when(pl.program_id(1) == 0)
    def _():
        s_ref[...] = jnp.zeros_like(s_ref)

    def store(rows, cols, value):
        o_ref[rows, cols] = value.astype(o_ref.dtype)

    _drain(_deltanet_stages(act_ref, z_ref, gates_ref, alog_ref, dtb_ref, nw_ref, s_ref,
                            chunks, store))


def _deltanet(act, z, gates, alog_vec, dtb_vec, nw, batch, lp, chunks):
    rows = act.shape[0]
    rs = chunks * CHUNK
    steps = lp // rs
    idx = lambda b, s: (b * steps + s, 0)
    return pl.pallas_call(
        functools.partial(_deltanet_body, chunks=chunks),
        grid=(batch, steps),
        in_specs=[pl.BlockSpec((rs, act.shape[1]), idx), pl.BlockSpec((rs, z.shape[1]), idx),
                  pl.BlockSpec((rs, gates.shape[1]), idx),
                  _resident(alog_vec.shape), _resident(dtb_vec.shape), _resident(nw.shape)],
        out_specs=pl.BlockSpec((rs, DN_DIM), idx),
        out_shape=jax.ShapeDtypeStruct((rows, DN_DIM), BF16),
        scratch_shapes=[pltpu.VMEM((DN_HEADS, DN_HEAD_DIM, DN_HEAD_DIM), F32)],
        compiler_params=_params("parallel", "arbitrary"),
        name="deltanet",
    )(act, z, gates, alog_vec, dtb_vec, nw)


def _mix_ffn_body(*refs, blocks, chunks, steps, last):
    *refs, wg_s, wu_s, wd_s = refs
    prep = _prepare_weights(refs[20:23], (wg_s, wu_s, wd_s))
    t = pl.program_id(0) - prep
    refs = refs[:20] + [wg_s, wu_s, wd_s] + refs[23:]

    @pl.when(t >= 0)
    def _():
        _mix_ffn_step(t, *refs, blocks=blocks, chunks=chunks, steps=steps, last=last)


def _mix_ffn_step(t, sink_ref, q_ref, kc_ref, kp_ref, km_ref, vc_ref, vp_ref, vm_ref, bias_ref,
                  act_ref, z_ref, gates_ref, alog_ref, dtb_ref, dnw_ref,
                  h_ref, wo_ref, nwo_ref, nwa_ref, nwb_ref, wg_ref, wu_ref, wd_ref,
                  o_ref, s_ref, mix_ref, a_ref, *, blocks, chunks, steps, last):
    seq_step = jnp.minimum(t, last) % steps
    write_slot = t % 2
    read_slot = 1 - write_slot
    swa_width = q_ref.shape[1]
    groups = MIX_SPLIT
    rows = q_ref.shape[0] // groups

    @pl.when(t == 0)
    def _():
        mix_ref[...] = jnp.zeros_like(mix_ref)

    @pl.when(seq_step == 0)
    def _():
        s_ref[...] = jnp.zeros_like(s_ref)

    def store_swa(rows, cols, value):
        mix_ref[write_slot, rows, cols] = value.astype(mix_ref.dtype)

    def store_dn(rows, cols, value):
        cols = slice(swa_width + cols.start, swa_width + cols.stop)
        mix_ref[write_slot, rows, cols] = value.astype(mix_ref.dtype)

    def mixer_stages():
        dn = _deltanet_stages(act_ref, z_ref, gates_ref, alog_ref, dtb_ref, dnw_ref,
                              s_ref, chunks, store_dn)
        swa = _swa_stages(sink_ref, q_ref, kc_ref, kp_ref, km_ref, vc_ref, vp_ref,
                          vm_ref, bias_ref, seq_step, blocks, store_swa)
        units = SWA_KV_HEADS * blocks
        for index, in_recurrence in enumerate(dn):
            yield
            if ((index < units or in_recurrence)
                    and next(swa, StopIteration) is not StopIteration):
                yield
        yield from swa

    def ffn_stages():
        accs = []
        for part in range(groups):
            accs.append(_dot(mix_ref[read_slot, _group(part, rows), :], wo_ref[...]))
            yield
        hs = [h_ref[_group(part, rows), :] + _rms(acc, nwo_ref[...])
              for part, acc in enumerate(accs)]
        outs = []
        yield from _ffn_stages(hs, nwa_ref[...], nwb_ref[...], wg_ref, wu_ref, wd_ref, a_ref,
                               outs)
        for part, out in enumerate(outs):
            o_ref[_group(part, rows), :] = out

    _interleave(ffn_stages(), mixer_stages())


def _mix_ffn(qa, ka, va, bias, sinks, act, z, gates, alog_vec, dtb_vec, dn_nw, h, w_out, norm4,
             layer, wg, wu, wd, batch, lp, blocks, chunks):
    d = norm4.shape[-1]
    d_ff = wg.shape[-1]
    rs = blocks * SWA_BLOCK
    assert rs == chunks * CHUNK and rs % (16 * MIX_SPLIT) == 0
    nb = lp // SWA_BLOCK
    steps = lp // rs
    last = batch * steps - 1
    meta_blocks = lp // N_META

    def tile(t):
        return jnp.minimum(t, last)

    cur = lambda t: (tile(t), 0)
    prev = lambda t: ((tile(t) // steps) * nb + jnp.maximum((tile(t) % steps) * blocks - 1, 0), 0)
    meta = lambda t: ((tile(t) // steps) * meta_blocks + PAD // N_META, 0)
    behind = lambda t: (jnp.maximum(t - 1, 0), 0)
    kv_w = ka.shape[1]
    prep = d_ff // FF_TILE
    in_specs = [pl.BlockSpec(memory_space=pltpu.SMEM),
                pl.BlockSpec((rs, qa.shape[1]), cur),
                pl.BlockSpec((rs, kv_w), cur), pl.BlockSpec((SWA_BLOCK, kv_w), prev),
                pl.BlockSpec((N_META, kv_w), meta),
                pl.BlockSpec((rs, kv_w), cur), pl.BlockSpec((SWA_BLOCK, kv_w), prev),
                pl.BlockSpec((N_META, kv_w), meta),
                _resident(bias.shape),
                pl.BlockSpec((rs, act.shape[1]), cur), pl.BlockSpec((rs, z.shape[1]), cur),
                pl.BlockSpec((rs, gates.shape[1]), cur),
                _resident(alog_vec.shape), _resident(dtb_vec.shape), _resident(dn_nw.shape),
                pl.BlockSpec((rs, d), behind), _resident(w_out.shape),
                _pick(norm4, layer, 3), _pick(norm4, layer, 4), _pick(norm4, layer, 5)]
    return pl.pallas_call(
        functools.partial(_mix_ffn_body, blocks=blocks, chunks=chunks, steps=steps, last=last),
        grid=(prep + last + 2,),
        in_specs=[_after_prep(s, prep) for s in in_specs]
        + _weight_chunk_specs(wg, wd, layer, 1),
        out_specs=_after_prep(pl.BlockSpec((rs, d), behind), prep),
        out_shape=jax.ShapeDtypeStruct(h.shape, F32),
        scratch_shapes=[pltpu.VMEM((DN_HEADS, DN_HEAD_DIM, DN_HEAD_DIM), F32),
                        pltpu.VMEM((2, rs, qa.shape[1] + DN_DIM), BF16),
                        pltpu.VMEM((rs, d_ff), BF16)] + _weight_scratch(wg),
        compiler_params=_params("arbitrary"),
        name="mix_ffn",
    )(sinks, qa, ka, ka, ka, va, va, va, bias, act, z, gates, alog_vec, dtb_vec, dn_nw,
      h, w_out, norm4, norm4, norm4, wg, wu, wd)


def _gla_body(q_ref, k_ref, v_ref, g_ref, gk_ref, wgu_ref, bg_ref, nw_ref, o_ref, s_ref, *,
              chunks):
    nh = GLA_HEADS
    dk = q_ref.shape[1] // nh
    dv = v_ref.shape[1] // nh

    @pl.when(pl.program_id(1) == 0)
    def _():
        s_ref[...] = jnp.zeros_like(s_ref)

    logits = _dot(gk_ref[...].astype(BF16), wgu_ref[...]) + bg_ref[...]
    glog = (jnp.minimum(logits, 0.0) - jnp.log(1.0 + jnp.exp(-jnp.abs(logits)))) / GLA_GATE_NORM
    incl = _tri_mask()
    tri = incl.astype(F32)
    bcum = [_dot(tri, glog[c * CHUNK:(c + 1) * CHUNK, :], HIGHEST) for c in range(chunks)]

    items = [(c, h) for c in range(chunks) for h in range(nh)]
    qd16, kn16, kdt16, decay = [], [], [], []
    for c, h in items:
        r = slice(c * CHUNK, (c + 1) * CHUNK)
        kc = slice(h * dk, (h + 1) * dk)
        bc = bcum[c][:, kc]
        q = q_ref[r, kc].astype(F32) * dk ** -0.5
        k = k_ref[r, kc].astype(F32)
        b_last = bc[CHUNK - 1:CHUNK, :]
        qd16.append((q * jnp.exp(bc)).astype(BF16))
        kn16.append((k * jnp.exp(-bc)).astype(BF16))
        kdt16.append(jnp.transpose(k * jnp.exp(b_last - bc)).astype(BF16))
        decay_rows = jnp.broadcast_to(jnp.exp(b_last), (8, dk))
        decay.append(jnp.transpose(decay_rows)[:, :1])
    values = [v_ref[c * CHUNK:(c + 1) * CHUNK, h * dv:(h + 1) * dv] for c, h in items]
    attn16 = [jnp.where(incl, _dot_nt(a, b), 0.0).astype(BF16) for a, b in zip(qd16, kn16)]
    intra = [_dot(a, v) for a, v in zip(attn16, values)]
    update = [_dot(kt, v) for kt, v in zip(kdt16, values)]

    nw = nw_ref[...]
    state = [s_ref[h] for h in range(nh)]
    for c in range(chunks):
        r = slice(c * CHUNK, (c + 1) * CHUNK)
        for h in range(nh):
            i = c * nh + h
            vc = slice(h * dv, (h + 1) * dv)
            o = intra[i] + _dot(qd16[i], state[h].astype(BF16))
            state[h] = state[h] * decay[i] + update[i]
            o = _rms(o, nw) * _silu(g_ref[r, vc].astype(F32))
            o_ref[r, vc] = o.astype(o_ref.dtype)
    for h in range(nh):
        s_ref[h] = state[h]


def _gla(q, k, v, g, gk, wgu, bg, nw, batch, lp, chunks):
    rows = q.shape[0]
    rs = chunks * CHUNK
    steps = lp // rs
    idx = lambda b, s: (b * steps + s, 0)
    dk = q.shape[1] // GLA_HEADS
    dv = v.shape[1] // GLA_HEADS
    return pl.pallas_call(
        functools.partial(_gla_body, chunks=chunks),
        grid=(batch, steps),
        in_specs=[pl.BlockSpec((rs, q.shape[1]), idx), pl.BlockSpec((rs, k.shape[1]), idx),
                  pl.BlockSpec((rs, v.shape[1]), idx), pl.BlockSpec((rs, g.shape[1]), idx),
                  pl.BlockSpec((rs, gk.shape[1]), idx), _resident(wgu.shape),
                  _resident(bg.shape), _resident(nw.shape)],
        out_specs=pl.BlockSpec((rs, v.shape[1]), idx),
        out_shape=jax.ShapeDtypeStruct((rows, v.shape[1]), BF16),
        scratch_shapes=[pltpu.VMEM((GLA_HEADS, dk, dv), F32)],
        compiler_params=_params("parallel", "arbitrary"),
        name="gla",
    )(q, k, v, g, gk, wgu, bg, nw)


def _per_step(n, preferred):
    for c in preferred:
        if n % c == 0:
            return c
    return 1


def kernel(x, meta_tokens, norm_w, ffn_w_gate, ffn_w_up, ffn_w_down, rel_bias_table,
           even_w_in, even_conv_w, swa_sinks, dn_a_log, dn_dt_bias, dn_norm_w, even_w_out,
           odd_w_in, gla_w_gate_up, gla_b_gate, gla_norm_w, odd_w_out):
    batch, seq, d = x.shape
    depth = norm_w.shape[0]
    lp = SWA_BLOCK + seq
    assert seq % ROW_TILE == 0 and (batch * lp) % ROW_TILE == 0
    rows = batch * lp
    chunks = _per_step(lp // CHUNK, (6, 3, 2))
    swa_blocks = _per_step(lp // SWA_BLOCK, (3, 2))

    meta_block = jnp.pad(meta_tokens.astype(x.dtype), ((PAD, 0), (0, 0)))
    norm4 = norm_w[:, :, None, :]
    wg, wu, wd = ffn_w_gate, ffn_w_up, ffn_w_down

    swa_q = SWA_Q_HEADS * SWA_HEAD_DIM
    swa_kv = SWA_KV_HEADS * SWA_HEAD_DIM
    bias = None
    h = x.reshape(batch * seq, d)
    for layer in range(depth):
        i = layer // 2
        embed = meta_block if layer == 0 else None
        final = layer == depth - 1
        if layer % 2 == 0:
            assert even_w_in.shape[2] == swa_q + 2 * swa_kv + 4 * DN_DIM + 2 * DN_HEADS
            widths = (swa_q, swa_kv, swa_kv, 3 * DN_DIM, DN_DIM, LANE)
            h, qa, ka, va, act_b, z_b, gates = _ffn_proj(
                h, embed, norm4, layer, wg, wu, wd, even_w_in, i, widths,
                (BF16, BF16, BF16, BF16, BF16, F32), rows, lp, seq,
                conv_out=3, conv_w=even_conv_w[i])
            if bias is None:
                bias = _swa_bias(rel_bias_table)
            lane_pad = (DN_HEADS, LANE - 2 * DN_HEADS)
            alog_vec = jnp.pad(dn_a_log[i], lane_pad)[None, :]
            dtb_vec = jnp.pad(dn_dt_bias[i], lane_pad)[None, :]
            dn_nw = dn_norm_w[i][None, :]
            w_out = even_w_out[i]
            if not final and swa_blocks * SWA_BLOCK == chunks * CHUNK:
                h = _mix_ffn(qa, ka, va, bias, swa_sinks[i], act_b, z_b, gates, alog_vec,
                             dtb_vec, dn_nw, h, w_out.astype(BF16), norm4, layer, wg, wu, wd,
                             batch, lp, swa_blocks, chunks)
                continue
            o_a = _swa(qa, ka, va, bias, swa_sinks[i], batch, lp, swa_blocks)
            o_b = _deltanet(act_b, z_b, gates, alog_vec, dtb_vec, dn_nw, batch, lp, chunks)
            mixed = [o_a, o_b]
        else:
            key_dim = gla_w_gate_up.shape[2]
            val_dim = odd_w_out.shape[1]
            widths = (key_dim, key_dim, val_dim, val_dim, LANE)
            h, q, k, v, g, gk = _ffn_proj(
                h, embed, norm4, layer, wg, wu, wd, odd_w_in, i, widths,
                (BF16, BF16, BF16, BF16, F32), rows, lp, seq)
            wgu = jnp.pad(gla_w_gate_up[i], ((0, LANE - GLA_GATE_RANK), (0, 0))).astype(BF16)
            o = _gla(q, k, v, g, gk, wgu, gla_b_gate[i][None, :], gla_norm_w[i][None, :],
                     batch, lp, _per_step(lp // CHUNK, (11, 6, 3, 2)))
            mixed, w_out = [o], odd_w_out[i]
        h = _proj_ffn(mixed, h, w_out.astype(BF16), norm4, layer, wg, wu, wd, final,
                      batch, lp, seq)
    return h.reshape(batch, seq, d)
```
